```python
import math
import jax, jax.numpy as jnp
from jax import lax
import numpy as np

D_MODEL = 1024
BATCH = 8
SEQ = 2048
DEPTH = 1
DEC_BATCH = 128
DEC_SEQ = 1
PAST_LEN = 2048
PAGE_SIZE = 128

MIX_WIDTH = D_MODEL
FOX_HEADS = 8
FOX_HEAD_DIM = 64
FOX_WIDTH = FOX_HEADS * FOX_HEAD_DIM
GDN_HEADS = 4
GDN_HEAD_DIM = 128
GDN_WIDTH = GDN_HEADS * GDN_HEAD_DIM
GDN_CONV = 4
GDN_CHUNK = 64
FFN_DIM = 2816
FFN_CONV = 3
Q_BLOCK = 128
EPS = 1e-6
NEG_BIG = -1e30
FORGET_BIAS_INIT = 2.0
IN_SECTIONS = (FOX_WIDTH, FOX_WIDTH, FOX_WIDTH, FOX_HEADS,
               GDN_WIDTH, GDN_WIDTH, GDN_WIDTH, GDN_HEADS, GDN_HEADS, GDN_WIDTH)
IN_WIDTH = sum(IN_SECTIONS)
SPLIT_POINTS = tuple(sum(IN_SECTIONS[:i + 1]) for i in range(len(IN_SECTIONS) - 1))
GDN_QKV_SPLIT = (GDN_WIDTH, 2 * GDN_WIDTH)

kernel_name = 'fox_gdn_parallel_heads_convffn_step'


def rms_norm(x, g):
    xf = x.astype(jnp.float32)
    y = xf * lax.rsqrt(jnp.mean(xf * xf, axis=-1, keepdims=True) + EPS)
    return (y * g.astype(jnp.float32)).astype(x.dtype)


def l2_normalize(x):
    xf = x.astype(jnp.float32)
    return xf * lax.rsqrt(jnp.sum(xf * xf, axis=-1, keepdims=True) + EPS)


def gather_pages(pool, page_table):
    pages = pool[page_table]
    b, n_pages = page_table.shape
    return pages.reshape((b, n_pages * pool.shape[1]) + pool.shape[2:])


def causal_depthwise_conv(x, ctx, w, b=None):
    width = w.shape[0]
    t = x.shape[1]
    xc = jnp.concatenate([ctx.astype(x.dtype), x], axis=1)
    y = xc[:, 0:t] * w[0]
    for i in range(1, width):
        y = y + xc[:, i:i + t] * w[i]
    if b is not None:
        y = y + b
    return y, xc[:, -(width - 1):]


def fox_attention(q, k, v, cq, ck, q_pos, k_pos):
    b, tq, h, dh = q.shape
    blk = min(Q_BLOCK, tq)
    n_blk = -(-tq // blk)
    pad = n_blk * blk - tq
    q = jnp.pad(q, ((0, 0), (0, pad), (0, 0), (0, 0)))
    cq = jnp.pad(cq, ((0, 0), (0, pad), (0, 0)))
    q_pos = jnp.pad(q_pos, (0, pad), mode='edge')
    qb = q.reshape(b, n_blk, blk, h, dh).transpose(1, 0, 2, 3, 4)
    cqb = cq.reshape(b, n_blk, blk, h).transpose(1, 0, 3, 2)
    pb = q_pos.reshape(n_blk, blk)
    ck_t = jnp.transpose(ck, (0, 2, 1))
    scale = dh ** -0.5

    def one_block(args):
        q_blk, cq_blk, p_blk = args
        s = jnp.einsum('bqhd,bkhd->bhqk', q_blk, k,
                       preferred_element_type=jnp.float32) * scale
        s = s + (cq_blk[..., :, None] - ck_t[:, :, None, :])
        mask = k_pos[None, :] <= p_blk[:, None]
        s = jnp.where(mask, s, NEG_BIG)
        p = jax.nn.softmax(s, axis=-1)
        return jnp.einsum('bhqk,bkhd->bqhd', p.astype(v.dtype), v)

    out = lax.map(one_block, (qb, cqb, pb))
    return out.transpose(1, 0, 2, 3, 4).reshape(b, n_blk * blk, h, dh)[:, :tq]


def gated_delta_chunked(q, k, v, g, beta, s0):
    b, t, h, dk = q.shape
    dv = v.shape[-1]
    c = min(GDN_CHUNK, t)
    n = -(-t // c)
    pad = n * c - t
    f32 = jnp.float32
    q = q.astype(f32) * (dk ** -0.5)
    k = k.astype(f32)
    v = v.astype(f32)
    pad4 = ((0, 0), (0, pad), (0, 0), (0, 0))
    pad3 = ((0, 0), (0, pad), (0, 0))
    q, k, v = jnp.pad(q, pad4), jnp.pad(k, pad4), jnp.pad(v, pad4)
    g, beta = jnp.pad(g, pad3), jnp.pad(beta, pad3)

    def chunks(x):
        return x.reshape(b, n, c, h, x.shape[-1]).transpose(0, 3, 1, 2, 4)

    qc, kc, vc = chunks(q), chunks(k), chunks(v)
    gc = g.reshape(b, n, c, h).transpose(0, 3, 1, 2)
    bc = beta.reshape(b, n, c, h).transpose(0, 3, 1, 2)
    gcum = jnp.cumsum(gc, axis=-1)
    diff = gcum[..., :, None] - gcum[..., None, :]
    lower = jnp.tril(jnp.ones((c, c), dtype=bool))
    strict = jnp.tril(jnp.ones((c, c), dtype=bool), -1)
    decay = jnp.where(lower, jnp.exp(jnp.where(lower, diff, 0.0)), 0.0)
    kk = jnp.einsum('bhnid,bhnjd->bhnij', kc, kc)
    a_mat = jnp.where(strict, bc[..., :, None] * kk * decay, 0.0)
    lhs = a_mat + jnp.eye(c, dtype=f32)
    rhs = jnp.concatenate([vc * bc[..., None], kc * (bc * jnp.exp(gcum))[..., None]], axis=-1)
    sol = lax.linalg.triangular_solve(lhs, rhs, left_side=True, lower=True, unit_diagonal=True)
    u, w = sol[..., :dv], sol[..., dv:]
    qk = jnp.einsum('bhnid,bhnjd->bhnij', qc, kc) * decay
    q_dec = qc * jnp.exp(gcum)[..., None]
    k_dec = kc * jnp.exp(gcum[..., -1:] - gcum)[..., None]
    g_last = jnp.exp(gcum[..., -1])

    def step(s, xs):
        u_i, w_i, qk_i, qd_i, kd_i, gl_i = xs
        v_new = u_i - jnp.einsum('bhcd,bhde->bhce', w_i, s)
        o = jnp.einsum('bhcd,bhde->bhce', qd_i, s) + jnp.einsum('bhij,bhje->bhie', qk_i, v_new)
        s = s * gl_i[..., None, None] + jnp.einsum('bhcd,bhce->bhde', kd_i, v_new)
        return s, o

    xs = tuple(jnp.moveaxis(x, 2, 0) for x in (u, w, qk, q_dec, k_dec, g_last))
    s_final, o = lax.scan(step, s0.astype(f32), xs)
    o = o.transpose(1, 0, 3, 2, 4).reshape(b, n * c, h, dv)[:, :t]
    return o, s_final.astype(s0.dtype)


def decoder_layer(x, past_k, past_v, past_lf, s0, gdn_ctx, ffn_ctx,
                  norm_mix, w_in, b_forget, gdn_a_log, gdn_dt_bias, w_gdn_conv,
                  gdn_out_norm, w_out, norm_ffn, w_up, w_ffn_conv, b_ffn_conv, w_down):
    b, t, _ = x.shape
    p_len = past_k.shape[1]
    f32 = jnp.float32
    h = rms_norm(x, norm_mix)
    proj = h @ w_in
    fq, fk, fv, ff, gq, gk, gv, ga, gb, gz = jnp.split(proj, SPLIT_POINTS, axis=-1)

    fq = fq.reshape(b, t, FOX_HEADS, FOX_HEAD_DIM)
    fk = fk.reshape(b, t, FOX_HEADS, FOX_HEAD_DIM)
    fv = fv.reshape(b, t, FOX_HEADS, FOX_HEAD_DIM)
    lf = jax.nn.log_sigmoid(ff.astype(f32) + b_forget.astype(f32))
    k_all = jnp.concatenate([past_k.astype(fk.dtype), fk], axis=1)
    v_all = jnp.concatenate([past_v.astype(fv.dtype), fv], axis=1)
    cum = jnp.cumsum(jnp.concatenate([past_lf.astype(f32), lf], axis=1), axis=1)
    pos = jnp.arange(p_len + t)
    fox_out = fox_attention(fq, k_all, v_all, cum[:, p_len:], cum, pos[p_len:], pos)

    qkv = jnp.concatenate([gq, gk, gv], axis=-1)
    qkv_c, new_gdn_ctx = causal_depthwise_conv(qkv, gdn_ctx, w_gdn_conv)
    qkv_c = jax.nn.silu(qkv_c)
    cq, ck, cv = jnp.split(qkv_c, GDN_QKV_SPLIT, axis=-1)
    cq = l2_normalize(cq.reshape(b, t, GDN_HEADS, GDN_HEAD_DIM))
    ck = l2_normalize(ck.reshape(b, t, GDN_HEADS, GDN_HEAD_DIM))
    cv = cv.reshape(b, t, GDN_HEADS, GDN_HEAD_DIM)
    beta = jax.nn.sigmoid(gb.astype(f32))
    g = -jnp.exp(gdn_a_log.astype(f32)) * jax.nn.softplus(ga.astype(f32) + gdn_dt_bias.astype(f32))
    o, new_s = gated_delta_chunked(cq, ck, cv, g, beta, s0)
    o = rms_norm(o, gdn_out_norm) * jax.nn.silu(gz.reshape(b, t, GDN_HEADS, GDN_HEAD_DIM).astype(f32))

    mix = jnp.concatenate([fox_out.reshape(b, t, FOX_WIDTH),
                           o.reshape(b, t, GDN_WIDTH).astype(x.dtype)], axis=-1)
    x = x + mix @ w_out

    h2 = rms_norm(x, norm_ffn)
    gate, up = jnp.split(h2 @ w_up, 2, axis=-1)
    gate_c, new_ffn_ctx = causal_depthwise_conv(gate, ffn_ctx, w_ffn_conv, b_ffn_conv)
    x = x + (jax.nn.silu(gate_c) * up) @ w_down
    return x, (fk, fv, lf, new_s, new_gdn_ctx, new_ffn_ctx)


def setup_inputs(seed: int = 0) -> dict:
    key = jax.random.key(seed)
    ks = jax.random.split(key, 24)
    f32 = jnp.float32
    n_pages = PAST_LEN // PAGE_SIZE
    n_pool = (5 * DEC_BATCH * n_pages) // 4

    def nrm(k, shape, scale):
        return jax.random.normal(k, shape, f32) * scale

    x_prompt = nrm(ks[0], (BATCH, SEQ, D_MODEL), 1.0)
    x_sample = nrm(ks[1], (DEC_BATCH, DEC_SEQ, D_MODEL), 1.0)
    cache_k = nrm(ks[2], (DEPTH, n_pool, PAGE_SIZE, FOX_HEADS, FOX_HEAD_DIM), 1.0)
    cache_v = nrm(ks[3], (DEPTH, n_pool, PAGE_SIZE, FOX_HEADS, FOX_HEAD_DIM), 1.0)
    cache_logf = jax.nn.log_sigmoid(FORGET_BIAS_INIT + nrm(ks[4], (DEPTH, n_pool, PAGE_SIZE, FOX_HEADS), 1.0))
    state_gdn = nrm(ks[5], (DEPTH, DEC_BATCH, GDN_HEADS, GDN_HEAD_DIM, GDN_HEAD_DIM), GDN_HEAD_DIM ** -0.5)
    state_gdn_conv = nrm(ks[6], (DEPTH, DEC_BATCH, GDN_CONV - 1, 3 * GDN_WIDTH), 1.0)
    state_ffn_conv = nrm(ks[7], (DEPTH, DEC_BATCH, FFN_CONV - 1, FFN_DIM), 1.0)
    page_table = jax.random.permutation(ks[8], n_pool)[: DEC_BATCH * n_pages]
    page_table = page_table.reshape(DEC_BATCH, n_pages).astype(jnp.int32)

    norm_mix = 1.0 + nrm(ks[9], (DEPTH, D_MODEL), 0.02)
    w_in = nrm(ks[10], (DEPTH, D_MODEL, IN_WIDTH), D_MODEL ** -0.5)
    b_forget = FORGET_BIAS_INIT + nrm(ks[11], (DEPTH, FOX_HEADS), 0.5)
    gdn_a_log = jnp.log(jax.random.uniform(ks[12], (DEPTH, GDN_HEADS), f32, 1.0, 16.0))
    dt = jnp.exp(jax.random.uniform(ks[13], (DEPTH, GDN_HEADS), f32, math.log(1e-3), math.log(1e-1)))
    gdn_dt_bias = dt + jnp.log(-jnp.expm1(-dt))
    w_gdn_conv = nrm(ks[14], (DEPTH, GDN_CONV, 3 * GDN_WIDTH), GDN_CONV ** -0.5)
    gdn_out_norm = 1.0 + nrm(ks[15], (DEPTH, GDN_HEAD_DIM), 0.02)
    w_out = nrm(ks[16], (DEPTH, MIX_WIDTH, D_MODEL), MIX_WIDTH ** -0.5)
    norm_ffn = 1.0 + nrm(ks[17], (DEPTH, D_MODEL), 0.02)
    w_up = nrm(ks[18], (DEPTH, D_MODEL, 2 * FFN_DIM), D_MODEL ** -0.5)
    w_ffn_conv = nrm(ks[19], (DEPTH, FFN_CONV, FFN_DIM), FFN_CONV ** -0.5)
    b_ffn_conv = nrm(ks[20], (DEPTH, FFN_DIM), 0.02)
    w_down = nrm(ks[21], (DEPTH, FFN_DIM, D_MODEL), FFN_DIM ** -0.5)
    norm_final = 1.0 + nrm(ks[22], (D_MODEL,), 0.02)
    return {
        'x_prompt': x_prompt, 'x_sample': x_sample,
        'cache_k': cache_k, 'cache_v': cache_v, 'cache_logf': cache_logf,
        'state_gdn': state_gdn, 'state_gdn_conv': state_gdn_conv, 'state_ffn_conv': state_ffn_conv,
        'page_table': page_table,
        'norm_mix': norm_mix, 'w_in': w_in, 'b_forget': b_forget,
        'gdn_a_log': gdn_a_log, 'gdn_dt_bias': gdn_dt_bias, 'w_gdn_conv': w_gdn_conv,
        'gdn_out_norm': gdn_out_norm, 'w_out': w_out, 'norm_ffn': norm_ffn,
        'w_up': w_up, 'w_ffn_conv': w_ffn_conv, 'b_ffn_conv': b_ffn_conv,
        'w_down': w_down, 'norm_final': norm_final,
    }


def reference(x_prompt, x_sample, cache_k, cache_v, cache_logf, state_gdn, state_gdn_conv,
              state_ffn_conv, page_table, norm_mix, w_in, b_forget, gdn_a_log, gdn_dt_bias,
              w_gdn_conv, gdn_out_norm, w_out, norm_ffn, w_up, w_ffn_conv, b_ffn_conv,
              w_down, norm_final):
    yp, ys = x_prompt, x_sample
    bp = x_prompt.shape[0]
    new_p, new_s = [], []
    for layer in range(DEPTH):
        params = (norm_mix[layer], w_in[layer], b_forget[layer], gdn_a_log[layer],
                  gdn_dt_bias[layer], w_gdn_conv[layer], gdn_out_norm[layer], w_out[layer],
                  norm_ffn[layer], w_up[layer], w_ffn_conv[layer], b_ffn_conv[layer], w_down[layer])
        yp, st_p = decoder_layer(
            yp,
            jnp.zeros((bp, 0, FOX_HEADS, FOX_HEAD_DIM), yp.dtype),
            jnp.zeros((bp, 0, FOX_HEADS, FOX_HEAD_DIM), yp.dtype),
            jnp.zeros((bp, 0, FOX_HEADS), jnp.float32),
            jnp.zeros((bp, GDN_HEADS, GDN_HEAD_DIM, GDN_HEAD_DIM), jnp.float32),
            jnp.zeros((bp, GDN_CONV - 1, 3 * GDN_WIDTH), yp.dtype),
            jnp.zeros((bp, FFN_CONV - 1, FFN_DIM), yp.dtype),
            *params)
        ys, st_s = decoder_layer(
            ys,
            gather_pages(cache_k[layer], page_table),
            gather_pages(cache_v[layer], page_table),
            gather_pages(cache_logf[layer], page_table),
            state_gdn[layer], state_gdn_conv[layer], state_ffn_conv[layer],
            *params)
        new_p.append(st_p)
        new_s.append(st_s)
    y_prompt = rms_norm(yp, norm_final)
    y_sample = rms_norm(ys, norm_final)
    k_p, v_p, lf_p, s_p, gc_p, fc_p = [jnp.stack(t, axis=0) for t in zip(*new_p)]
    k_s, v_s, lf_s, s_s, gc_s, fc_s = [jnp.stack(t, axis=0) for t in zip(*new_s)]
    return (y_prompt, y_sample, k_p, v_p, lf_p, s_p, gc_p, fc_p, k_s, v_s, lf_s, s_s, gc_s, fc_s)
```

```python
import functools

import jax
import jax.numpy as jnp
from jax import lax
from jax.experimental import pallas as pl
from jax.experimental.pallas import tpu as pltpu

D_MODEL = 1024
FOX_HEADS = 8
FOX_HEAD_DIM = 64
FOX_WIDTH = FOX_HEADS * FOX_HEAD_DIM
GDN_HEADS = 4
GDN_HEAD_DIM = 128
GDN_WIDTH = GDN_HEADS * GDN_HEAD_DIM
GDN_CONV = 4
GDN_CHUNK = 64
FFN_DIM = 2816
FFN_CONV = 3
EPS = 1e-6
NEG_BIG = -1e30
FOX_SCALE = FOX_HEAD_DIM ** -0.5
GDN_SCALE = GDN_HEAD_DIM ** -0.5

LANES = 128
SUBLANES = 8
ROW_TILE = 256
FOX_TILE = 256
MIB = 1024 * 1024

SM_LF = 0
SM_G = 8
SM_BETA = 12
SM_CUM = 16
SM_GCUM = 24

F32 = jnp.float32
BF16 = jnp.bfloat16


def _sigmoid(x):
    return 1.0 / (1.0 + jnp.exp(-x))


def _silu(x):
    return x * _sigmoid(x)


def _rms(x, g):
    return x * lax.rsqrt(jnp.mean(x * x, axis=-1, keepdims=True) + EPS) * g


def _dot(a, b):
    return jnp.dot(a, b, preferred_element_type=F32)


def _dot_nt(a, b):
    return lax.dot_general(a, b, (((1,), (1,)), ((), ())), preferred_element_type=F32)


def _hdot(a, b):
    return jnp.dot(a, b, preferred_element_type=F32, precision=lax.Precision.HIGHEST)


def _shift_rows(x, prev8, k):
    r = pltpu.roll(x, k, 0)
    row8 = lax.broadcasted_iota(jnp.int32, prev8.shape, 0)
    top = jnp.where(row8 < k, pltpu.roll(prev8, k, 0), r[0:SUBLANES])
    return jnp.concatenate([top, r[SUBLANES:]], axis=0)


def _gate_activations(raw, par_ref):
    z = raw + par_ref[0:1, :]
    lane = lax.broadcasted_iota(jnp.int32, z.shape, 1)
    t = jnp.log1p(jnp.exp(-jnp.abs(z)))
    lf = jnp.minimum(z, 0.0) - t
    softplus = jnp.maximum(z, 0.0) + t
    g = -jnp.exp(par_ref[1:2, :]) * softplus
    beta = _sigmoid(z)
    return jnp.where(lane < SM_G, lf, jnp.where(lane < SM_BETA, g, jnp.where(lane < SM_CUM, beta, 0.0)))


def _gdn_qkv_norm(c, out_ref):
    for part, scale in ((0, GDN_SCALE), (1, 1.0)):
        for hh in range(GDN_HEADS):
            off = part * GDN_WIDTH + hh * GDN_HEAD_DIM
            seg = c[:, off:off + GDN_HEAD_DIM]
            n = lax.rsqrt(jnp.sum(seg * seg, axis=-1, keepdims=True) + EPS)
            out_ref[:, off:off + GDN_HEAD_DIM] = seg * n * scale
    out_ref[:, 2 * GDN_WIDTH:] = c[:, 2 * GDN_WIDTH:]


def _proj_prompt_body(x_ref, nrm_ref, wbig_ref, wsm_ref, par_ref, cw_ref,
                      fk_ref, fv_ref, q16_ref, k16_ref, v16_ref, gqkv_ref, gz_ref,
                      sm_ref, smt_ref, cst_ref, carry_ref, prev_ref, *, tiles_per_seq):
    i = pl.program_id(0)

    @pl.when(i % tiles_per_seq == 0)
    def _():
        carry_ref[...] = jnp.zeros_like(carry_ref)
        prev_ref[...] = jnp.zeros_like(prev_ref)

    tm = x_ref.shape[0]
    h16 = _rms(x_ref[...], nrm_ref[...]).astype(BF16)

    fox = _dot(h16, wbig_ref[:, 0:3 * FOX_WIDTH])
    fq, fk, fv = fox[:, :FOX_WIDTH], fox[:, FOX_WIDTH:2 * FOX_WIDTH], fox[:, 2 * FOX_WIDTH:]
    q16_ref[...] = (fq * FOX_SCALE).astype(BF16)
    fk_ref[...] = fk
    fv_ref[...] = fv
    k16_ref[...] = fk.astype(BF16)
    v16_ref[...] = fv.astype(BF16)

    act = _gate_activations(_dot(h16, wsm_ref[...]), par_ref)
    row = lax.broadcasted_iota(jnp.int32, act.shape, 0)
    lane = lax.broadcasted_iota(jnp.int32, act.shape, 1)
    y = act
    yc = act
    k = 1
    while k < tm:
        y = y + jnp.where(row >= k, pltpu.roll(y, k, 0), 0.0)
        if k < GDN_CHUNK:
            yc = yc + jnp.where((row & (GDN_CHUNK - 1)) >= k, pltpu.roll(yc, k, 0), 0.0)
        k *= 2
    y = y + carry_ref[0:1, :]
    carry_ref[0:1, :] = y[tm - 1:tm, :]
    shift = SM_CUM - SM_LF
    sm = jnp.where(lane < SM_CUM, act,
                   jnp.where(lane < SM_GCUM, pltpu.roll(y, shift, 1),
                             jnp.where(lane < SM_GCUM + GDN_HEADS, pltpu.roll(yc, shift, 1), 0.0)))
    sm_ref[...] = sm
    smt_ref[...] = sm.T

    pre = _dot(h16, wbig_ref[:, 3 * FOX_WIDTH:3 * FOX_WIDTH + 3 * GDN_WIDTH])
    prev = prev_ref[...]
    acc = pre * cw_ref[GDN_CONV - 1:GDN_CONV, :]
    for kk in range(1, GDN_CONV):
        acc = acc + _shift_rows(pre, prev, kk) * cw_ref[GDN_CONV - 1 - kk:GDN_CONV - kk, :]
    prev_ref[...] = pre[tm - SUBLANES:, :]
    cst_ref[...] = pre[tm - SUBLANES:, :]
    _gdn_qkv_norm(_silu(acc), gqkv_ref)

    gz_ref[...] = _dot(h16, wbig_ref[:, 3 * FOX_WIDTH + 3 * GDN_WIDTH:])


def _proj_prompt(x, nrm, wbig, wsm, par, cw, seq_len):
    rows = x.shape[0]
    tm = ROW_TILE
    n_seq = rows // seq_len
    tiles_per_seq = seq_len // tm
    wb = wbig.shape[1]
    row_spec = lambda w: pl.BlockSpec((tm, w), lambda i: (i, 0))
    const = lambda s: pl.BlockSpec(s, lambda i: (0, 0))
    out_shape = (
        jax.ShapeDtypeStruct((rows, FOX_WIDTH), F32),
        jax.ShapeDtypeStruct((rows, FOX_WIDTH), F32),
        jax.ShapeDtypeStruct((rows, FOX_WIDTH), BF16),
        jax.ShapeDtypeStruct((rows, FOX_WIDTH), BF16),
        jax.ShapeDtypeStruct((rows, FOX_WIDTH), BF16),
        jax.ShapeDtypeStruct((rows, 3 * GDN_WIDTH), F32),
        jax.ShapeDtypeStruct((rows, GDN_WIDTH), F32),
        jax.ShapeDtypeStruct((rows, LANES), F32),
        jax.ShapeDtypeStruct((LANES, rows), F32),
        jax.ShapeDtypeStruct((n_seq * SUBLANES, 3 * GDN_WIDTH), F32),
    )
    out_specs = (
        row_spec(FOX_WIDTH), row_spec(FOX_WIDTH), row_spec(FOX_WIDTH), row_spec(FOX_WIDTH),
        row_spec(FOX_WIDTH), row_spec(3 * GDN_WIDTH), row_spec(GDN_WIDTH), row_spec(LANES),
        pl.BlockSpec((LANES, tm), lambda i: (0, i)),
        pl.BlockSpec((SUBLANES, 3 * GDN_WIDTH), lambda i: (i // tiles_per_seq, 0)),
    )
    return pl.pallas_call(
        functools.partial(_proj_prompt_body, tiles_per_seq=tiles_per_seq),
        grid=(rows // tm,),
        in_specs=[row_spec(D_MODEL), const((1, D_MODEL)), const((D_MODEL, wb)), const((D_MODEL, LANES)),
                  const((SUBLANES, LANES)), const((SUBLANES, 3 * GDN_WIDTH))],
        out_specs=out_specs,
        out_shape=out_shape,
        scratch_shapes=[pltpu.VMEM((SUBLANES, LANES), F32), pltpu.VMEM((SUBLANES, 3 * GDN_WIDTH), F32)],
        compiler_params=pltpu.CompilerParams(dimension_semantics=("arbitrary",), vmem_limit_bytes=48 * MIB),
        name="proj_prompt",
    )(x, nrm, wbig, wsm, par, cw)


def _proj_sample_body(x_ref, nrm_ref, wbig_ref, wsm_ref, par_ref, cw_ref, c0_ref, c1_ref, c2_ref,
                      fk_ref, fv_ref, q_ref, gqkv_ref, gz_ref, sm_ref, pre_ref):
    h16 = _rms(x_ref[...], nrm_ref[...]).astype(BF16)
    fox = _dot(h16, wbig_ref[:, 0:3 * FOX_WIDTH])
    q_ref[...] = fox[:, :FOX_WIDTH] * FOX_SCALE
    fk_ref[...] = fox[:, FOX_WIDTH:2 * FOX_WIDTH]
    fv_ref[...] = fox[:, 2 * FOX_WIDTH:]
    sm_ref[...] = _gate_activations(_dot(h16, wsm_ref[...]), par_ref)
    pre = _dot(h16, wbig_ref[:, 3 * FOX_WIDTH:3 * FOX_WIDTH + 3 * GDN_WIDTH])
    pre_ref[...] = pre
    acc = (pre * cw_ref[3:4, :] + c2_ref[...] * cw_ref[2:3, :]
           + c1_ref[...] * cw_ref[1:2, :] + c0_ref[...] * cw_ref[0:1, :])
    _gdn_qkv_norm(_silu(acc), gqkv_ref)
    gz_ref[...] = _dot(h16, wbig_ref[:, 3 * FOX_WIDTH + 3 * GDN_WIDTH:])


def _proj_sample(x, nrm, wbig, wsm, par, cw, c0, c1, c2):
    rows = x.shape[0]
    full = lambda a: pl.BlockSpec(a.shape, lambda i: (0,) * a.ndim)
    args = (x, nrm, wbig, wsm, par, cw, c0, c1, c2)
    shapes = ((rows, FOX_WIDTH), (rows, FOX_WIDTH), (rows, FOX_WIDTH), (rows, 3 * GDN_WIDTH),
              (rows, GDN_WIDTH), (rows, LANES), (rows, 3 * GDN_WIDTH))
    return pl.pallas_call(
        _proj_sample_body,
        grid=(1,),
        in_specs=[full(a) for a in args],
        out_specs=tuple(pl.BlockSpec(s, lambda i: (0, 0)) for s in shapes),
        out_shape=tuple(jax.ShapeDtypeStruct(s, F32) for s in shapes),
        compiler_params=pltpu.CompilerParams(dimension_semantics=("arbitrary",), vmem_limit_bytes=48 * MIB),
        name="proj_sample",
    )(*args)


def _fox_prompt_body(q_ref, k_ref, v_ref, cum_ref, o_ref):
    i = pl.program_id(1)
    t = FOX_TILE
    lane = lax.broadcasted_iota(jnp.int32, (t, LANES), 1)
    rr = lax.broadcasted_iota(jnp.int32, (t, t), 0)
    cc = lax.broadcasted_iota(jnp.int32, (t, t), 1)
    causal = cc <= rr
    qoff = pl.multiple_of(i * t, t)
    for p in range(FOX_HEADS // 2):
        ls = slice(p * LANES, (p + 1) * LANES)
        q2 = q_ref[:, ls]
        qs = (jnp.where(lane < FOX_HEAD_DIM, q2, jnp.zeros_like(q2)),
              jnp.where(lane >= FOX_HEAD_DIM, q2, jnp.zeros_like(q2)))
        crefs = tuple(cum_ref[2 * p + e:2 * p + e + 1, pl.ds(qoff, LANES)][:, 0:1] for e in range(2))

        def tile(j, carry, masked):
            koff = pl.multiple_of(j * t, t)
            k2 = k_ref[pl.ds(koff, t), ls]
            v2 = v_ref[pl.ds(koff, t), ls]
            out = []
            for e in range(2):
                m, l, a = carry[e]
                s = _dot_nt(qs[e], k2) + (crefs[e] - cum_ref[2 * p + e:2 * p + e + 1, pl.ds(koff, t)])
                if masked:
                    s = jnp.where(causal, s, NEG_BIG)
                m_new = jnp.maximum(m, jnp.max(s, axis=-1, keepdims=True))
                alpha = jnp.exp(m - m_new)
                pm = jnp.exp(s - m_new)
                l_new = alpha * l + jnp.sum(pm, axis=-1, keepdims=True)
                a_new = alpha * a + _dot(pm.astype(BF16), v2)
                out.append((m_new, l_new, a_new))
            return tuple(out)

        init = tuple((jnp.full((t, 1), NEG_BIG, F32), jnp.zeros((t, 1), F32), jnp.zeros((t, LANES), F32))
                     for _ in range(2))
        carry = lax.fori_loop(0, i, lambda j, c: tile(j, c, False), init)
        (_, l0, a0), (_, l1, a1) = tile(i, carry, True)
        o_ref[:, ls] = jnp.where(lane < FOX_HEAD_DIM, a0 / l0, a1 / l1).astype(BF16)


def _fox_prompt(q16, k16, v16, smt, n_seq, seq_len):
    t = FOX_TILE
    nq = seq_len // t
    cum_block = SM_CUM // SUBLANES
    return pl.pallas_call(
        _fox_prompt_body,
        grid=(n_seq, nq),
        in_specs=[pl.BlockSpec((t, FOX_WIDTH), lambda b, i: (b * nq + i, 0)),
                  pl.BlockSpec((seq_len, FOX_WIDTH), lambda b, i: (b, 0)),
                  pl.BlockSpec((seq_len, FOX_WIDTH), lambda b, i: (b, 0)),
                  pl.BlockSpec((SUBLANES, seq_len), lambda b, i: (cum_block, b))],
        out_specs=pl.BlockSpec((t, FOX_WIDTH), lambda b, i: (b * nq + i, 0)),
        out_shape=jax.ShapeDtypeStruct(q16.shape, BF16),
        compiler_params=pltpu.CompilerParams(dimension_semantics=("arbitrary", "arbitrary"),
                                             vmem_limit_bytes=40 * MIB),
        name="fox_prompt",
    )(q16, k16, v16, smt)


def _unit_lower_inverse(a, r, c):
    eye = (r == c).astype(F32)
    blk16 = (r // 16) == (c // 16)
    blk32 = (r // 32) == (c // 32)
    p = jnp.where(blk16, -a, 0.0)
    t = eye + p
    for _ in range(3):
        p = _hdot(p, p)
        t = t + _hdot(t, p)
    a1 = jnp.where(jnp.logical_and(blk32, jnp.logical_not(blk16)), a, 0.0)
    t = t - _hdot(_hdot(t, a1), t)
    a2 = jnp.where(blk32, 0.0, a)
    t = t - _hdot(_hdot(t, a2), t)
    return t


def _gdn_prompt_body(qkv_ref, sm_ref, gcr_ref, z_ref, gn_ref, o_ref, sout_ref, s_ref):
    ci = pl.program_id(1)

    @pl.when(ci == 0)
    def _():
        s_ref[...] = jnp.zeros_like(s_ref)

    ch = GDN_CHUNK
    r = lax.broadcasted_iota(jnp.int32, (ch, ch), 0)
    c = lax.broadcasted_iota(jnp.int32, (ch, ch), 1)
    lower = r >= c
    strict = r > c
    for cidx in range(qkv_ref.shape[0] // ch):
        rows = slice(cidx * ch, (cidx + 1) * ch)
        for hh in range(GDN_HEADS):
            hs = slice(hh * GDN_HEAD_DIM, (hh + 1) * GDN_HEAD_DIM)
            q = qkv_ref[rows, hs]
            k = qkv_ref[rows, GDN_WIDTH + hh * GDN_HEAD_DIM:GDN_WIDTH + (hh + 1) * GDN_HEAD_DIM]
            v = qkv_ref[rows, 2 * GDN_WIDTH + hh * GDN_HEAD_DIM:2 * GDN_WIDTH + (hh + 1) * GDN_HEAD_DIM]
            beta = sm_ref[rows, SM_BETA + hh:SM_BETA + hh + 1]
            gc = sm_ref[rows, SM_GCUM + hh:SM_GCUM + hh + 1]
            gr = gcr_ref[hh:hh + 1, rows]
            decay = jnp.where(lower, jnp.exp(jnp.where(lower, gc - gr, 0.0)), 0.0)
            k16 = k.astype(BF16)
            q16 = q.astype(BF16)
            a = jnp.where(strict, beta * _dot_nt(k16, k16) * decay, 0.0)
            tinv = _unit_lower_inverse(a, r, c)
            eg = jnp.exp(gc)
            u = _hdot(tinv, v * beta)
            w = _hdot(tinv, k * (beta * eg))
            qk = _dot_nt(q16, k16) * decay
            gl = gc[ch - 1:ch, :]
            qd = q * eg
            kd = k * jnp.exp(gl - gc)
            s = s_ref[hh]
            s16 = s.astype(BF16)
            v_new = u - _dot(w.astype(BF16), s16)
            vn16 = v_new.astype(BF16)
            o = _dot(qd.astype(BF16), s16) + _dot(qk.astype(BF16), vn16)
            s_ref[hh] = s * jnp.exp(gl) + _dot(kd.T.astype(BF16), vn16)
            o_ref[rows, hs] = (_rms(o, gn_ref[...]) * _silu(z_ref[rows, hs])).astype(BF16)

    @pl.when(ci == pl.num_programs(1) - 1)
    def _():
        sout_ref[0] = s_ref[...]


def _gdn_prompt(gqkv, sm, smt, gz, gn, n_seq, seq_len):
    tm = ROW_TILE
    nt = seq_len // tm
    gcum_block = SM_GCUM // SUBLANES
    rows = gqkv.shape[0]
    return pl.pallas_call(
        _gdn_prompt_body,
        grid=(n_seq, nt),
        in_specs=[pl.BlockSpec((tm, 3 * GDN_WIDTH), lambda b, i: (b * nt + i, 0)),
                  pl.BlockSpec((tm, LANES), lambda b, i: (b * nt + i, 0)),
                  pl.BlockSpec((SUBLANES, tm), lambda b, i: (gcum_block, b * nt + i)),
                  pl.BlockSpec((tm, GDN_WIDTH), lambda b, i: (b * nt + i, 0)),
                  pl.BlockSpec((1, GDN_HEAD_DIM), lambda b, i: (0, 0))],
        out_specs=(pl.BlockSpec((tm, GDN_WIDTH), lambda b, i: (b * nt + i, 0)),
                   pl.BlockSpec((1, GDN_HEADS, GDN_HEAD_DIM, GDN_HEAD_DIM), lambda b, i: (b, 0, 0, 0))),
        out_shape=(jax.ShapeDtypeStruct((rows, GDN_WIDTH), BF16),
                   jax.ShapeDtypeStruct((n_seq, GDN_HEADS, GDN_HEAD_DIM, GDN_HEAD_DIM), F32)),
        scratch_shapes=[pltpu.VMEM((GDN_HEADS, GDN_HEAD_DIM, GDN_HEAD_DIM), F32)],
        compiler_params=pltpu.CompilerParams(dimension_semantics=("arbitrary", "arbitrary"),
                                             vmem_limit_bytes=32 * MIB),
        name="gdn_prompt",
    )(gqkv, sm, smt, gz, gn)


def _ffn_tail(x, fox_ref, gdn_ref, wo_ref, nf_ref, wup_ref, wd_ref, nfin_ref, conv):
    mix = _dot(fox_ref[...], wo_ref[0:FOX_WIDTH, :]) + _dot(gdn_ref[...], wo_ref[FOX_WIDTH:, :])
    x2 = x + mix
    h2 = _rms(x2, nf_ref[...]).astype(BF16)
    gu = _dot(h2, wup_ref[...])
    gate, up = gu[:, :FFN_DIM], gu[:, FFN_DIM:]
    act = (_silu(conv(gate)) * up).astype(BF16)
    x3 = x2 + _dot(act, wd_ref[...])
    return _rms(x3, nfin_ref[...])


def _ffn_prompt_body(x_ref, fox_ref, gdn_ref, wo_ref, nf_ref, wup_ref, cw_ref, cb_ref, wd_ref, nfin_ref,
                     y_ref, cst_ref, prev_ref, *, tiles_per_seq):
    i = pl.program_id(0)

    @pl.when(i % tiles_per_seq == 0)
    def _():
        prev_ref[...] = jnp.zeros_like(prev_ref)

    tm = x_ref.shape[0]

    def conv(gate):
        prev = prev_ref[...]
        out = gate * cw_ref[FFN_CONV - 1:FFN_CONV, :] + cb_ref[...]
        for kk in range(1, FFN_CONV):
            out = out + _shift_rows(gate, prev, kk) * cw_ref[FFN_CONV - 1 - kk:FFN_CONV - kk, :]
        prev_ref[...] = gate[tm - SUBLANES:, :]
        cst_ref[...] = gate[tm - SUBLANES:, :]
        return out

    y_ref[...] = _ffn_tail(x_ref[...], fox_ref, gdn_ref, wo_ref, nf_ref, wup_ref, wd_ref, nfin_ref, conv)


def _ffn_prompt(x, fox, gdn, wo, nf, wup, cw, cb, wd, nfin, seq_len):
    rows = x.shape[0]
    tm = ROW_TILE
    n_seq = rows // seq_len
    tiles_per_seq = seq_len // tm
    row_spec = lambda w: pl.BlockSpec((tm, w), lambda i: (i, 0))
    const = lambda a: pl.BlockSpec(a.shape, lambda i: (0, 0), pipeline_mode=pl.Buffered(1))
    return pl.pallas_call(
        functools.partial(_ffn_prompt_body, tiles_per_seq=tiles_per_seq),
        grid=(rows // tm,),
        in_specs=[row_spec(D_MODEL), row_spec(FOX_WIDTH), row_spec(GDN_WIDTH), const(wo), const(nf),
                  const(wup), const(cw), const(cb), const(wd), const(nfin)],
        out_specs=(row_spec(D_MODEL),
                   pl.BlockSpec((SUBLANES, FFN_DIM), lambda i: (i // tiles_per_seq, 0))),
        out_shape=(jax.ShapeDtypeStruct((rows, D_MODEL), F32),
                   jax.ShapeDtypeStruct((n_seq * SUBLANES, FFN_DIM), F32)),
        scratch_shapes=[pltpu.VMEM((SUBLANES, FFN_DIM), F32)],
        compiler_params=pltpu.CompilerParams(dimension_semantics=("arbitrary",), vmem_limit_bytes=56 * MIB),
        name="ffn_prompt",
    )(x, fox, gdn, wo, nf, wup, cw, cb, wd, nfin)


def _ffn_sample_body(x_ref, fox_ref, gdn_ref, wo_ref, nf_ref, wup_ref, cw_ref, cb_ref, wd_ref, nfin_ref,
                     c0_ref, c1_ref, y_ref, gate_ref):
    def conv(gate):
        gate_ref[...] = gate
        return (gate * cw_ref[2:3, :] + c1_ref[...] * cw_ref[1:2, :] + c0_ref[...] * cw_ref[0:1, :]
                + cb_ref[...])

    y_ref[...] = _ffn_tail(x_ref[...], fox_ref, gdn_ref, wo_ref, nf_ref, wup_ref, wd_ref, nfin_ref, conv)


def _ffn_sample(x, fox, gdn, wo, nf, wup, cw, cb, wd, nfin, c0, c1):
    rows = x.shape[0]
    args = (x, fox, gdn, wo, nf, wup, cw, cb, wd, nfin, c0, c1)
    full = lambda a: pl.BlockSpec(a.shape, lambda i: (0, 0), pipeline_mode=pl.Buffered(1))
    return pl.pallas_call(
        _ffn_sample_body,
        grid=(1,),
        in_specs=[full(a) for a in args],
        out_specs=(pl.BlockSpec((rows, D_MODEL), lambda i: (0, 0)),
                   pl.BlockSpec((rows, FFN_DIM), lambda i: (0, 0))),
        out_shape=(jax.ShapeDtypeStruct((rows, D_MODEL), F32), jax.ShapeDtypeStruct((rows, FFN_DIM), F32)),
        compiler_params=pltpu.CompilerParams(dimension_semantics=("arbitrary",), vmem_limit_bytes=56 * MIB),
        name="ffn_sample",
    )(*args)


def _logf_prefix_body(x_ref, m1_ref, m2_ref, w_ref, t_ref):
    x = x_ref[...]
    hi = x.astype(BF16)
    r1 = x - hi.astype(F32)
    mid = r1.astype(BF16)
    lo = (r1 - mid.astype(F32)).astype(BF16)
    m1 = m1_ref[...]
    m2 = m2_ref[...]
    w_ref[...] = _dot(hi, m1) + _dot(mid, m1) + _dot(lo, m1)
    t_ref[...] = _dot(hi, m2) + _dot(mid, m2) + _dot(lo, m2)


def _logf_prefix(logf2d, page_size):
    n_pool, width = logf2d.shape
    tm = ROW_TILE if n_pool % ROW_TILE == 0 else n_pool
    src = jnp.arange(width)
    dst = jnp.arange(width)
    src_t, src_h = src // FOX_HEADS, src % FOX_HEADS
    dst_h, dst_t = dst // page_size, dst % page_size
    m1 = ((src_h[:, None] == dst_h[None, :]) & (src_t[:, None] > dst_t[None, :])).astype(BF16)
    m2 = (src_h[:, None] == jnp.arange(LANES)[None, :]).astype(BF16)
    return pl.pallas_call(
        _logf_prefix_body,
        grid=(n_pool // tm,),
        in_specs=[pl.BlockSpec((tm, width), lambda i: (i, 0)),
                  pl.BlockSpec((width, width), lambda i: (0, 0)),
                  pl.BlockSpec((width, LANES), lambda i: (0, 0))],
        out_specs=(pl.BlockSpec((tm, width), lambda i: (i, 0)), pl.BlockSpec((tm, LANES), lambda i: (i, 0))),
        out_shape=(jax.ShapeDtypeStruct((n_pool, width), F32), jax.ShapeDtypeStruct((n_pool, LANES), F32)),
        compiler_params=pltpu.CompilerParams(dimension_semantics=("arbitrary",), vmem_limit_bytes=32 * MIB),
        name="logf_prefix",
    )(logf2d, m1, m2)


def _decode_body(pt_ref, q_ref, kn_ref, vn_ref, sm_ref, seg_ref, exp_ref, kpool, vpool, wpool, tpool,
                 o_ref, kbuf, vbuf, wbuf, tbuf, st_ref, sems, *, n_pages, page):
    b = pl.program_id(0)
    nb = pl.num_programs(0)
    past = n_pages * page
    slot = b % 2

    def copies(bb, sl):
        out = []
        for j in range(n_pages):
            pg = pt_ref[bb * n_pages + j]
            out.append(pltpu.make_async_copy(kpool.at[pg], kbuf.at[sl, pl.ds(j * page, page), :], sems.at[sl, 0]))
            out.append(pltpu.make_async_copy(vpool.at[pg], vbuf.at[sl, pl.ds(j * page, page), :], sems.at[sl, 1]))
            out.append(pltpu.make_async_copy(wpool.at[pg], wbuf.at[sl, :, pl.ds(j * page, page)], sems.at[sl, 2]))
            out.append(pltpu.make_async_copy(tpool.at[pg], tbuf.at[sl, pl.ds(j, 1), :], sems.at[sl, 3]))
        return out

    @pl.when(b == 0)
    def _():
        for cp in copies(0, 0):
            cp.start()
        for sl in range(2):
            kbuf[sl, pl.ds(past, LANES), :] = jnp.zeros((LANES, FOX_WIDTH), F32)
            vbuf[sl, pl.ds(past, LANES), :] = jnp.zeros((LANES, FOX_WIDTH), F32)

    @pl.when(b + 1 < nb)
    def _():
        for cp in copies(b + 1, 1 - slot):
            cp.start()

    for cp in copies(b, slot):
        cp.wait()

    kb = kbuf.at[slot]
    vb = vbuf.at[slot]
    kb[pl.ds(past, 1), :] = kn_ref[0]
    vb[pl.ds(past, 1), :] = vn_ref[0]
    qrow = q_ref[0]
    n_blk = n_pages + 1
    seg = seg_ref[...]
    for j in range(n_blk):
        blk = pl.ds(j * page, page)
        sc = _dot((kb[blk, :] * qrow).astype(BF16), seg)
        st_ref[:, blk] = sc.T[0:FOX_HEADS, :]

    r8 = lax.broadcasted_iota(jnp.int32, (SUBLANES, LANES), 0)
    c8 = lax.broadcasted_iota(jnp.int32, (SUBLANES, LANES), 1)
    lf_new = jnp.sum(jnp.where(r8 == c8, sm_ref[0], 0.0), axis=1, keepdims=True)
    tot = tbuf[slot]
    rj = lax.broadcasted_iota(jnp.int32, tot.shape, 0)
    suf = tot
    k = 1
    while k < n_pages:
        suf = suf + jnp.where(rj + k < n_pages, pltpu.roll(suf, n_pages - k, 0), 0.0)
        k *= 2
    suf = suf - tot
    suf_t = jnp.concatenate([suf, jnp.zeros((LANES - n_pages, LANES), F32)], axis=0).T[0:FOX_HEADS, :]
    lane_p = lax.broadcasted_iota(jnp.int32, (FOX_HEADS, page), 1)
    logits = []
    for j in range(n_blk):
        blk = pl.ds(j * page, page)
        if j < n_pages:
            bias = wbuf[slot, :, blk] + suf_t[:, j:j + 1] + lf_new
        else:
            bias = jnp.where(lane_p == 0, 0.0, NEG_BIG)
        logits.append(st_ref[:, blk] + bias)
    m = functools.reduce(jnp.maximum, [jnp.max(x, axis=-1, keepdims=True) for x in logits])
    ps = [jnp.exp(x - m) for x in logits]
    inv_l = 1.0 / functools.reduce(lambda a, c: a + c, [jnp.sum(x, axis=-1, keepdims=True) for x in ps])
    expand = exp_ref[...]
    acc = jnp.zeros((SUBLANES, FOX_WIDTH), F32)
    pad = jnp.zeros((LANES - FOX_HEADS, page), F32)
    for j in range(n_blk):
        blk = pl.ds(j * page, page)
        pt = jnp.concatenate([ps[j] * inv_l, pad], axis=0).T
        pv = _dot(pt.astype(BF16), expand) * vb[blk, :]
        acc = acc + jnp.sum(pv.reshape(page // SUBLANES, SUBLANES, FOX_WIDTH), axis=0)
    o_ref[0] = jnp.sum(acc, axis=0, keepdims=True)


def _decode_attention(page_table, q, kn, vn, sm, kpool, vpool, wpool, tpool):
    nb, n_pages = page_table.shape
    page = kpool.shape[1]
    past = n_pages * page
    lane = jnp.arange(LANES)
    col = jnp.arange(FOX_WIDTH)
    seg = (col[:, None] // FOX_HEAD_DIM == lane[None, :]).astype(BF16)
    expand = (lane[:, None] == col[None, :] // FOX_HEAD_DIM).astype(BF16)
    full = lambda a: pl.BlockSpec(a.shape, lambda i, pt: (0,) * a.ndim)
    row = lambda a: pl.BlockSpec((1, 1, a.shape[-1]), lambda i, pt: (i, 0, 0))
    hbm = pl.BlockSpec(memory_space=pl.ANY)
    q, kn, vn, sm = (a.reshape(nb, 1, a.shape[-1]) for a in (q, kn, vn, sm))
    grid_spec = pltpu.PrefetchScalarGridSpec(
        num_scalar_prefetch=1,
        grid=(nb,),
        in_specs=[row(q), row(kn), row(vn), row(sm), full(seg), full(expand), hbm, hbm, hbm, hbm],
        out_specs=pl.BlockSpec((1, 1, FOX_WIDTH), lambda i, pt: (i, 0, 0)),
        scratch_shapes=[pltpu.VMEM((2, past + LANES, FOX_WIDTH), F32),
                        pltpu.VMEM((2, past + LANES, FOX_WIDTH), F32),
                        pltpu.VMEM((2, FOX_HEADS, past), F32),
                        pltpu.VMEM((2, n_pages, LANES), F32),
                        pltpu.VMEM((FOX_HEADS, past + LANES), F32),
                        pltpu.SemaphoreType.DMA((2, 4))],
    )
    out = pl.pallas_call(
        functools.partial(_decode_body, n_pages=n_pages, page=page),
        grid_spec=grid_spec,
        out_shape=jax.ShapeDtypeStruct((nb, 1, FOX_WIDTH), F32),
        compiler_params=pltpu.CompilerParams(dimension_semantics=("arbitrary",), vmem_limit_bytes=40 * MIB),
        name="fox_decode",
    )(page_table.reshape(-1), q, kn, vn, sm, seg, expand, kpool, vpool, wpool, tpool)
    return out.reshape(nb, FOX_WIDTH)


def _gdn_sample_body(s_ref, qkv_ref, sm_ref, z_ref, gn_ref, sout_ref, o_ref):
    d = GDN_HEAD_DIM
    eye = lax.broadcasted_iota(jnp.int32, (d, d), 0) == lax.broadcasted_iota(jnp.int32, (d, d), 1)

    def col(row):
        return jnp.sum(jnp.where(eye, row, 0.0), axis=1, keepdims=True)

    for bb in range(s_ref.shape[0]):
        rs = slice(bb, bb + 1)
        for hh in range(GDN_HEADS):
            hs = slice(hh * d, (hh + 1) * d)
            q = qkv_ref[rs, hs]
            k = qkv_ref[rs, GDN_WIDTH + hh * d:GDN_WIDTH + (hh + 1) * d]
            v = qkv_ref[rs, 2 * GDN_WIDTH + hh * d:2 * GDN_WIDTH + (hh + 1) * d]
            eg = jnp.exp(sm_ref[rs, SM_G + hh:SM_G + hh + 1])
            beta = sm_ref[rs, SM_BETA + hh:SM_BETA + hh + 1]
            s = s_ref[bb, hh]
            v_new = v * beta - jnp.sum(col(k * (beta * eg)) * s, axis=0, keepdims=True)
            o = (jnp.sum(col(q * eg) * s, axis=0, keepdims=True)
                 + jnp.sum(q * k, axis=1, keepdims=True) * v_new)
            sout_ref[bb, hh] = s * eg + col(k) * v_new
            o_ref[rs, hs] = _rms(o, gn_ref[...]) * _silu(z_ref[rs, hs])


def _gdn_sample(state, gqkv, sm, gz, gn):
    nb = state.shape[0]
    bb = SUBLANES
    d = GDN_HEAD_DIM
    row_spec = lambda w: pl.BlockSpec((bb, w), lambda i: (i, 0))
    st_spec = pl.BlockSpec((bb, GDN_HEADS, d, d), lambda i: (i, 0, 0, 0))
    return pl.pallas_call(
        _gdn_sample_body,
        grid=(nb // bb,),
        in_specs=[st_spec, row_spec(3 * GDN_WIDTH), row_spec(LANES), row_spec(GDN_WIDTH),
                  pl.BlockSpec((1, d), lambda i: (0, 0))],
        out_specs=(st_spec, row_spec(GDN_WIDTH)),
        out_shape=(jax.ShapeDtypeStruct(state.shape, F32), jax.ShapeDtypeStruct((nb, GDN_WIDTH), F32)),
        compiler_params=pltpu.CompilerParams(dimension_semantics=("arbitrary",), vmem_limit_bytes=32 * MIB),
        name="gdn_sample",
    )(state, gqkv, sm, gz, gn)


def _pad_rows(a, rows):
    return jnp.concatenate([a, jnp.zeros((rows - a.shape[0],) + a.shape[1:], a.dtype)], axis=0)


def kernel(x_prompt, x_sample, cache_k, cache_v, cache_logf, state_gdn, state_gdn_conv, state_ffn_conv,
           page_table, norm_mix, w_in, b_forget, gdn_a_log, gdn_dt_bias, w_gdn_conv, gdn_out_norm, w_out,
           norm_ffn, w_up, w_ffn_conv, b_ffn_conv, w_down, norm_final):
    assert w_in.shape[0] == 1, "single-layer trunk"
    n_seq, seq_len, _ = x_prompt.shape
    nb = x_sample.shape[0]
    n_pool, page = cache_k.shape[1], cache_k.shape[2]

    w = w_in[0]
    o_ff = 3 * FOX_WIDTH
    o_g = o_ff + FOX_HEADS
    o_ga = o_g + 3 * GDN_WIDTH
    o_gz = o_ga + 2 * GDN_HEADS
    wbig = jnp.concatenate([w[:, :o_ff], w[:, o_g:o_ga], w[:, o_gz:]], axis=1).astype(BF16)
    wsm = jnp.concatenate([w[:, o_ff:o_g], w[:, o_ga:o_gz],
                           jnp.zeros((D_MODEL, LANES - FOX_HEADS - 2 * GDN_HEADS), F32)], axis=1).astype(BF16)
    par = jnp.zeros((SUBLANES, LANES), F32)
    par = par.at[0, SM_LF:SM_LF + FOX_HEADS].set(b_forget[0])
    par = par.at[0, SM_G:SM_G + GDN_HEADS].set(gdn_dt_bias[0])
    par = par.at[1, SM_G:SM_G + GDN_HEADS].set(gdn_a_log[0])
    nrm = norm_mix[0][None, :]
    cw = _pad_rows(w_gdn_conv[0], SUBLANES)
    gn = gdn_out_norm[0][None, :]
    wo = w_out[0].astype(BF16)
    nf = norm_ffn[0][None, :]
    wup = w_up[0].astype(BF16)
    cwf = _pad_rows(w_ffn_conv[0], SUBLANES)
    cbf = b_ffn_conv[0][None, :]
    wd = w_down[0].astype(BF16)
    nfin = norm_final[None, :]

    xp = x_prompt.reshape(n_seq * seq_len, D_MODEL)
    fk, fv, q16, k16, v16, gqkv, gz, sm, smt, cst = _proj_prompt(xp, nrm, wbig, wsm, par, cw, seq_len)
    fox = _fox_prompt(q16, k16, v16, smt, n_seq, seq_len)
    gdn, s_p = _gdn_prompt(gqkv, sm, smt, gz, gn, n_seq, seq_len)
    yp, cstf = _ffn_prompt(xp, fox, gdn, wo, nf, wup, cwf, cbf, wd, nfin, seq_len)

    xs = x_sample.reshape(nb, D_MODEL)
    gctx = state_gdn_conv[0]
    fctx = state_ffn_conv[0]
    fk_s, fv_s, q_s, gqkv_s, gz_s, sm_s, pre_s = _proj_sample(
        xs, nrm, wbig, wsm, par, cw, gctx[:, 0], gctx[:, 1], gctx[:, 2])
    wexc, tot = _logf_prefix(cache_logf[0].reshape(n_pool, page * FOX_HEADS), page)
    fox_s = _decode_attention(page_table, q_s, fk_s, fv_s, sm_s,
                              cache_k[0].reshape(n_pool, page, FOX_WIDTH),
                              cache_v[0].reshape(n_pool, page, FOX_WIDTH),
                              wexc.reshape(n_pool, FOX_HEADS, page), tot.reshape(n_pool, 1, LANES))
    s_s, gdn_s = _gdn_sample(state_gdn[0], gqkv_s, sm_s, gz_s, gn)
    ys, gate_s = _ffn_sample(xs, fox_s.astype(BF16), gdn_s.astype(BF16), wo, nf, wup, cwf, cbf, wd, nfin,
                             fctx[:, 0], fctx[:, 1])

    kv_shape_p = (1, n_seq, seq_len, FOX_HEADS, FOX_HEAD_DIM)
    kv_shape_s = (1, nb, 1, FOX_HEADS, FOX_HEAD_DIM)
    return (
        yp.reshape(n_seq, seq_len, D_MODEL),
        ys.reshape(nb, 1, D_MODEL),
        fk.reshape(kv_shape_p),
        fv.reshape(kv_shape_p),
        sm[:, SM_LF:SM_LF + FOX_HEADS].reshape(1, n_seq, seq_len, FOX_HEADS),
        s_p[None],
        cst.reshape(n_seq, SUBLANES, 3 * GDN_WIDTH)[None, :, SUBLANES - (GDN_CONV - 1):],
        cstf.reshape(n_seq, SUBLANES, FFN_DIM)[None, :, SUBLANES - (FFN_CONV - 1):],
        fk_s.reshape(kv_shape_s),
        fv_s.reshape(kv_shape_s),
        sm_s[:, SM_LF:SM_LF + FOX_HEADS].reshape(1, nb, 1, FOX_HEADS),
        s_s[None],
        jnp.concatenate([gctx[:, 1:], pre_s[:, None, :]], axis=1)[None],
        jnp.concatenate([fctx[:, 1:], gate_s[:, None, :]], axis=1)[None],
    )
```

```python
import functools

import jax
import jax.numpy as jnp
from jax import lax
from jax.experimental import pallas as pl
from jax.experimental.pallas import tpu as pltpu

D_MODEL = 1024
FOX_HEADS = 8
FOX_HEAD_DIM = 64
FOX_WIDTH = FOX_HEADS * FOX_HEAD_DIM
GDN_HEADS = 4
GDN_HEAD_DIM = 128
GDN_WIDTH = GDN_HEADS * GDN_HEAD_DIM
GDN_CONV = 4
GDN_CHUNK = 64
FFN_DIM = 2816
FFN_CONV = 3
EPS = 1e-6
NEG_BIG = -1e30
FOX_SCALE = FOX_HEAD_DIM ** -0.5
GDN_SCALE = GDN_HEAD_DIM ** -0.5

LANES = 128
SUBLANES = 8
ROW_TILE = 256
FOX_TILE = 256
MIB = 1024 * 1024

SM_LF = 0
SM_G = 8
SM_BETA = 12
SM_CUM = 16
SM_GCUM = 24

F32 = jnp.float32
BF16 = jnp.bfloat16


def _sigmoid(x):
    return 1.0 / (1.0 + jnp.exp(-x))


def _silu(x):
    return x * _sigmoid(x)


def _rms(x, g):
    return x * lax.rsqrt(jnp.mean(x * x, axis=-1, keepdims=True) + EPS) * g


def _dot(a, b):
    return jnp.dot(a, b, preferred_element_type=F32)


def _dot_nt(a, b):
    return lax.dot_general(a, b, (((1,), (1,)), ((), ())), preferred_element_type=F32)


def _hdot(a, b):
    return jnp.dot(a, b, preferred_element_type=F32, precision=lax.Precision.HIGHEST)


def _shift_rows(x, prev8, k):
    r = pltpu.roll(x, k, 0)
    row8 = lax.broadcasted_iota(jnp.int32, prev8.shape, 0)
    top = jnp.where(row8 < k, pltpu.roll(prev8, k, 0), r[0:SUBLANES])
    return jnp.concatenate([top, r[SUBLANES:]], axis=0)


def _gate_activations(raw, par_ref):
    z = raw + par_ref[0:1, :]
    lane = lax.broadcasted_iota(jnp.int32, z.shape, 1)
    t = jnp.log1p(jnp.exp(-jnp.abs(z)))
    lf = jnp.minimum(z, 0.0) - t
    softplus = jnp.maximum(z, 0.0) + t
    g = -jnp.exp(par_ref[1:2, :]) * softplus
    beta = _sigmoid(z)
    return jnp.where(lane < SM_G, lf, jnp.where(lane < SM_BETA, g, jnp.where(lane < SM_CUM, beta, 0.0)))


def _gdn_qkv_norm(c, out_ref):
    for part, scale in ((0, GDN_SCALE), (1, 1.0)):
        for hh in range(GDN_HEADS):
            off = part * GDN_WIDTH + hh * GDN_HEAD_DIM
            seg = c[:, off:off + GDN_HEAD_DIM]
            n = lax.rsqrt(jnp.sum(seg * seg, axis=-1, keepdims=True) + EPS)
            out_ref[:, off:off + GDN_HEAD_DIM] = seg * n * scale
    out_ref[:, 2 * GDN_WIDTH:] = c[:, 2 * GDN_WIDTH:]


def _proj_prompt_body(x_ref, nrm_ref, wbig_ref, wsm_ref, par_ref, cw_ref,
                      fk_ref, fv_ref, q16_ref, k16_ref, v16_ref, gqkv_ref, gz_ref,
                      sm_ref, smt_ref, cst_ref, carry_ref, prev_ref, *, tiles_per_seq):
    i = pl.program_id(0)

    @pl.when(i % tiles_per_seq == 0)
    def _():
        carry_ref[...] = jnp.zeros_like(carry_ref)
        prev_ref[...] = jnp.zeros_like(prev_ref)

    tm = x_ref.shape[0]
    h16 = _rms(x_ref[...], nrm_ref[...]).astype(BF16)

    fox = _dot(h16, wbig_ref[:, 0:3 * FOX_WIDTH])
    fq, fk, fv = fox[:, :FOX_WIDTH], fox[:, FOX_WIDTH:2 * FOX_WIDTH], fox[:, 2 * FOX_WIDTH:]
    q16_ref[...] = (fq * FOX_SCALE).astype(BF16)
    fk_ref[...] = fk
    fv_ref[...] = fv
    k16_ref[...] = fk.astype(BF16)
    v16_ref[...] = fv.astype(BF16)

    act = _gate_activations(_dot(h16, wsm_ref[...]), par_ref)
    row = lax.broadcasted_iota(jnp.int32, act.shape, 0)
    lane = lax.broadcasted_iota(jnp.int32, act.shape, 1)
    y = act
    yc = act
    k = 1
    while k < tm:
        y = y + jnp.where(row >= k, pltpu.roll(y, k, 0), 0.0)
        if k < GDN_CHUNK:
            yc = yc + jnp.where((row & (GDN_CHUNK - 1)) >= k, pltpu.roll(yc, k, 0), 0.0)
        k *= 2
    y = y + carry_ref[0:1, :]
    carry_ref[0:1, :] = y[tm - 1:tm, :]
    shift = SM_CUM - SM_LF
    sm = jnp.where(lane < SM_CUM, act,
                   jnp.where(lane < SM_GCUM, pltpu.roll(y, shift, 1),
                             jnp.where(lane < SM_GCUM + GDN_HEADS, pltpu.roll(yc, shift, 1), 0.0)))
    sm_ref[...] = sm
    smt_ref[...] = sm.T

    pre = _dot(h16, wbig_ref[:, 3 * FOX_WIDTH:3 * FOX_WIDTH + 3 * GDN_WIDTH])
    prev = prev_ref[...]
    acc = pre * cw_ref[GDN_CONV - 1:GDN_CONV, :]
    for kk in range(1, GDN_CONV):
        acc = acc + _shift_rows(pre, prev, kk) * cw_ref[GDN_CONV - 1 - kk:GDN_CONV - kk, :]
    prev_ref[...] = pre[tm - SUBLANES:, :]
    cst_ref[...] = pre[tm - SUBLANES:, :]
    _gdn_qkv_norm(_silu(acc), gqkv_ref)

    gz_ref[...] = _dot(h16, wbig_ref[:, 3 * FOX_WIDTH + 3 * GDN_WIDTH:])


def _proj_prompt(x, nrm, wbig, wsm, par, cw, seq_len):
    rows = x.shape[0]
    tm = ROW_TILE
    n_seq = rows // seq_len
    tiles_per_seq = seq_len // tm
    wb = wbig.shape[1]
    row_spec = lambda w: pl.BlockSpec((tm, w), lambda i: (i, 0))
    const = lambda s: pl.BlockSpec(s, lambda i: (0, 0))
    out_shape = (
        jax.ShapeDtypeStruct((rows, FOX_WIDTH), F32),
        jax.ShapeDtypeStruct((rows, FOX_WIDTH), F32),
        jax.ShapeDtypeStruct((rows, FOX_WIDTH), BF16),
        jax.ShapeDtypeStruct((rows, FOX_WIDTH), BF16),
        jax.ShapeDtypeStruct((rows, FOX_WIDTH), BF16),
        jax.ShapeDtypeStruct((rows, 3 * GDN_WIDTH), F32),
        jax.ShapeDtypeStruct((rows, GDN_WIDTH), F32),
        jax.ShapeDtypeStruct((rows, LANES), F32),
        jax.ShapeDtypeStruct((LANES, rows), F32),
        jax.ShapeDtypeStruct((n_seq * SUBLANES, 3 * GDN_WIDTH), F32),
    )
    out_specs = (
        row_spec(FOX_WIDTH), row_spec(FOX_WIDTH), row_spec(FOX_WIDTH), row_spec(FOX_WIDTH),
        row_spec(FOX_WIDTH), row_spec(3 * GDN_WIDTH), row_spec(GDN_WIDTH), row_spec(LANES),
        pl.BlockSpec((LANES, tm), lambda i: (0, i)),
        pl.BlockSpec((SUBLANES, 3 * GDN_WIDTH), lambda i: (i // tiles_per_seq, 0)),
    )
    return pl.pallas_call(
        functools.partial(_proj_prompt_body, tiles_per_seq=tiles_per_seq),
        grid=(rows // tm,),
        in_specs=[row_spec(D_MODEL), const((1, D_MODEL)), const((D_MODEL, wb)), const((D_MODEL, LANES)),
                  const((SUBLANES, LANES)), const((SUBLANES, 3 * GDN_WIDTH))],
        out_specs=out_specs,
        out_shape=out_shape,
        scratch_shapes=[pltpu.VMEM((SUBLANES, LANES), F32), pltpu.VMEM((SUBLANES, 3 * GDN_WIDTH), F32)],
        compiler_params=pltpu.CompilerParams(dimension_semantics=("arbitrary",), vmem_limit_bytes=48 * MIB),
        name="proj_prompt",
    )(x, nrm, wbig, wsm, par, cw)


def _proj_sample_body(x_ref, nrm_ref, wbig_ref, wsm_ref, par_ref, cw_ref, c0_ref, c1_ref, c2_ref,
                      fk_ref, fv_ref, q_ref, gqkv_ref, gz_ref, sm_ref, pre_ref):
    h16 = _rms(x_ref[...], nrm_ref[...]).astype(BF16)
    fox = _dot(h16, wbig_ref[:, 0:3 * FOX_WIDTH])
    q_ref[...] = fox[:, :FOX_WIDTH] * FOX_SCALE
    fk_ref[...] = fox[:, FOX_WIDTH:2 * FOX_WIDTH]
    fv_ref[...] = fox[:, 2 * FOX_WIDTH:]
    sm_ref[...] = _gate_activations(_dot(h16, wsm_ref[...]), par_ref)
    pre = _dot(h16, wbig_ref[:, 3 * FOX_WIDTH:3 * FOX_WIDTH + 3 * GDN_WIDTH])
    pre_ref[...] = pre
    acc = (pre * cw_ref[3:4, :] + c2_ref[...] * cw_ref[2:3, :]
           + c1_ref[...] * cw_ref[1:2, :] + c0_ref[...] * cw_ref[0:1, :])
    _gdn_qkv_norm(_silu(acc), gqkv_ref)
    gz_ref[...] = _dot(h16, wbig_ref[:, 3 * FOX_WIDTH + 3 * GDN_WIDTH:])


def _proj_sample(x, nrm, wbig, wsm, par, cw, c0, c1, c2):
    rows = x.shape[0]
    full = lambda a: pl.BlockSpec(a.shape, lambda i: (0,) * a.ndim)
    args = (x, nrm, wbig, wsm, par, cw, c0, c1, c2)
    shapes = ((rows, FOX_WIDTH), (rows, FOX_WIDTH), (rows, FOX_WIDTH), (rows, 3 * GDN_WIDTH),
              (rows, GDN_WIDTH), (rows, LANES), (rows, 3 * GDN_WIDTH))
    return pl.pallas_call(
        _proj_sample_body,
        grid=(1,),
        in_specs=[full(a) for a in args],
        out_specs=tuple(pl.BlockSpec(s, lambda i: (0, 0)) for s in shapes),
        out_shape=tuple(jax.ShapeDtypeStruct(s, F32) for s in shapes),
        compiler_params=pltpu.CompilerParams(dimension_semantics=("arbitrary",), vmem_limit_bytes=48 * MIB),
        name="proj_sample",
    )(*args)


def _fox_prompt_body(q_ref, k_ref, v_ref, cum_ref, o_ref):
    i = pl.program_id(1)
    t = FOX_TILE
    lane = lax.broadcasted_iota(jnp.int32, (t, LANES), 1)
    rr = lax.broadcasted_iota(jnp.int32, (t, t), 0)
    cc = lax.broadcasted_iota(jnp.int32, (t, t), 1)
    causal = cc <= rr
    qoff = pl.multiple_of(i * t, t)
    qs, crefs = [], []
    for h in range(FOX_HEADS):
        q2 = q_ref[:, (h // 2) * LANES:(h // 2 + 1) * LANES]
        keep = (lane < FOX_HEAD_DIM) if h % 2 == 0 else (lane >= FOX_HEAD_DIM)
        qs.append(jnp.where(keep, q2, jnp.zeros_like(q2)))
        crefs.append(cum_ref[h:h + 1, pl.ds(qoff, LANES)][:, 0:1])

    def tile(j, carry, masked):
        koff = pl.multiple_of(j * t, t)
        out = []
        for h in range(FOX_HEADS):
            ls = slice((h // 2) * LANES, (h // 2 + 1) * LANES)
            m, l, a = carry[h]
            s = _dot_nt(qs[h], k_ref[pl.ds(koff, t), ls]) + (crefs[h] - cum_ref[h:h + 1, pl.ds(koff, t)])
            if masked:
                s = jnp.where(causal, s, NEG_BIG)
            m_new = jnp.maximum(m, jnp.max(s, axis=-1, keepdims=True))
            alpha = jnp.exp(m - m_new)
            pm = jnp.exp(s - m_new)
            l_new = alpha * l + jnp.sum(pm, axis=-1, keepdims=True)
            a_new = alpha * a + _dot(pm.astype(BF16), v_ref[pl.ds(koff, t), ls])
            out.append((m_new, l_new, a_new))
        return tuple(out)

    init = tuple((jnp.full((t, 1), NEG_BIG, F32), jnp.zeros((t, 1), F32), jnp.zeros((t, LANES), F32))
                 for _ in range(FOX_HEADS))
    carry = lax.fori_loop(0, i, lambda j, c: tile(j, c, False), init)
    final = tile(i, carry, True)
    for p in range(FOX_HEADS // 2):
        (_, l0, a0), (_, l1, a1) = final[2 * p], final[2 * p + 1]
        o_ref[:, p * LANES:(p + 1) * LANES] = jnp.where(lane < FOX_HEAD_DIM, a0 / l0, a1 / l1).astype(BF16)


def _fox_prompt(q16, k16, v16, smt, n_seq, seq_len):
    t = FOX_TILE
    nq = seq_len // t
    cum_block = SM_CUM // SUBLANES
    return pl.pallas_call(
        _fox_prompt_body,
        grid=(n_seq, nq),
        in_specs=[pl.BlockSpec((t, FOX_WIDTH), lambda b, i: (b * nq + i, 0)),
                  pl.BlockSpec((seq_len, FOX_WIDTH), lambda b, i: (b, 0)),
                  pl.BlockSpec((seq_len, FOX_WIDTH), lambda b, i: (b, 0)),
                  pl.BlockSpec((SUBLANES, seq_len), lambda b, i: (cum_block, b))],
        out_specs=pl.BlockSpec((t, FOX_WIDTH), lambda b, i: (b * nq + i, 0)),
        out_shape=jax.ShapeDtypeStruct(q16.shape, BF16),
        compiler_params=pltpu.CompilerParams(dimension_semantics=("arbitrary", "arbitrary"),
                                             vmem_limit_bytes=40 * MIB),
        name="fox_prompt",
    )(q16, k16, v16, smt)


def _bdot(a, b):
    return lax.dot_general(a.astype(BF16), b.astype(BF16), (((2,), (1,)), ((0,), (0,))),
                           preferred_element_type=F32)


def _bdot_nt(a, b):
    return lax.dot_general(a.astype(BF16), b.astype(BF16), (((2,), (2,)), ((0,), (0,))),
                           preferred_element_type=F32)


def _unit_lower_inverse_minus_eye(a, r, c):
    blk16 = (r // 16) == (c // 16)
    blk32 = (r // 32) == (c // 32)
    p = jnp.where(blk16, -a, 0.0)
    dt = p
    for _ in range(3):
        p = _bdot(p, p)
        dt = dt + p + _bdot(dt, p)
    for off in (jnp.where(jnp.logical_and(blk32, jnp.logical_not(blk16)), a, 0.0),
                jnp.where(blk32, 0.0, a)):
        x = off + _bdot(dt, off)
        dt = dt - (x + _bdot(x, dt))
    return dt


def _gdn_prompt_body(qkv_ref, sm_ref, gcr_ref, z_ref, gn_ref, o_ref, sout_ref, s_ref):
    ci = pl.program_id(1)

    @pl.when(ci == 0)
    def _():
        s_ref[...] = jnp.zeros_like(s_ref)

    ch = GDN_CHUNK
    d = GDN_HEAD_DIM
    nh = GDN_HEADS
    n_chunks = qkv_ref.shape[0] // ch
    units = [(cidx, hh) for cidx in range(n_chunks) for hh in range(nh)]

    def gather(fn):
        return jnp.stack([fn(slice(cidx * ch, (cidx + 1) * ch), hh) for cidx, hh in units], axis=0)

    q = gather(lambda rows, hh: qkv_ref[rows, hh * d:(hh + 1) * d])
    k = gather(lambda rows, hh: qkv_ref[rows, GDN_WIDTH + hh * d:GDN_WIDTH + (hh + 1) * d])
    v = gather(lambda rows, hh: qkv_ref[rows, 2 * GDN_WIDTH + hh * d:2 * GDN_WIDTH + (hh + 1) * d])
    beta = gather(lambda rows, hh: sm_ref[rows, SM_BETA + hh:SM_BETA + hh + 1])
    gc = gather(lambda rows, hh: sm_ref[rows, SM_GCUM + hh:SM_GCUM + hh + 1])
    gr = gather(lambda rows, hh: gcr_ref[hh:hh + 1, rows])

    r = lax.broadcasted_iota(jnp.int32, (1, ch, ch), 1)
    c = lax.broadcasted_iota(jnp.int32, (1, ch, ch), 2)
    lower = r >= c
    decay = jnp.where(lower, jnp.exp(jnp.where(lower, gc - gr, 0.0)), 0.0)
    qk_kk = _bdot_nt(jnp.concatenate([q, k], axis=1), k)
    qk = qk_kk[:, :ch] * decay
    a = jnp.where(r > c, beta * qk_kk[:, ch:] * decay, 0.0)
    dt = _unit_lower_inverse_minus_eye(a, r, c)
    eg = jnp.exp(gc)
    rhs = jnp.concatenate([v * beta, k * (beta * eg)], axis=-1)
    uw = rhs + _bdot(dt, rhs)
    gl = gc[:, ch - 1:ch, :]
    qd = (q * eg).astype(BF16)
    kd = k * jnp.exp(gl - gc)
    g_last = jnp.exp(gl)
    qk16 = qk.astype(BF16)

    s = s_ref[...]
    for cidx in range(n_chunks):
        us = slice(cidx * nh, (cidx + 1) * nh)
        rows = slice(cidx * ch, (cidx + 1) * ch)
        s16 = s.astype(BF16)
        v_new = uw[us, :, :d] - _bdot(uw[us, :, d:], s16)
        o = _bdot(qd[us], s16) + _bdot(qk16[us], v_new)
        vn16 = v_new.astype(BF16)
        upd = jnp.stack([_dot(kd[cidx * nh + hh].T.astype(BF16), vn16[hh]) for hh in range(nh)], axis=0)
        s = s * g_last[us] + upd
        for hh in range(nh):
            hs = slice(hh * d, (hh + 1) * d)
            o_ref[rows, hs] = (_rms(o[hh], gn_ref[...]) * _silu(z_ref[rows, hs])).astype(BF16)
    s_ref[...] = s

    @pl.when(ci == pl.num_programs(1) - 1)
    def _():
        sout_ref[0] = s


def _gdn_prompt(gqkv, sm, smt, gz, gn, n_seq, seq_len):
    tm = ROW_TILE
    nt = seq_len // tm
    gcum_block = SM_GCUM // SUBLANES
    rows = gqkv.shape[0]
    return pl.pallas_call(
        _gdn_prompt_body,
        grid=(n_seq, nt),
        in_specs=[pl.BlockSpec((tm, 3 * GDN_WIDTH), lambda b, i: (b * nt + i, 0)),
                  pl.BlockSpec((tm, LANES), lambda b, i: (b * nt + i, 0)),
                  pl.BlockSpec((SUBLANES, tm), lambda b, i: (gcum_block, b * nt + i)),
                  pl.BlockSpec((tm, GDN_WIDTH), lambda b, i: (b * nt + i, 0)),
                  pl.BlockSpec((1, GDN_HEAD_DIM), lambda b, i: (0, 0))],
        out_specs=(pl.BlockSpec((tm, GDN_WIDTH), lambda b, i: (b * nt + i, 0)),
                   pl.BlockSpec((1, GDN_HEADS, GDN_HEAD_DIM, GDN_HEAD_DIM), lambda b, i: (b, 0, 0, 0))),
        out_shape=(jax.ShapeDtypeStruct((rows, GDN_WIDTH), BF16),
                   jax.ShapeDtypeStruct((n_seq, GDN_HEADS, GDN_HEAD_DIM, GDN_HEAD_DIM), F32)),
        scratch_shapes=[pltpu.VMEM((GDN_HEADS, GDN_HEAD_DIM, GDN_HEAD_DIM), F32)],
        compiler_params=pltpu.CompilerParams(dimension_semantics=("arbitrary", "arbitrary"),
                                             vmem_limit_bytes=32 * MIB),
        name="gdn_prompt",
    )(gqkv, sm, smt, gz, gn)


def _ffn_tail(x, fox_ref, gdn_ref, wo_ref, nf_ref, wup_ref, wd_ref, nfin_ref, conv):
    mix = _dot(fox_ref[...], wo_ref[0:FOX_WIDTH, :]) + _dot(gdn_ref[...], wo_ref[FOX_WIDTH:, :])
    x2 = x + mix
    h2 = _rms(x2, nf_ref[...]).astype(BF16)
    gu = _dot(h2, wup_ref[...])
    gate, up = gu[:, :FFN_DIM], gu[:, FFN_DIM:]
    act = (_silu(conv(gate)) * up).astype(BF16)
    x3 = x2 + _dot(act, wd_ref[...])
    return _rms(x3, nfin_ref[...])


def _ffn_prompt_body(x_ref, fox_ref, gdn_ref, wo_ref, nf_ref, wup_ref, cw_ref, cb_ref, wd_ref, nfin_ref,
                     y_ref, cst_ref, prev_ref, *, tiles_per_seq):
    i = pl.program_id(0)

    @pl.when(i % tiles_per_seq == 0)
    def _():
        prev_ref[...] = jnp.zeros_like(prev_ref)

    tm = x_ref.shape[0]

    def conv(gate):
        prev = prev_ref[...]
        out = gate * cw_ref[FFN_CONV - 1:FFN_CONV, :] + cb_ref[...]
        for kk in range(1, FFN_CONV):
            out = out + _shift_rows(gate, prev, kk) * cw_ref[FFN_CONV - 1 - kk:FFN_CONV - kk, :]
        prev_ref[...] = gate[tm - SUBLANES:, :]
        cst_ref[...] = gate[tm - SUBLANES:, :]
        return out

    y_ref[...] = _ffn_tail(x_ref[...], fox_ref, gdn_ref, wo_ref, nf_ref, wup_ref, wd_ref, nfin_ref, conv)


def _ffn_prompt(x, fox, gdn, wo, nf, wup, cw, cb, wd, nfin, seq_len):
    rows = x.shape[0]
    tm = ROW_TILE
    n_seq = rows // seq_len
    tiles_per_seq = seq_len // tm
    row_spec = lambda w: pl.BlockSpec((tm, w), lambda i: (i, 0))
    const = lambda a: pl.BlockSpec(a.shape, lambda i: (0, 0), pipeline_mode=pl.Buffered(1))
    return pl.pallas_call(
        functools.partial(_ffn_prompt_body, tiles_per_seq=tiles_per_seq),
        grid=(rows // tm,),
        in_specs=[row_spec(D_MODEL), row_spec(FOX_WIDTH), row_spec(GDN_WIDTH), const(wo), const(nf),
                  const(wup), const(cw), const(cb), const(wd), const(nfin)],
        out_specs=(row_spec(D_MODEL),
                   pl.BlockSpec((SUBLANES, FFN_DIM), lambda i: (i // tiles_per_seq, 0))),
        out_shape=(jax.ShapeDtypeStruct((rows, D_MODEL), F32),
                   jax.ShapeDtypeStruct((n_seq * SUBLANES, FFN_DIM), F32)),
        scratch_shapes=[pltpu.VMEM((SUBLANES, FFN_DIM), F32)],
        compiler_params=pltpu.CompilerParams(dimension_semantics=("arbitrary",), vmem_limit_bytes=56 * MIB),
        name="ffn_prompt",
    )(x, fox, gdn, wo, nf, wup, cw, cb, wd, nfin)


def _ffn_sample_body(x_ref, fox_ref, gdn_ref, wo_ref, nf_ref, wup_ref, cw_ref, cb_ref, wd_ref, nfin_ref,
                     c0_ref, c1_ref, y_ref, gate_ref):
    def conv(gate):
        gate_ref[...] = gate
        return (gate * cw_ref[2:3, :] + c1_ref[...] * cw_ref[1:2, :] + c0_ref[...] * cw_ref[0:1, :]
                + cb_ref[...])

    y_ref[...] = _ffn_tail(x_ref[...], fox_ref, gdn_ref, wo_ref, nf_ref, wup_ref, wd_ref, nfin_ref, conv)


def _ffn_sample(x, fox, gdn, wo, nf, wup, cw, cb, wd, nfin, c0, c1):
    rows = x.shape[0]
    args = (x, fox, gdn, wo, nf, wup, cw, cb, wd, nfin, c0, c1)
    full = lambda a: pl.BlockSpec(a.shape, lambda i: (0, 0), pipeline_mode=pl.Buffered(1))
    return pl.pallas_call(
        _ffn_sample_body,
        grid=(1,),
        in_specs=[full(a) for a in args],
        out_specs=(pl.BlockSpec((rows, D_MODEL), lambda i: (0, 0)),
                   pl.BlockSpec((rows, FFN_DIM), lambda i: (0, 0))),
        out_shape=(jax.ShapeDtypeStruct((rows, D_MODEL), F32), jax.ShapeDtypeStruct((rows, FFN_DIM), F32)),
        compiler_params=pltpu.CompilerParams(dimension_semantics=("arbitrary",), vmem_limit_bytes=56 * MIB),
        name="ffn_sample",
    )(*args)


def _logf_prefix_body(x_ref, m1_ref, m2_ref, w_ref, t_ref):
    x = x_ref[...]
    hi = x.astype(BF16)
    r1 = x - hi.astype(F32)
    mid = r1.astype(BF16)
    lo = (r1 - mid.astype(F32)).astype(BF16)
    m1 = m1_ref[...]
    m2 = m2_ref[...]
    w_ref[...] = _dot(hi, m1) + _dot(mid, m1) + _dot(lo, m1)
    t_ref[...] = _dot(hi, m2) + _dot(mid, m2) + _dot(lo, m2)


def _logf_prefix(logf2d, page_size):
    n_pool, width = logf2d.shape
    tm = ROW_TILE if n_pool % ROW_TILE == 0 else n_pool
    src = jnp.arange(width)
    dst = jnp.arange(width)
    src_t, src_h = src // FOX_HEADS, src % FOX_HEADS
    dst_h, dst_t = dst // page_size, dst % page_size
    m1 = ((src_h[:, None] == dst_h[None, :]) & (src_t[:, None] > dst_t[None, :])).astype(BF16)
    m2 = (src_h[:, None] == jnp.arange(LANES)[None, :]).astype(BF16)
    return pl.pallas_call(
        _logf_prefix_body,
        grid=(n_pool // tm,),
        in_specs=[pl.BlockSpec((tm, width), lambda i: (i, 0)),
                  pl.BlockSpec((width, width), lambda i: (0, 0)),
                  pl.BlockSpec((width, LANES), lambda i: (0, 0))],
        out_specs=(pl.BlockSpec((tm, width), lambda i: (i, 0)), pl.BlockSpec((tm, LANES), lambda i: (i, 0))),
        out_shape=(jax.ShapeDtypeStruct((n_pool, width), F32), jax.ShapeDtypeStruct((n_pool, LANES), F32)),
        compiler_params=pltpu.CompilerParams(dimension_semantics=("arbitrary",), vmem_limit_bytes=32 * MIB),
        name="logf_prefix",
    )(logf2d, m1, m2)


def _decode_body(pt_ref, q_ref, kn_ref, vn_ref, sm_ref, seg_ref, exp_ref, kpool, vpool, wpool, tpool,
                 o_ref, kbuf, vbuf, wbuf, tbuf, st_ref, sems, *, n_pages, page):
    b = pl.program_id(0)
    nb = pl.num_programs(0)
    past = n_pages * page
    slot = b % 2

    def copies(bb, sl):
        out = []
        for j in range(n_pages):
            pg = pt_ref[bb * n_pages + j]
            out.append(pltpu.make_async_copy(kpool.at[pg], kbuf.at[sl, pl.ds(j * page, page), :], sems.at[sl, 0]))
            out.append(pltpu.make_async_copy(vpool.at[pg], vbuf.at[sl, pl.ds(j * page, page), :], sems.at[sl, 1]))
            out.append(pltpu.make_async_copy(wpool.at[pg], wbuf.at[sl, :, pl.ds(j * page, page)], sems.at[sl, 2]))
            out.append(pltpu.make_async_copy(tpool.at[pg], tbuf.at[sl, pl.ds(j, 1), :], sems.at[sl, 3]))
        return out

    @pl.when(b == 0)
    def _():
        for cp in copies(0, 0):
            cp.start()
        for sl in range(2):
            kbuf[sl, pl.ds(past, LANES), :] = jnp.zeros((LANES, FOX_WIDTH), F32)
            vbuf[sl, pl.ds(past, LANES), :] = jnp.zeros((LANES, FOX_WIDTH), F32)

    @pl.when(b + 1 < nb)
    def _():
        for cp in copies(b + 1, 1 - slot):
            cp.start()

    for cp in copies(b, slot):
        cp.wait()

    kb = kbuf.at[slot]
    vb = vbuf.at[slot]
    kb[pl.ds(past, 1), :] = kn_ref[0]
    vb[pl.ds(past, 1), :] = vn_ref[0]
    qrow = q_ref[0]
    n_blk = n_pages + 1
    seg = seg_ref[...]
    for j in range(n_blk):
        blk = pl.ds(j * page, page)
        sc = _dot((kb[blk, :] * qrow).astype(BF16), seg)
        st_ref[:, blk] = sc.T[0:FOX_HEADS, :]

    r8 = lax.broadcasted_iota(jnp.int32, (SUBLANES, LANES), 0)
    c8 = lax.broadcasted_iota(jnp.int32, (SUBLANES, LANES), 1)
    lf_new = jnp.sum(jnp.where(r8 == c8, sm_ref[0], 0.0), axis=1, keepdims=True)
    tot = tbuf[slot]
    rj = lax.broadcasted_iota(jnp.int32, tot.shape, 0)
    suf = tot
    k = 1
    while k < n_pages:
        suf = suf + jnp.where(rj + k < n_pages, pltpu.roll(suf, n_pages - k, 0), 0.0)
        k *= 2
    suf = suf - tot
    suf_t = jnp.concatenate([suf, jnp.zeros((LANES - n_pages, LANES), F32)], axis=0).T[0:FOX_HEADS, :]
    lane_p = lax.broadcasted_iota(jnp.int32, (FOX_HEADS, page), 1)
    logits = []
    for j in range(n_blk):
        blk = pl.ds(j * page, page)
        if j < n_pages:
            bias = wbuf[slot, :, blk] + suf_t[:, j:j + 1] + lf_new
        else:
            bias = jnp.where(lane_p == 0, 0.0, NEG_BIG)
        logits.append(st_ref[:, blk] + bias)
    m = functools.reduce(jnp.maximum, [jnp.max(x, axis=-1, keepdims=True) for x in logits])
    ps = [jnp.exp(x - m) for x in logits]
    inv_l = 1.0 / functools.reduce(lambda a, c: a + c, [jnp.sum(x, axis=-1, keepdims=True) for x in ps])
    expand = exp_ref[...]
    acc = jnp.zeros((SUBLANES, FOX_WIDTH), F32)
    pad = jnp.zeros((LANES - FOX_HEADS, page), F32)
    for j in range(n_blk):
        blk = pl.ds(j * page, page)
        pt = jnp.concatenate([ps[j] * inv_l, pad], axis=0).T
        pv = _dot(pt.astype(BF16), expand) * vb[blk, :]
        acc = acc + jnp.sum(pv.reshape(page // SUBLANES, SUBLANES, FOX_WIDTH), axis=0)
    o_ref[0] = jnp.sum(acc, axis=0, keepdims=True)


def _decode_attention(page_table, q, kn, vn, sm, kpool, vpool, wpool, tpool):
    nb, n_pages = page_table.shape
    page = kpool.shape[1]
    past = n_pages * page
    lane = jnp.arange(LANES)
    col = jnp.arange(FOX_WIDTH)
    seg = (col[:, None] // FOX_HEAD_DIM == lane[None, :]).astype(BF16)
    expand = (lane[:, None] == col[None, :] // FOX_HEAD_DIM).astype(BF16)
    full = lambda a: pl.BlockSpec(a.shape, lambda i, pt: (0,) * a.ndim)
    row = lambda a: pl.BlockSpec((1, 1, a.shape[-1]), lambda i, pt: (i, 0, 0))
    hbm = pl.BlockSpec(memory_space=pl.ANY)
    q, kn, vn, sm = (a.reshape(nb, 1, a.shape[-1]) for a in (q, kn, vn, sm))
    grid_spec = pltpu.PrefetchScalarGridSpec(
        num_scalar_prefetch=1,
        grid=(nb,),
        in_specs=[row(q), row(kn), row(vn), row(sm), full(seg), full(expand), hbm, hbm, hbm, hbm],
        out_specs=pl.BlockSpec((1, 1, FOX_WIDTH), lambda i, pt: (i, 0, 0)),
        scratch_shapes=[pltpu.VMEM((2, past + LANES, FOX_WIDTH), F32),
                        pltpu.VMEM((2, past + LANES, FOX_WIDTH), F32),
                        pltpu.VMEM((2, FOX_HEADS, past), F32),
                        pltpu.VMEM((2, n_pages, LANES), F32),
                        pltpu.VMEM((FOX_HEADS, past + LANES), F32),
                        pltpu.SemaphoreType.DMA((2, 4))],
    )
    out = pl.pallas_call(
        functools.partial(_decode_body, n_pages=n_pages, page=page),
        grid_spec=grid_spec,
        out_shape=jax.ShapeDtypeStruct((nb, 1, FOX_WIDTH), F32),
        compiler_params=pltpu.CompilerParams(dimension_semantics=("arbitrary",), vmem_limit_bytes=40 * MIB),
        name="fox_decode",
    )(page_table.reshape(-1), q, kn, vn, sm, seg, expand, kpool, vpool, wpool, tpool)
    return out.reshape(nb, FOX_WIDTH)


def _gdn_sample_body(s_ref, qkv_ref, sm_ref, z_ref, gn_ref, sout_ref, o_ref):
    d = GDN_HEAD_DIM
    eye = lax.broadcasted_iota(jnp.int32, (d, d), 0) == lax.broadcasted_iota(jnp.int32, (d, d), 1)

    def col(rows):
        return jnp.sum(jnp.where(eye[None], rows[:, None, :], 0.0), axis=2, keepdims=True)

    for hh in range(GDN_HEADS):
        hs = slice(hh * d, (hh + 1) * d)
        q = qkv_ref[:, hs]
        k = qkv_ref[:, GDN_WIDTH + hh * d:GDN_WIDTH + (hh + 1) * d]
        v = qkv_ref[:, 2 * GDN_WIDTH + hh * d:2 * GDN_WIDTH + (hh + 1) * d]
        eg = jnp.exp(sm_ref[:, SM_G + hh:SM_G + hh + 1])
        beta = sm_ref[:, SM_BETA + hh:SM_BETA + hh + 1]
        s = s_ref[:, hh]
        v_new = v * beta - jnp.sum(col(k * (beta * eg)) * s, axis=1)
        o = jnp.sum(col(q * eg) * s, axis=1) + jnp.sum(q * k, axis=1, keepdims=True) * v_new
        sout_ref[:, hh] = s * eg[:, :, None] + col(k) * v_new[:, None, :]
        o_ref[:, hs] = _rms(o, gn_ref[...]) * _silu(z_ref[:, hs])


def _gdn_sample(state, gqkv, sm, gz, gn):
    nb = state.shape[0]
    bb = SUBLANES
    d = GDN_HEAD_DIM
    row_spec = lambda w: pl.BlockSpec((bb, w), lambda i: (i, 0))
    st_spec = pl.BlockSpec((bb, GDN_HEADS, d, d), lambda i: (i, 0, 0, 0))
    return pl.pallas_call(
        _gdn_sample_body,
        grid=(nb // bb,),
        in_specs=[st_spec, row_spec(3 * GDN_WIDTH), row_spec(LANES), row_spec(GDN_WIDTH),
                  pl.BlockSpec((1, d), lambda i: (0, 0))],
        out_specs=(st_spec, row_spec(GDN_WIDTH)),
        out_shape=(jax.ShapeDtypeStruct(state.shape, F32), jax.ShapeDtypeStruct((nb, GDN_WIDTH), F32)),
        compiler_params=pltpu.CompilerParams(dimension_semantics=("arbitrary",), vmem_limit_bytes=32 * MIB),
        name="gdn_sample",
    )(state, gqkv, sm, gz, gn)


def _pad_rows(a, rows):
    return jnp.concatenate([a, jnp.zeros((rows - a.shape[0],) + a.shape[1:], a.dtype)], axis=0)


def kernel(x_prompt, x_sample, cache_k, cache_v, cache_logf, state_gdn, state_gdn_conv, state_ffn_conv,
           page_table, norm_mix, w_in, b_forget, gdn_a_log, gdn_dt_bias, w_gdn_conv, gdn_out_norm, w_out,
           norm_ffn, w_up, w_ffn_conv, b_ffn_conv, w_down, norm_final):
    assert w_in.shape[0] == 1, "single-layer trunk"
    n_seq, seq_len, _ = x_prompt.shape
    nb = x_sample.shape[0]
    n_pool, page = cache_k.shape[1], cache_k.shape[2]

    w = w_in[0]
    o_ff = 3 * FOX_WIDTH
    o_g = o_ff + FOX_HEADS
    o_ga = o_g + 3 * GDN_WIDTH
    o_gz = o_ga + 2 * GDN_HEADS
    wbig = jnp.concatenate([w[:, :o_ff], w[:, o_g:o_ga], w[:, o_gz:]], axis=1).astype(BF16)
    wsm = jnp.concatenate([w[:, o_ff:o_g], w[:, o_ga:o_gz],
                           jnp.zeros((D_MODEL, LANES - FOX_HEADS - 2 * GDN_HEADS), F32)], axis=1).astype(BF16)
    par = jnp.zeros((SUBLANES, LANES), F32)
    par = par.at[0, SM_LF:SM_LF + FOX_HEADS].set(b_forget[0])
    par = par.at[0, SM_G:SM_G + GDN_HEADS].set(gdn_dt_bias[0])
    par = par.at[1, SM_G:SM_G + GDN_HEADS].set(gdn_a_log[0])
    nrm = norm_mix[0][None, :]
    cw = _pad_rows(w_gdn_conv[0], SUBLANES)
    gn = gdn_out_norm[0][None, :]
    wo = w_out[0].astype(BF16)
    nf = norm_ffn[0][None, :]
    wup = w_up[0].astype(BF16)
    cwf = _pad_rows(w_ffn_conv[0], SUBLANES)
    cbf = b_ffn_conv[0][None, :]
    wd = w_down[0].astype(BF16)
    nfin = norm_final[None, :]

    xp = x_prompt.reshape(n_seq * seq_len, D_MODEL)
    fk, fv, q16, k16, v16, gqkv, gz, sm, smt, cst = _proj_prompt(xp, nrm, wbig, wsm, par, cw, seq_len)
    fox = _fox_prompt(q16, k16, v16, smt, n_seq, seq_len)
    gdn, s_p = _gdn_prompt(gqkv, sm, smt, gz, gn, n_seq, seq_len)
    yp, cstf = _ffn_prompt(xp, fox, gdn, wo, nf, wup, cwf, cbf, wd, nfin, seq_len)

    xs = x_sample.reshape(nb, D_MODEL)
    gctx = state_gdn_conv[0]
    fctx = state_ffn_conv[0]
    fk_s, fv_s, q_s, gqkv_s, gz_s, sm_s, pre_s = _proj_sample(
        xs, nrm, wbig, wsm, par, cw, gctx[:, 0], gctx[:, 1], gctx[:, 2])
    wexc, tot = _logf_prefix(cache_logf[0].reshape(n_pool, page * FOX_HEADS), page)
    fox_s = _decode_attention(page_table, q_s, fk_s, fv_s, sm_s,
                              cache_k[0].reshape(n_pool, page, FOX_WIDTH),
                              cache_v[0].reshape(n_pool, page, FOX_WIDTH),
                              wexc.reshape(n_pool, FOX_HEADS, page), tot.reshape(n_pool, 1, LANES))
    s_s, gdn_s = _gdn_sample(state_gdn[0], gqkv_s, sm_s, gz_s, gn)
    ys, gate_s = _ffn_sample(xs, fox_s.astype(BF16), gdn_s.astype(BF16), wo, nf, wup, cwf, cbf, wd, nfin,
                             fctx[:, 0], fctx[:, 1])

    kv_shape_p = (1, n_seq, seq_len, FOX_HEADS, FOX_HEAD_DIM)
    kv_shape_s = (1, nb, 1, FOX_HEADS, FOX_HEAD_DIM)
    return (
        yp.reshape(n_seq, seq_len, D_MODEL),
        ys.reshape(nb, 1, D_MODEL),
        fk.reshape(kv_shape_p),
        fv.reshape(kv_shape_p),
        sm[:, SM_LF:SM_LF + FOX_HEADS].reshape(1, n_seq, seq_len, FOX_HEADS),
        s_p[None],
        cst.reshape(n_seq, SUBLANES, 3 * GDN_WIDTH)[None, :, SUBLANES - (GDN_CONV - 1):],
        cstf.reshape(n_seq, SUBLANES, FFN_DIM)[None, :, SUBLANES - (FFN_CONV - 1):],
        fk_s.reshape(kv_shape_s),
        fv_s.reshape(kv_shape_s),
        sm_s[:, SM_LF:SM_LF + FOX_HEADS].reshape(1, nb, 1, FOX_HEADS),
        s_s[None],
        jnp.concatenate([gctx[:, 1:], pre_s[:, None, :]], axis=1)[None],
        jnp.concatenate([fctx[:, 1:], gate_s[:, None, :]], axis=1)[None],
    )
```

```python
import functools

import jax
import jax.numpy as jnp
from jax import lax
from jax.experimental import pallas as pl
from jax.experimental.pallas import tpu as pltpu

D_MODEL = 1024
FOX_HEADS = 8
FOX_HEAD_DIM = 64
FOX_WIDTH = FOX_HEADS * FOX_HEAD_DIM
GDN_HEADS = 4
GDN_HEAD_DIM = 128
GDN_WIDTH = GDN_HEADS * GDN_HEAD_DIM
GDN_CONV = 4
GDN_CHUNK = 64
FFN_DIM = 2816
FFN_CONV = 3
EPS = 1e-6
NEG_BIG = -1e30
FOX_SCALE = FOX_HEAD_DIM ** -0.5
GDN_SCALE = GDN_HEAD_DIM ** -0.5

LANES = 128
SUBLANES = 8
ROW_TILE = 256
FOX_TILE = 256
MIB = 1024 * 1024

SM_LF = 0
SM_G = 8
SM_BETA = 12
SM_CUM = 16
SM_GCUM = 24

F32 = jnp.float32
BF16 = jnp.bfloat16


def _sigmoid(x):
    return 1.0 / (1.0 + jnp.exp(-x))


def _silu(x):
    return x * _sigmoid(x)


def _rms(x, g):
    return x * lax.rsqrt(jnp.mean(x * x, axis=-1, keepdims=True) + EPS) * g


def _dot(a, b):
    return jnp.dot(a, b, preferred_element_type=F32)


def _dot_nt(a, b):
    return lax.dot_general(a, b, (((1,), (1,)), ((), ())), preferred_element_type=F32)


def _hdot(a, b):
    return jnp.dot(a, b, preferred_element_type=F32, precision=lax.Precision.HIGHEST)


def _shift_rows(x, prev8, k):
    r = pltpu.roll(x, k, 0)
    row8 = lax.broadcasted_iota(jnp.int32, prev8.shape, 0)
    top = jnp.where(row8 < k, pltpu.roll(prev8, k, 0), r[0:SUBLANES])
    return jnp.concatenate([top, r[SUBLANES:]], axis=0)


def _gate_activations(raw, par_ref):
    z = raw + par_ref[0:1, :]
    lane = lax.broadcasted_iota(jnp.int32, z.shape, 1)
    t = jnp.log1p(jnp.exp(-jnp.abs(z)))
    lf = jnp.minimum(z, 0.0) - t
    softplus = jnp.maximum(z, 0.0) + t
    g = -jnp.exp(par_ref[1:2, :]) * softplus
    beta = _sigmoid(z)
    return jnp.where(lane < SM_G, lf, jnp.where(lane < SM_BETA, g, jnp.where(lane < SM_CUM, beta, 0.0)))


def _gdn_qkv_norm(c, out_ref):
    for part, scale in ((0, GDN_SCALE), (1, 1.0)):
        for hh in range(GDN_HEADS):
            off = part * GDN_WIDTH + hh * GDN_HEAD_DIM
            seg = c[:, off:off + GDN_HEAD_DIM]
            n = lax.rsqrt(jnp.sum(seg * seg, axis=-1, keepdims=True) + EPS)
            out_ref[:, off:off + GDN_HEAD_DIM] = seg * n * scale
    out_ref[:, 2 * GDN_WIDTH:] = c[:, 2 * GDN_WIDTH:]


def _proj_prompt_body(x_ref, nrm_ref, wbig_ref, wsm_ref, par_ref, cw_ref,
                      fk_ref, fv_ref, q16_ref, k16_ref, v16_ref, gqkv_ref, gz_ref,
                      sm_ref, smt_ref, cst_ref, carry_ref, prev_ref, *, tiles_per_seq):
    i = pl.program_id(0)

    @pl.when(i % tiles_per_seq == 0)
    def _():
        carry_ref[...] = jnp.zeros_like(carry_ref)
        prev_ref[...] = jnp.zeros_like(prev_ref)

    tm = x_ref.shape[0]
    h16 = _rms(x_ref[...], nrm_ref[...]).astype(BF16)

    fox = _dot(h16, wbig_ref[:, 0:3 * FOX_WIDTH])
    fq, fk, fv = fox[:, :FOX_WIDTH], fox[:, FOX_WIDTH:2 * FOX_WIDTH], fox[:, 2 * FOX_WIDTH:]
    q16_ref[...] = (fq * FOX_SCALE).astype(BF16)
    fk_ref[...] = fk
    fv_ref[...] = fv
    k16_ref[...] = fk.astype(BF16)
    v16_ref[...] = fv.astype(BF16)

    act = _gate_activations(_dot(h16, wsm_ref[...]), par_ref)
    row = lax.broadcasted_iota(jnp.int32, act.shape, 0)
    lane = lax.broadcasted_iota(jnp.int32, act.shape, 1)
    y = act
    yc = act
    k = 1
    while k < tm:
        y = y + jnp.where(row >= k, pltpu.roll(y, k, 0), 0.0)
        if k < GDN_CHUNK:
            yc = yc + jnp.where((row & (GDN_CHUNK - 1)) >= k, pltpu.roll(yc, k, 0), 0.0)
        k *= 2
    y = y + carry_ref[0:1, :]
    carry_ref[0:1, :] = y[tm - 1:tm, :]
    shift = SM_CUM - SM_LF
    sm = jnp.where(lane < SM_CUM, act,
                   jnp.where(lane < SM_GCUM, pltpu.roll(y, shift, 1),
                             jnp.where(lane < SM_GCUM + GDN_HEADS, pltpu.roll(yc, shift, 1), 0.0)))
    sm_ref[...] = sm
    smt_ref[...] = sm.T

    pre = _dot(h16, wbig_ref[:, 3 * FOX_WIDTH:3 * FOX_WIDTH + 3 * GDN_WIDTH])
    prev = prev_ref[...]
    acc = pre * cw_ref[GDN_CONV - 1:GDN_CONV, :]
    for kk in range(1, GDN_CONV):
        acc = acc + _shift_rows(pre, prev, kk) * cw_ref[GDN_CONV - 1 - kk:GDN_CONV - kk, :]
    prev_ref[...] = pre[tm - SUBLANES:, :]
    cst_ref[...] = pre[tm - SUBLANES:, :]
    _gdn_qkv_norm(_silu(acc), gqkv_ref)

    gz_ref[...] = _dot(h16, wbig_ref[:, 3 * FOX_WIDTH + 3 * GDN_WIDTH:])


def _proj_prompt(x, nrm, wbig, wsm, par, cw, seq_len):
    rows = x.shape[0]
    tm = ROW_TILE
    n_seq = rows // seq_len
    tiles_per_seq = seq_len // tm
    wb = wbig.shape[1]
    row_spec = lambda w: pl.BlockSpec((tm, w), lambda i: (i, 0))
    const = lambda s: pl.BlockSpec(s, lambda i: (0, 0))
    out_shape = (
        jax.ShapeDtypeStruct((rows, FOX_WIDTH), F32),
        jax.ShapeDtypeStruct((rows, FOX_WIDTH), F32),
        jax.ShapeDtypeStruct((rows, FOX_WIDTH), BF16),
        jax.ShapeDtypeStruct((rows, FOX_WIDTH), BF16),
        jax.ShapeDtypeStruct((rows, FOX_WIDTH), BF16),
        jax.ShapeDtypeStruct((rows, 3 * GDN_WIDTH), F32),
        jax.ShapeDtypeStruct((rows, GDN_WIDTH), F32),
        jax.ShapeDtypeStruct((rows, LANES), F32),
        jax.ShapeDtypeStruct((LANES, rows), F32),
        jax.ShapeDtypeStruct((n_seq * SUBLANES, 3 * GDN_WIDTH), F32),
    )
    out_specs = (
        row_spec(FOX_WIDTH), row_spec(FOX_WIDTH), row_spec(FOX_WIDTH), row_spec(FOX_WIDTH),
        row_spec(FOX_WIDTH), row_spec(3 * GDN_WIDTH), row_spec(GDN_WIDTH), row_spec(LANES),
        pl.BlockSpec((LANES, tm), lambda i: (0, i)),
        pl.BlockSpec((SUBLANES, 3 * GDN_WIDTH), lambda i: (i // tiles_per_seq, 0)),
    )
    return pl.pallas_call(
        functools.partial(_proj_prompt_body, tiles_per_seq=tiles_per_seq),
        grid=(rows // tm,),
        in_specs=[row_spec(D_MODEL), const((1, D_MODEL)), const((D_MODEL, wb)), const((D_MODEL, LANES)),
                  const((SUBLANES, LANES)), const((SUBLANES, 3 * GDN_WIDTH))],
        out_specs=out_specs,
        out_shape=out_shape,
        scratch_shapes=[pltpu.VMEM((SUBLANES, LANES), F32), pltpu.VMEM((SUBLANES, 3 * GDN_WIDTH), F32)],
        compiler_params=pltpu.CompilerParams(dimension_semantics=("arbitrary",), vmem_limit_bytes=48 * MIB),
        name="proj_prompt",
    )(x, nrm, wbig, wsm, par, cw)


def _proj_sample_body(x_ref, nrm_ref, wbig_ref, wsm_ref, par_ref, cw_ref, c0_ref, c1_ref, c2_ref,
                      fk_ref, fv_ref, q_ref, gqkv_ref, gz_ref, sm_ref, pre_ref):
    h16 = _rms(x_ref[...], nrm_ref[...]).astype(BF16)
    fox = _dot(h16, wbig_ref[:, 0:3 * FOX_WIDTH])
    q_ref[...] = fox[:, :FOX_WIDTH] * FOX_SCALE
    fk_ref[...] = fox[:, FOX_WIDTH:2 * FOX_WIDTH]
    fv_ref[...] = fox[:, 2 * FOX_WIDTH:]
    sm_ref[...] = _gate_activations(_dot(h16, wsm_ref[...]), par_ref)
    pre = _dot(h16, wbig_ref[:, 3 * FOX_WIDTH:3 * FOX_WIDTH + 3 * GDN_WIDTH])
    pre_ref[...] = pre
    acc = (pre * cw_ref[3:4, :] + c2_ref[...] * cw_ref[2:3, :]
           + c1_ref[...] * cw_ref[1:2, :] + c0_ref[...] * cw_ref[0:1, :])
    _gdn_qkv_norm(_silu(acc), gqkv_ref)
    gz_ref[...] = _dot(h16, wbig_ref[:, 3 * FOX_WIDTH + 3 * GDN_WIDTH:])


def _proj_sample(x, nrm, wbig, wsm, par, cw, c0, c1, c2):
    rows = x.shape[0]
    full = lambda a: pl.BlockSpec(a.shape, lambda i: (0,) * a.ndim)
    args = (x, nrm, wbig, wsm, par, cw, c0, c1, c2)
    shapes = ((rows, FOX_WIDTH), (rows, FOX_WIDTH), (rows, FOX_WIDTH), (rows, 3 * GDN_WIDTH),
              (rows, GDN_WIDTH), (rows, LANES), (rows, 3 * GDN_WIDTH))
    return pl.pallas_call(
        _proj_sample_body,
        grid=(1,),
        in_specs=[full(a) for a in args],
        out_specs=tuple(pl.BlockSpec(s, lambda i: (0, 0)) for s in shapes),
        out_shape=tuple(jax.ShapeDtypeStruct(s, F32) for s in shapes),
        compiler_params=pltpu.CompilerParams(dimension_semantics=("arbitrary",), vmem_limit_bytes=48 * MIB),
        name="proj_sample",
    )(*args)


def _fox_prompt_body(q_ref, k_ref, v_ref, cum_ref, o_ref):
    i = pl.program_id(1)
    t = FOX_TILE
    lane = lax.broadcasted_iota(jnp.int32, (t, LANES), 1)
    rr = lax.broadcasted_iota(jnp.int32, (t, t), 0)
    cc = lax.broadcasted_iota(jnp.int32, (t, t), 1)
    causal = cc <= rr
    qoff = pl.multiple_of(i * t, t)
    qs, crefs = [], []
    for h in range(FOX_HEADS):
        q2 = q_ref[:, (h // 2) * LANES:(h // 2 + 1) * LANES]
        keep = (lane < FOX_HEAD_DIM) if h % 2 == 0 else (lane >= FOX_HEAD_DIM)
        qs.append(jnp.where(keep, q2, jnp.zeros_like(q2)))
        crefs.append(cum_ref[h:h + 1, pl.ds(qoff, LANES)][:, 0:1])

    def tile(j, carry, masked):
        koff = pl.multiple_of(j * t, t)
        out = []
        for h in range(FOX_HEADS):
            ls = slice((h // 2) * LANES, (h // 2 + 1) * LANES)
            m, l, a = carry[h]
            s = _dot_nt(qs[h], k_ref[pl.ds(koff, t), ls]) + (crefs[h] - cum_ref[h:h + 1, pl.ds(koff, t)])
            if masked:
                s = jnp.where(causal, s, NEG_BIG)
            m_new = jnp.maximum(m, jnp.max(s, axis=-1, keepdims=True))
            alpha = jnp.exp(m - m_new)
            pm = jnp.exp(s - m_new)
            l_new = alpha * l + jnp.sum(pm, axis=-1, keepdims=True)
            a_new = alpha * a + _dot(pm.astype(BF16), v_ref[pl.ds(koff, t), ls])
            out.append((m_new, l_new, a_new))
        return tuple(out)

    init = tuple((jnp.full((t, 1), NEG_BIG, F32), jnp.zeros((t, 1), F32), jnp.zeros((t, LANES), F32))
                 for _ in range(FOX_HEADS))
    carry = lax.fori_loop(0, i, lambda j, c: tile(j, c, False), init)
    final = tile(i, carry, True)
    for p in range(FOX_HEADS // 2):
        (_, l0, a0), (_, l1, a1) = final[2 * p], final[2 * p + 1]
        o_ref[:, p * LANES:(p + 1) * LANES] = jnp.where(lane < FOX_HEAD_DIM, a0 / l0, a1 / l1).astype(BF16)


def _fox_prompt(q16, k16, v16, smt, n_seq, seq_len):
    t = FOX_TILE
    nq = seq_len // t
    cum_block = SM_CUM // SUBLANES
    return pl.pallas_call(
        _fox_prompt_body,
        grid=(n_seq, nq),
        in_specs=[pl.BlockSpec((t, FOX_WIDTH), lambda b, i: (b * nq + i, 0)),
                  pl.BlockSpec((seq_len, FOX_WIDTH), lambda b, i: (b, 0)),
                  pl.BlockSpec((seq_len, FOX_WIDTH), lambda b, i: (b, 0)),
                  pl.BlockSpec((SUBLANES, seq_len), lambda b, i: (cum_block, b))],
        out_specs=pl.BlockSpec((t, FOX_WIDTH), lambda b, i: (b * nq + i, 0)),
        out_shape=jax.ShapeDtypeStruct(q16.shape, BF16),
        compiler_params=pltpu.CompilerParams(dimension_semantics=("arbitrary", "arbitrary"),
                                             vmem_limit_bytes=40 * MIB),
        name="fox_prompt",
    )(q16, k16, v16, smt)


def _bdot(a, b):
    return lax.dot_general(a.astype(BF16), b.astype(BF16), (((2,), (1,)), ((0,), (0,))),
                           preferred_element_type=F32)


def _bdot_nt(a, b):
    return lax.dot_general(a.astype(BF16), b.astype(BF16), (((2,), (2,)), ((0,), (0,))),
                           preferred_element_type=F32)


def _unit_lower_inverse_minus_eye(a, r, c):
    blk16 = (r // 16) == (c // 16)
    blk32 = (r // 32) == (c // 32)
    p = jnp.where(blk16, -a, 0.0)
    dt = p
    for _ in range(3):
        p = _bdot(p, p)
        dt = dt + p + _bdot(dt, p)
    for off in (jnp.where(jnp.logical_and(blk32, jnp.logical_not(blk16)), a, 0.0),
                jnp.where(blk32, 0.0, a)):
        x = off + _bdot(dt, off)
        dt = dt - (x + _bdot(x, dt))
    return dt


def _gdn_prompt_body(qkv_ref, sm_ref, gcr_ref, z_ref, gn_ref, o_ref, sout_ref, s_ref):
    ci = pl.program_id(1)

    @pl.when(ci == 0)
    def _():
        s_ref[...] = jnp.zeros_like(s_ref)

    ch = GDN_CHUNK
    d = GDN_HEAD_DIM
    nh = GDN_HEADS
    n_chunks = qkv_ref.shape[0] // ch
    units = [(cidx, hh) for cidx in range(n_chunks) for hh in range(nh)]

    def gather(fn):
        return jnp.stack([fn(slice(cidx * ch, (cidx + 1) * ch), hh) for cidx, hh in units], axis=0)

    q = gather(lambda rows, hh: qkv_ref[rows, hh * d:(hh + 1) * d])
    k = gather(lambda rows, hh: qkv_ref[rows, GDN_WIDTH + hh * d:GDN_WIDTH + (hh + 1) * d])
    v = gather(lambda rows, hh: qkv_ref[rows, 2 * GDN_WIDTH + hh * d:2 * GDN_WIDTH + (hh + 1) * d])
    beta = gather(lambda rows, hh: sm_ref[rows, SM_BETA + hh:SM_BETA + hh + 1])
    gc = gather(lambda rows, hh: sm_ref[rows, SM_GCUM + hh:SM_GCUM + hh + 1])
    gr = gather(lambda rows, hh: gcr_ref[hh:hh + 1, rows])

    r = lax.broadcasted_iota(jnp.int32, (1, ch, ch), 1)
    c = lax.broadcasted_iota(jnp.int32, (1, ch, ch), 2)
    lower = r >= c
    decay = jnp.where(lower, jnp.exp(jnp.where(lower, gc - gr, 0.0)), 0.0)
    qk_kk = _bdot_nt(jnp.concatenate([q, k], axis=1), k)
    qk = qk_kk[:, :ch] * decay
    a = jnp.where(r > c, beta * qk_kk[:, ch:] * decay, 0.0)
    dt = _unit_lower_inverse_minus_eye(a, r, c)
    eg = jnp.exp(gc)
    rhs = jnp.concatenate([v * beta, k * (beta * eg)], axis=-1)
    uw = rhs + _bdot(dt, rhs)
    gl = gc[:, ch - 1:ch, :]
    qd = (q * eg).astype(BF16)
    kd = k * jnp.exp(gl - gc)
    g_last = jnp.exp(gl)
    qk16 = qk.astype(BF16)

    s = s_ref[...]
    for cidx in range(n_chunks):
        us = slice(cidx * nh, (cidx + 1) * nh)
        rows = slice(cidx * ch, (cidx + 1) * ch)
        s16 = s.astype(BF16)
        v_new = uw[us, :, :d] - _bdot(uw[us, :, d:], s16)
        o = _bdot(qd[us], s16) + _bdot(qk16[us], v_new)
        vn16 = v_new.astype(BF16)
        upd = jnp.stack([_dot(kd[cidx * nh + hh].T.astype(BF16), vn16[hh]) for hh in range(nh)], axis=0)
        s = s * g_last[us] + upd
        for hh in range(nh):
            hs = slice(hh * d, (hh + 1) * d)
            o_ref[rows, hs] = (_rms(o[hh], gn_ref[...]) * _silu(z_ref[rows, hs])).astype(BF16)
    s_ref[...] = s

    @pl.when(ci == pl.num_programs(1) - 1)
    def _():
        sout_ref[0] = s


def _gdn_prompt(gqkv, sm, smt, gz, gn, n_seq, seq_len):
    tm = ROW_TILE
    nt = seq_len // tm
    gcum_block = SM_GCUM // SUBLANES
    rows = gqkv.shape[0]
    return pl.pallas_call(
        _gdn_prompt_body,
        grid=(n_seq, nt),
        in_specs=[pl.BlockSpec((tm, 3 * GDN_WIDTH), lambda b, i: (b * nt + i, 0)),
                  pl.BlockSpec((tm, LANES), lambda b, i: (b * nt + i, 0)),
                  pl.BlockSpec((SUBLANES, tm), lambda b, i: (gcum_block, b * nt + i)),
                  pl.BlockSpec((tm, GDN_WIDTH), lambda b, i: (b * nt + i, 0)),
                  pl.BlockSpec((1, GDN_HEAD_DIM), lambda b, i: (0, 0))],
        out_specs=(pl.BlockSpec((tm, GDN_WIDTH), lambda b, i: (b * nt + i, 0)),
                   pl.BlockSpec((1, GDN_HEADS, GDN_HEAD_DIM, GDN_HEAD_DIM), lambda b, i: (b, 0, 0, 0))),
        out_shape=(jax.ShapeDtypeStruct((rows, GDN_WIDTH), BF16),
                   jax.ShapeDtypeStruct((n_seq, GDN_HEADS, GDN_HEAD_DIM, GDN_HEAD_DIM), F32)),
        scratch_shapes=[pltpu.VMEM((GDN_HEADS, GDN_HEAD_DIM, GDN_HEAD_DIM), F32)],
        compiler_params=pltpu.CompilerParams(dimension_semantics=("arbitrary", "arbitrary"),
                                             vmem_limit_bytes=32 * MIB),
        name="gdn_prompt",
    )(gqkv, sm, smt, gz, gn)


def _ffn_tail(x, fox_ref, gdn_ref, wo_ref, nf_ref, wup_ref, wd_ref, nfin_ref, conv):
    mix = _dot(fox_ref[...], wo_ref[0:FOX_WIDTH, :]) + _dot(gdn_ref[...], wo_ref[FOX_WIDTH:, :])
    x2 = x + mix
    h2 = _rms(x2, nf_ref[...]).astype(BF16)
    gu = _dot(h2, wup_ref[...])
    gate, up = gu[:, :FFN_DIM], gu[:, FFN_DIM:]
    act = (_silu(conv(gate)) * up).astype(BF16)
    x3 = x2 + _dot(act, wd_ref[...])
    return _rms(x3, nfin_ref[...])


def _ffn_prompt_body(x_ref, fox_ref, gdn_ref, wo_ref, nf_ref, wup_ref, cw_ref, cb_ref, wd_ref, nfin_ref,
                     y_ref, cst_ref, prev_ref, *, tiles_per_seq):
    i = pl.program_id(0)

    @pl.when(i % tiles_per_seq == 0)
    def _():
        prev_ref[...] = jnp.zeros_like(prev_ref)

    tm = x_ref.shape[0]

    def conv(gate):
        prev = prev_ref[...]
        out = gate * cw_ref[FFN_CONV - 1:FFN_CONV, :] + cb_ref[...]
        for kk in range(1, FFN_CONV):
            out = out + _shift_rows(gate, prev, kk) * cw_ref[FFN_CONV - 1 - kk:FFN_CONV - kk, :]
        prev_ref[...] = gate[tm - SUBLANES:, :]
        cst_ref[...] = gate[tm - SUBLANES:, :]
        return out

    y_ref[...] = _ffn_tail(x_ref[...], fox_ref, gdn_ref, wo_ref, nf_ref, wup_ref, wd_ref, nfin_ref, conv)


def _ffn_prompt(x, fox, gdn, wo, nf, wup, cw, cb, wd, nfin, seq_len):
    rows = x.shape[0]
    tm = ROW_TILE
    n_seq = rows // seq_len
    tiles_per_seq = seq_len // tm
    row_spec = lambda w: pl.BlockSpec((tm, w), lambda i: (i, 0))
    const = lambda a: pl.BlockSpec(a.shape, lambda i: (0, 0), pipeline_mode=pl.Buffered(1))
    return pl.pallas_call(
        functools.partial(_ffn_prompt_body, tiles_per_seq=tiles_per_seq),
        grid=(rows // tm,),
        in_specs=[row_spec(D_MODEL), row_spec(FOX_WIDTH), row_spec(GDN_WIDTH), const(wo), const(nf),
                  const(wup), const(cw), const(cb), const(wd), const(nfin)],
        out_specs=(row_spec(D_MODEL),
                   pl.BlockSpec((SUBLANES, FFN_DIM), lambda i: (i // tiles_per_seq, 0))),
        out_shape=(jax.ShapeDtypeStruct((rows, D_MODEL), F32),
                   jax.ShapeDtypeStruct((n_seq * SUBLANES, FFN_DIM), F32)),
        scratch_shapes=[pltpu.VMEM((SUBLANES, FFN_DIM), F32)],
        compiler_params=pltpu.CompilerParams(dimension_semantics=("arbitrary",), vmem_limit_bytes=56 * MIB),
        name="ffn_prompt",
    )(x, fox, gdn, wo, nf, wup, cw, cb, wd, nfin)


def _ffn_sample_body(x_ref, fox_ref, gdn_ref, wo_ref, nf_ref, wup_ref, cw_ref, cb_ref, wd_ref, nfin_ref,
                     c0_ref, c1_ref, y_ref, gate_ref):
    def conv(gate):
        gate_ref[...] = gate
        return (gate * cw_ref[2:3, :] + c1_ref[...] * cw_ref[1:2, :] + c0_ref[...] * cw_ref[0:1, :]
                + cb_ref[...])

    y_ref[...] = _ffn_tail(x_ref[...], fox_ref, gdn_ref, wo_ref, nf_ref, wup_ref, wd_ref, nfin_ref, conv)


def _ffn_sample(x, fox, gdn, wo, nf, wup, cw, cb, wd, nfin, c0, c1):
    rows = x.shape[0]
    args = (x, fox, gdn, wo, nf, wup, cw, cb, wd, nfin, c0, c1)
    full = lambda a: pl.BlockSpec(a.shape, lambda i: (0, 0), pipeline_mode=pl.Buffered(1))
    return pl.pallas_call(
        _ffn_sample_body,
        grid=(1,),
        in_specs=[full(a) for a in args],
        out_specs=(pl.BlockSpec((rows, D_MODEL), lambda i: (0, 0)),
                   pl.BlockSpec((rows, FFN_DIM), lambda i: (0, 0))),
        out_shape=(jax.ShapeDtypeStruct((rows, D_MODEL), F32), jax.ShapeDtypeStruct((rows, FFN_DIM), F32)),
        compiler_params=pltpu.CompilerParams(dimension_semantics=("arbitrary",), vmem_limit_bytes=56 * MIB),
        name="ffn_sample",
    )(*args)


def _split3(x):
    hi = x.astype(BF16)
    r1 = x - hi.astype(F32)
    mid = r1.astype(BF16)
    lo = (r1 - mid.astype(F32)).astype(BF16)
    return hi, mid, lo


def _logf_prefix_body(x_ref, m1_ref, m2_ref, w_ref, t_ref):
    parts = _split3(x_ref[...])
    m1 = m1_ref[...]
    m2 = m2_ref[...]
    w = _dot(parts[0], m1) + _dot(parts[1], m1) + _dot(parts[2], m1)
    for r, term in enumerate(_split3(w)):
        w_ref[:, r, :] = term.astype(F32)
    for r in range(3, SUBLANES):
        w_ref[:, r, :] = jnp.zeros(w.shape, F32)
    tb = _dot(parts[0], m2) + _dot(parts[1], m2) + _dot(parts[2], m2)
    for h in range(FOX_HEADS):
        t_ref[:, h, :] = tb[:, h * LANES:(h + 1) * LANES]


def _logf_prefix(logf2d, page_size):
    n_pool, width = logf2d.shape
    tm = LANES if n_pool % LANES == 0 else n_pool
    lane = jnp.arange(width)
    tok, head = lane // FOX_HEADS, lane % FOX_HEADS
    m1 = ((head[:, None] == head[None, :]) & (tok[:, None] > tok[None, :])).astype(BF16)
    m2 = (head[:, None] == (lane // LANES)[None, :]).astype(BF16)
    return pl.pallas_call(
        _logf_prefix_body,
        grid=(n_pool // tm,),
        in_specs=[pl.BlockSpec((tm, width), lambda i: (i, 0)),
                  pl.BlockSpec((width, width), lambda i: (0, 0)),
                  pl.BlockSpec((width, width), lambda i: (0, 0))],
        out_specs=(pl.BlockSpec((tm, SUBLANES, width), lambda i: (i, 0, 0)),
                   pl.BlockSpec((tm, FOX_HEADS, LANES), lambda i: (i, 0, 0))),
        out_shape=(jax.ShapeDtypeStruct((n_pool, SUBLANES, width), F32),
                   jax.ShapeDtypeStruct((n_pool, FOX_HEADS, LANES), F32)),
        compiler_params=pltpu.CompilerParams(dimension_semantics=("arbitrary",), vmem_limit_bytes=40 * MIB),
        name="logf_prefix",
    )(logf2d, m1, m2)


def _fold_tokens(op, x3):
    parts = [x3[i] for i in range(x3.shape[0])]
    while len(parts) > 1:
        parts = [op(parts[i], parts[i + 1]) for i in range(0, len(parts), 2)]
    return parts[0]


def _decode_body(pt_ref, q_ref, kn_ref, vn_ref, sm_ref, kflat, vflat, wpool, tpool,
                 o_ref, kbuf, vbuf, wbuf, tbuf, sems, *, n_pages, page):
    b = pl.program_id(0)
    nb = pl.num_programs(0)
    slot = b % 2
    rows = page * FOX_HEADS
    hd = FOX_HEAD_DIM

    def copies(bb, sl):
        out = []
        for j in range(n_pages):
            pg = pt_ref[bb * n_pages + j]
            src = pl.ds(pl.multiple_of(pg * rows, rows), rows)
            dst = pl.ds(j * rows, rows)
            out.append(pltpu.make_async_copy(kflat.at[src, :], kbuf.at[sl, dst, :], sems.at[sl, 0]))
            out.append(pltpu.make_async_copy(vflat.at[src, :], vbuf.at[sl, dst, :], sems.at[sl, 1]))
            out.append(pltpu.make_async_copy(wpool.at[pg], wbuf.at[sl, j], sems.at[sl, 2]))
            out.append(pltpu.make_async_copy(tpool.at[pg], tbuf.at[sl, j], sems.at[sl, 3]))
        return out

    @pl.when(b == 0)
    def _():
        for cp in copies(0, 0):
            cp.start()

    @pl.when(b + 1 < nb)
    def _():
        for cp in copies(b + 1, 1 - slot):
            cp.start()

    for cp in copies(b, slot):
        cp.wait()

    q = q_ref[0]
    r8 = lax.broadcasted_iota(jnp.int32, (SUBLANES, LANES), 0)
    c8 = lax.broadcasted_iota(jnp.int32, (SUBLANES, LANES), 1)
    lf_new = jnp.sum(jnp.where(r8 == c8, sm_ref[0], 0.0), axis=1, keepdims=True)
    ones_k = jnp.ones((hd, hd), BF16)
    ones_w = jnp.ones((SUBLANES, hd), BF16)

    def fold_page(jj, carry):
        m, l, acc, pb = carry
        j = n_pages - 1 - jj
        r0 = pl.multiple_of(j * rows, rows)
        k3 = kbuf[slot, pl.ds(r0, rows), :].reshape(page, FOX_HEADS, hd)
        prod = (k3 * q[None]).reshape(rows, hd)
        w_t = wbuf[slot, j].T
        s3 = (_dot(prod.astype(BF16), ones_k) + _dot(w_t.astype(BF16), ones_w)).reshape(page, FOX_HEADS, hd)
        m_loc = _fold_tokens(jnp.maximum, s3)
        p3 = jnp.exp(s3 - m_loc[None])
        v3 = vbuf[slot, pl.ds(r0, rows), :].reshape(page, FOX_HEADS, hd)
        l_loc = _fold_tokens(jnp.add, p3)
        a_loc = _fold_tokens(jnp.add, p3 * v3)
        m_pg = m_loc + pb
        m_new = jnp.maximum(m, m_pg)
        w_old = jnp.exp(m - m_new)
        w_pg = jnp.exp(m_pg - m_new)
        return (m_new, w_old * l + w_pg * l_loc, w_old * acc + w_pg * a_loc, pb + tbuf[slot, j][:, :hd])

    m0 = jnp.broadcast_to(jnp.sum(kn_ref[0] * q, axis=-1, keepdims=True), (FOX_HEADS, hd))
    init = (m0, jnp.ones((FOX_HEADS, hd), F32), vn_ref[0], jnp.broadcast_to(lf_new, (FOX_HEADS, hd)))
    _, l, acc, _ = lax.fori_loop(0, n_pages, fold_page, init, unroll=4)
    o_ref[0] = acc / l


def _decode_attention(page_table, q, kn, vn, sm, kflat, vflat, wpool, tpool):
    nb, n_pages = page_table.shape
    page = wpool.shape[2] // FOX_HEADS
    rows = page * FOX_HEADS
    tile = lambda a: pl.BlockSpec((1,) + a.shape[1:], lambda i, pt: (i,) + (0,) * (a.ndim - 1))
    hbm = pl.BlockSpec(memory_space=pl.ANY)
    sm = sm.reshape(nb, 1, LANES)
    grid_spec = pltpu.PrefetchScalarGridSpec(
        num_scalar_prefetch=1,
        grid=(nb,),
        in_specs=[tile(q), tile(kn), tile(vn), tile(sm), hbm, hbm, hbm, hbm],
        out_specs=pl.BlockSpec((1, FOX_HEADS, FOX_HEAD_DIM), lambda i, pt: (i, 0, 0)),
        scratch_shapes=[pltpu.VMEM((2, n_pages * rows, FOX_HEAD_DIM), F32),
                        pltpu.VMEM((2, n_pages * rows, FOX_HEAD_DIM), F32),
                        pltpu.VMEM((2, n_pages, SUBLANES, rows), F32),
                        pltpu.VMEM((2, n_pages, FOX_HEADS, LANES), F32),
                        pltpu.SemaphoreType.DMA((2, 4))],
    )
    return pl.pallas_call(
        functools.partial(_decode_body, n_pages=n_pages, page=page),
        grid_spec=grid_spec,
        out_shape=jax.ShapeDtypeStruct((nb, FOX_HEADS, FOX_HEAD_DIM), F32),
        compiler_params=pltpu.CompilerParams(dimension_semantics=("arbitrary",), vmem_limit_bytes=56 * MIB),
        name="fox_decode",
    )(page_table.reshape(-1), q, kn, vn, sm, kflat, vflat, wpool, tpool)


def _gdn_sample_body(s_ref, qkv_ref, sm_ref, z_ref, gn_ref, sout_ref, o_ref):
    d = GDN_HEAD_DIM
    eye = lax.broadcasted_iota(jnp.int32, (d, d), 0) == lax.broadcasted_iota(jnp.int32, (d, d), 1)

    def col(rows):
        return jnp.sum(jnp.where(eye[None], rows[:, None, :], 0.0), axis=2, keepdims=True)

    for hh in range(GDN_HEADS):
        hs = slice(hh * d, (hh + 1) * d)
        q = qkv_ref[:, hs]
        k = qkv_ref[:, GDN_WIDTH + hh * d:GDN_WIDTH + (hh + 1) * d]
        v = qkv_ref[:, 2 * GDN_WIDTH + hh * d:2 * GDN_WIDTH + (hh + 1) * d]
        eg = jnp.exp(sm_ref[:, SM_G + hh:SM_G + hh + 1])
        beta = sm_ref[:, SM_BETA + hh:SM_BETA + hh + 1]
        s = s_ref[:, hh]
        v_new = v * beta - jnp.sum(col(k * (beta * eg)) * s, axis=1)
        o = jnp.sum(col(q * eg) * s, axis=1) + jnp.sum(q * k, axis=1, keepdims=True) * v_new
        sout_ref[:, hh] = s * eg[:, :, None] + col(k) * v_new[:, None, :]
        o_ref[:, hs] = _rms(o, gn_ref[...]) * _silu(z_ref[:, hs])


def _gdn_sample(state, gqkv, sm, gz, gn):
    nb = state.shape[0]
    bb = SUBLANES
    d = GDN_HEAD_DIM
    row_spec = lambda w: pl.BlockSpec((bb, w), lambda i: (i, 0))
    st_spec = pl.BlockSpec((bb, GDN_HEADS, d, d), lambda i: (i, 0, 0, 0))
    return pl.pallas_call(
        _gdn_sample_body,
        grid=(nb // bb,),
        in_specs=[st_spec, row_spec(3 * GDN_WIDTH), row_spec(LANES), row_spec(GDN_WIDTH),
                  pl.BlockSpec((1, d), lambda i: (0, 0))],
        out_specs=(st_spec, row_spec(GDN_WIDTH)),
        out_shape=(jax.ShapeDtypeStruct(state.shape, F32), jax.ShapeDtypeStruct((nb, GDN_WIDTH), F32)),
        compiler_params=pltpu.CompilerParams(dimension_semantics=("arbitrary",), vmem_limit_bytes=32 * MIB),
        name="gdn_sample",
    )(state, gqkv, sm, gz, gn)


def _pad_rows(a, rows):
    return jnp.concatenate([a, jnp.zeros((rows - a.shape[0],) + a.shape[1:], a.dtype)], axis=0)


def kernel(x_prompt, x_sample, cache_k, cache_v, cache_logf, state_gdn, state_gdn_conv, state_ffn_conv,
           page_table, norm_mix, w_in, b_forget, gdn_a_log, gdn_dt_bias, w_gdn_conv, gdn_out_norm, w_out,
           norm_ffn, w_up, w_ffn_conv, b_ffn_conv, w_down, norm_final):
    assert w_in.shape[0] == 1, "single-layer trunk"
    n_seq, seq_len, _ = x_prompt.shape
    nb = x_sample.shape[0]
    n_pool, page = cache_k.shape[1], cache_k.shape[2]

    w = w_in[0]
    o_ff = 3 * FOX_WIDTH
    o_g = o_ff + FOX_HEADS
    o_ga = o_g + 3 * GDN_WIDTH
    o_gz = o_ga + 2 * GDN_HEADS
    wbig = jnp.concatenate([w[:, :o_ff], w[:, o_g:o_ga], w[:, o_gz:]], axis=1).astype(BF16)
    wsm = jnp.concatenate([w[:, o_ff:o_g], w[:, o_ga:o_gz],
                           jnp.zeros((D_MODEL, LANES - FOX_HEADS - 2 * GDN_HEADS), F32)], axis=1).astype(BF16)
    par = jnp.zeros((SUBLANES, LANES), F32)
    par = par.at[0, SM_LF:SM_LF + FOX_HEADS].set(b_forget[0])
    par = par.at[0, SM_G:SM_G + GDN_HEADS].set(gdn_dt_bias[0])
    par = par.at[1, SM_G:SM_G + GDN_HEADS].set(gdn_a_log[0])
    nrm = norm_mix[0][None, :]
    cw = _pad_rows(w_gdn_conv[0], SUBLANES)
    gn = gdn_out_norm[0][None, :]
    wo = w_out[0].astype(BF16)
    nf = norm_ffn[0][None, :]
    wup = w_up[0].astype(BF16)
    cwf = _pad_rows(w_ffn_conv[0], SUBLANES)
    cbf = b_ffn_conv[0][None, :]
    wd = w_down[0].astype(BF16)
    nfin = norm_final[None, :]

    xp = x_prompt.reshape(n_seq * seq_len, D_MODEL)
    fk, fv, q16, k16, v16, gqkv, gz, sm, smt, cst = _proj_prompt(xp, nrm, wbig, wsm, par, cw, seq_len)
    fox = _fox_prompt(q16, k16, v16, smt, n_seq, seq_len)
    gdn, s_p = _gdn_prompt(gqkv, sm, smt, gz, gn, n_seq, seq_len)
    yp, cstf = _ffn_prompt(xp, fox, gdn, wo, nf, wup, cwf, cbf, wd, nfin, seq_len)

    xs = x_sample.reshape(nb, D_MODEL)
    gctx = state_gdn_conv[0]
    fctx = state_ffn_conv[0]
    fk_s, fv_s, q_s, gqkv_s, gz_s, sm_s, pre_s = _proj_sample(
        xs, nrm, wbig, wsm, par, cw, gctx[:, 0], gctx[:, 1], gctx[:, 2])
    assert page == LANES, "log-f suffix layout assumes 128-token pages"
    wexc, tot = _logf_prefix(cache_logf[0].reshape(n_pool, page * FOX_HEADS), page)
    heads = lambda a: a.reshape(nb, FOX_HEADS, FOX_HEAD_DIM)
    fox_s = _decode_attention(page_table, heads(q_s), heads(fk_s), heads(fv_s), sm_s,
                              cache_k.reshape(n_pool * page * FOX_HEADS, FOX_HEAD_DIM),
                              cache_v.reshape(n_pool * page * FOX_HEADS, FOX_HEAD_DIM), wexc, tot)
    s_s, gdn_s = _gdn_sample(state_gdn[0], gqkv_s, sm_s, gz_s, gn)
    ys, gate_s = _ffn_sample(xs, fox_s.reshape(nb, FOX_WIDTH).astype(BF16), gdn_s.astype(BF16), wo, nf, wup,
                             cwf, cbf, wd, nfin, fctx[:, 0], fctx[:, 1])

    kv_shape_p = (1, n_seq, seq_len, FOX_HEADS, FOX_HEAD_DIM)
    kv_shape_s = (1, nb, 1, FOX_HEADS, FOX_HEAD_DIM)
    return (
        yp.reshape(n_seq, seq_len, D_MODEL),
        ys.reshape(nb, 1, D_MODEL),
        fk.reshape(kv_shape_p),
        fv.reshape(kv_shape_p),
        sm[:, SM_LF:SM_LF + FOX_HEADS].reshape(1, n_seq, seq_len, FOX_HEADS),
        s_p[None],
        cst.reshape(n_seq, SUBLANES, 3 * GDN_WIDTH)[None, :, SUBLANES - (GDN_CONV - 1):],
        cstf.reshape(n_seq, SUBLANES, FFN_DIM)[None, :, SUBLANES - (FFN_CONV - 1):],
        fk_s.reshape(kv_shape_s),
        fv_s.reshape(kv_shape_s),
        sm_s[:, SM_LF:SM_LF + FOX_HEADS].reshape(1, nb, 1, FOX_HEADS),
        s_s[None],
        jnp.concatenate([gctx[:, 1:], pre_s[:, None, :]], axis=1)[None],
        jnp.concatenate([fctx[:, 1:], gate_s[:, None, :]], axis=1)[None],
    )
```

```python
import functools

import jax
import jax.numpy as jnp
from jax import lax
from jax.experimental import pallas as pl
from jax.experimental.pallas import tpu as pltpu

D_MODEL = 1024
FOX_HEADS = 8
FOX_HEAD_DIM = 64
FOX_WIDTH = FOX_HEADS * FOX_HEAD_DIM
GDN_HEADS = 4
GDN_HEAD_DIM = 128
GDN_WIDTH = GDN_HEADS * GDN_HEAD_DIM
GDN_CONV = 4
GDN_CHUNK = 64
FFN_DIM = 2816
FFN_CONV = 3
EPS = 1e-6
NEG_BIG = -1e30
FOX_SCALE = FOX_HEAD_DIM ** -0.5
GDN_SCALE = GDN_HEAD_DIM ** -0.5

LANES = 128
SUBLANES = 8
ROW_TILE = 256
FOX_TILE = 256
MIB = 1024 * 1024

SM_LF = 0
SM_G = 8
SM_BETA = 12
SM_CUM = 16
SM_GCUM = 24

F32 = jnp.float32
BF16 = jnp.bfloat16


def _sigmoid(x):
    return 1.0 / (1.0 + jnp.exp(-x))


def _silu(x):
    return x * _sigmoid(x)


def _rms(x, g):
    return x * lax.rsqrt(jnp.mean(x * x, axis=-1, keepdims=True) + EPS) * g


def _dot(a, b):
    return jnp.dot(a, b, preferred_element_type=F32)


def _dot_nt(a, b):
    return lax.dot_general(a, b, (((1,), (1,)), ((), ())), preferred_element_type=F32)


def _hdot(a, b):
    return jnp.dot(a, b, preferred_element_type=F32, precision=lax.Precision.HIGHEST)


def _shift_rows(x, prev8, k):
    r = pltpu.roll(x, k, 0)
    row8 = lax.broadcasted_iota(jnp.int32, prev8.shape, 0)
    top = jnp.where(row8 < k, pltpu.roll(prev8, k, 0), r[0:SUBLANES])
    return jnp.concatenate([top, r[SUBLANES:]], axis=0)


def _gate_activations(raw, par_ref):
    z = raw + par_ref[0:1, :]
    lane = lax.broadcasted_iota(jnp.int32, z.shape, 1)
    t = jnp.log1p(jnp.exp(-jnp.abs(z)))
    lf = jnp.minimum(z, 0.0) - t
    softplus = jnp.maximum(z, 0.0) + t
    g = -jnp.exp(par_ref[1:2, :]) * softplus
    beta = _sigmoid(z)
    return jnp.where(lane < SM_G, lf, jnp.where(lane < SM_BETA, g, jnp.where(lane < SM_CUM, beta, 0.0)))


def _gdn_qkv_norm(c, out_ref):
    for part, scale in ((0, GDN_SCALE), (1, 1.0)):
        for hh in range(GDN_HEADS):
            off = part * GDN_WIDTH + hh * GDN_HEAD_DIM
            seg = c[:, off:off + GDN_HEAD_DIM]
            n = lax.rsqrt(jnp.sum(seg * seg, axis=-1, keepdims=True) + EPS)
            out_ref[:, off:off + GDN_HEAD_DIM] = seg * n * scale
    out_ref[:, 2 * GDN_WIDTH:] = c[:, 2 * GDN_WIDTH:]


def _proj_prompt_body(x_ref, nrm_ref, wbig_ref, wsm_ref, par_ref, cw_ref,
                      fk_ref, fv_ref, q16_ref, k16_ref, v16_ref, gqkv_ref, gz_ref,
                      sm_ref, smt_ref, cst_ref, carry_ref, prev_ref, *, tiles_per_seq):
    i = pl.program_id(0)

    @pl.when(i % tiles_per_seq == 0)
    def _():
        carry_ref[...] = jnp.zeros_like(carry_ref)
        prev_ref[...] = jnp.zeros_like(prev_ref)

    tm = x_ref.shape[0]
    h16 = _rms(x_ref[...], nrm_ref[...]).astype(BF16)

    fox = _dot(h16, wbig_ref[:, 0:3 * FOX_WIDTH])
    fq, fk, fv = fox[:, :FOX_WIDTH], fox[:, FOX_WIDTH:2 * FOX_WIDTH], fox[:, 2 * FOX_WIDTH:]
    q16_ref[...] = (fq * FOX_SCALE).astype(BF16)
    fk_ref[...] = fk
    fv_ref[...] = fv
    k16_ref[...] = fk.astype(BF16)
    v16_ref[...] = fv.astype(BF16)

    act = _gate_activations(_dot(h16, wsm_ref[...]), par_ref)
    row = lax.broadcasted_iota(jnp.int32, act.shape, 0)
    lane = lax.broadcasted_iota(jnp.int32, act.shape, 1)
    y = act
    yc = act
    k = 1
    while k < tm:
        y = y + jnp.where(row >= k, pltpu.roll(y, k, 0), 0.0)
        if k < GDN_CHUNK:
            yc = yc + jnp.where((row & (GDN_CHUNK - 1)) >= k, pltpu.roll(yc, k, 0), 0.0)
        k *= 2
    y = y + carry_ref[0:1, :]
    carry_ref[0:1, :] = y[tm - 1:tm, :]
    shift = SM_CUM - SM_LF
    sm = jnp.where(lane < SM_CUM, act,
                   jnp.where(lane < SM_GCUM, pltpu.roll(y, shift, 1),
                             jnp.where(lane < SM_GCUM + GDN_HEADS, pltpu.roll(yc, shift, 1), 0.0)))
    sm_ref[...] = sm
    smt_ref[...] = sm.T

    pre = _dot(h16, wbig_ref[:, 3 * FOX_WIDTH:3 * FOX_WIDTH + 3 * GDN_WIDTH])
    prev = prev_ref[...]
    acc = pre * cw_ref[GDN_CONV - 1:GDN_CONV, :]
    for kk in range(1, GDN_CONV):
        acc = acc + _shift_rows(pre, prev, kk) * cw_ref[GDN_CONV - 1 - kk:GDN_CONV - kk, :]
    prev_ref[...] = pre[tm - SUBLANES:, :]
    cst_ref[...] = pre[tm - SUBLANES:, :]
    _gdn_qkv_norm(_silu(acc), gqkv_ref)

    gz_ref[...] = _dot(h16, wbig_ref[:, 3 * FOX_WIDTH + 3 * GDN_WIDTH:])


def _proj_prompt(x, nrm, wbig, wsm, par, cw, seq_len):
    rows = x.shape[0]
    tm = ROW_TILE
    n_seq = rows // seq_len
    tiles_per_seq = seq_len // tm
    wb = wbig.shape[1]
    row_spec = lambda w: pl.BlockSpec((tm, w), lambda i: (i, 0))
    const = lambda s: pl.BlockSpec(s, lambda i: (0, 0))
    out_shape = (
        jax.ShapeDtypeStruct((rows, FOX_WIDTH), F32),
        jax.ShapeDtypeStruct((rows, FOX_WIDTH), F32),
        jax.ShapeDtypeStruct((rows, FOX_WIDTH), BF16),
        jax.ShapeDtypeStruct((rows, FOX_WIDTH), BF16),
        jax.ShapeDtypeStruct((rows, FOX_WIDTH), BF16),
        jax.ShapeDtypeStruct((rows, 3 * GDN_WIDTH), F32),
        jax.ShapeDtypeStruct((rows, GDN_WIDTH), F32),
        jax.ShapeDtypeStruct((rows, LANES), F32),
        jax.ShapeDtypeStruct((LANES, rows), F32),
        jax.ShapeDtypeStruct((n_seq * SUBLANES, 3 * GDN_WIDTH), F32),
    )
    out_specs = (
        row_spec(FOX_WIDTH), row_spec(FOX_WIDTH), row_spec(FOX_WIDTH), row_spec(FOX_WIDTH),
        row_spec(FOX_WIDTH), row_spec(3 * GDN_WIDTH), row_spec(GDN_WIDTH), row_spec(LANES),
        pl.BlockSpec((LANES, tm), lambda i: (0, i)),
        pl.BlockSpec((SUBLANES, 3 * GDN_WIDTH), lambda i: (i // tiles_per_seq, 0)),
    )
    return pl.pallas_call(
        functools.partial(_proj_prompt_body, tiles_per_seq=tiles_per_seq),
        grid=(rows // tm,),
        in_specs=[row_spec(D_MODEL), const((1, D_MODEL)), const((D_MODEL, wb)), const((D_MODEL, LANES)),
                  const((SUBLANES, LANES)), const((SUBLANES, 3 * GDN_WIDTH))],
        out_specs=out_specs,
        out_shape=out_shape,
        scratch_shapes=[pltpu.VMEM((SUBLANES, LANES), F32), pltpu.VMEM((SUBLANES, 3 * GDN_WIDTH), F32)],
        compiler_params=pltpu.CompilerParams(dimension_semantics=("arbitrary",), vmem_limit_bytes=48 * MIB),
        name="proj_prompt",
    )(x, nrm, wbig, wsm, par, cw)


def _proj_sample_body(x_ref, nrm_ref, wbig_ref, wsm_ref, par_ref, cw_ref, c0_ref, c1_ref, c2_ref,
                      fk_ref, fv_ref, q_ref, gqkv_ref, gz_ref, sm_ref, pre_ref):
    h16 = _rms(x_ref[...], nrm_ref[...]).astype(BF16)
    fox = _dot(h16, wbig_ref[:, 0:3 * FOX_WIDTH])
    q_ref[...] = fox[:, :FOX_WIDTH] * FOX_SCALE
    fk_ref[...] = fox[:, FOX_WIDTH:2 * FOX_WIDTH]
    fv_ref[...] = fox[:, 2 * FOX_WIDTH:]
    sm_ref[...] = _gate_activations(_dot(h16, wsm_ref[...]), par_ref)
    pre = _dot(h16, wbig_ref[:, 3 * FOX_WIDTH:3 * FOX_WIDTH + 3 * GDN_WIDTH])
    pre_ref[...] = pre
    acc = (pre * cw_ref[3:4, :] + c2_ref[...] * cw_ref[2:3, :]
           + c1_ref[...] * cw_ref[1:2, :] + c0_ref[...] * cw_ref[0:1, :])
    _gdn_qkv_norm(_silu(acc), gqkv_ref)
    gz_ref[...] = _dot(h16, wbig_ref[:, 3 * FOX_WIDTH + 3 * GDN_WIDTH:])


def _proj_sample(x, nrm, wbig, wsm, par, cw, c0, c1, c2):
    rows = x.shape[0]
    full = lambda a: pl.BlockSpec(a.shape, lambda i: (0,) * a.ndim)
    args = (x, nrm, wbig, wsm, par, cw, c0, c1, c2)
    shapes = ((rows, FOX_WIDTH), (rows, FOX_WIDTH), (rows, FOX_WIDTH), (rows, 3 * GDN_WIDTH),
              (rows, GDN_WIDTH), (rows, LANES), (rows, 3 * GDN_WIDTH))
    return pl.pallas_call(
        _proj_sample_body,
        grid=(1,),
        in_specs=[full(a) for a in args],
        out_specs=tuple(pl.BlockSpec(s, lambda i: (0, 0)) for s in shapes),
        out_shape=tuple(jax.ShapeDtypeStruct(s, F32) for s in shapes),
        compiler_params=pltpu.CompilerParams(dimension_semantics=("arbitrary",), vmem_limit_bytes=48 * MIB),
        name="proj_sample",
    )(*args)


def _fox_prompt_body(q_ref, k_ref, v_ref, cum_ref, o_ref):
    i = pl.program_id(1)
    t = FOX_TILE
    lane = lax.broadcasted_iota(jnp.int32, (t, LANES), 1)
    rr = lax.broadcasted_iota(jnp.int32, (t, t), 0)
    cc = lax.broadcasted_iota(jnp.int32, (t, t), 1)
    causal = cc <= rr
    qoff = pl.multiple_of(i * t, t)
    qs, crefs = [], []
    for h in range(FOX_HEADS):
        q2 = q_ref[:, (h // 2) * LANES:(h // 2 + 1) * LANES]
        keep = (lane < FOX_HEAD_DIM) if h % 2 == 0 else (lane >= FOX_HEAD_DIM)
        qs.append(jnp.where(keep, q2, jnp.zeros_like(q2)))
        crefs.append(cum_ref[h:h + 1, pl.ds(qoff, LANES)][:, 0:1])

    def tile(j, carry, masked):
        koff = pl.multiple_of(j * t, t)
        out = []
        for h in range(FOX_HEADS):
            ls = slice((h // 2) * LANES, (h // 2 + 1) * LANES)
            m, l, a = carry[h]
            s = _dot_nt(qs[h], k_ref[pl.ds(koff, t), ls]) + (crefs[h] - cum_ref[h:h + 1, pl.ds(koff, t)])
            if masked:
                s = jnp.where(causal, s, NEG_BIG)
            m_new = jnp.maximum(m, jnp.max(s, axis=-1, keepdims=True))
            alpha = jnp.exp(m - m_new)
            pm = jnp.exp(s - m_new)
            l_new = alpha * l + jnp.sum(pm, axis=-1, keepdims=True)
            a_new = alpha * a + _dot(pm.astype(BF16), v_ref[pl.ds(koff, t), ls])
            out.append((m_new, l_new, a_new))
        return tuple(out)

    init = tuple((jnp.full((t, 1), NEG_BIG, F32), jnp.zeros((t, 1), F32), jnp.zeros((t, LANES), F32))
                 for _ in range(FOX_HEADS))
    carry = lax.fori_loop(0, i, lambda j, c: tile(j, c, False), init)
    final = tile(i, carry, True)
    for p in range(FOX_HEADS // 2):
        (_, l0, a0), (_, l1, a1) = final[2 * p], final[2 * p + 1]
        o_ref[:, p * LANES:(p + 1) * LANES] = jnp.where(lane < FOX_HEAD_DIM, a0 / l0, a1 / l1).astype(BF16)


def _fox_prompt(q16, k16, v16, smt, n_seq, seq_len):
    t = FOX_TILE
    nq = seq_len // t
    cum_block = SM_CUM // SUBLANES
    return pl.pallas_call(
        _fox_prompt_body,
        grid=(n_seq, nq),
        in_specs=[pl.BlockSpec((t, FOX_WIDTH), lambda b, i: (b * nq + i, 0)),
                  pl.BlockSpec((seq_len, FOX_WIDTH), lambda b, i: (b, 0)),
                  pl.BlockSpec((seq_len, FOX_WIDTH), lambda b, i: (b, 0)),
                  pl.BlockSpec((SUBLANES, seq_len), lambda b, i: (cum_block, b))],
        out_specs=pl.BlockSpec((t, FOX_WIDTH), lambda b, i: (b * nq + i, 0)),
        out_shape=jax.ShapeDtypeStruct(q16.shape, BF16),
        compiler_params=pltpu.CompilerParams(dimension_semantics=("arbitrary", "arbitrary"),
                                             vmem_limit_bytes=40 * MIB),
        name="fox_prompt",
    )(q16, k16, v16, smt)


def _bdot(a, b):
    return lax.dot_general(a.astype(BF16), b.astype(BF16), (((2,), (1,)), ((0,), (0,))),
                           preferred_element_type=F32)


def _bdot_nt(a, b):
    return lax.dot_general(a.astype(BF16), b.astype(BF16), (((2,), (2,)), ((0,), (0,))),
                           preferred_element_type=F32)


def _unit_lower_inverse_minus_eye(a, r, c):
    blk16 = (r // 16) == (c // 16)
    blk32 = (r // 32) == (c // 32)
    p = jnp.where(blk16, -a, 0.0)
    dt = p
    for _ in range(3):
        p = _bdot(p, p)
        dt = dt + p + _bdot(dt, p)
    for off in (jnp.where(jnp.logical_and(blk32, jnp.logical_not(blk16)), a, 0.0),
                jnp.where(blk32, 0.0, a)):
        x = off + _bdot(dt, off)
        dt = dt - (x + _bdot(x, dt))
    return dt


def _gdn_prompt_body(qkv_ref, sm_ref, gcr_ref, z_ref, gn_ref, o_ref, sout_ref, s_ref):
    ci = pl.program_id(1)

    @pl.when(ci == 0)
    def _():
        s_ref[...] = jnp.zeros_like(s_ref)

    ch = GDN_CHUNK
    d = GDN_HEAD_DIM
    nh = GDN_HEADS
    n_chunks = qkv_ref.shape[0] // ch
    units = [(cidx, hh) for cidx in range(n_chunks) for hh in range(nh)]

    def gather(fn):
        return jnp.stack([fn(slice(cidx * ch, (cidx + 1) * ch), hh) for cidx, hh in units], axis=0)

    q = gather(lambda rows, hh: qkv_ref[rows, hh * d:(hh + 1) * d])
    k = gather(lambda rows, hh: qkv_ref[rows, GDN_WIDTH + hh * d:GDN_WIDTH + (hh + 1) * d])
    v = gather(lambda rows, hh: qkv_ref[rows, 2 * GDN_WIDTH + hh * d:2 * GDN_WIDTH + (hh + 1) * d])
    beta = gather(lambda rows, hh: sm_ref[rows, SM_BETA + hh:SM_BETA + hh + 1])
    gc = gather(lambda rows, hh: sm_ref[rows, SM_GCUM + hh:SM_GCUM + hh + 1])
    gr = gather(lambda rows, hh: gcr_ref[hh:hh + 1, rows])

    r = lax.broadcasted_iota(jnp.int32, (1, ch, ch), 1)
    c = lax.broadcasted_iota(jnp.int32, (1, ch, ch), 2)
    lower = r >= c
    decay = jnp.where(lower, jnp.exp(jnp.where(lower, gc - gr, 0.0)), 0.0)
    qk_kk = _bdot_nt(jnp.concatenate([q, k], axis=1), k)
    qk = qk_kk[:, :ch] * decay
    a = jnp.where(r > c, beta * qk_kk[:, ch:] * decay, 0.0)
    dt = _unit_lower_inverse_minus_eye(a, r, c)
    eg = jnp.exp(gc)
    rhs = jnp.concatenate([v * beta, k * (beta * eg)], axis=-1)
    uw = rhs + _bdot(dt, rhs)
    gl = gc[:, ch - 1:ch, :]
    qd = (q * eg).astype(BF16)
    kd = k * jnp.exp(gl - gc)
    g_last = jnp.exp(gl)
    qk16 = qk.astype(BF16)

    s = s_ref[...]
    for cidx in range(n_chunks):
        us = slice(cidx * nh, (cidx + 1) * nh)
        rows = slice(cidx * ch, (cidx + 1) * ch)
        s16 = s.astype(BF16)
        v_new = uw[us, :, :d] - _bdot(uw[us, :, d:], s16)
        o = _bdot(qd[us], s16) + _bdot(qk16[us], v_new)
        vn16 = v_new.astype(BF16)
        upd = jnp.stack([_dot(kd[cidx * nh + hh].T.astype(BF16), vn16[hh]) for hh in range(nh)], axis=0)
        s = s * g_last[us] + upd
        for hh in range(nh):
            hs = slice(hh * d, (hh + 1) * d)
            o_ref[rows, hs] = (_rms(o[hh], gn_ref[...]) * _silu(z_ref[rows, hs])).astype(BF16)
    s_ref[...] = s

    @pl.when(ci == pl.num_programs(1) - 1)
    def _():
        sout_ref[0] = s


def _gdn_prompt(gqkv, sm, smt, gz, gn, n_seq, seq_len):
    tm = ROW_TILE
    nt = seq_len // tm
    gcum_block = SM_GCUM // SUBLANES
    rows = gqkv.shape[0]
    return pl.pallas_call(
        _gdn_prompt_body,
        grid=(n_seq, nt),
        in_specs=[pl.BlockSpec((tm, 3 * GDN_WIDTH), lambda b, i: (b * nt + i, 0)),
                  pl.BlockSpec((tm, LANES), lambda b, i: (b * nt + i, 0)),
                  pl.BlockSpec((SUBLANES, tm), lambda b, i: (gcum_block, b * nt + i)),
                  pl.BlockSpec((tm, GDN_WIDTH), lambda b, i: (b * nt + i, 0)),
                  pl.BlockSpec((1, GDN_HEAD_DIM), lambda b, i: (0, 0))],
        out_specs=(pl.BlockSpec((tm, GDN_WIDTH), lambda b, i: (b * nt + i, 0)),
                   pl.BlockSpec((1, GDN_HEADS, GDN_HEAD_DIM, GDN_HEAD_DIM), lambda b, i: (b, 0, 0, 0))),
        out_shape=(jax.ShapeDtypeStruct((rows, GDN_WIDTH), BF16),
                   jax.ShapeDtypeStruct((n_seq, GDN_HEADS, GDN_HEAD_DIM, GDN_HEAD_DIM), F32)),
        scratch_shapes=[pltpu.VMEM((GDN_HEADS, GDN_HEAD_DIM, GDN_HEAD_DIM), F32)],
        compiler_params=pltpu.CompilerParams(dimension_semantics=("arbitrary", "arbitrary"),
                                             vmem_limit_bytes=32 * MIB),
        name="gdn_prompt",
    )(gqkv, sm, smt, gz, gn)


def _ffn_tail(x, fox_ref, gdn_ref, wo_ref, nf_ref, wup_ref, wd_ref, nfin_ref, conv):
    mix = _dot(fox_ref[...], wo_ref[0:FOX_WIDTH, :]) + _dot(gdn_ref[...], wo_ref[FOX_WIDTH:, :])
    x2 = x + mix
    h2 = _rms(x2, nf_ref[...]).astype(BF16)
    gu = _dot(h2, wup_ref[...])
    gate, up = gu[:, :FFN_DIM], gu[:, FFN_DIM:]
    act = (_silu(conv(gate)) * up).astype(BF16)
    x3 = x2 + _dot(act, wd_ref[...])
    return _rms(x3, nfin_ref[...])


def _ffn_prompt_body(x_ref, fox_ref, gdn_ref, wo_ref, nf_ref, wup_ref, cw_ref, cb_ref, wd_ref, nfin_ref,
                     y_ref, cst_ref, prev_ref, *, tiles_per_seq):
    i = pl.program_id(0)

    @pl.when(i % tiles_per_seq == 0)
    def _():
        prev_ref[...] = jnp.zeros_like(prev_ref)

    tm = x_ref.shape[0]

    def conv(gate):
        prev = prev_ref[...]
        out = gate * cw_ref[FFN_CONV - 1:FFN_CONV, :] + cb_ref[...]
        for kk in range(1, FFN_CONV):
            out = out + _shift_rows(gate, prev, kk) * cw_ref[FFN_CONV - 1 - kk:FFN_CONV - kk, :]
        prev_ref[...] = gate[tm - SUBLANES:, :]
        cst_ref[...] = gate[tm - SUBLANES:, :]
        return out

    y_ref[...] = _ffn_tail(x_ref[...], fox_ref, gdn_ref, wo_ref, nf_ref, wup_ref, wd_ref, nfin_ref, conv)


def _ffn_prompt(x, fox, gdn, wo, nf, wup, cw, cb, wd, nfin, seq_len):
    rows = x.shape[0]
    tm = ROW_TILE
    n_seq = rows // seq_len
    tiles_per_seq = seq_len // tm
    row_spec = lambda w: pl.BlockSpec((tm, w), lambda i: (i, 0))
    const = lambda a: pl.BlockSpec(a.shape, lambda i: (0, 0), pipeline_mode=pl.Buffered(1))
    return pl.pallas_call(
        functools.partial(_ffn_prompt_body, tiles_per_seq=tiles_per_seq),
        grid=(rows // tm,),
        in_specs=[row_spec(D_MODEL), row_spec(FOX_WIDTH), row_spec(GDN_WIDTH), const(wo), const(nf),
                  const(wup), const(cw), const(cb), const(wd), const(nfin)],
        out_specs=(row_spec(D_MODEL),
                   pl.BlockSpec((SUBLANES, FFN_DIM), lambda i: (i // tiles_per_seq, 0))),
        out_shape=(jax.ShapeDtypeStruct((rows, D_MODEL), F32),
                   jax.ShapeDtypeStruct((n_seq * SUBLANES, FFN_DIM), F32)),
        scratch_shapes=[pltpu.VMEM((SUBLANES, FFN_DIM), F32)],
        compiler_params=pltpu.CompilerParams(dimension_semantics=("arbitrary",), vmem_limit_bytes=56 * MIB),
        name="ffn_prompt",
    )(x, fox, gdn, wo, nf, wup, cw, cb, wd, nfin)


def _ffn_sample_body(x_ref, fox_ref, gdn_ref, wo_ref, nf_ref, wup_ref, cw_ref, cb_ref, wd_ref, nfin_ref,
                     c0_ref, c1_ref, y_ref, gate_ref):
    def conv(gate):
        gate_ref[...] = gate
        return (gate * cw_ref[2:3, :] + c1_ref[...] * cw_ref[1:2, :] + c0_ref[...] * cw_ref[0:1, :]
                + cb_ref[...])

    y_ref[...] = _ffn_tail(x_ref[...], fox_ref, gdn_ref, wo_ref, nf_ref, wup_ref, wd_ref, nfin_ref, conv)


def _ffn_sample(x, fox, gdn, wo, nf, wup, cw, cb, wd, nfin, c0, c1):
    rows = x.shape[0]
    args = (x, fox, gdn, wo, nf, wup, cw, cb, wd, nfin, c0, c1)
    full = lambda a: pl.BlockSpec(a.shape, lambda i: (0, 0), pipeline_mode=pl.Buffered(1))
    return pl.pallas_call(
        _ffn_sample_body,
        grid=(1,),
        in_specs=[full(a) for a in args],
        out_specs=(pl.BlockSpec((rows, D_MODEL), lambda i: (0, 0)),
                   pl.BlockSpec((rows, FFN_DIM), lambda i: (0, 0))),
        out_shape=(jax.ShapeDtypeStruct((rows, D_MODEL), F32), jax.ShapeDtypeStruct((rows, FFN_DIM), F32)),
        compiler_params=pltpu.CompilerParams(dimension_semantics=("arbitrary",), vmem_limit_bytes=56 * MIB),
        name="ffn_sample",
    )(*args)


def _split3(x):
    hi = x.astype(BF16)
    r1 = x - hi.astype(F32)
    mid = r1.astype(BF16)
    lo = (r1 - mid.astype(F32)).astype(BF16)
    return hi, mid, lo


def _decode_body(pt_ref, q_ref, kn_ref, vn_ref, sm_ref, msuf_ref, kt_pool, vt_pool, lf_pool,
                 o_ref, kbuf, vbuf, lbuf, sems, *, n_pages):
    b = pl.program_id(0)
    nb = pl.num_programs(0)
    slot = b % 2
    hd = FOX_HEAD_DIM
    page = kbuf.shape[-1]

    def copies(bb, sl):
        out = []
        for j in range(n_pages):
            pg = pt_ref[bb * n_pages + j]
            out.append(pltpu.make_async_copy(kt_pool.at[pg], kbuf.at[sl, j], sems.at[sl, 0]))
            out.append(pltpu.make_async_copy(vt_pool.at[pg], vbuf.at[sl, j], sems.at[sl, 1]))
            out.append(pltpu.make_async_copy(lf_pool.at[pg], lbuf.at[sl, j], sems.at[sl, 2]))
        return out

    @pl.when(b == 0)
    def _():
        for cp in copies(0, 0):
            cp.start()

    @pl.when(b + 1 < nb)
    def _():
        for cp in copies(b + 1, 1 - slot):
            cp.start()

    for cp in copies(b, slot):
        cp.wait()

    r8 = lax.broadcasted_iota(jnp.int32, (SUBLANES, LANES), 0)
    c8 = lax.broadcasted_iota(jnp.int32, (SUBLANES, LANES), 1)
    lf_new = jnp.sum(jnp.where(r8 == c8, sm_ref[0], 0.0), axis=1, keepdims=True)
    n_rows = n_pages * FOX_HEADS
    lf = lbuf[slot].reshape(n_rows, page)
    msuf = msuf_ref[...]
    within = functools.reduce(lambda a, c: a + c, [_dot(t, msuf) for t in _split3(lf)])
    tot = jnp.broadcast_to(jnp.sum(lf, axis=1, keepdims=True), (n_rows, page))
    row = lax.broadcasted_iota(jnp.int32, (n_rows, page), 0)
    later = tot
    k = FOX_HEADS
    while k < n_rows:
        later = later + jnp.where(row + k < n_rows, pltpu.roll(later, n_rows - k, 0), 0.0)
        k *= 2
    bias = (within + (later - tot)).reshape(n_pages, FOX_HEADS, page) + lf_new[None]

    q = q_ref[0]
    q_t = q.T
    qb = jnp.stack([jnp.broadcast_to(q_t[:, h:h + 1], (hd, page)) for h in range(FOX_HEADS)], axis=0)
    logits = [jnp.sum(kbuf[slot, j] * qb, axis=1) + bias[j] for j in range(n_pages)]
    s_new = jnp.sum(kn_ref[0] * q, axis=-1, keepdims=True)
    m = functools.reduce(jnp.maximum, [jnp.max(x, axis=-1, keepdims=True) for x in logits] + [s_new])
    ps = [jnp.exp(x - m) for x in logits]
    p_new = jnp.exp(s_new - m)
    l = functools.reduce(lambda a, c: a + c, [jnp.sum(x, axis=-1, keepdims=True) for x in ps] + [p_new])

    cols = []
    for h in range(FOX_HEADS):
        acc = vbuf[slot, 0, h] * ps[0][h:h + 1, :]
        for j in range(1, n_pages):
            acc = acc + vbuf[slot, j, h] * ps[j][h:h + 1, :]
        cols.append(jnp.sum(acc, axis=1, keepdims=True))
    mat = jnp.concatenate(cols + [jnp.zeros((hd, LANES - FOX_HEADS), F32)], axis=1)
    o_ref[0] = (mat.T[0:FOX_HEADS, :] + p_new * vn_ref[0]) / l


def _decode_attention(page_table, q, kn, vn, sm, kt_pool, vt_pool, lf_pool):
    nb, n_pages = page_table.shape
    page = kt_pool.shape[-1]
    tok = jnp.arange(page)
    msuf = (tok[:, None] > tok[None, :]).astype(BF16)
    tile = lambda a: pl.BlockSpec((1,) + a.shape[1:], lambda i, pt: (i,) + (0,) * (a.ndim - 1))
    hbm = pl.BlockSpec(memory_space=pl.ANY)
    sm = sm.reshape(nb, 1, LANES)
    grid_spec = pltpu.PrefetchScalarGridSpec(
        num_scalar_prefetch=1,
        grid=(nb,),
        in_specs=[tile(q), tile(kn), tile(vn), tile(sm), pl.BlockSpec(msuf.shape, lambda i, pt: (0, 0)),
                  hbm, hbm, hbm],
        out_specs=pl.BlockSpec((1, FOX_HEADS, FOX_HEAD_DIM), lambda i, pt: (i, 0, 0)),
        scratch_shapes=[pltpu.VMEM((2, n_pages, FOX_HEADS, FOX_HEAD_DIM, page), F32),
                        pltpu.VMEM((2, n_pages, FOX_HEADS, FOX_HEAD_DIM, page), F32),
                        pltpu.VMEM((2, n_pages, FOX_HEADS, page), F32),
                        pltpu.SemaphoreType.DMA((2, 3))],
    )
    return pl.pallas_call(
        functools.partial(_decode_body, n_pages=n_pages),
        grid_spec=grid_spec,
        out_shape=jax.ShapeDtypeStruct((nb, FOX_HEADS, FOX_HEAD_DIM), F32),
        compiler_params=pltpu.CompilerParams(dimension_semantics=("arbitrary",), vmem_limit_bytes=48 * MIB),
        name="fox_decode",
    )(page_table.reshape(-1), q, kn, vn, sm, msuf, kt_pool, vt_pool, lf_pool)


def _gdn_sample_body(s_ref, qkv_ref, sm_ref, z_ref, gn_ref, sout_ref, o_ref):
    d = GDN_HEAD_DIM
    eye = lax.broadcasted_iota(jnp.int32, (d, d), 0) == lax.broadcasted_iota(jnp.int32, (d, d), 1)

    def col(rows):
        return jnp.sum(jnp.where(eye[None], rows[:, None, :], 0.0), axis=2, keepdims=True)

    for hh in range(GDN_HEADS):
        hs = slice(hh * d, (hh + 1) * d)
        q = qkv_ref[:, hs]
        k = qkv_ref[:, GDN_WIDTH + hh * d:GDN_WIDTH + (hh + 1) * d]
        v = qkv_ref[:, 2 * GDN_WIDTH + hh * d:2 * GDN_WIDTH + (hh + 1) * d]
        eg = jnp.exp(sm_ref[:, SM_G + hh:SM_G + hh + 1])
        beta = sm_ref[:, SM_BETA + hh:SM_BETA + hh + 1]
        s = s_ref[:, hh]
        v_new = v * beta - jnp.sum(col(k * (beta * eg)) * s, axis=1)
        o = jnp.sum(col(q * eg) * s, axis=1) + jnp.sum(q * k, axis=1, keepdims=True) * v_new
        sout_ref[:, hh] = s * eg[:, :, None] + col(k) * v_new[:, None, :]
        o_ref[:, hs] = _rms(o, gn_ref[...]) * _silu(z_ref[:, hs])


def _gdn_sample(state, gqkv, sm, gz, gn):
    nb = state.shape[0]
    bb = SUBLANES
    d = GDN_HEAD_DIM
    row_spec = lambda w: pl.BlockSpec((bb, w), lambda i: (i, 0))
    st_spec = pl.BlockSpec((bb, GDN_HEADS, d, d), lambda i: (i, 0, 0, 0))
    return pl.pallas_call(
        _gdn_sample_body,
        grid=(nb // bb,),
        in_specs=[st_spec, row_spec(3 * GDN_WIDTH), row_spec(LANES), row_spec(GDN_WIDTH),
                  pl.BlockSpec((1, d), lambda i: (0, 0))],
        out_specs=(st_spec, row_spec(GDN_WIDTH)),
        out_shape=(jax.ShapeDtypeStruct(state.shape, F32), jax.ShapeDtypeStruct((nb, GDN_WIDTH), F32)),
        compiler_params=pltpu.CompilerParams(dimension_semantics=("arbitrary",), vmem_limit_bytes=32 * MIB),
        name="gdn_sample",
    )(state, gqkv, sm, gz, gn)


def _pad_rows(a, rows):
    return jnp.concatenate([a, jnp.zeros((rows - a.shape[0],) + a.shape[1:], a.dtype)], axis=0)


def kernel(x_prompt, x_sample, cache_k, cache_v, cache_logf, state_gdn, state_gdn_conv, state_ffn_conv,
           page_table, norm_mix, w_in, b_forget, gdn_a_log, gdn_dt_bias, w_gdn_conv, gdn_out_norm, w_out,
           norm_ffn, w_up, w_ffn_conv, b_ffn_conv, w_down, norm_final):
    assert w_in.shape[0] == 1, "single-layer trunk"
    n_seq, seq_len, _ = x_prompt.shape
    nb = x_sample.shape[0]
    n_pool, page = cache_k.shape[1], cache_k.shape[2]

    w = w_in[0]
    o_ff = 3 * FOX_WIDTH
    o_g = o_ff + FOX_HEADS
    o_ga = o_g + 3 * GDN_WIDTH
    o_gz = o_ga + 2 * GDN_HEADS
    wbig = jnp.concatenate([w[:, :o_ff], w[:, o_g:o_ga], w[:, o_gz:]], axis=1).astype(BF16)
    wsm = jnp.concatenate([w[:, o_ff:o_g], w[:, o_ga:o_gz],
                           jnp.zeros((D_MODEL, LANES - FOX_HEADS - 2 * GDN_HEADS), F32)], axis=1).astype(BF16)
    par = jnp.zeros((SUBLANES, LANES), F32)
    par = par.at[0, SM_LF:SM_LF + FOX_HEADS].set(b_forget[0])
    par = par.at[0, SM_G:SM_G + GDN_HEADS].set(gdn_dt_bias[0])
    par = par.at[1, SM_G:SM_G + GDN_HEADS].set(gdn_a_log[0])
    nrm = norm_mix[0][None, :]
    cw = _pad_rows(w_gdn_conv[0], SUBLANES)
    gn = gdn_out_norm[0][None, :]
    wo = w_out[0].astype(BF16)
    nf = norm_ffn[0][None, :]
    wup = w_up[0].astype(BF16)
    cwf = _pad_rows(w_ffn_conv[0], SUBLANES)
    cbf = b_ffn_conv[0][None, :]
    wd = w_down[0].astype(BF16)
    nfin = norm_final[None, :]

    xp = x_prompt.reshape(n_seq * seq_len, D_MODEL)
    fk, fv, q16, k16, v16, gqkv, gz, sm, smt, cst = _proj_prompt(xp, nrm, wbig, wsm, par, cw, seq_len)
    fox = _fox_prompt(q16, k16, v16, smt, n_seq, seq_len)
    gdn, s_p = _gdn_prompt(gqkv, sm, smt, gz, gn, n_seq, seq_len)
    yp, cstf = _ffn_prompt(xp, fox, gdn, wo, nf, wup, cwf, cbf, wd, nfin, seq_len)

    xs = x_sample.reshape(nb, D_MODEL)
    gctx = state_gdn_conv[0]
    fctx = state_ffn_conv[0]
    fk_s, fv_s, q_s, gqkv_s, gz_s, sm_s, pre_s = _proj_sample(
        xs, nrm, wbig, wsm, par, cw, gctx[:, 0], gctx[:, 1], gctx[:, 2])
    heads = lambda a: a.reshape(nb, FOX_HEADS, FOX_HEAD_DIM)
    fox_s = _decode_attention(page_table, heads(q_s), heads(fk_s), heads(fv_s), sm_s,
                              jnp.transpose(cache_k[0], (0, 2, 3, 1)), jnp.transpose(cache_v[0], (0, 2, 3, 1)),
                              jnp.transpose(cache_logf[0], (0, 2, 1)))
    s_s, gdn_s = _gdn_sample(state_gdn[0], gqkv_s, sm_s, gz_s, gn)
    ys, gate_s = _ffn_sample(xs, fox_s.reshape(nb, FOX_WIDTH).astype(BF16), gdn_s.astype(BF16), wo, nf, wup,
                             cwf, cbf, wd, nfin, fctx[:, 0], fctx[:, 1])

    kv_shape_p = (1, n_seq, seq_len, FOX_HEADS, FOX_HEAD_DIM)
    kv_shape_s = (1, nb, 1, FOX_HEADS, FOX_HEAD_DIM)
    return (
        yp.reshape(n_seq, seq_len, D_MODEL),
        ys.reshape(nb, 1, D_MODEL),
        fk.reshape(kv_shape_p),
        fv.reshape(kv_shape_p),
        sm[:, SM_LF:SM_LF + FOX_HEADS].reshape(1, n_seq, seq_len, FOX_HEADS),
        s_p[None],
        cst.reshape(n_seq, SUBLANES, 3 * GDN_WIDTH)[None, :, SUBLANES - (GDN_CONV - 1):],
        cstf.reshape(n_seq, SUBLANES, FFN_DIM)[None, :, SUBLANES - (FFN_CONV - 1):],
        fk_s.reshape(kv_shape_s),
        fv_s.reshape(kv_shape_s),
        sm_s[:, SM_LF:SM_LF + FOX_HEADS].reshape(1, nb, 1, FOX_HEADS),
        s_s[None],
        jnp.concatenate([gctx[:, 1:], pre_s[:, None, :]], axis=1)[None],
        jnp.concatenate([fctx[:, 1:], gate_s[:, None, :]], axis=1)[None],
    )
```

```python
import functools

import jax
import jax.numpy as jnp
from jax import lax
from jax.experimental import pallas as pl
from jax.experimental.pallas import tpu as pltpu

D_MODEL = 1024
FOX_HEADS = 8
FOX_HEAD_DIM = 64
FOX_WIDTH = FOX_HEADS * FOX_HEAD_DIM
GDN_HEADS = 4
GDN_HEAD_DIM = 128
GDN_WIDTH = GDN_HEADS * GDN_HEAD_DIM
GDN_CONV = 4
GDN_CHUNK = 64
FFN_DIM = 2816
FFN_CONV = 3
EPS = 1e-6
NEG_BIG = -1e30
FOX_SCALE = FOX_HEAD_DIM ** -0.5
GDN_SCALE = GDN_HEAD_DIM ** -0.5
LOG2E = 1.4426950408889634
FOX_QSCALE = FOX_SCALE * LOG2E
FOX_AUG = 3

LANES = 128
SUBLANES = 8
ROW_TILE = 256
FOX_TILE = 256
MIB = 1024 * 1024

SM_LF = 0
SM_G = 8
SM_BETA = 12
SM_CUM = 16
SM_GCUM = 24

F32 = jnp.float32
BF16 = jnp.bfloat16


def _sigmoid(x):
    return 1.0 / (1.0 + jnp.exp(-x))


def _silu(x):
    return x * _sigmoid(x)


def _rms(x, g):
    return x * lax.rsqrt(jnp.mean(x * x, axis=-1, keepdims=True) + EPS) * g


def _dot(a, b):
    return jnp.dot(a, b, preferred_element_type=F32)


def _dot_nt(a, b):
    return lax.dot_general(a, b, (((1,), (1,)), ((), ())), preferred_element_type=F32)


def _hdot(a, b):
    return jnp.dot(a, b, preferred_element_type=F32, precision=lax.Precision.HIGHEST)


def _shift_rows(x, prev8, k):
    r = pltpu.roll(x, k, 0)
    row8 = lax.broadcasted_iota(jnp.int32, prev8.shape, 0)
    top = jnp.where(row8 < k, pltpu.roll(prev8, k, 0), r[0:SUBLANES])
    return jnp.concatenate([top, r[SUBLANES:]], axis=0)


def _gate_activations(raw, par_ref):
    z = raw + par_ref[0:1, :]
    lane = lax.broadcasted_iota(jnp.int32, z.shape, 1)
    t = jnp.log1p(jnp.exp(-jnp.abs(z)))
    lf = jnp.minimum(z, 0.0) - t
    softplus = jnp.maximum(z, 0.0) + t
    g = -jnp.exp(par_ref[1:2, :]) * softplus
    beta = _sigmoid(z)
    return jnp.where(lane < SM_G, lf, jnp.where(lane < SM_BETA, g, jnp.where(lane < SM_CUM, beta, 0.0)))


def _gdn_qkv_norm(c, out_ref):
    for part, scale in ((0, GDN_SCALE), (1, 1.0)):
        for hh in range(GDN_HEADS):
            off = part * GDN_WIDTH + hh * GDN_HEAD_DIM
            seg = c[:, off:off + GDN_HEAD_DIM]
            n = lax.rsqrt(jnp.sum(seg * seg, axis=-1, keepdims=True) + EPS)
            out_ref[:, off:off + GDN_HEAD_DIM] = seg * n * scale
    out_ref[:, 2 * GDN_WIDTH:] = c[:, 2 * GDN_WIDTH:]


def _proj_prompt_body(x_ref, nrm_ref, wbig_ref, wsm_ref, par_ref, cw_ref, scat_ref,
                      kt_ref, vt_ref, qt16_ref, kaug_ref, vt16_ref, gqkv_ref, gz_ref,
                      sm_ref, smt_ref, cst_ref, carry_ref, prev_ref, *, tiles_per_seq):
    i = pl.program_id(0)

    @pl.when(i % tiles_per_seq == 0)
    def _():
        carry_ref[...] = jnp.zeros_like(carry_ref)
        prev_ref[...] = jnp.zeros_like(prev_ref)

    tm = x_ref.shape[0]
    h16 = _rms(x_ref[...], nrm_ref[...]).astype(BF16)

    fox = _dot(h16, wbig_ref[:, 0:3 * FOX_WIDTH])
    fq, fk, fv = fox[:, :FOX_WIDTH], fox[:, FOX_WIDTH:2 * FOX_WIDTH], fox[:, 2 * FOX_WIDTH:]
    fv_t = fv.T
    kt_ref[0] = fk.T
    vt_ref[0] = fv_t
    vt16_ref[0] = fv_t.astype(BF16)
    lane_p = lax.broadcasted_iota(jnp.int32, (tm, LANES), 1)
    ones_blk = jnp.where(lane_p < 2 * FOX_AUG, 1.0, 0.0)
    q_aug = []
    for p in range(FOX_HEADS // 2):
        q_aug += [fq[:, p * LANES:(p + 1) * LANES] * FOX_QSCALE, ones_blk]
        kaug_ref[:, 2 * p * LANES:(2 * p + 1) * LANES] = fk[:, p * LANES:(p + 1) * LANES].astype(BF16)
    qt16_ref[0] = jnp.concatenate(q_aug, axis=1).T.astype(BF16)

    act = _gate_activations(_dot(h16, wsm_ref[...]), par_ref)
    row = lax.broadcasted_iota(jnp.int32, act.shape, 0)
    lane = lax.broadcasted_iota(jnp.int32, act.shape, 1)
    y = act
    yc = act
    k = 1
    while k < tm:
        y = y + jnp.where(row >= k, pltpu.roll(y, k, 0), 0.0)
        if k < GDN_CHUNK:
            yc = yc + jnp.where((row & (GDN_CHUNK - 1)) >= k, pltpu.roll(yc, k, 0), 0.0)
        k *= 2
    y = y + carry_ref[0:1, :]
    carry_ref[0:1, :] = y[tm - 1:tm, :]
    terms = jnp.concatenate(_split3(y * (-LOG2E)), axis=1)
    extra = _dot(terms, scat_ref[...]).astype(BF16)
    for p in range(FOX_HEADS // 2):
        kaug_ref[:, (2 * p + 1) * LANES:(2 * p + 2) * LANES] = extra[:, p * LANES:(p + 1) * LANES]
    shift = SM_CUM - SM_LF
    sm = jnp.where(lane < SM_CUM, act,
                   jnp.where(lane < SM_GCUM, pltpu.roll(y, shift, 1),
                             jnp.where(lane < SM_GCUM + GDN_HEADS, pltpu.roll(yc, shift, 1), 0.0)))
    sm_ref[...] = sm
    smt_ref[...] = sm.T

    pre = _dot(h16, wbig_ref[:, 3 * FOX_WIDTH:3 * FOX_WIDTH + 3 * GDN_WIDTH])
    prev = prev_ref[...]
    acc = pre * cw_ref[GDN_CONV - 1:GDN_CONV, :]
    for kk in range(1, GDN_CONV):
        acc = acc + _shift_rows(pre, prev, kk) * cw_ref[GDN_CONV - 1 - kk:GDN_CONV - kk, :]
    prev_ref[...] = pre[tm - SUBLANES:, :]
    cst_ref[...] = pre[tm - SUBLANES:, :]
    _gdn_qkv_norm(_silu(acc), gqkv_ref)

    gz_ref[...] = _dot(h16, wbig_ref[:, 3 * FOX_WIDTH + 3 * GDN_WIDTH:])


def _proj_prompt(x, nrm, wbig, wsm, par, cw, seq_len):
    rows = x.shape[0]
    tm = ROW_TILE
    n_seq = rows // seq_len
    tiles_per_seq = seq_len // tm
    wb = wbig.shape[1]
    row_spec = lambda w: pl.BlockSpec((tm, w), lambda i: (i, 0))
    col_spec = lambda w: pl.BlockSpec((1, w, tm), lambda i: (i // tiles_per_seq, 0, i % tiles_per_seq))
    const = lambda s: pl.BlockSpec(s, lambda i: (0, 0))
    aug_width = FOX_HEADS * LANES
    src = jnp.arange(3 * LANES)
    term, head = src // LANES, src % LANES
    dst = (head // 2) * LANES + (head % 2) * FOX_AUG + term
    scat = ((jnp.arange(FOX_WIDTH)[None, :] == dst[:, None]) & (head[:, None] < FOX_HEADS)).astype(BF16)
    out_shape = (
        jax.ShapeDtypeStruct((n_seq, FOX_WIDTH, seq_len), F32),
        jax.ShapeDtypeStruct((n_seq, FOX_WIDTH, seq_len), F32),
        jax.ShapeDtypeStruct((n_seq, aug_width, seq_len), BF16),
        jax.ShapeDtypeStruct((rows, aug_width), BF16),
        jax.ShapeDtypeStruct((n_seq, FOX_WIDTH, seq_len), BF16),
        jax.ShapeDtypeStruct((rows, 3 * GDN_WIDTH), F32),
        jax.ShapeDtypeStruct((rows, GDN_WIDTH), F32),
        jax.ShapeDtypeStruct((rows, LANES), F32),
        jax.ShapeDtypeStruct((LANES, rows), F32),
        jax.ShapeDtypeStruct((n_seq * SUBLANES, 3 * GDN_WIDTH), F32),
    )
    out_specs = (
        col_spec(FOX_WIDTH), col_spec(FOX_WIDTH), col_spec(aug_width), row_spec(aug_width),
        col_spec(FOX_WIDTH), row_spec(3 * GDN_WIDTH), row_spec(GDN_WIDTH), row_spec(LANES),
        pl.BlockSpec((LANES, tm), lambda i: (0, i)),
        pl.BlockSpec((SUBLANES, 3 * GDN_WIDTH), lambda i: (i // tiles_per_seq, 0)),
    )
    return pl.pallas_call(
        functools.partial(_proj_prompt_body, tiles_per_seq=tiles_per_seq),
        grid=(rows // tm,),
        in_specs=[row_spec(D_MODEL), const((1, D_MODEL)), const((D_MODEL, wb)), const((D_MODEL, LANES)),
                  const((SUBLANES, LANES)), const((SUBLANES, 3 * GDN_WIDTH)), const(scat.shape)],
        out_specs=out_specs,
        out_shape=out_shape,
        scratch_shapes=[pltpu.VMEM((SUBLANES, LANES), F32), pltpu.VMEM((SUBLANES, 3 * GDN_WIDTH), F32)],
        compiler_params=pltpu.CompilerParams(dimension_semantics=("arbitrary",), vmem_limit_bytes=48 * MIB),
        name="proj_prompt",
    )(x, nrm, wbig, wsm, par, cw, scat)


def _proj_sample_body(x_ref, nrm_ref, wbig_ref, wsm_ref, par_ref, cw_ref, c0_ref, c1_ref, c2_ref,
                      fk_ref, fv_ref, q_ref, gqkv_ref, gz_ref, sm_ref, pre_ref):
    h16 = _rms(x_ref[...], nrm_ref[...]).astype(BF16)
    fox = _dot(h16, wbig_ref[:, 0:3 * FOX_WIDTH])
    q_ref[...] = fox[:, :FOX_WIDTH] * FOX_SCALE
    fk_ref[...] = fox[:, FOX_WIDTH:2 * FOX_WIDTH]
    fv_ref[...] = fox[:, 2 * FOX_WIDTH:]
    sm_ref[...] = _gate_activations(_dot(h16, wsm_ref[...]), par_ref)
    pre = _dot(h16, wbig_ref[:, 3 * FOX_WIDTH:3 * FOX_WIDTH + 3 * GDN_WIDTH])
    pre_ref[...] = pre
    acc = (pre * cw_ref[3:4, :] + c2_ref[...] * cw_ref[2:3, :]
           + c1_ref[...] * cw_ref[1:2, :] + c0_ref[...] * cw_ref[0:1, :])
    _gdn_qkv_norm(_silu(acc), gqkv_ref)
    gz_ref[...] = _dot(h16, wbig_ref[:, 3 * FOX_WIDTH + 3 * GDN_WIDTH:])


def _proj_sample(x, nrm, wbig, wsm, par, cw, c0, c1, c2):
    rows = x.shape[0]
    full = lambda a: pl.BlockSpec(a.shape, lambda i: (0,) * a.ndim)
    args = (x, nrm, wbig, wsm, par, cw, c0, c1, c2)
    shapes = ((rows, FOX_WIDTH), (rows, FOX_WIDTH), (rows, FOX_WIDTH), (rows, 3 * GDN_WIDTH),
              (rows, GDN_WIDTH), (rows, LANES), (rows, 3 * GDN_WIDTH))
    return pl.pallas_call(
        _proj_sample_body,
        grid=(1,),
        in_specs=[full(a) for a in args],
        out_specs=tuple(pl.BlockSpec(s, lambda i: (0, 0)) for s in shapes),
        out_shape=tuple(jax.ShapeDtypeStruct(s, F32) for s in shapes),
        compiler_params=pltpu.CompilerParams(dimension_semantics=("arbitrary",), vmem_limit_bytes=48 * MIB),
        name="proj_sample",
    )(*args)


def _fox_prompt_body(qt_ref, kaug_ref, vt_ref, o_ref):
    i = pl.program_id(1)
    t = FOX_TILE
    hd = FOX_HEAD_DIM
    aug = 2 * LANES
    rr = lax.broadcasted_iota(jnp.int32, (t, t), 0)
    cc = lax.broadcasted_iota(jnp.int32, (t, t), 1)
    causal = rr <= cc
    row = lax.broadcasted_iota(jnp.int32, (aug, t), 0)
    qs = []
    for h in range(FOX_HEADS):
        e = h % 2
        qp = qt_ref[0, (h // 2) * aug:(h // 2 + 1) * aug, :]
        own = jnp.logical_or(jnp.logical_and(row >= e * hd, row < (e + 1) * hd),
                             jnp.logical_and(row >= LANES + e * FOX_AUG, row < LANES + (e + 1) * FOX_AUG))
        qs.append(jnp.where(own, qp, jnp.zeros_like(qp)))

    def tile(j, carry, masked):
        koff = pl.multiple_of(j * t, t)
        ss = [_dot(kaug_ref[pl.ds(koff, t), (h // 2) * aug:(h // 2 + 1) * aug], qs[h])
              for h in range(FOX_HEADS)]
        stats, pms = [], []
        for h in range(FOX_HEADS):
            m, l, _ = carry[h]
            s = jnp.where(causal, ss[h], NEG_BIG) if masked else ss[h]
            m_new = jnp.maximum(m, jnp.max(s, axis=0, keepdims=True))
            alpha = jnp.exp2(m - m_new)
            pm = jnp.exp2(s - m_new)
            stats.append((m_new, alpha, alpha * l + jnp.sum(pm, axis=0, keepdims=True)))
            pms.append(pm.astype(BF16))
        out = []
        for h in range(FOX_HEADS):
            m_new, alpha, l_new = stats[h]
            a_new = alpha * carry[h][2] + _dot(vt_ref[0, h * hd:(h + 1) * hd, pl.ds(koff, t)], pms[h])
            out.append((m_new, l_new, a_new))
        return tuple(out)

    init = tuple((jnp.full((1, t), NEG_BIG, F32), jnp.zeros((1, t), F32), jnp.zeros((hd, t), F32))
                 for _ in range(FOX_HEADS))
    carry = lax.fori_loop(0, i, lambda j, c: tile(j, c, False), init)
    final = tile(i, carry, True)
    for p in range(FOX_HEADS // 2):
        (_, l0, a0), (_, l1, a1) = final[2 * p], final[2 * p + 1]
        o_ref[:, p * LANES:(p + 1) * LANES] = jnp.concatenate([a0 / l0, a1 / l1], axis=0).T.astype(BF16)


def _fox_prompt(qt16, kaug, vt16, n_seq, seq_len):
    t = FOX_TILE
    nq = seq_len // t
    aug_width = kaug.shape[1]
    return pl.pallas_call(
        _fox_prompt_body,
        grid=(n_seq, nq),
        in_specs=[pl.BlockSpec((1, aug_width, t), lambda b, i: (b, 0, i)),
                  pl.BlockSpec((seq_len, aug_width), lambda b, i: (b, 0)),
                  pl.BlockSpec((1, FOX_WIDTH, seq_len), lambda b, i: (b, 0, 0))],
        out_specs=pl.BlockSpec((t, FOX_WIDTH), lambda b, i: (b * nq + i, 0)),
        out_shape=jax.ShapeDtypeStruct((n_seq * seq_len, FOX_WIDTH), BF16),
        compiler_params=pltpu.CompilerParams(dimension_semantics=("arbitrary", "arbitrary"),
                                             vmem_limit_bytes=40 * MIB),
        name="fox_prompt",
    )(qt16, kaug, vt16)


def _bdot(a, b):
    return lax.dot_general(a.astype(BF16), b.astype(BF16), (((2,), (1,)), ((0,), (0,))),
                           preferred_element_type=F32)


def _bdot_nt(a, b):
    return lax.dot_general(a.astype(BF16), b.astype(BF16), (((2,), (2,)), ((0,), (0,))),
                           preferred_element_type=F32)


def _unit_lower_inverse_minus_eye(a, r, c):
    blk16 = (r // 16) == (c // 16)
    blk32 = (r // 32) == (c // 32)
    p = jnp.where(blk16, -a, 0.0)
    dt = p
    for _ in range(3):
        p = _bdot(p, p)
        dt = dt + p + _bdot(dt, p)
    for off in (jnp.where(jnp.logical_and(blk32, jnp.logical_not(blk16)), a, 0.0),
                jnp.where(blk32, 0.0, a)):
        x = off + _bdot(dt, off)
        dt = dt - (x + _bdot(x, dt))
    return dt


def _gdn_prompt_body(qkv_ref, sm_ref, gcr_ref, z_ref, gn_ref, o_ref, sout_ref, s_ref):
    ci = pl.program_id(1)

    @pl.when(ci == 0)
    def _():
        s_ref[...] = jnp.zeros_like(s_ref)

    ch = GDN_CHUNK
    d = GDN_HEAD_DIM
    nh = GDN_HEADS
    n_chunks = qkv_ref.shape[0] // ch
    units = [(cidx, hh) for cidx in range(n_chunks) for hh in range(nh)]

    def gather(fn):
        return jnp.stack([fn(slice(cidx * ch, (cidx + 1) * ch), hh) for cidx, hh in units], axis=0)

    q = gather(lambda rows, hh: qkv_ref[rows, hh * d:(hh + 1) * d])
    k = gather(lambda rows, hh: qkv_ref[rows, GDN_WIDTH + hh * d:GDN_WIDTH + (hh + 1) * d])
    v = gather(lambda rows, hh: qkv_ref[rows, 2 * GDN_WIDTH + hh * d:2 * GDN_WIDTH + (hh + 1) * d])
    beta = gather(lambda rows, hh: sm_ref[rows, SM_BETA + hh:SM_BETA + hh + 1])
    gc = gather(lambda rows, hh: sm_ref[rows, SM_GCUM + hh:SM_GCUM + hh + 1])
    gr = gather(lambda rows, hh: gcr_ref[hh:hh + 1, rows])

    r = lax.broadcasted_iota(jnp.int32, (1, ch, ch), 1)
    c = lax.broadcasted_iota(jnp.int32, (1, ch, ch), 2)
    lower = r >= c
    decay = jnp.where(lower, jnp.exp(jnp.where(lower, gc - gr, 0.0)), 0.0)
    qk_kk = _bdot_nt(jnp.concatenate([q, k], axis=1), k)
    qk = qk_kk[:, :ch] * decay
    a = jnp.where(r > c, beta * qk_kk[:, ch:] * decay, 0.0)
    dt = _unit_lower_inverse_minus_eye(a, r, c)
    eg = jnp.exp(gc)
    rhs = jnp.concatenate([v * beta, k * (beta * eg)], axis=-1)
    uw = rhs + _bdot(dt, rhs)
    gl = gc[:, ch - 1:ch, :]
    qd = (q * eg).astype(BF16)
    kd = k * jnp.exp(gl - gc)
    g_last = jnp.exp(gl)
    qk16 = qk.astype(BF16)

    s = s_ref[...]
    for cidx in range(n_chunks):
        us = slice(cidx * nh, (cidx + 1) * nh)
        rows = slice(cidx * ch, (cidx + 1) * ch)
        s16 = s.astype(BF16)
        v_new = uw[us, :, :d] - _bdot(uw[us, :, d:], s16)
        o = _bdot(qd[us], s16) + _bdot(qk16[us], v_new)
        vn16 = v_new.astype(BF16)
        upd = jnp.stack([_dot(kd[cidx * nh + hh].T.astype(BF16), vn16[hh]) for hh in range(nh)], axis=0)
        s = s * g_last[us] + upd
        for hh in range(nh):
            hs = slice(hh * d, (hh + 1) * d)
            o_ref[rows, hs] = (_rms(o[hh], gn_ref[...]) * _silu(z_ref[rows, hs])).astype(BF16)
    s_ref[...] = s

    @pl.when(ci == pl.num_programs(1) - 1)
    def _():
        sout_ref[0] = s


def _gdn_prompt(gqkv, sm, smt, gz, gn, n_seq, seq_len):
    tm = ROW_TILE
    nt = seq_len // tm
    gcum_block = SM_GCUM // SUBLANES
    rows = gqkv.shape[0]
    return pl.pallas_call(
        _gdn_prompt_body,
        grid=(n_seq, nt),
        in_specs=[pl.BlockSpec((tm, 3 * GDN_WIDTH), lambda b, i: (b * nt + i, 0)),
                  pl.BlockSpec((tm, LANES), lambda b, i: (b * nt + i, 0)),
                  pl.BlockSpec((SUBLANES, tm), lambda b, i: (gcum_block, b * nt + i)),
                  pl.BlockSpec((tm, GDN_WIDTH), lambda b, i: (b * nt + i, 0)),
                  pl.BlockSpec((1, GDN_HEAD_DIM), lambda b, i: (0, 0))],
        out_specs=(pl.BlockSpec((tm, GDN_WIDTH), lambda b, i: (b * nt + i, 0)),
                   pl.BlockSpec((1, GDN_HEADS, GDN_HEAD_DIM, GDN_HEAD_DIM), lambda b, i: (b, 0, 0, 0))),
        out_shape=(jax.ShapeDtypeStruct((rows, GDN_WIDTH), BF16),
                   jax.ShapeDtypeStruct((n_seq, GDN_HEADS, GDN_HEAD_DIM, GDN_HEAD_DIM), F32)),
        scratch_shapes=[pltpu.VMEM((GDN_HEADS, GDN_HEAD_DIM, GDN_HEAD_DIM), F32)],
        compiler_params=pltpu.CompilerParams(dimension_semantics=("arbitrary", "arbitrary"),
                                             vmem_limit_bytes=32 * MIB),
        name="gdn_prompt",
    )(gqkv, sm, smt, gz, gn)


def _ffn_tail(x, fox_ref, gdn_ref, wo_ref, nf_ref, wup_ref, wd_ref, nfin_ref, conv):
    mix = _dot(fox_ref[...], wo_ref[0:FOX_WIDTH, :]) + _dot(gdn_ref[...], wo_ref[FOX_WIDTH:, :])
    x2 = x + mix
    h2 = _rms(x2, nf_ref[...]).astype(BF16)
    gu = _dot(h2, wup_ref[...])
    gate, up = gu[:, :FFN_DIM], gu[:, FFN_DIM:]
    act = (_silu(conv(gate)) * up).astype(BF16)
    x3 = x2 + _dot(act, wd_ref[...])
    return _rms(x3, nfin_ref[...])


def _ffn_prompt_body(x_ref, fox_ref, gdn_ref, wo_ref, nf_ref, wup_ref, cw_ref, cb_ref, wd_ref, nfin_ref,
                     y_ref, cst_ref, prev_ref, *, tiles_per_seq):
    i = pl.program_id(0)

    @pl.when(i % tiles_per_seq == 0)
    def _():
        prev_ref[...] = jnp.zeros_like(prev_ref)

    tm = x_ref.shape[0]

    def conv(gate):
        prev = prev_ref[...]
        out = gate * cw_ref[FFN_CONV - 1:FFN_CONV, :] + cb_ref[...]
        for kk in range(1, FFN_CONV):
            out = out + _shift_rows(gate, prev, kk) * cw_ref[FFN_CONV - 1 - kk:FFN_CONV - kk, :]
        prev_ref[...] = gate[tm - SUBLANES:, :]
        cst_ref[...] = gate[tm - SUBLANES:, :]
        return out

    y_ref[...] = _ffn_tail(x_ref[...], fox_ref, gdn_ref, wo_ref, nf_ref, wup_ref, wd_ref, nfin_ref, conv)


def _ffn_prompt(x, fox, gdn, wo, nf, wup, cw, cb, wd, nfin, seq_len):
    rows = x.shape[0]
    tm = ROW_TILE
    n_seq = rows // seq_len
    tiles_per_seq = seq_len // tm
    row_spec = lambda w: pl.BlockSpec((tm, w), lambda i: (i, 0))
    const = lambda a: pl.BlockSpec(a.shape, lambda i: (0, 0), pipeline_mode=pl.Buffered(1))
    return pl.pallas_call(
        functools.partial(_ffn_prompt_body, tiles_per_seq=tiles_per_seq),
        grid=(rows // tm,),
        in_specs=[row_spec(D_MODEL), row_spec(FOX_WIDTH), row_spec(GDN_WIDTH), const(wo), const(nf),
                  const(wup), const(cw), const(cb), const(wd), const(nfin)],
        out_specs=(row_spec(D_MODEL),
                   pl.BlockSpec((SUBLANES, FFN_DIM), lambda i: (i // tiles_per_seq, 0))),
        out_shape=(jax.ShapeDtypeStruct((rows, D_MODEL), F32),
                   jax.ShapeDtypeStruct((n_seq * SUBLANES, FFN_DIM), F32)),
        scratch_shapes=[pltpu.VMEM((SUBLANES, FFN_DIM), F32)],
        compiler_params=pltpu.CompilerParams(dimension_semantics=("arbitrary",), vmem_limit_bytes=56 * MIB),
        name="ffn_prompt",
    )(x, fox, gdn, wo, nf, wup, cw, cb, wd, nfin)


def _ffn_sample_body(x_ref, fox_ref, gdn_ref, wo_ref, nf_ref, wup_ref, cw_ref, cb_ref, wd_ref, nfin_ref,
                     c0_ref, c1_ref, y_ref, gate_ref):
    def conv(gate):
        gate_ref[...] = gate
        return (gate * cw_ref[2:3, :] + c1_ref[...] * cw_ref[1:2, :] + c0_ref[...] * cw_ref[0:1, :]
                + cb_ref[...])

    y_ref[...] = _ffn_tail(x_ref[...], fox_ref, gdn_ref, wo_ref, nf_ref, wup_ref, wd_ref, nfin_ref, conv)


def _ffn_sample(x, fox, gdn, wo, nf, wup, cw, cb, wd, nfin, c0, c1):
    rows = x.shape[0]
    args = (x, fox, gdn, wo, nf, wup, cw, cb, wd, nfin, c0, c1)
    full = lambda a: pl.BlockSpec(a.shape, lambda i: (0, 0), pipeline_mode=pl.Buffered(1))
    return pl.pallas_call(
        _ffn_sample_body,
        grid=(1,),
        in_specs=[full(a) for a in args],
        out_specs=(pl.BlockSpec((rows, D_MODEL), lambda i: (0, 0)),
                   pl.BlockSpec((rows, FFN_DIM), lambda i: (0, 0))),
        out_shape=(jax.ShapeDtypeStruct((rows, D_MODEL), F32), jax.ShapeDtypeStruct((rows, FFN_DIM), F32)),
        compiler_params=pltpu.CompilerParams(dimension_semantics=("arbitrary",), vmem_limit_bytes=56 * MIB),
        name="ffn_sample",
    )(*args)


def _split3(x):
    hi = x.astype(BF16)
    r1 = x - hi.astype(F32)
    mid = r1.astype(BF16)
    lo = (r1 - mid.astype(F32)).astype(BF16)
    return hi, mid, lo


def _decode_body(pt_ref, q_ref, kn_ref, vn_ref, sm_ref, msuf_ref, kt_pool, vt_pool, lf_pool,
                 o_ref, kbuf, vbuf, lbuf, sems, *, n_pages):
    b = pl.program_id(0)
    nb = pl.num_programs(0)
    slot = b % 2
    hd = FOX_HEAD_DIM
    page = kbuf.shape[-1]

    def copies(bb, sl):
        out = []
        for j in range(n_pages):
            pg = pt_ref[bb * n_pages + j]
            out.append(pltpu.make_async_copy(kt_pool.at[pg], kbuf.at[sl, j], sems.at[sl, 0]))
            out.append(pltpu.make_async_copy(vt_pool.at[pg], vbuf.at[sl, j], sems.at[sl, 1]))
            out.append(pltpu.make_async_copy(lf_pool.at[pg], lbuf.at[sl, j], sems.at[sl, 2]))
        return out

    @pl.when(b == 0)
    def _():
        for cp in copies(0, 0):
            cp.start()

    @pl.when(b + 1 < nb)
    def _():
        for cp in copies(b + 1, 1 - slot):
            cp.start()

    for cp in copies(b, slot):
        cp.wait()

    r8 = lax.broadcasted_iota(jnp.int32, (SUBLANES, LANES), 0)
    c8 = lax.broadcasted_iota(jnp.int32, (SUBLANES, LANES), 1)
    lf_new = jnp.sum(jnp.where(r8 == c8, sm_ref[0], 0.0), axis=1, keepdims=True)
    n_rows = n_pages * FOX_HEADS
    lf = lbuf[slot].reshape(n_rows, page)
    msuf = msuf_ref[...]
    within = functools.reduce(lambda a, c: a + c, [_dot(t, msuf) for t in _split3(lf)])
    tot = jnp.broadcast_to(jnp.sum(lf, axis=1, keepdims=True), (n_rows, page))
    row = lax.broadcasted_iota(jnp.int32, (n_rows, page), 0)
    later = tot
    k = FOX_HEADS
    while k < n_rows:
        later = later + jnp.where(row + k < n_rows, pltpu.roll(later, n_rows - k, 0), 0.0)
        k *= 2
    bias = (within + (later - tot)).reshape(n_pages, FOX_HEADS, page) + lf_new[None]

    q = q_ref[0]
    q_t = q.T
    qb = jnp.stack([jnp.broadcast_to(q_t[:, h:h + 1], (hd, page)) for h in range(FOX_HEADS)], axis=0)
    logits = [jnp.sum(kbuf[slot, j] * qb, axis=1) + bias[j] for j in range(n_pages)]
    s_new = jnp.sum(kn_ref[0] * q, axis=-1, keepdims=True)
    m = functools.reduce(jnp.maximum, [jnp.max(x, axis=-1, keepdims=True) for x in logits] + [s_new])
    ps = [jnp.exp(x - m) for x in logits]
    p_new = jnp.exp(s_new - m)
    l = functools.reduce(lambda a, c: a + c, [jnp.sum(x, axis=-1, keepdims=True) for x in ps] + [p_new])

    cols = []
    for h in range(FOX_HEADS):
        acc = vbuf[slot, 0, h] * ps[0][h:h + 1, :]
        for j in range(1, n_pages):
            acc = acc + vbuf[slot, j, h] * ps[j][h:h + 1, :]
        cols.append(jnp.sum(acc, axis=1, keepdims=True))
    mat = jnp.concatenate(cols + [jnp.zeros((hd, LANES - FOX_HEADS), F32)], axis=1)
    o_ref[0] = (mat.T[0:FOX_HEADS, :] + p_new * vn_ref[0]) / l


def _decode_attention(page_table, q, kn, vn, sm, kt_pool, vt_pool, lf_pool):
    nb, n_pages = page_table.shape
    page = kt_pool.shape[-1]
    tok = jnp.arange(page)
    msuf = (tok[:, None] > tok[None, :]).astype(BF16)
    tile = lambda a: pl.BlockSpec((1,) + a.shape[1:], lambda i, pt: (i,) + (0,) * (a.ndim - 1))
    hbm = pl.BlockSpec(memory_space=pl.ANY)
    sm = sm.reshape(nb, 1, LANES)
    grid_spec = pltpu.PrefetchScalarGridSpec(
        num_scalar_prefetch=1,
        grid=(nb,),
        in_specs=[tile(q), tile(kn), tile(vn), tile(sm), pl.BlockSpec(msuf.shape, lambda i, pt: (0, 0)),
                  hbm, hbm, hbm],
        out_specs=pl.BlockSpec((1, FOX_HEADS, FOX_HEAD_DIM), lambda i, pt: (i, 0, 0)),
        scratch_shapes=[pltpu.VMEM((2, n_pages, FOX_HEADS, FOX_HEAD_DIM, page), F32),
                        pltpu.VMEM((2, n_pages, FOX_HEADS, FOX_HEAD_DIM, page), F32),
                        pltpu.VMEM((2, n_pages, FOX_HEADS, page), F32),
                        pltpu.SemaphoreType.DMA((2, 3))],
    )
    return pl.pallas_call(
        functools.partial(_decode_body, n_pages=n_pages),
        grid_spec=grid_spec,
        out_shape=jax.ShapeDtypeStruct((nb, FOX_HEADS, FOX_HEAD_DIM), F32),
        compiler_params=pltpu.CompilerParams(dimension_semantics=("arbitrary",), vmem_limit_bytes=48 * MIB),
        name="fox_decode",
    )(page_table.reshape(-1), q, kn, vn, sm, msuf, kt_pool, vt_pool, lf_pool)


def _gdn_sample_body(s_ref, qkv_ref, sm_ref, z_ref, gn_ref, sout_ref, o_ref):
    d = GDN_HEAD_DIM
    eye = lax.broadcasted_iota(jnp.int32, (d, d), 0) == lax.broadcasted_iota(jnp.int32, (d, d), 1)

    def col(rows):
        return jnp.sum(jnp.where(eye[None], rows[:, None, :], 0.0), axis=2, keepdims=True)

    for hh in range(GDN_HEADS):
        hs = slice(hh * d, (hh + 1) * d)
        q = qkv_ref[:, hs]
        k = qkv_ref[:, GDN_WIDTH + hh * d:GDN_WIDTH + (hh + 1) * d]
        v = qkv_ref[:, 2 * GDN_WIDTH + hh * d:2 * GDN_WIDTH + (hh + 1) * d]
        eg = jnp.exp(sm_ref[:, SM_G + hh:SM_G + hh + 1])
        beta = sm_ref[:, SM_BETA + hh:SM_BETA + hh + 1]
        s = s_ref[:, hh]
        v_new = v * beta - jnp.sum(col(k * (beta * eg)) * s, axis=1)
        o = jnp.sum(col(q * eg) * s, axis=1) + jnp.sum(q * k, axis=1, keepdims=True) * v_new
        sout_ref[:, hh] = s * eg[:, :, None] + col(k) * v_new[:, None, :]
        o_ref[:, hs] = _rms(o, gn_ref[...]) * _silu(z_ref[:, hs])


def _gdn_sample(state, gqkv, sm, gz, gn):
    nb = state.shape[0]
    bb = SUBLANES
    d = GDN_HEAD_DIM
    row_spec = lambda w: pl.BlockSpec((bb, w), lambda i: (i, 0))
    st_spec = pl.BlockSpec((bb, GDN_HEADS, d, d), lambda i: (i, 0, 0, 0))
    return pl.pallas_call(
        _gdn_sample_body,
        grid=(nb // bb,),
        in_specs=[st_spec, row_spec(3 * GDN_WIDTH), row_spec(LANES), row_spec(GDN_WIDTH),
                  pl.BlockSpec((1, d), lambda i: (0, 0))],
        out_specs=(st_spec, row_spec(GDN_WIDTH)),
        out_shape=(jax.ShapeDtypeStruct(state.shape, F32), jax.ShapeDtypeStruct((nb, GDN_WIDTH), F32)),
        compiler_params=pltpu.CompilerParams(dimension_semantics=("arbitrary",), vmem_limit_bytes=32 * MIB),
        name="gdn_sample",
    )(state, gqkv, sm, gz, gn)


def _pad_rows(a, rows):
    return jnp.concatenate([a, jnp.zeros((rows - a.shape[0],) + a.shape[1:], a.dtype)], axis=0)


def kernel(x_prompt, x_sample, cache_k, cache_v, cache_logf, state_gdn, state_gdn_conv, state_ffn_conv,
           page_table, norm_mix, w_in, b_forget, gdn_a_log, gdn_dt_bias, w_gdn_conv, gdn_out_norm, w_out,
           norm_ffn, w_up, w_ffn_conv, b_ffn_conv, w_down, norm_final):
    assert w_in.shape[0] == 1, "single-layer trunk"
    n_seq, seq_len, _ = x_prompt.shape
    nb = x_sample.shape[0]
    n_pool, page = cache_k.shape[1], cache_k.shape[2]

    w = w_in[0]
    o_ff = 3 * FOX_WIDTH
    o_g = o_ff + FOX_HEADS
    o_ga = o_g + 3 * GDN_WIDTH
    o_gz = o_ga + 2 * GDN_HEADS
    wbig = jnp.concatenate([w[:, :o_ff], w[:, o_g:o_ga], w[:, o_gz:]], axis=1).astype(BF16)
    wsm = jnp.concatenate([w[:, o_ff:o_g], w[:, o_ga:o_gz],
                           jnp.zeros((D_MODEL, LANES - FOX_HEADS - 2 * GDN_HEADS), F32)], axis=1).astype(BF16)
    par = jnp.zeros((SUBLANES, LANES), F32)
    par = par.at[0, SM_LF:SM_LF + FOX_HEADS].set(b_forget[0])
    par = par.at[0, SM_G:SM_G + GDN_HEADS].set(gdn_dt_bias[0])
    par = par.at[1, SM_G:SM_G + GDN_HEADS].set(gdn_a_log[0])
    nrm = norm_mix[0][None, :]
    cw = _pad_rows(w_gdn_conv[0], SUBLANES)
    gn = gdn_out_norm[0][None, :]
    wo = w_out[0].astype(BF16)
    nf = norm_ffn[0][None, :]
    wup = w_up[0].astype(BF16)
    cwf = _pad_rows(w_ffn_conv[0], SUBLANES)
    cbf = b_ffn_conv[0][None, :]
    wd = w_down[0].astype(BF16)
    nfin = norm_final[None, :]

    xp = x_prompt.reshape(n_seq * seq_len, D_MODEL)
    kt, vt, qt16, kaug, vt16, gqkv, gz, sm, smt, cst = _proj_prompt(xp, nrm, wbig, wsm, par, cw, seq_len)
    fox = _fox_prompt(qt16, kaug, vt16, n_seq, seq_len)
    gdn, s_p = _gdn_prompt(gqkv, sm, smt, gz, gn, n_seq, seq_len)
    yp, cstf = _ffn_prompt(xp, fox, gdn, wo, nf, wup, cwf, cbf, wd, nfin, seq_len)

    xs = x_sample.reshape(nb, D_MODEL)
    gctx = state_gdn_conv[0]
    fctx = state_ffn_conv[0]
    fk_s, fv_s, q_s, gqkv_s, gz_s, sm_s, pre_s = _proj_sample(
        xs, nrm, wbig, wsm, par, cw, gctx[:, 0], gctx[:, 1], gctx[:, 2])
    heads = lambda a: a.reshape(nb, FOX_HEADS, FOX_HEAD_DIM)
    fox_s = _decode_attention(page_table, heads(q_s), heads(fk_s), heads(fv_s), sm_s,
                              jnp.transpose(cache_k[0], (0, 2, 3, 1)), jnp.transpose(cache_v[0], (0, 2, 3, 1)),
                              jnp.transpose(cache_logf[0], (0, 2, 1)))
    s_s, gdn_s = _gdn_sample(state_gdn[0], gqkv_s, sm_s, gz_s, gn)
    ys, gate_s = _ffn_sample(xs, fox_s.reshape(nb, FOX_WIDTH).astype(BF16), gdn_s.astype(BF16), wo, nf, wup,
                             cwf, cbf, wd, nfin, fctx[:, 0], fctx[:, 1])

    kv_shape_s = (1, nb, 1, FOX_HEADS, FOX_HEAD_DIM)
    new_kv_p = lambda a: jnp.transpose(a.reshape(1, n_seq, FOX_HEADS, FOX_HEAD_DIM, seq_len), (0, 1, 4, 2, 3))
    return (
        yp.reshape(n_seq, seq_len, D_MODEL),
        ys.reshape(nb, 1, D_MODEL),
        new_kv_p(kt),
        new_kv_p(vt),
        sm[:, SM_LF:SM_LF + FOX_HEADS].reshape(1, n_seq, seq_len, FOX_HEADS),
        s_p[None],
        cst.reshape(n_seq, SUBLANES, 3 * GDN_WIDTH)[None, :, SUBLANES - (GDN_CONV - 1):],
        cstf.reshape(n_seq, SUBLANES, FFN_DIM)[None, :, SUBLANES - (FFN_CONV - 1):],
        fk_s.reshape(kv_shape_s),
        fv_s.reshape(kv_shape_s),
        sm_s[:, SM_LF:SM_LF + FOX_HEADS].reshape(1, nb, 1, FOX_HEADS),
        s_s[None],
        jnp.concatenate([gctx[:, 1:], pre_s[:, None, :]], axis=1)[None],
        jnp.concatenate([fctx[:, 1:], gate_s[:, None, :]], axis=1)[None],
    )
```

```python
import functools

import jax
import jax.numpy as jnp
from jax import lax
from jax.experimental import pallas as pl
from jax.experimental.pallas import tpu as pltpu

D_MODEL = 1024
FOX_HEADS = 8
FOX_HEAD_DIM = 64
FOX_WIDTH = FOX_HEADS * FOX_HEAD_DIM
GDN_HEADS = 4
GDN_HEAD_DIM = 128
GDN_WIDTH = GDN_HEADS * GDN_HEAD_DIM
GDN_CONV = 4
GDN_CHUNK = 64
FFN_DIM = 2816
FFN_CONV = 3
EPS = 1e-6
NEG_BIG = -1e30
FOX_SCALE = FOX_HEAD_DIM ** -0.5
GDN_SCALE = GDN_HEAD_DIM ** -0.5
LOG2E = 1.4426950408889634
FOX_QSCALE = FOX_SCALE * LOG2E
FOX_AUG = 3

LANES = 128
SUBLANES = 8
ROW_TILE = 256
FOX_TILE = 256
MIB = 1024 * 1024

SM_LF = 0
SM_G = 8
SM_BETA = 12
SM_CUM = 16
SM_GCUM = 24

F32 = jnp.float32
BF16 = jnp.bfloat16


def _sigmoid(x):
    return 1.0 / (1.0 + jnp.exp(-x))


def _silu(x):
    return x * _sigmoid(x)


def _rms(x, g):
    return x * lax.rsqrt(jnp.mean(x * x, axis=-1, keepdims=True) + EPS) * g


def _dot(a, b):
    return jnp.dot(a, b, preferred_element_type=F32)


def _dot_nt(a, b):
    return lax.dot_general(a, b, (((1,), (1,)), ((), ())), preferred_element_type=F32)


def _hdot(a, b):
    return jnp.dot(a, b, preferred_element_type=F32, precision=lax.Precision.HIGHEST)


def _shift_rows(x, prev8, k):
    r = pltpu.roll(x, k, 0)
    row8 = lax.broadcasted_iota(jnp.int32, prev8.shape, 0)
    top = jnp.where(row8 < k, pltpu.roll(prev8, k, 0), r[0:SUBLANES])
    return jnp.concatenate([top, r[SUBLANES:]], axis=0)


def _gate_activations(raw, par_ref):
    z = raw + par_ref[0:1, :]
    lane = lax.broadcasted_iota(jnp.int32, z.shape, 1)
    t = jnp.log1p(jnp.exp(-jnp.abs(z)))
    lf = jnp.minimum(z, 0.0) - t
    softplus = jnp.maximum(z, 0.0) + t
    g = -jnp.exp(par_ref[1:2, :]) * softplus
    beta = _sigmoid(z)
    return jnp.where(lane < SM_G, lf, jnp.where(lane < SM_BETA, g, jnp.where(lane < SM_CUM, beta, 0.0)))


def _gdn_qkv_norm(c, out_ref):
    for part, scale in ((0, GDN_SCALE), (1, 1.0)):
        for hh in range(GDN_HEADS):
            off = part * GDN_WIDTH + hh * GDN_HEAD_DIM
            seg = c[:, off:off + GDN_HEAD_DIM]
            n = lax.rsqrt(jnp.sum(seg * seg, axis=-1, keepdims=True) + EPS)
            out_ref[:, off:off + GDN_HEAD_DIM] = seg * n * scale
    out_ref[:, 2 * GDN_WIDTH:] = c[:, 2 * GDN_WIDTH:]


def _proj_prompt_body(pt_ref, *refs, tiles_per_seq, rider_cfg):
    (x_ref, nrm_ref, wbig_ref, wsm_ref, par_ref, cw_ref, scat_ref) = refs[:7]
    rider_in = refs[7:7 + RIDER_INPUTS]
    (kt_ref, vt_ref, qt16_ref, kaug_ref, vt16_ref, gqkv_ref, gz_ref, sm_ref, smt_ref, cst_ref,
     rider_out, carry_ref, prev_ref) = refs[7 + RIDER_INPUTS:20 + RIDER_INPUTS]
    rider_scratch = refs[20 + RIDER_INPUTS:]
    i = pl.program_id(0)

    @pl.when(i % tiles_per_seq == 0)
    def _():
        carry_ref[...] = jnp.zeros_like(carry_ref)
        prev_ref[...] = jnp.zeros_like(prev_ref)

    _rider_step(pt_ref, rider_in, rider_out, rider_scratch, rider_cfg)

    tm = x_ref.shape[0]
    h16 = _rms(x_ref[...], nrm_ref[...]).astype(BF16)

    fox = _dot(h16, wbig_ref[:, 0:3 * FOX_WIDTH])
    fq, fk, fv = fox[:, :FOX_WIDTH], fox[:, FOX_WIDTH:2 * FOX_WIDTH], fox[:, 2 * FOX_WIDTH:]
    fv_t = fv.T
    kt_ref[0] = fk.T
    vt_ref[0] = fv_t
    vt16_ref[0] = fv_t.astype(BF16)
    lane_p = lax.broadcasted_iota(jnp.int32, (tm, LANES), 1)
    ones_blk = jnp.where(lane_p < 2 * FOX_AUG, 1.0, 0.0)
    q_aug = []
    for p in range(FOX_HEADS // 2):
        q_aug += [fq[:, p * LANES:(p + 1) * LANES] * FOX_QSCALE, ones_blk]
        kaug_ref[:, 2 * p * LANES:(2 * p + 1) * LANES] = fk[:, p * LANES:(p + 1) * LANES].astype(BF16)
    qt16_ref[0] = jnp.concatenate(q_aug, axis=1).T.astype(BF16)

    act = _gate_activations(_dot(h16, wsm_ref[...]), par_ref)
    row = lax.broadcasted_iota(jnp.int32, act.shape, 0)
    lane = lax.broadcasted_iota(jnp.int32, act.shape, 1)
    y = act
    yc = act
    k = 1
    while k < tm:
        y = y + jnp.where(row >= k, pltpu.roll(y, k, 0), 0.0)
        if k < GDN_CHUNK:
            yc = yc + jnp.where((row & (GDN_CHUNK - 1)) >= k, pltpu.roll(yc, k, 0), 0.0)
        k *= 2
    y = y + carry_ref[0:1, :]
    carry_ref[0:1, :] = y[tm - 1:tm, :]
    terms = jnp.concatenate(_split3(y * (-LOG2E)), axis=1)
    extra = _dot(terms, scat_ref[...]).astype(BF16)
    for p in range(FOX_HEADS // 2):
        kaug_ref[:, (2 * p + 1) * LANES:(2 * p + 2) * LANES] = extra[:, p * LANES:(p + 1) * LANES]
    shift = SM_CUM - SM_LF
    sm = jnp.where(lane < SM_CUM, act,
                   jnp.where(lane < SM_GCUM, pltpu.roll(y, shift, 1),
                             jnp.where(lane < SM_GCUM + GDN_HEADS, pltpu.roll(yc, shift, 1), 0.0)))
    sm_ref[...] = sm
    smt_ref[...] = sm.T

    pre = _dot(h16, wbig_ref[:, 3 * FOX_WIDTH:3 * FOX_WIDTH + 3 * GDN_WIDTH])
    prev = prev_ref[...]
    acc = pre * cw_ref[GDN_CONV - 1:GDN_CONV, :]
    for kk in range(1, GDN_CONV):
        acc = acc + _shift_rows(pre, prev, kk) * cw_ref[GDN_CONV - 1 - kk:GDN_CONV - kk, :]
    prev_ref[...] = pre[tm - SUBLANES:, :]
    cst_ref[...] = pre[tm - SUBLANES:, :]
    _gdn_qkv_norm(_silu(acc), gqkv_ref)

    gz_ref[...] = _dot(h16, wbig_ref[:, 3 * FOX_WIDTH + 3 * GDN_WIDTH:])


def _proj_prompt(x, nrm, wbig, wsm, par, cw, seq_len, make_rider):
    rows = x.shape[0]
    tm = ROW_TILE
    n_seq = rows // seq_len
    tiles_per_seq = seq_len // tm
    wb = wbig.shape[1]
    row_spec = lambda w: pl.BlockSpec((tm, w), lambda i, pt: (i, 0))
    col_spec = lambda w: pl.BlockSpec((1, w, tm), lambda i, pt: (i // tiles_per_seq, 0, i % tiles_per_seq))
    const = lambda s: pl.BlockSpec(s, lambda i, pt: (0, 0), pipeline_mode=pl.Buffered(1))
    rider = make_rider(rows // tm)
    aug_width = FOX_HEADS * LANES
    src = jnp.arange(3 * LANES)
    term, head = src // LANES, src % LANES
    dst = (head // 2) * LANES + (head % 2) * FOX_AUG + term
    scat = ((jnp.arange(FOX_WIDTH)[None, :] == dst[:, None]) & (head[:, None] < FOX_HEADS)).astype(BF16)
    out_shape = (
        jax.ShapeDtypeStruct((n_seq, FOX_WIDTH, seq_len), F32),
        jax.ShapeDtypeStruct((n_seq, FOX_WIDTH, seq_len), F32),
        jax.ShapeDtypeStruct((n_seq, aug_width, seq_len), BF16),
        jax.ShapeDtypeStruct((rows, aug_width), BF16),
        jax.ShapeDtypeStruct((n_seq, FOX_WIDTH, seq_len), BF16),
        jax.ShapeDtypeStruct((rows, 3 * GDN_WIDTH), F32),
        jax.ShapeDtypeStruct((rows, GDN_WIDTH), F32),
        jax.ShapeDtypeStruct((rows, LANES), F32),
        jax.ShapeDtypeStruct((LANES, rows), F32),
        jax.ShapeDtypeStruct((n_seq * SUBLANES, 3 * GDN_WIDTH), F32),
    )
    out_specs = (
        col_spec(FOX_WIDTH), col_spec(FOX_WIDTH), col_spec(aug_width), row_spec(aug_width),
        col_spec(FOX_WIDTH), row_spec(3 * GDN_WIDTH), row_spec(GDN_WIDTH), row_spec(LANES),
        pl.BlockSpec((LANES, tm), lambda i, pt: (0, i)),
        pl.BlockSpec((SUBLANES, 3 * GDN_WIDTH), lambda i, pt: (i // tiles_per_seq, 0)),
    )
    grid_spec = pltpu.PrefetchScalarGridSpec(
        num_scalar_prefetch=1,
        grid=(rows // tm,),
        in_specs=[row_spec(D_MODEL), const((1, D_MODEL)), const((D_MODEL, wb)), const((D_MODEL, LANES)),
                  const((SUBLANES, LANES)), const((SUBLANES, 3 * GDN_WIDTH)), const(scat.shape)]
        + rider["in_specs"],
        out_specs=out_specs + (rider["out_spec"],),
        scratch_shapes=[pltpu.VMEM((SUBLANES, LANES), F32), pltpu.VMEM((SUBLANES, 3 * GDN_WIDTH), F32)]
        + rider["scratch"],
    )
    return pl.pallas_call(
        functools.partial(_proj_prompt_body, tiles_per_seq=tiles_per_seq, rider_cfg=rider["cfg"]),
        grid_spec=grid_spec,
        out_shape=out_shape + (rider["out_shape"],),
        compiler_params=pltpu.CompilerParams(dimension_semantics=("arbitrary",), vmem_limit_bytes=56 * MIB),
        name="proj_prompt",
    )(rider["prefetch"], x, nrm, wbig, wsm, par, cw, scat, *rider["operands"])


def _proj_sample_body(x_ref, nrm_ref, wbig_ref, wsm_ref, par_ref, cw_ref, c0_ref, c1_ref, c2_ref,
                      fk_ref, fv_ref, q_ref, gqkv_ref, gz_ref, sm_ref, pre_ref):
    h16 = _rms(x_ref[...], nrm_ref[...]).astype(BF16)
    fox = _dot(h16, wbig_ref[:, 0:3 * FOX_WIDTH])
    q_ref[...] = fox[:, :FOX_WIDTH] * FOX_SCALE
    fk_ref[...] = fox[:, FOX_WIDTH:2 * FOX_WIDTH]
    fv_ref[...] = fox[:, 2 * FOX_WIDTH:]
    sm_ref[...] = _gate_activations(_dot(h16, wsm_ref[...]), par_ref)
    pre = _dot(h16, wbig_ref[:, 3 * FOX_WIDTH:3 * FOX_WIDTH + 3 * GDN_WIDTH])
    pre_ref[...] = pre
    acc = (pre * cw_ref[3:4, :] + c2_ref[...] * cw_ref[2:3, :]
           + c1_ref[...] * cw_ref[1:2, :] + c0_ref[...] * cw_ref[0:1, :])
    _gdn_qkv_norm(_silu(acc), gqkv_ref)
    gz_ref[...] = _dot(h16, wbig_ref[:, 3 * FOX_WIDTH + 3 * GDN_WIDTH:])


def _proj_sample(x, nrm, wbig, wsm, par, cw, c0, c1, c2):
    rows = x.shape[0]
    full = lambda a: pl.BlockSpec(a.shape, lambda i: (0,) * a.ndim)
    args = (x, nrm, wbig, wsm, par, cw, c0, c1, c2)
    shapes = ((rows, FOX_WIDTH), (rows, FOX_WIDTH), (rows, FOX_WIDTH), (rows, 3 * GDN_WIDTH),
              (rows, GDN_WIDTH), (rows, LANES), (rows, 3 * GDN_WIDTH))
    return pl.pallas_call(
        _proj_sample_body,
        grid=(1,),
        in_specs=[full(a) for a in args],
        out_specs=tuple(pl.BlockSpec(s, lambda i: (0, 0)) for s in shapes),
        out_shape=tuple(jax.ShapeDtypeStruct(s, F32) for s in shapes),
        compiler_params=pltpu.CompilerParams(dimension_semantics=("arbitrary",), vmem_limit_bytes=48 * MIB),
        name="proj_sample",
    )(*args)


def _fox_prompt_body(qt_ref, kaug_ref, vt_ref, o_ref):
    i = pl.program_id(1)
    t = FOX_TILE
    hd = FOX_HEAD_DIM
    aug = 2 * LANES
    rr = lax.broadcasted_iota(jnp.int32, (t, t), 0)
    cc = lax.broadcasted_iota(jnp.int32, (t, t), 1)
    causal = rr <= cc
    row = lax.broadcasted_iota(jnp.int32, (aug, t), 0)
    qs = []
    for h in range(FOX_HEADS):
        e = h % 2
        qp = qt_ref[0, (h // 2) * aug:(h // 2 + 1) * aug, :]
        own = jnp.logical_or(jnp.logical_and(row >= e * hd, row < (e + 1) * hd),
                             jnp.logical_and(row >= LANES + e * FOX_AUG, row < LANES + (e + 1) * FOX_AUG))
        qs.append(jnp.where(own, qp, jnp.zeros_like(qp)))

    def tile(j, carry, masked):
        koff = pl.multiple_of(j * t, t)
        ss = [_dot(kaug_ref[pl.ds(koff, t), (h // 2) * aug:(h // 2 + 1) * aug], qs[h])
              for h in range(FOX_HEADS)]
        stats, pms = [], []
        for h in range(FOX_HEADS):
            m, l, _ = carry[h]
            s = jnp.where(causal, ss[h], NEG_BIG) if masked else ss[h]
            m_new = jnp.maximum(m, jnp.max(s, axis=0, keepdims=True))
            alpha = jnp.exp2(m - m_new)
            pm = jnp.exp2(s - m_new)
            stats.append((m_new, alpha, alpha * l + jnp.sum(pm, axis=0, keepdims=True)))
            pms.append(pm.astype(BF16))
        out = []
        for h in range(FOX_HEADS):
            m_new, alpha, l_new = stats[h]
            a_new = alpha * carry[h][2] + _dot(vt_ref[0, h * hd:(h + 1) * hd, pl.ds(koff, t)], pms[h])
            out.append((m_new, l_new, a_new))
        return tuple(out)

    init = tuple((jnp.full((1, t), NEG_BIG, F32), jnp.zeros((1, t), F32), jnp.zeros((hd, t), F32))
                 for _ in range(FOX_HEADS))
    carry = lax.fori_loop(0, i, lambda j, c: tile(j, c, False), init)
    final = tile(i, carry, True)
    for p in range(FOX_HEADS // 2):
        (_, l0, a0), (_, l1, a1) = final[2 * p], final[2 * p + 1]
        o_ref[:, p * LANES:(p + 1) * LANES] = jnp.concatenate([a0 / l0, a1 / l1], axis=0).T.astype(BF16)


def _fox_prompt(qt16, kaug, vt16, n_seq, seq_len):
    t = FOX_TILE
    nq = seq_len // t
    aug_width = kaug.shape[1]
    return pl.pallas_call(
        _fox_prompt_body,
        grid=(n_seq, nq),
        in_specs=[pl.BlockSpec((1, aug_width, t), lambda b, i: (b, 0, i)),
                  pl.BlockSpec((seq_len, aug_width), lambda b, i: (b, 0)),
                  pl.BlockSpec((1, FOX_WIDTH, seq_len), lambda b, i: (b, 0, 0))],
        out_specs=pl.BlockSpec((t, FOX_WIDTH), lambda b, i: (b * nq + i, 0)),
        out_shape=jax.ShapeDtypeStruct((n_seq * seq_len, FOX_WIDTH), BF16),
        compiler_params=pltpu.CompilerParams(dimension_semantics=("arbitrary", "arbitrary"),
                                             vmem_limit_bytes=40 * MIB),
        name="fox_prompt",
    )(qt16, kaug, vt16)


def _bdot(a, b):
    return lax.dot_general(a.astype(BF16), b.astype(BF16), (((2,), (1,)), ((0,), (0,))),
                           preferred_element_type=F32)


def _bdot_nt(a, b):
    return lax.dot_general(a.astype(BF16), b.astype(BF16), (((2,), (2,)), ((0,), (0,))),
                           preferred_element_type=F32)


def _unit_lower_inverse_minus_eye(a, r, c):
    blk16 = (r // 16) == (c // 16)
    blk32 = (r // 32) == (c // 32)
    p = jnp.where(blk16, -a, 0.0)
    dt = p
    for _ in range(3):
        p = _bdot(p, p)
        dt = dt + p + _bdot(dt, p)
    for off in (jnp.where(jnp.logical_and(blk32, jnp.logical_not(blk16)), a, 0.0),
                jnp.where(blk32, 0.0, a)):
        x = off + _bdot(dt, off)
        dt = dt - (x + _bdot(x, dt))
    return dt


def _gdn_prompt_body(qkv_ref, sm_ref, gcr_ref, z_ref, gn_ref, o_ref, sout_ref, s_ref):
    ci = pl.program_id(1)

    @pl.when(ci == 0)
    def _():
        s_ref[...] = jnp.zeros_like(s_ref)

    ch = GDN_CHUNK
    d = GDN_HEAD_DIM
    nh = GDN_HEADS
    n_chunks = qkv_ref.shape[0] // ch
    units = [(cidx, hh) for cidx in range(n_chunks) for hh in range(nh)]

    def gather(fn):
        return jnp.stack([fn(slice(cidx * ch, (cidx + 1) * ch), hh) for cidx, hh in units], axis=0)

    q = gather(lambda rows, hh: qkv_ref[rows, hh * d:(hh + 1) * d])
    k = gather(lambda rows, hh: qkv_ref[rows, GDN_WIDTH + hh * d:GDN_WIDTH + (hh + 1) * d])
    v = gather(lambda rows, hh: qkv_ref[rows, 2 * GDN_WIDTH + hh * d:2 * GDN_WIDTH + (hh + 1) * d])
    beta = gather(lambda rows, hh: sm_ref[rows, SM_BETA + hh:SM_BETA + hh + 1])
    gc = gather(lambda rows, hh: sm_ref[rows, SM_GCUM + hh:SM_GCUM + hh + 1])
    gr = gather(lambda rows, hh: gcr_ref[hh:hh + 1, rows])

    r = lax.broadcasted_iota(jnp.int32, (1, ch, ch), 1)
    c = lax.broadcasted_iota(jnp.int32, (1, ch, ch), 2)
    lower = r >= c
    decay = jnp.where(lower, jnp.exp(jnp.where(lower, gc - gr, 0.0)), 0.0)
    qk_kk = _bdot_nt(jnp.concatenate([q, k], axis=1), k)
    qk = qk_kk[:, :ch] * decay
    a = jnp.where(r > c, beta * qk_kk[:, ch:] * decay, 0.0)
    dt = _unit_lower_inverse_minus_eye(a, r, c)
    eg = jnp.exp(gc)
    rhs = jnp.concatenate([v * beta, k * (beta * eg)], axis=-1)
    uw = rhs + _bdot(dt, rhs)
    gl = gc[:, ch - 1:ch, :]
    qd = (q * eg).astype(BF16)
    kd = k * jnp.exp(gl - gc)
    g_last = jnp.exp(gl)
    qk16 = qk.astype(BF16)

    s = s_ref[...]
    for cidx in range(n_chunks):
        us = slice(cidx * nh, (cidx + 1) * nh)
        rows = slice(cidx * ch, (cidx + 1) * ch)
        s16 = s.astype(BF16)
        v_new = uw[us, :, :d] - _bdot(uw[us, :, d:], s16)
        o = _bdot(qd[us], s16) + _bdot(qk16[us], v_new)
        vn16 = v_new.astype(BF16)
        upd = jnp.stack([_dot(kd[cidx * nh + hh].T.astype(BF16), vn16[hh]) for hh in range(nh)], axis=0)
        s = s * g_last[us] + upd
        for hh in range(nh):
            hs = slice(hh * d, (hh + 1) * d)
            o_ref[rows, hs] = (_rms(o[hh], gn_ref[...]) * _silu(z_ref[rows, hs])).astype(BF16)
    s_ref[...] = s

    @pl.when(ci == pl.num_programs(1) - 1)
    def _():
        sout_ref[0] = s


def _gdn_prompt(gqkv, sm, smt, gz, gn, n_seq, seq_len):
    tm = ROW_TILE
    nt = seq_len // tm
    gcum_block = SM_GCUM // SUBLANES
    rows = gqkv.shape[0]
    return pl.pallas_call(
        _gdn_prompt_body,
        grid=(n_seq, nt),
        in_specs=[pl.BlockSpec((tm, 3 * GDN_WIDTH), lambda b, i: (b * nt + i, 0)),
                  pl.BlockSpec((tm, LANES), lambda b, i: (b * nt + i, 0)),
                  pl.BlockSpec((SUBLANES, tm), lambda b, i: (gcum_block, b * nt + i)),
                  pl.BlockSpec((tm, GDN_WIDTH), lambda b, i: (b * nt + i, 0)),
                  pl.BlockSpec((1, GDN_HEAD_DIM), lambda b, i: (0, 0))],
        out_specs=(pl.BlockSpec((tm, GDN_WIDTH), lambda b, i: (b * nt + i, 0)),
                   pl.BlockSpec((1, GDN_HEADS, GDN_HEAD_DIM, GDN_HEAD_DIM), lambda b, i: (b, 0, 0, 0))),
        out_shape=(jax.ShapeDtypeStruct((rows, GDN_WIDTH), BF16),
                   jax.ShapeDtypeStruct((n_seq, GDN_HEADS, GDN_HEAD_DIM, GDN_HEAD_DIM), F32)),
        scratch_shapes=[pltpu.VMEM((GDN_HEADS, GDN_HEAD_DIM, GDN_HEAD_DIM), F32)],
        compiler_params=pltpu.CompilerParams(dimension_semantics=("arbitrary", "arbitrary"),
                                             vmem_limit_bytes=32 * MIB),
        name="gdn_prompt",
    )(gqkv, sm, smt, gz, gn)


def _ffn_tail(x, fox_ref, gdn_ref, wo_ref, nf_ref, wup_ref, wd_ref, nfin_ref, conv):
    mix = _dot(fox_ref[...], wo_ref[0:FOX_WIDTH, :]) + _dot(gdn_ref[...], wo_ref[FOX_WIDTH:, :])
    x2 = x + mix
    h2 = _rms(x2, nf_ref[...]).astype(BF16)
    gu = _dot(h2, wup_ref[...])
    gate, up = gu[:, :FFN_DIM], gu[:, FFN_DIM:]
    act = (_silu(conv(gate)) * up).astype(BF16)
    x3 = x2 + _dot(act, wd_ref[...])
    return _rms(x3, nfin_ref[...])


def _ffn_prompt_body(pt_ref, *refs, tiles_per_seq, rider_cfg):
    (x_ref, fox_ref, gdn_ref, wo_ref, nf_ref, wup_ref, cw_ref, cb_ref, wd_ref, nfin_ref) = refs[:10]
    rider_in = refs[10:10 + RIDER_INPUTS]
    y_ref, cst_ref, rider_out, prev_ref = refs[10 + RIDER_INPUTS:14 + RIDER_INPUTS]
    rider_scratch = refs[14 + RIDER_INPUTS:]
    i = pl.program_id(0)

    @pl.when(i % tiles_per_seq == 0)
    def _():
        prev_ref[...] = jnp.zeros_like(prev_ref)

    _rider_step(pt_ref, rider_in, rider_out, rider_scratch, rider_cfg)

    tm = x_ref.shape[0]

    def conv(gate):
        prev = prev_ref[...]
        out = gate * cw_ref[FFN_CONV - 1:FFN_CONV, :] + cb_ref[...]
        for kk in range(1, FFN_CONV):
            out = out + _shift_rows(gate, prev, kk) * cw_ref[FFN_CONV - 1 - kk:FFN_CONV - kk, :]
        prev_ref[...] = gate[tm - SUBLANES:, :]
        cst_ref[...] = gate[tm - SUBLANES:, :]
        return out

    y_ref[...] = _ffn_tail(x_ref[...], fox_ref, gdn_ref, wo_ref, nf_ref, wup_ref, wd_ref, nfin_ref, conv)


def _ffn_prompt(x, fox, gdn, wo, nf, wup, cw, cb, wd, nfin, seq_len, make_rider):
    rows = x.shape[0]
    tm = ROW_TILE
    n_seq = rows // seq_len
    tiles_per_seq = seq_len // tm
    row_spec = lambda w: pl.BlockSpec((tm, w), lambda i, pt: (i, 0))
    const = lambda a: pl.BlockSpec(a.shape, lambda i, pt: (0, 0), pipeline_mode=pl.Buffered(1))
    rider = make_rider(rows // tm)
    grid_spec = pltpu.PrefetchScalarGridSpec(
        num_scalar_prefetch=1,
        grid=(rows // tm,),
        in_specs=[row_spec(D_MODEL), row_spec(FOX_WIDTH), row_spec(GDN_WIDTH), const(wo), const(nf),
                  const(wup), const(cw), const(cb), const(wd), const(nfin)] + rider["in_specs"],
        out_specs=(row_spec(D_MODEL),
                   pl.BlockSpec((SUBLANES, FFN_DIM), lambda i, pt: (i // tiles_per_seq, 0)),
                   rider["out_spec"]),
        scratch_shapes=[pltpu.VMEM((SUBLANES, FFN_DIM), F32)] + rider["scratch"],
    )
    return pl.pallas_call(
        functools.partial(_ffn_prompt_body, tiles_per_seq=tiles_per_seq, rider_cfg=rider["cfg"]),
        grid_spec=grid_spec,
        out_shape=(jax.ShapeDtypeStruct((rows, D_MODEL), F32),
                   jax.ShapeDtypeStruct((n_seq * SUBLANES, FFN_DIM), F32),
                   rider["out_shape"]),
        compiler_params=pltpu.CompilerParams(dimension_semantics=("arbitrary",), vmem_limit_bytes=58 * MIB),
        name="ffn_prompt",
    )(rider["prefetch"], x, fox, gdn, wo, nf, wup, cw, cb, wd, nfin, *rider["operands"])


def _ffn_sample_body(x_ref, fox_ref, gdn_ref, wo_ref, nf_ref, wup_ref, cw_ref, cb_ref, wd_ref, nfin_ref,
                     c0_ref, c1_ref, y_ref, gate_ref):
    def conv(gate):
        gate_ref[...] = gate
        return (gate * cw_ref[2:3, :] + c1_ref[...] * cw_ref[1:2, :] + c0_ref[...] * cw_ref[0:1, :]
                + cb_ref[...])

    y_ref[...] = _ffn_tail(x_ref[...], fox_ref, gdn_ref, wo_ref, nf_ref, wup_ref, wd_ref, nfin_ref, conv)


def _ffn_sample(x, fox, gdn, wo, nf, wup, cw, cb, wd, nfin, c0, c1):
    rows = x.shape[0]
    args = (x, fox, gdn, wo, nf, wup, cw, cb, wd, nfin, c0, c1)
    full = lambda a: pl.BlockSpec(a.shape, lambda i: (0, 0), pipeline_mode=pl.Buffered(1))
    return pl.pallas_call(
        _ffn_sample_body,
        grid=(1,),
        in_specs=[full(a) for a in args],
        out_specs=(pl.BlockSpec((rows, D_MODEL), lambda i: (0, 0)),
                   pl.BlockSpec((rows, FFN_DIM), lambda i: (0, 0))),
        out_shape=(jax.ShapeDtypeStruct((rows, D_MODEL), F32), jax.ShapeDtypeStruct((rows, FFN_DIM), F32)),
        compiler_params=pltpu.CompilerParams(dimension_semantics=("arbitrary",), vmem_limit_bytes=56 * MIB),
        name="ffn_sample",
    )(*args)


def _split3(x):
    hi = x.astype(BF16)
    r1 = x - hi.astype(F32)
    mid = r1.astype(BF16)
    lo = (r1 - mid.astype(F32)).astype(BF16)
    return hi, mid, lo


RIDER_INPUTS = 8
RIDER_SCRATCH = 4


def _rider_step(pt_ref, rider_in, o_ref, rider_scratch, cfg):
    q_ref, kn_ref, vn_ref, sm_ref, msuf_ref, kt_pool, vt_pool, lf_pool = rider_in
    kbuf, vbuf, lbuf, sems = rider_scratch
    base, count, per_step, n_pages = cfg
    i = pl.program_id(0)

    def copies(g, sl):
        out = []
        for j in range(n_pages):
            pg = pt_ref[g * n_pages + j]
            out.append(pltpu.make_async_copy(kt_pool.at[pg], kbuf.at[sl, j], sems.at[sl, 0]))
            out.append(pltpu.make_async_copy(vt_pool.at[pg], vbuf.at[sl, j], sems.at[sl, 1]))
            out.append(pltpu.make_async_copy(lf_pool.at[pg], lbuf.at[sl, j], sems.at[sl, 2]))
        return out

    @pl.when(i == 0)
    def _():
        for cp in copies(base, base % 2):
            cp.start()

    for r in range(per_step):
        g = base + i * per_step + r
        slot = g % 2

        @pl.when(g + 1 < base + count)
        def _():
            for cp in copies(g + 1, 1 - slot):
                cp.start()

        for cp in copies(g, slot):
            cp.wait()
        o_ref[r] = _decode_row(slot, q_ref[r], kn_ref[r], vn_ref[r], sm_ref[r], msuf_ref[...],
                               kbuf, vbuf, lbuf, n_pages)


def _decode_row(slot, q, k_new, v_new, sm_row, msuf, kbuf, vbuf, lbuf, n_pages):
    hd = FOX_HEAD_DIM
    page = kbuf.shape[-1]

    r8 = lax.broadcasted_iota(jnp.int32, (SUBLANES, LANES), 0)
    c8 = lax.broadcasted_iota(jnp.int32, (SUBLANES, LANES), 1)
    lf_new = jnp.sum(jnp.where(r8 == c8, sm_row, 0.0), axis=1, keepdims=True)
    n_rows = n_pages * FOX_HEADS
    lf = lbuf[slot].reshape(n_rows, page)
    within = functools.reduce(lambda a, c: a + c, [_dot(t, msuf) for t in _split3(lf)])
    tot = jnp.broadcast_to(jnp.sum(lf, axis=1, keepdims=True), (n_rows, page))
    row = lax.broadcasted_iota(jnp.int32, (n_rows, page), 0)
    later = tot
    k = FOX_HEADS
    while k < n_rows:
        later = later + jnp.where(row + k < n_rows, pltpu.roll(later, n_rows - k, 0), 0.0)
        k *= 2
    bias = (within + (later - tot)).reshape(n_pages, FOX_HEADS, page) + lf_new[None]

    q_t = q.T
    qb = jnp.stack([jnp.broadcast_to(q_t[:, h:h + 1], (hd, page)) for h in range(FOX_HEADS)], axis=0)
    logits = [jnp.sum(kbuf[slot, j] * qb, axis=1) + bias[j] for j in range(n_pages)]
    s_new = jnp.sum(k_new * q, axis=-1, keepdims=True)
    m = functools.reduce(jnp.maximum, [jnp.max(x, axis=-1, keepdims=True) for x in logits] + [s_new])
    ps = [jnp.exp(x - m) for x in logits]
    p_new = jnp.exp(s_new - m)
    l = functools.reduce(lambda a, c: a + c, [jnp.sum(x, axis=-1, keepdims=True) for x in ps] + [p_new])

    cols = []
    for h in range(FOX_HEADS):
        acc = vbuf[slot, 0, h] * ps[0][h:h + 1, :]
        for j in range(1, n_pages):
            acc = acc + vbuf[slot, j, h] * ps[j][h:h + 1, :]
        cols.append(jnp.sum(acc, axis=1, keepdims=True))
    mat = jnp.concatenate(cols + [jnp.zeros((hd, LANES - FOX_HEADS), F32)], axis=1)
    return (mat.T[0:FOX_HEADS, :] + p_new * v_new) / l


def _rider(page_table, q, kn, vn, sm, kt_pool, vt_pool, lf_pool, base, count, n_steps):
    n_pages = page_table.shape[1]
    page = kt_pool.shape[-1]
    per_step = count // n_steps
    assert per_step * n_steps == count, "sample rows must split evenly over the host's grid steps"
    tok = jnp.arange(page)
    msuf = (tok[:, None] > tok[None, :]).astype(BF16)
    rows = lambda a: a[base:base + count]
    tile = lambda a: pl.BlockSpec((per_step,) + a.shape[1:], lambda i, pt: (i,) + (0,) * (a.ndim - 1))
    hbm = pl.BlockSpec(memory_space=pl.ANY)
    operands = [rows(q), rows(kn), rows(vn), rows(sm).reshape(count, 1, LANES), msuf, kt_pool, vt_pool, lf_pool]
    kv_buf = pltpu.VMEM((2, n_pages, FOX_HEADS, FOX_HEAD_DIM, page), F32)
    return dict(
        prefetch=page_table.reshape(-1),
        operands=operands,
        in_specs=[tile(a) for a in operands[:4]] + [pl.BlockSpec(msuf.shape, lambda i, pt: (0, 0)), hbm, hbm, hbm],
        out_spec=pl.BlockSpec((per_step, FOX_HEADS, FOX_HEAD_DIM), lambda i, pt: (i, 0, 0)),
        out_shape=jax.ShapeDtypeStruct((count, FOX_HEADS, FOX_HEAD_DIM), F32),
        scratch=[kv_buf, kv_buf, pltpu.VMEM((2, n_pages, FOX_HEADS, page), F32), pltpu.SemaphoreType.DMA((2, 3))],
        cfg=(base, count, per_step, n_pages),
    )


def _gdn_sample_body(s_ref, qkv_ref, sm_ref, z_ref, gn_ref, sout_ref, o_ref):
    d = GDN_HEAD_DIM
    eye = lax.broadcasted_iota(jnp.int32, (d, d), 0) == lax.broadcasted_iota(jnp.int32, (d, d), 1)

    def col(rows):
        return jnp.sum(jnp.where(eye[None], rows[:, None, :], 0.0), axis=2, keepdims=True)

    for hh in range(GDN_HEADS):
        hs = slice(hh * d, (hh + 1) * d)
        q = qkv_ref[:, hs]
        k = qkv_ref[:, GDN_WIDTH + hh * d:GDN_WIDTH + (hh + 1) * d]
        v = qkv_ref[:, 2 * GDN_WIDTH + hh * d:2 * GDN_WIDTH + (hh + 1) * d]
        eg = jnp.exp(sm_ref[:, SM_G + hh:SM_G + hh + 1])
        beta = sm_ref[:, SM_BETA + hh:SM_BETA + hh + 1]
        s = s_ref[:, hh]
        v_new = v * beta - jnp.sum(col(k * (beta * eg)) * s, axis=1)
        o = jnp.sum(col(q * eg) * s, axis=1) + jnp.sum(q * k, axis=1, keepdims=True) * v_new
        sout_ref[:, hh] = s * eg[:, :, None] + col(k) * v_new[:, None, :]
        o_ref[:, hs] = _rms(o, gn_ref[...]) * _silu(z_ref[:, hs])


def _gdn_sample(state, gqkv, sm, gz, gn):
    nb = state.shape[0]
    bb = SUBLANES
    d = GDN_HEAD_DIM
    row_spec = lambda w: pl.BlockSpec((bb, w), lambda i: (i, 0))
    st_spec = pl.BlockSpec((bb, GDN_HEADS, d, d), lambda i: (i, 0, 0, 0))
    return pl.pallas_call(
        _gdn_sample_body,
        grid=(nb // bb,),
        in_specs=[st_spec, row_spec(3 * GDN_WIDTH), row_spec(LANES), row_spec(GDN_WIDTH),
                  pl.BlockSpec((1, d), lambda i: (0, 0))],
        out_specs=(st_spec, row_spec(GDN_WIDTH)),
        out_shape=(jax.ShapeDtypeStruct(state.shape, F32), jax.ShapeDtypeStruct((nb, GDN_WIDTH), F32)),
        compiler_params=pltpu.CompilerParams(dimension_semantics=("arbitrary",), vmem_limit_bytes=32 * MIB),
        name="gdn_sample",
    )(state, gqkv, sm, gz, gn)


def _pad_rows(a, rows):
    return jnp.concatenate([a, jnp.zeros((rows - a.shape[0],) + a.shape[1:], a.dtype)], axis=0)


def kernel(x_prompt, x_sample, cache_k, cache_v, cache_logf, state_gdn, state_gdn_conv, state_ffn_conv,
           page_table, norm_mix, w_in, b_forget, gdn_a_log, gdn_dt_bias, w_gdn_conv, gdn_out_norm, w_out,
           norm_ffn, w_up, w_ffn_conv, b_ffn_conv, w_down, norm_final):
    assert w_in.shape[0] == 1, "single-layer trunk"
    n_seq, seq_len, _ = x_prompt.shape
    nb = x_sample.shape[0]
    n_pool, page = cache_k.shape[1], cache_k.shape[2]

    w = w_in[0]
    o_ff = 3 * FOX_WIDTH
    o_g = o_ff + FOX_HEADS
    o_ga = o_g + 3 * GDN_WIDTH
    o_gz = o_ga + 2 * GDN_HEADS
    wbig = jnp.concatenate([w[:, :o_ff], w[:, o_g:o_ga], w[:, o_gz:]], axis=1).astype(BF16)
    wsm = jnp.concatenate([w[:, o_ff:o_g], w[:, o_ga:o_gz],
                           jnp.zeros((D_MODEL, LANES - FOX_HEADS - 2 * GDN_HEADS), F32)], axis=1).astype(BF16)
    par = jnp.zeros((SUBLANES, LANES), F32)
    par = par.at[0, SM_LF:SM_LF + FOX_HEADS].set(b_forget[0])
    par = par.at[0, SM_G:SM_G + GDN_HEADS].set(gdn_dt_bias[0])
    par = par.at[1, SM_G:SM_G + GDN_HEADS].set(gdn_a_log[0])
    nrm = norm_mix[0][None, :]
    cw = _pad_rows(w_gdn_conv[0], SUBLANES)
    gn = gdn_out_norm[0][None, :]
    wo = w_out[0].astype(BF16)
    nf = norm_ffn[0][None, :]
    wup = w_up[0].astype(BF16)
    cwf = _pad_rows(w_ffn_conv[0], SUBLANES)
    cbf = b_ffn_conv[0][None, :]
    wd = w_down[0].astype(BF16)
    nfin = norm_final[None, :]

    xs = x_sample.reshape(nb, D_MODEL)
    gctx = state_gdn_conv[0]
    fctx = state_ffn_conv[0]
    fk_s, fv_s, q_s, gqkv_s, gz_s, sm_s, pre_s = _proj_sample(
        xs, nrm, wbig, wsm, par, cw, gctx[:, 0], gctx[:, 1], gctx[:, 2])
    heads = lambda a: a.reshape(nb, FOX_HEADS, FOX_HEAD_DIM)
    half = nb // 2
    decode_args = (page_table, heads(q_s), heads(fk_s), heads(fv_s), sm_s,
                   jnp.transpose(cache_k[0], (0, 2, 3, 1)), jnp.transpose(cache_v[0], (0, 2, 3, 1)),
                   jnp.transpose(cache_logf[0], (0, 2, 1)))
    rider_a = functools.partial(_rider, *decode_args, 0, half)
    rider_b = functools.partial(_rider, *decode_args, half, nb - half)

    xp = x_prompt.reshape(n_seq * seq_len, D_MODEL)
    kt, vt, qt16, kaug, vt16, gqkv, gz, sm, smt, cst, fox_sa = _proj_prompt(
        xp, nrm, wbig, wsm, par, cw, seq_len, rider_a)
    fox = _fox_prompt(qt16, kaug, vt16, n_seq, seq_len)
    gdn, s_p = _gdn_prompt(gqkv, sm, smt, gz, gn, n_seq, seq_len)
    yp, cstf, fox_sb = _ffn_prompt(xp, fox, gdn, wo, nf, wup, cwf, cbf, wd, nfin, seq_len, rider_b)

    fox_s = jnp.concatenate([fox_sa, fox_sb], axis=0)
    s_s, gdn_s = _gdn_sample(state_gdn[0], gqkv_s, sm_s, gz_s, gn)
    ys, gate_s = _ffn_sample(xs, fox_s.reshape(nb, FOX_WIDTH).astype(BF16), gdn_s.astype(BF16), wo, nf, wup,
                             cwf, cbf, wd, nfin, fctx[:, 0], fctx[:, 1])

    kv_shape_s = (1, nb, 1, FOX_HEADS, FOX_HEAD_DIM)
    new_kv_p = lambda a: jnp.transpose(a.reshape(1, n_seq, FOX_HEADS, FOX_HEAD_DIM, seq_len), (0, 1, 4, 2, 3))
    return (
        yp.reshape(n_seq, seq_len, D_MODEL),
        ys.reshape(nb, 1, D_MODEL),
        new_kv_p(kt),
        new_kv_p(vt),
        sm[:, SM_LF:SM_LF + FOX_HEADS].reshape(1, n_seq, seq_len, FOX_HEADS),
        s_p[None],
        cst.reshape(n_seq, SUBLANES, 3 * GDN_WIDTH)[None, :, SUBLANES - (GDN_CONV - 1):],
        cstf.reshape(n_seq, SUBLANES, FFN_DIM)[None, :, SUBLANES - (FFN_CONV - 1):],
        fk_s.reshape(kv_shape_s),
        fv_s.reshape(kv_shape_s),
        sm_s[:, SM_LF:SM_LF + FOX_HEADS].reshape(1, nb, 1, FOX_HEADS),
        s_s[None],
        jnp.concatenate([gctx[:, 1:], pre_s[:, None, :]], axis=1)[None],
        jnp.concatenate([fctx[:, 1:], gate_s[:, None, :]], axis=1)[None],
    )
```

```python
import functools

import jax
import jax.numpy as jnp
from jax import lax
from jax.experimental import pallas as pl
from jax.experimental.pallas import tpu as pltpu

D_MODEL = 1024
FOX_HEADS = 8
FOX_HEAD_DIM = 64
FOX_WIDTH = FOX_HEADS * FOX_HEAD_DIM
GDN_HEADS = 4
GDN_HEAD_DIM = 128
GDN_WIDTH = GDN_HEADS * GDN_HEAD_DIM
GDN_CONV = 4
GDN_CHUNK = 64
FFN_DIM = 2816
FFN_CONV = 3
EPS = 1e-6
NEG_BIG = -1e30
FOX_SCALE = FOX_HEAD_DIM ** -0.5
GDN_SCALE = GDN_HEAD_DIM ** -0.5
LOG2E = 1.4426950408889634
FOX_QSCALE = FOX_SCALE * LOG2E
FOX_AUG = 3
FOX_DEN_ROWS = 16

LANES = 128
SUBLANES = 8
ROW_TILE = 256
FOX_TILE = 256
MIB = 1024 * 1024

SM_LF = 0
SM_G = 8
SM_BETA = 12
SM_CUM = 16
SM_GCUM = 24

F32 = jnp.float32
BF16 = jnp.bfloat16


def _sigmoid(x):
    return 1.0 / (1.0 + jnp.exp(-x))


def _silu(x):
    return x * _sigmoid(x)


def _rms(x, g):
    return x * lax.rsqrt(jnp.mean(x * x, axis=-1, keepdims=True) + EPS) * g


def _dot(a, b):
    return jnp.dot(a, b, preferred_element_type=F32)


def _shift_rows(x, prev8, k):
    r = pltpu.roll(x, k, 0)
    row8 = lax.broadcasted_iota(jnp.int32, prev8.shape, 0)
    top = jnp.where(row8 < k, pltpu.roll(prev8, k, 0), r[0:SUBLANES])
    return jnp.concatenate([top, r[SUBLANES:]], axis=0)


def _gate_activations(raw, par_ref):
    z = raw + par_ref[0:1, :]
    lane = lax.broadcasted_iota(jnp.int32, z.shape, 1)
    t = jnp.log1p(jnp.exp(-jnp.abs(z)))
    lf = jnp.minimum(z, 0.0) - t
    softplus = jnp.maximum(z, 0.0) + t
    g = -jnp.exp(par_ref[1:2, :]) * softplus
    beta = _sigmoid(z)
    return jnp.where(lane < SM_G, lf, jnp.where(lane < SM_BETA, g, jnp.where(lane < SM_CUM, beta, 0.0)))


def _gdn_qkv_norm(c, out_ref):
    for part, scale in ((0, GDN_SCALE), (1, 1.0)):
        for hh in range(GDN_HEADS):
            off = part * GDN_WIDTH + hh * GDN_HEAD_DIM
            seg = c[:, off:off + GDN_HEAD_DIM]
            n = lax.rsqrt(jnp.sum(seg * seg, axis=-1, keepdims=True) + EPS)
            out_ref[:, off:off + GDN_HEAD_DIM] = seg * n * scale
    out_ref[:, 2 * GDN_WIDTH:] = c[:, 2 * GDN_WIDTH:]


def _proj_prompt_body(pt_ref, *refs, tiles_per_seq, rider_cfg):
    (x_ref, nrm_ref, wbig_ref, wsm_ref, par_ref, cw_ref, scat_ref) = refs[:7]
    rider_in = refs[7:7 + RIDER_INPUTS]
    (kt_ref, vt_ref, qt16_ref, kaug_ref, vt16_ref, gqkv_ref, gz_ref, sm_ref, smt_ref, cst_ref,
     rider_out, carry_ref, prev_ref) = refs[7 + RIDER_INPUTS:20 + RIDER_INPUTS]
    rider_scratch = refs[20 + RIDER_INPUTS:]
    i = pl.program_id(0)

    @pl.when(i % tiles_per_seq == 0)
    def _():
        carry_ref[...] = jnp.zeros_like(carry_ref)
        prev_ref[...] = jnp.zeros_like(prev_ref)

    _rider_step(pt_ref, rider_in, rider_out, rider_scratch, rider_cfg)

    tm = x_ref.shape[0]
    h16 = _rms(x_ref[...], nrm_ref[...]).astype(BF16)

    fox = _dot(h16, wbig_ref[:, 0:3 * FOX_WIDTH])
    fq, fk, fv = fox[:, :FOX_WIDTH], fox[:, FOX_WIDTH:2 * FOX_WIDTH], fox[:, 2 * FOX_WIDTH:]
    fv_t = fv.T
    kt_ref[0] = fk.T
    vt_ref[0] = fv_t
    vt16_ref[0] = fv_t.astype(BF16)
    lane_p = lax.broadcasted_iota(jnp.int32, (tm, LANES), 1)
    ones_blk = jnp.where(lane_p < 2 * FOX_AUG, 1.0, 0.0)
    q_aug = []
    for p in range(FOX_HEADS // 2):
        q_aug += [fq[:, p * LANES:(p + 1) * LANES] * FOX_QSCALE, ones_blk]
        kaug_ref[:, 2 * p * LANES:(2 * p + 1) * LANES] = fk[:, p * LANES:(p + 1) * LANES].astype(BF16)
    qt16_ref[0] = jnp.concatenate(q_aug, axis=1).T.astype(BF16)

    act = _gate_activations(_dot(h16, wsm_ref[...]), par_ref)
    row = lax.broadcasted_iota(jnp.int32, act.shape, 0)
    lane = lax.broadcasted_iota(jnp.int32, act.shape, 1)
    y = act
    yc = act
    k = 1
    while k < tm:
        y = y + jnp.where(row >= k, pltpu.roll(y, k, 0), 0.0)
        if k < GDN_CHUNK:
            yc = yc + jnp.where((row & (GDN_CHUNK - 1)) >= k, pltpu.roll(yc, k, 0), 0.0)
        k *= 2
    y = y + carry_ref[0:1, :]
    carry_ref[0:1, :] = y[tm - 1:tm, :]
    terms = jnp.concatenate(_split3(y * (-LOG2E)), axis=1)
    extra = _dot(terms, scat_ref[...]).astype(BF16)
    for p in range(FOX_HEADS // 2):
        kaug_ref[:, (2 * p + 1) * LANES:(2 * p + 2) * LANES] = extra[:, p * LANES:(p + 1) * LANES]
    shift = SM_CUM - SM_LF
    sm = jnp.where(lane < SM_CUM, act,
                   jnp.where(lane < SM_GCUM, pltpu.roll(y, shift, 1),
                             jnp.where(lane < SM_GCUM + GDN_HEADS, pltpu.roll(yc, shift, 1), 0.0)))
    sm_ref[...] = sm
    smt_ref[...] = sm.T

    pre = _dot(h16, wbig_ref[:, 3 * FOX_WIDTH:3 * FOX_WIDTH + 3 * GDN_WIDTH])
    prev = prev_ref[...]
    acc = pre * cw_ref[GDN_CONV - 1:GDN_CONV, :]
    for kk in range(1, GDN_CONV):
        acc = acc + _shift_rows(pre, prev, kk) * cw_ref[GDN_CONV - 1 - kk:GDN_CONV - kk, :]
    prev_ref[...] = pre[tm - SUBLANES:, :]
    cst_ref[...] = pre[tm - SUBLANES:, :]
    _gdn_qkv_norm(_silu(acc), gqkv_ref)

    gz_ref[...] = _dot(h16, wbig_ref[:, 3 * FOX_WIDTH + 3 * GDN_WIDTH:])


def _proj_prompt(x, nrm, wbig, wsm, par, cw, seq_len, make_rider):
    rows = x.shape[0]
    tm = ROW_TILE
    n_seq = rows // seq_len
    tiles_per_seq = seq_len // tm
    wb = wbig.shape[1]
    row_spec = lambda w: pl.BlockSpec((tm, w), lambda i, pt: (i, 0))
    col_spec = lambda w: pl.BlockSpec((1, w, tm), lambda i, pt: (i // tiles_per_seq, 0, i % tiles_per_seq))
    const = lambda s: pl.BlockSpec(s, lambda i, pt: (0, 0), pipeline_mode=pl.Buffered(1))
    rider = make_rider(rows // tm)
    aug_width = FOX_HEADS * LANES
    src = jnp.arange(3 * LANES)
    term, head = src // LANES, src % LANES
    dst = (head // 2) * LANES + (head % 2) * FOX_AUG + term
    scat = ((jnp.arange(FOX_WIDTH)[None, :] == dst[:, None]) & (head[:, None] < FOX_HEADS)).astype(BF16)
    out_shape = (
        jax.ShapeDtypeStruct((n_seq, FOX_WIDTH, seq_len), F32),
        jax.ShapeDtypeStruct((n_seq, FOX_WIDTH, seq_len), F32),
        jax.ShapeDtypeStruct((n_seq, aug_width, seq_len), BF16),
        jax.ShapeDtypeStruct((rows, aug_width), BF16),
        jax.ShapeDtypeStruct((n_seq, FOX_WIDTH, seq_len), BF16),
        jax.ShapeDtypeStruct((rows, 3 * GDN_WIDTH), F32),
        jax.ShapeDtypeStruct((rows, GDN_WIDTH), F32),
        jax.ShapeDtypeStruct((rows, LANES), F32),
        jax.ShapeDtypeStruct((LANES, rows), F32),
        jax.ShapeDtypeStruct((n_seq * SUBLANES, 3 * GDN_WIDTH), F32),
    )
    out_specs = (
        col_spec(FOX_WIDTH), col_spec(FOX_WIDTH), col_spec(aug_width), row_spec(aug_width),
        col_spec(FOX_WIDTH), row_spec(3 * GDN_WIDTH), row_spec(GDN_WIDTH), row_spec(LANES),
        pl.BlockSpec((LANES, tm), lambda i, pt: (0, i)),
        pl.BlockSpec((SUBLANES, 3 * GDN_WIDTH), lambda i, pt: (i // tiles_per_seq, 0)),
    )
    grid_spec = pltpu.PrefetchScalarGridSpec(
        num_scalar_prefetch=1,
        grid=(rows // tm,),
        in_specs=[row_spec(D_MODEL), const((1, D_MODEL)), const((D_MODEL, wb)), const((D_MODEL, LANES)),
                  const((SUBLANES, LANES)), const((SUBLANES, 3 * GDN_WIDTH)), const(scat.shape)]
        + rider["in_specs"],
        out_specs=out_specs + (rider["out_spec"],),
        scratch_shapes=[pltpu.VMEM((SUBLANES, LANES), F32), pltpu.VMEM((SUBLANES, 3 * GDN_WIDTH), F32)]
        + rider["scratch"],
    )
    return pl.pallas_call(
        functools.partial(_proj_prompt_body, tiles_per_seq=tiles_per_seq, rider_cfg=rider["cfg"]),
        grid_spec=grid_spec,
        out_shape=out_shape + (rider["out_shape"],),
        compiler_params=pltpu.CompilerParams(dimension_semantics=("arbitrary",), vmem_limit_bytes=56 * MIB),
        name="proj_prompt",
    )(rider["prefetch"], x, nrm, wbig, wsm, par, cw, scat, *rider["operands"])


def _proj_sample_body(x_ref, nrm_ref, wbig_ref, wsm_ref, par_ref, cw_ref, c0_ref, c1_ref, c2_ref,
                      fk_ref, fv_ref, q_ref, gqkv_ref, gz_ref, sm_ref, pre_ref):
    h16 = _rms(x_ref[...], nrm_ref[...]).astype(BF16)
    fox = _dot(h16, wbig_ref[:, 0:3 * FOX_WIDTH])
    q_ref[...] = fox[:, :FOX_WIDTH] * FOX_SCALE
    fk_ref[...] = fox[:, FOX_WIDTH:2 * FOX_WIDTH]
    fv_ref[...] = fox[:, 2 * FOX_WIDTH:]
    sm_ref[...] = _gate_activations(_dot(h16, wsm_ref[...]), par_ref)
    pre = _dot(h16, wbig_ref[:, 3 * FOX_WIDTH:3 * FOX_WIDTH + 3 * GDN_WIDTH])
    pre_ref[...] = pre
    acc = (pre * cw_ref[3:4, :] + c2_ref[...] * cw_ref[2:3, :]
           + c1_ref[...] * cw_ref[1:2, :] + c0_ref[...] * cw_ref[0:1, :])
    _gdn_qkv_norm(_silu(acc), gqkv_ref)
    gz_ref[...] = _dot(h16, wbig_ref[:, 3 * FOX_WIDTH + 3 * GDN_WIDTH:])


def _proj_sample(x, nrm, wbig, wsm, par, cw, c0, c1, c2):
    rows = x.shape[0]
    full = lambda a: pl.BlockSpec(a.shape, lambda i: (0,) * a.ndim)
    args = (x, nrm, wbig, wsm, par, cw, c0, c1, c2)
    shapes = ((rows, FOX_WIDTH), (rows, FOX_WIDTH), (rows, FOX_WIDTH), (rows, 3 * GDN_WIDTH),
              (rows, GDN_WIDTH), (rows, LANES), (rows, 3 * GDN_WIDTH))
    return pl.pallas_call(
        _proj_sample_body,
        grid=(1,),
        in_specs=[full(a) for a in args],
        out_specs=tuple(pl.BlockSpec(s, lambda i: (0, 0)) for s in shapes),
        out_shape=tuple(jax.ShapeDtypeStruct(s, F32) for s in shapes),
        compiler_params=pltpu.CompilerParams(dimension_semantics=("arbitrary",), vmem_limit_bytes=48 * MIB),
        name="proj_sample",
    )(*args)


def _fox_prompt_body(qt_ref, kaug_ref, vt_ref, o_ref):
    i = pl.program_id(1)
    t = FOX_TILE
    hd = FOX_HEAD_DIM
    aug = 2 * LANES
    rr = lax.broadcasted_iota(jnp.int32, (t, t), 0)
    cc = lax.broadcasted_iota(jnp.int32, (t, t), 1)
    causal = rr <= cc
    row = lax.broadcasted_iota(jnp.int32, (aug, t), 0)
    qs = []
    for h in range(FOX_HEADS):
        e = h % 2
        qp = qt_ref[0, (h // 2) * aug:(h // 2 + 1) * aug, :]
        own = jnp.logical_or(jnp.logical_and(row >= e * hd, row < (e + 1) * hd),
                             jnp.logical_and(row >= LANES + e * FOX_AUG, row < LANES + (e + 1) * FOX_AUG))
        qs.append(jnp.where(own, qp, jnp.zeros_like(qp)))

    def tile(j, carry, masked):
        koff = pl.multiple_of(j * t, t)
        ss = [_dot(kaug_ref[pl.ds(koff, t), (h // 2) * aug:(h // 2 + 1) * aug], qs[h])
              for h in range(FOX_HEADS)]
        stats, pms = [], []
        for h in range(FOX_HEADS):
            m = carry[h][0]
            s = jnp.where(causal, ss[h], NEG_BIG) if masked else ss[h]
            m_new = jnp.maximum(m, jnp.max(s, axis=0, keepdims=True))
            stats.append((m_new, jnp.exp2(m - m_new)))
            pms.append(jnp.exp2(s - m_new).astype(BF16))
        out = []
        for h in range(FOX_HEADS):
            m_new, alpha = stats[h]
            v_ones = jnp.concatenate([vt_ref[0, h * hd:(h + 1) * hd, pl.ds(koff, t)], ones_rows], axis=0)
            out.append((m_new, alpha * carry[h][1] + _dot(v_ones, pms[h])))
        return tuple(out)

    ones_rows = jnp.ones((FOX_DEN_ROWS, t), BF16)
    init = tuple((jnp.full((1, t), NEG_BIG, F32), jnp.zeros((hd + FOX_DEN_ROWS, t), F32))
                 for _ in range(FOX_HEADS))
    carry = lax.fori_loop(0, i, lambda j, c: tile(j, c, False), init)
    final = tile(i, carry, True)
    for p in range(FOX_HEADS // 2):
        outs = [a[:hd] / a[hd:hd + 1] for _, a in final[2 * p:2 * p + 2]]
        o_ref[:, p * LANES:(p + 1) * LANES] = jnp.concatenate(outs, axis=0).T.astype(BF16)


def _fox_prompt(qt16, kaug, vt16, n_seq, seq_len):
    t = FOX_TILE
    nq = seq_len // t
    aug_width = kaug.shape[1]
    return pl.pallas_call(
        _fox_prompt_body,
        grid=(n_seq, nq),
        in_specs=[pl.BlockSpec((1, aug_width, t), lambda b, i: (b, 0, i)),
                  pl.BlockSpec((seq_len, aug_width), lambda b, i: (b, 0)),
                  pl.BlockSpec((1, FOX_WIDTH, seq_len), lambda b, i: (b, 0, 0))],
        out_specs=pl.BlockSpec((t, FOX_WIDTH), lambda b, i: (b * nq + i, 0)),
        out_shape=jax.ShapeDtypeStruct((n_seq * seq_len, FOX_WIDTH), BF16),
        compiler_params=pltpu.CompilerParams(dimension_semantics=("arbitrary", "arbitrary"),
                                             vmem_limit_bytes=40 * MIB),
        name="fox_prompt",
    )(qt16, kaug, vt16)


def _bdot(a, b):
    return lax.dot_general(a.astype(BF16), b.astype(BF16), (((2,), (1,)), ((0,), (0,))),
                           preferred_element_type=F32)


def _bdot_nt(a, b):
    return lax.dot_general(a.astype(BF16), b.astype(BF16), (((2,), (2,)), ((0,), (0,))),
                           preferred_element_type=F32)


def _unit_lower_inverse_minus_eye(a, r, c):
    blk16 = (r // 16) == (c // 16)
    blk32 = (r // 32) == (c // 32)
    p = jnp.where(blk16, -a, 0.0)
    dt = p
    for _ in range(3):
        p = _bdot(p, p)
        dt = dt + p + _bdot(dt, p)
    for off in (jnp.where(jnp.logical_and(blk32, jnp.logical_not(blk16)), a, 0.0),
                jnp.where(blk32, 0.0, a)):
        x = off + _bdot(dt, off)
        dt = dt - (x + _bdot(x, dt))
    return dt


def _gdn_prompt_body(qkv_ref, sm_ref, gcr_ref, z_ref, gn_ref, o_ref, sout_ref, s_ref):
    ci = pl.program_id(1)

    @pl.when(ci == 0)
    def _():
        s_ref[...] = jnp.zeros_like(s_ref)

    ch = GDN_CHUNK
    d = GDN_HEAD_DIM
    nh = GDN_HEADS
    n_chunks = qkv_ref.shape[0] // ch
    units = [(cidx, hh) for cidx in range(n_chunks) for hh in range(nh)]

    def gather(fn):
        return jnp.stack([fn(slice(cidx * ch, (cidx + 1) * ch), hh) for cidx, hh in units], axis=0)

    q = gather(lambda rows, hh: qkv_ref[rows, hh * d:(hh + 1) * d])
    k = gather(lambda rows, hh: qkv_ref[rows, GDN_WIDTH + hh * d:GDN_WIDTH + (hh + 1) * d])
    v = gather(lambda rows, hh: qkv_ref[rows, 2 * GDN_WIDTH + hh * d:2 * GDN_WIDTH + (hh + 1) * d])
    beta = gather(lambda rows, hh: sm_ref[rows, SM_BETA + hh:SM_BETA + hh + 1])
    gc = gather(lambda rows, hh: sm_ref[rows, SM_GCUM + hh:SM_GCUM + hh + 1])
    gr = gather(lambda rows, hh: gcr_ref[hh:hh + 1, rows])

    r = lax.broadcasted_iota(jnp.int32, (1, ch, ch), 1)
    c = lax.broadcasted_iota(jnp.int32, (1, ch, ch), 2)
    lower = r >= c
    beta = jnp.broadcast_to(beta, (len(units), ch, d))
    gc = jnp.broadcast_to(gc, (len(units), ch, d))
    decay = jnp.where(lower, jnp.exp(jnp.where(lower, gc[:, :, :ch] - gr, 0.0)), 0.0)
    qk_kk = _bdot_nt(jnp.concatenate([q, k], axis=1), k)
    qk = qk_kk[:, :ch] * decay
    a = jnp.where(r > c, beta[:, :, :ch] * qk_kk[:, ch:] * decay, 0.0)
    dt = _unit_lower_inverse_minus_eye(a, r, c)
    eg = jnp.exp(gc)
    rhs = jnp.concatenate([v * beta, k * (beta * eg)], axis=-1)
    uw = rhs + _bdot(dt, rhs)
    gl = gc[:, ch - 1:ch, :]
    qd = (q * eg).astype(BF16)
    kd = k * jnp.exp(gl - gc)
    g_last = jnp.exp(gl)
    qk16 = qk.astype(BF16)

    s = s_ref[...]
    for cidx in range(n_chunks):
        us = slice(cidx * nh, (cidx + 1) * nh)
        rows = slice(cidx * ch, (cidx + 1) * ch)
        s16 = s.astype(BF16)
        v_new = uw[us, :, :d] - _bdot(uw[us, :, d:], s16)
        o = _bdot(qd[us], s16) + _bdot(qk16[us], v_new)
        vn16 = v_new.astype(BF16)
        upd = jnp.stack([_dot(kd[cidx * nh + hh].T.astype(BF16), vn16[hh]) for hh in range(nh)], axis=0)
        s = s * g_last[us] + upd
        for hh in range(nh):
            hs = slice(hh * d, (hh + 1) * d)
            o_ref[rows, hs] = (_rms(o[hh], gn_ref[...]) * _silu(z_ref[rows, hs])).astype(BF16)
    s_ref[...] = s

    @pl.when(ci == pl.num_programs(1) - 1)
    def _():
        sout_ref[0] = s


def _gdn_prompt(gqkv, sm, smt, gz, gn, n_seq, seq_len):
    tm = ROW_TILE
    nt = seq_len // tm
    gcum_block = SM_GCUM // SUBLANES
    rows = gqkv.shape[0]
    return pl.pallas_call(
        _gdn_prompt_body,
        grid=(n_seq, nt),
        in_specs=[pl.BlockSpec((tm, 3 * GDN_WIDTH), lambda b, i: (b * nt + i, 0)),
                  pl.BlockSpec((tm, LANES), lambda b, i: (b * nt + i, 0)),
                  pl.BlockSpec((SUBLANES, tm), lambda b, i: (gcum_block, b * nt + i)),
                  pl.BlockSpec((tm, GDN_WIDTH), lambda b, i: (b * nt + i, 0)),
                  pl.BlockSpec((1, GDN_HEAD_DIM), lambda b, i: (0, 0))],
        out_specs=(pl.BlockSpec((tm, GDN_WIDTH), lambda b, i: (b * nt + i, 0)),
                   pl.BlockSpec((1, GDN_HEADS, GDN_HEAD_DIM, GDN_HEAD_DIM), lambda b, i: (b, 0, 0, 0))),
        out_shape=(jax.ShapeDtypeStruct((rows, GDN_WIDTH), BF16),
                   jax.ShapeDtypeStruct((n_seq, GDN_HEADS, GDN_HEAD_DIM, GDN_HEAD_DIM), F32)),
        scratch_shapes=[pltpu.VMEM((GDN_HEADS, GDN_HEAD_DIM, GDN_HEAD_DIM), F32)],
        compiler_params=pltpu.CompilerParams(dimension_semantics=("arbitrary", "arbitrary"),
                                             vmem_limit_bytes=32 * MIB),
        name="gdn_prompt",
    )(gqkv, sm, smt, gz, gn)


def _ffn_tail(x, fox_ref, gdn_ref, wo_ref, nf_ref, wup_ref, wd_ref, nfin_ref, conv):
    mix = _dot(fox_ref[...], wo_ref[0:FOX_WIDTH, :]) + _dot(gdn_ref[...], wo_ref[FOX_WIDTH:, :])
    x2 = x + mix
    h2 = _rms(x2, nf_ref[...]).astype(BF16)
    gu = _dot(h2, wup_ref[...])
    gate, up = gu[:, :FFN_DIM], gu[:, FFN_DIM:]
    act = (_silu(conv(gate)) * up).astype(BF16)
    x3 = x2 + _dot(act, wd_ref[...])
    return _rms(x3, nfin_ref[...])


def _ffn_prompt_body(pt_ref, *refs, tiles_per_seq, rider_cfg):
    (x_ref, fox_ref, gdn_ref, wo_ref, nf_ref, wup_ref, cw_ref, cb_ref, wd_ref, nfin_ref) = refs[:10]
    rider_in = refs[10:10 + RIDER_INPUTS]
    y_ref, cst_ref, rider_out, prev_ref = refs[10 + RIDER_INPUTS:14 + RIDER_INPUTS]
    rider_scratch = refs[14 + RIDER_INPUTS:]
    i = pl.program_id(0)

    @pl.when(i % tiles_per_seq == 0)
    def _():
        prev_ref[...] = jnp.zeros_like(prev_ref)

    _rider_step(pt_ref, rider_in, rider_out, rider_scratch, rider_cfg)

    tm = x_ref.shape[0]

    def conv(gate):
        prev = prev_ref[...]
        out = gate * cw_ref[FFN_CONV - 1:FFN_CONV, :] + cb_ref[...]
        for kk in range(1, FFN_CONV):
            out = out + _shift_rows(gate, prev, kk) * cw_ref[FFN_CONV - 1 - kk:FFN_CONV - kk, :]
        prev_ref[...] = gate[tm - SUBLANES:, :]
        cst_ref[...] = gate[tm - SUBLANES:, :]
        return out

    y_ref[...] = _ffn_tail(x_ref[...], fox_ref, gdn_ref, wo_ref, nf_ref, wup_ref, wd_ref, nfin_ref, conv)


def _ffn_prompt(x, fox, gdn, wo, nf, wup, cw, cb, wd, nfin, seq_len, make_rider):
    rows = x.shape[0]
    tm = ROW_TILE
    n_seq = rows // seq_len
    tiles_per_seq = seq_len // tm
    row_spec = lambda w: pl.BlockSpec((tm, w), lambda i, pt: (i, 0))
    const = lambda a: pl.BlockSpec(a.shape, lambda i, pt: (0, 0), pipeline_mode=pl.Buffered(1))
    rider = make_rider(rows // tm)
    grid_spec = pltpu.PrefetchScalarGridSpec(
        num_scalar_prefetch=1,
        grid=(rows // tm,),
        in_specs=[row_spec(D_MODEL), row_spec(FOX_WIDTH), row_spec(GDN_WIDTH), const(wo), const(nf),
                  const(wup), const(cw), const(cb), const(wd), const(nfin)] + rider["in_specs"],
        out_specs=(row_spec(D_MODEL),
                   pl.BlockSpec((SUBLANES, FFN_DIM), lambda i, pt: (i // tiles_per_seq, 0)),
                   rider["out_spec"]),
        scratch_shapes=[pltpu.VMEM((SUBLANES, FFN_DIM), F32)] + rider["scratch"],
    )
    return pl.pallas_call(
        functools.partial(_ffn_prompt_body, tiles_per_seq=tiles_per_seq, rider_cfg=rider["cfg"]),
        grid_spec=grid_spec,
        out_shape=(jax.ShapeDtypeStruct((rows, D_MODEL), F32),
                   jax.ShapeDtypeStruct((n_seq * SUBLANES, FFN_DIM), F32),
                   rider["out_shape"]),
        compiler_params=pltpu.CompilerParams(dimension_semantics=("arbitrary",), vmem_limit_bytes=58 * MIB),
        name="ffn_prompt",
    )(rider["prefetch"], x, fox, gdn, wo, nf, wup, cw, cb, wd, nfin, *rider["operands"])


def _ffn_sample_body(x_ref, fox_ref, gdn_ref, wo_ref, nf_ref, wup_ref, cw_ref, cb_ref, wd_ref, nfin_ref,
                     c0_ref, c1_ref, y_ref, gate_ref):
    def conv(gate):
        gate_ref[...] = gate
        return (gate * cw_ref[2:3, :] + c1_ref[...] * cw_ref[1:2, :] + c0_ref[...] * cw_ref[0:1, :]
                + cb_ref[...])

    y_ref[...] = _ffn_tail(x_ref[...], fox_ref, gdn_ref, wo_ref, nf_ref, wup_ref, wd_ref, nfin_ref, conv)


def _ffn_sample(x, fox, gdn, wo, nf, wup, cw, cb, wd, nfin, c0, c1):
    rows = x.shape[0]
    args = (x, fox, gdn, wo, nf, wup, cw, cb, wd, nfin, c0, c1)
    full = lambda a: pl.BlockSpec(a.shape, lambda i: (0, 0), pipeline_mode=pl.Buffered(1))
    return pl.pallas_call(
        _ffn_sample_body,
        grid=(1,),
        in_specs=[full(a) for a in args],
        out_specs=(pl.BlockSpec((rows, D_MODEL), lambda i: (0, 0)),
                   pl.BlockSpec((rows, FFN_DIM), lambda i: (0, 0))),
        out_shape=(jax.ShapeDtypeStruct((rows, D_MODEL), F32), jax.ShapeDtypeStruct((rows, FFN_DIM), F32)),
        compiler_params=pltpu.CompilerParams(dimension_semantics=("arbitrary",), vmem_limit_bytes=56 * MIB),
        name="ffn_sample",
    )(*args)


def _split3(x):
    hi = x.astype(BF16)
    r1 = x - hi.astype(F32)
    mid = r1.astype(BF16)
    lo = (r1 - mid.astype(F32)).astype(BF16)
    return hi, mid, lo


RIDER_INPUTS = 8
RIDER_SCRATCH = 4


def _rider_step(pt_ref, rider_in, o_ref, rider_scratch, cfg):
    q_ref, kn_ref, vn_ref, sm_ref, msuf_ref, kt_pool, vt_pool, lf_pool = rider_in
    kbuf, vbuf, lbuf, sems = rider_scratch
    base, count, per_step, n_pages = cfg
    i = pl.program_id(0)

    def copies(g, sl):
        out = []
        for j in range(n_pages):
            pg = pt_ref[g * n_pages + j]
            out.append(pltpu.make_async_copy(kt_pool.at[pg], kbuf.at[sl, j], sems.at[sl, 0]))
            out.append(pltpu.make_async_copy(vt_pool.at[pg], vbuf.at[sl, j], sems.at[sl, 1]))
            out.append(pltpu.make_async_copy(lf_pool.at[pg], lbuf.at[sl, j], sems.at[sl, 2]))
        return out

    @pl.when(i == 0)
    def _():
        for cp in copies(base, base % 2):
            cp.start()

    for r in range(per_step):
        g = base + i * per_step + r
        slot = g % 2

        @pl.when(g + 1 < base + count)
        def _():
            for cp in copies(g + 1, 1 - slot):
                cp.start()

        for cp in copies(g, slot):
            cp.wait()
        o_ref[r] = _decode_row(slot, q_ref[r], kn_ref[r], vn_ref[r], sm_ref[r], msuf_ref[...],
                               kbuf, vbuf, lbuf, n_pages)


def _decode_row(slot, q, k_new, v_new, sm_row, msuf, kbuf, vbuf, lbuf, n_pages):
    hd = FOX_HEAD_DIM
    page = kbuf.shape[-1]

    r8 = lax.broadcasted_iota(jnp.int32, (SUBLANES, LANES), 0)
    c8 = lax.broadcasted_iota(jnp.int32, (SUBLANES, LANES), 1)
    lf_new = jnp.sum(jnp.where(r8 == c8, sm_row, 0.0), axis=1, keepdims=True)
    n_rows = n_pages * FOX_HEADS
    lf = lbuf[slot].reshape(n_rows, page)
    within = functools.reduce(lambda a, c: a + c, [_dot(t, msuf) for t in _split3(lf)])
    tot = jnp.broadcast_to(jnp.sum(lf, axis=1, keepdims=True), (n_rows, page))
    row = lax.broadcasted_iota(jnp.int32, (n_rows, page), 0)
    later = tot
    k = FOX_HEADS
    while k < n_rows:
        later = later + jnp.where(row + k < n_rows, pltpu.roll(later, n_rows - k, 0), 0.0)
        k *= 2
    bias = (within + (later - tot)).reshape(n_pages, FOX_HEADS, page) + lf_new[None]

    q_t = q.T
    qb = jnp.stack([jnp.broadcast_to(q_t[:, h:h + 1], (hd, page)) for h in range(FOX_HEADS)], axis=0)
    logits = [jnp.sum(kbuf[slot, j] * qb, axis=1) + bias[j] for j in range(n_pages)]
    s_new = jnp.sum(k_new * q, axis=-1, keepdims=True)
    m = functools.reduce(jnp.maximum, [jnp.max(x, axis=-1, keepdims=True) for x in logits] + [s_new])
    ps = [jnp.exp(x - m) for x in logits]
    p_new = jnp.exp(s_new - m)
    l = functools.reduce(lambda a, c: a + c, [jnp.sum(x, axis=-1, keepdims=True) for x in ps] + [p_new])

    cols = []
    for h in range(FOX_HEADS):
        acc = vbuf[slot, 0, h] * ps[0][h:h + 1, :]
        for j in range(1, n_pages):
            acc = acc + vbuf[slot, j, h] * ps[j][h:h + 1, :]
        cols.append(jnp.sum(acc, axis=1, keepdims=True))
    mat = jnp.concatenate(cols + [jnp.zeros((hd, LANES - FOX_HEADS), F32)], axis=1)
    return (mat.T[0:FOX_HEADS, :] + p_new * v_new) / l


def _rider(page_table, q, kn, vn, sm, kt_pool, vt_pool, lf_pool, base, count, n_steps):
    n_pages = page_table.shape[1]
    page = kt_pool.shape[-1]
    per_step = count // n_steps
    assert per_step * n_steps == count, "sample rows must split evenly over the host's grid steps"
    tok = jnp.arange(page)
    msuf = (tok[:, None] > tok[None, :]).astype(BF16)
    rows = lambda a: a[base:base + count]
    tile = lambda a: pl.BlockSpec((per_step,) + a.shape[1:], lambda i, pt: (i,) + (0,) * (a.ndim - 1))
    hbm = pl.BlockSpec(memory_space=pl.ANY)
    operands = [rows(q), rows(kn), rows(vn), rows(sm).reshape(count, 1, LANES), msuf, kt_pool, vt_pool, lf_pool]
    kv_buf = pltpu.VMEM((2, n_pages, FOX_HEADS, FOX_HEAD_DIM, page), F32)
    return dict(
        prefetch=page_table.reshape(-1),
        operands=operands,
        in_specs=[tile(a) for a in operands[:4]] + [pl.BlockSpec(msuf.shape, lambda i, pt: (0, 0)), hbm, hbm, hbm],
        out_spec=pl.BlockSpec((per_step, FOX_HEADS, FOX_HEAD_DIM), lambda i, pt: (i, 0, 0)),
        out_shape=jax.ShapeDtypeStruct((count, FOX_HEADS, FOX_HEAD_DIM), F32),
        scratch=[kv_buf, kv_buf, pltpu.VMEM((2, n_pages, FOX_HEADS, page), F32), pltpu.SemaphoreType.DMA((2, 3))],
        cfg=(base, count, per_step, n_pages),
    )


def _gdn_sample_body(s_ref, qkv_ref, sm_ref, z_ref, gn_ref, sout_ref, o_ref):
    d = GDN_HEAD_DIM
    eye = lax.broadcasted_iota(jnp.int32, (d, d), 0) == lax.broadcasted_iota(jnp.int32, (d, d), 1)

    def col(rows):
        return jnp.sum(jnp.where(eye[None], rows[:, None, :], 0.0), axis=2, keepdims=True)

    for hh in range(GDN_HEADS):
        hs = slice(hh * d, (hh + 1) * d)
        q = qkv_ref[:, hs]
        k = qkv_ref[:, GDN_WIDTH + hh * d:GDN_WIDTH + (hh + 1) * d]
        v = qkv_ref[:, 2 * GDN_WIDTH + hh * d:2 * GDN_WIDTH + (hh + 1) * d]
        eg = jnp.exp(sm_ref[:, SM_G + hh:SM_G + hh + 1])
        beta = sm_ref[:, SM_BETA + hh:SM_BETA + hh + 1]
        s = s_ref[:, hh]
        k_col = col(k)
        v_new = beta * (v - eg * jnp.sum(k_col * s, axis=1))
        o = eg * jnp.sum(col(q) * s, axis=1) + jnp.sum(q * k, axis=1, keepdims=True) * v_new
        sout_ref[:, hh] = s * eg[:, :, None] + k_col * v_new[:, None, :]
        o_ref[:, hs] = _rms(o, gn_ref[...]) * _silu(z_ref[:, hs])


def _gdn_sample(state, gqkv, sm, gz, gn):
    nb = state.shape[0]
    bb = SUBLANES
    d = GDN_HEAD_DIM
    row_spec = lambda w: pl.BlockSpec((bb, w), lambda i: (i, 0))
    st_spec = pl.BlockSpec((bb, GDN_HEADS, d, d), lambda i: (i, 0, 0, 0))
    return pl.pallas_call(
        _gdn_sample_body,
        grid=(nb // bb,),
        in_specs=[st_spec, row_spec(3 * GDN_WIDTH), row_spec(LANES), row_spec(GDN_WIDTH),
                  pl.BlockSpec((1, d), lambda i: (0, 0))],
        out_specs=(st_spec, row_spec(GDN_WIDTH)),
        out_shape=(jax.ShapeDtypeStruct(state.shape, F32), jax.ShapeDtypeStruct((nb, GDN_WIDTH), F32)),
        compiler_params=pltpu.CompilerParams(dimension_semantics=("arbitrary",), vmem_limit_bytes=32 * MIB),
        name="gdn_sample",
    )(state, gqkv, sm, gz, gn)


def _pad_rows(a, rows):
    return jnp.concatenate([a, jnp.zeros((rows - a.shape[0],) + a.shape[1:], a.dtype)], axis=0)


def kernel(x_prompt, x_sample, cache_k, cache_v, cache_logf, state_gdn, state_gdn_conv, state_ffn_conv,
           page_table, norm_mix, w_in, b_forget, gdn_a_log, gdn_dt_bias, w_gdn_conv, gdn_out_norm, w_out,
           norm_ffn, w_up, w_ffn_conv, b_ffn_conv, w_down, norm_final):
    assert w_in.shape[0] == 1, "single-layer trunk"
    n_seq, seq_len, _ = x_prompt.shape
    nb = x_sample.shape[0]
    n_pool, page = cache_k.shape[1], cache_k.shape[2]

    w = w_in[0]
    o_ff = 3 * FOX_WIDTH
    o_g = o_ff + FOX_HEADS
    o_ga = o_g + 3 * GDN_WIDTH
    o_gz = o_ga + 2 * GDN_HEADS
    wbig = jnp.concatenate([w[:, :o_ff], w[:, o_g:o_ga], w[:, o_gz:]], axis=1).astype(BF16)
    wsm = jnp.concatenate([w[:, o_ff:o_g], w[:, o_ga:o_gz],
                           jnp.zeros((D_MODEL, LANES - FOX_HEADS - 2 * GDN_HEADS), F32)], axis=1).astype(BF16)
    par = jnp.zeros((SUBLANES, LANES), F32)
    par = par.at[0, SM_LF:SM_LF + FOX_HEADS].set(b_forget[0])
    par = par.at[0, SM_G:SM_G + GDN_HEADS].set(gdn_dt_bias[0])
    par = par.at[1, SM_G:SM_G + GDN_HEADS].set(gdn_a_log[0])
    nrm = norm_mix[0][None, :]
    cw = _pad_rows(w_gdn_conv[0], SUBLANES)
    gn = gdn_out_norm[0][None, :]
    wo = w_out[0].astype(BF16)
    nf = norm_ffn[0][None, :]
    wup = w_up[0].astype(BF16)
    cwf = _pad_rows(w_ffn_conv[0], SUBLANES)
    cbf = b_ffn_conv[0][None, :]
    wd = w_down[0].astype(BF16)
    nfin = norm_final[None, :]

    xs = x_sample.reshape(nb, D_MODEL)
    gctx = state_gdn_conv[0]
    fctx = state_ffn_conv[0]
    fk_s, fv_s, q_s, gqkv_s, gz_s, sm_s, pre_s = _proj_sample(
        xs, nrm, wbig, wsm, par, cw, gctx[:, 0], gctx[:, 1], gctx[:, 2])
    heads = lambda a: a.reshape(nb, FOX_HEADS, FOX_HEAD_DIM)
    half = nb // 2
    decode_args = (page_table, heads(q_s), heads(fk_s), heads(fv_s), sm_s,
                   jnp.transpose(cache_k[0], (0, 2, 3, 1)), jnp.transpose(cache_v[0], (0, 2, 3, 1)),
                   jnp.transpose(cache_logf[0], (0, 2, 1)))
    rider_a = functools.partial(_rider, *decode_args, 0, half)
    rider_b = functools.partial(_rider, *decode_args, half, nb - half)

    xp = x_prompt.reshape(n_seq * seq_len, D_MODEL)
    kt, vt, qt16, kaug, vt16, gqkv, gz, sm, smt, cst, fox_sa = _proj_prompt(
        xp, nrm, wbig, wsm, par, cw, seq_len, rider_a)
    fox = _fox_prompt(qt16, kaug, vt16, n_seq, seq_len)
    gdn, s_p = _gdn_prompt(gqkv, sm, smt, gz, gn, n_seq, seq_len)
    yp, cstf, fox_sb = _ffn_prompt(xp, fox, gdn, wo, nf, wup, cwf, cbf, wd, nfin, seq_len, rider_b)

    fox_s = jnp.concatenate([fox_sa, fox_sb], axis=0)
    s_s, gdn_s = _gdn_sample(state_gdn[0], gqkv_s, sm_s, gz_s, gn)
    ys, gate_s = _ffn_sample(xs, fox_s.reshape(nb, FOX_WIDTH).astype(BF16), gdn_s.astype(BF16), wo, nf, wup,
                             cwf, cbf, wd, nfin, fctx[:, 0], fctx[:, 1])

    kv_shape_s = (1, nb, 1, FOX_HEADS, FOX_HEAD_DIM)
    new_kv_p = lambda a: jnp.transpose(a.reshape(1, n_seq, FOX_HEADS, FOX_HEAD_DIM, seq_len), (0, 1, 4, 2, 3))
    return (
        yp.reshape(n_seq, seq_len, D_MODEL),
        ys.reshape(nb, 1, D_MODEL),
        new_kv_p(kt),
        new_kv_p(vt),
        sm[:, SM_LF:SM_LF + FOX_HEADS].reshape(1, n_seq, seq_len, FOX_HEADS),
        s_p[None],
        cst.reshape(n_seq, SUBLANES, 3 * GDN_WIDTH)[None, :, SUBLANES - (GDN_CONV - 1):],
        cstf.reshape(n_seq, SUBLANES, FFN_DIM)[None, :, SUBLANES - (FFN_CONV - 1):],
        fk_s.reshape(kv_shape_s),
        fv_s.reshape(kv_shape_s),
        sm_s[:, SM_LF:SM_LF + FOX_HEADS].reshape(1, nb, 1, FOX_HEADS),
        s_s[None],
        jnp.concatenate([gctx[:, 1:], pre_s[:, None, :]], axis=1)[None],
        jnp.concatenate([fctx[:, 1:], gate_s[:, None, :]], axis=1)[None],
    )
```

```python
import functools

import jax
import jax.numpy as jnp
from jax import lax
from jax.experimental import pallas as pl
from jax.experimental.pallas import tpu as pltpu

D_MODEL = 1024
FOX_HEADS = 8
FOX_HEAD_DIM = 64
FOX_WIDTH = FOX_HEADS * FOX_HEAD_DIM
GDN_HEADS = 4
GDN_HEAD_DIM = 128
GDN_WIDTH = GDN_HEADS * GDN_HEAD_DIM
GDN_CONV = 4
GDN_CHUNK = 64
FFN_DIM = 2816
FFN_CONV = 3
EPS = 1e-6
NEG_BIG = -1e30
FOX_SCALE = FOX_HEAD_DIM ** -0.5
GDN_SCALE = GDN_HEAD_DIM ** -0.5
LOG2E = 1.4426950408889634
FOX_QSCALE = FOX_SCALE * LOG2E
FOX_AUG = 3
FOX_DEN_ROWS = 16

LANES = 128
SUBLANES = 8
ROW_TILE = 256
FOX_TILE = 256
MIB = 1024 * 1024

SM_LF = 0
SM_G = 8
SM_BETA = 12
SM_CUM = 16
SM_GCUM = 24

F32 = jnp.float32
BF16 = jnp.bfloat16


def _sigmoid(x):
    return 1.0 / (1.0 + jnp.exp(-x))


def _silu(x):
    return x * _sigmoid(x)


def _rms(x, g):
    return x * lax.rsqrt(jnp.mean(x * x, axis=-1, keepdims=True) + EPS) * g


def _dot(a, b):
    return jnp.dot(a, b, preferred_element_type=F32)


def _shift_rows(x, prev8, k):
    r = pltpu.roll(x, k, 0)
    row8 = lax.broadcasted_iota(jnp.int32, prev8.shape, 0)
    top = jnp.where(row8 < k, pltpu.roll(prev8, k, 0), r[0:SUBLANES])
    return jnp.concatenate([top, r[SUBLANES:]], axis=0)


def _gate_activations(raw, par_ref):
    z = raw + par_ref[0:1, :]
    lane = lax.broadcasted_iota(jnp.int32, z.shape, 1)
    t = jnp.log1p(jnp.exp(-jnp.abs(z)))
    lf = jnp.minimum(z, 0.0) - t
    softplus = jnp.maximum(z, 0.0) + t
    g = -jnp.exp(par_ref[1:2, :]) * softplus
    beta = _sigmoid(z)
    return jnp.where(lane < SM_G, lf, jnp.where(lane < SM_BETA, g, jnp.where(lane < SM_CUM, beta, 0.0)))


def _gdn_qkv_norm(c, out_ref):
    for part, scale in ((0, GDN_SCALE), (1, 1.0)):
        for hh in range(GDN_HEADS):
            off = part * GDN_WIDTH + hh * GDN_HEAD_DIM
            seg = c[:, off:off + GDN_HEAD_DIM]
            n = lax.rsqrt(jnp.sum(seg * seg, axis=-1, keepdims=True) + EPS)
            out_ref[:, off:off + GDN_HEAD_DIM] = seg * n * scale
    out_ref[:, 2 * GDN_WIDTH:] = c[:, 2 * GDN_WIDTH:]


def _proj_prompt_body(pt_ref, *refs, tiles_per_seq, rider_cfg):
    (x_ref, nrm_ref, wbig_ref, wsm_ref, par_ref, cw_ref, scat_ref) = refs[:7]
    rider_in = refs[7:7 + RIDER_INPUTS]
    (kt_ref, vt_ref, qt16_ref, kaug_ref, vt16_ref, gqkv_ref, gz_ref, sm_ref, smt_ref, cst_ref,
     rider_out, carry_ref, prev_ref) = refs[7 + RIDER_INPUTS:20 + RIDER_INPUTS]
    rider_scratch = refs[20 + RIDER_INPUTS:]
    i = pl.program_id(0)

    @pl.when(i % tiles_per_seq == 0)
    def _():
        carry_ref[...] = jnp.zeros_like(carry_ref)
        prev_ref[...] = jnp.zeros_like(prev_ref)

    _rider_step(pt_ref, rider_in, rider_out, rider_scratch, rider_cfg)

    tm = x_ref.shape[0]
    h16 = _rms(x_ref[...], nrm_ref[...]).astype(BF16)

    fox = _dot(h16, wbig_ref[:, 0:3 * FOX_WIDTH])
    fq, fk, fv = fox[:, :FOX_WIDTH], fox[:, FOX_WIDTH:2 * FOX_WIDTH], fox[:, 2 * FOX_WIDTH:]
    fv_t = fv.T
    kt_ref[0] = fk.T
    vt_ref[0] = fv_t
    vt16_ref[0] = fv_t.astype(BF16)
    lane_p = lax.broadcasted_iota(jnp.int32, (tm, LANES), 1)
    ones_blk = jnp.where(lane_p < 2 * FOX_AUG, 1.0, 0.0)
    q_aug = []
    for p in range(FOX_HEADS // 2):
        q_aug += [fq[:, p * LANES:(p + 1) * LANES] * FOX_QSCALE, ones_blk]
        kaug_ref[:, 2 * p * LANES:(2 * p + 1) * LANES] = fk[:, p * LANES:(p + 1) * LANES].astype(BF16)
    qt16_ref[0] = jnp.concatenate(q_aug, axis=1).T.astype(BF16)

    act = _gate_activations(_dot(h16, wsm_ref[...]), par_ref)
    row = lax.broadcasted_iota(jnp.int32, act.shape, 0)
    lane = lax.broadcasted_iota(jnp.int32, act.shape, 1)
    y = act
    yc = act
    k = 1
    while k < tm:
        y = y + jnp.where(row >= k, pltpu.roll(y, k, 0), 0.0)
        if k < GDN_CHUNK:
            yc = yc + jnp.where((row & (GDN_CHUNK - 1)) >= k, pltpu.roll(yc, k, 0), 0.0)
        k *= 2
    y = y + carry_ref[0:1, :]
    carry_ref[0:1, :] = y[tm - 1:tm, :]
    terms = jnp.concatenate(_split3(y * (-LOG2E)), axis=1)
    extra = _dot(terms, scat_ref[...]).astype(BF16)
    for p in range(FOX_HEADS // 2):
        kaug_ref[:, (2 * p + 1) * LANES:(2 * p + 2) * LANES] = extra[:, p * LANES:(p + 1) * LANES]
    shift = SM_CUM - SM_LF
    sm = jnp.where(lane < SM_CUM, act,
                   jnp.where(lane < SM_GCUM, pltpu.roll(y, shift, 1),
                             jnp.where(lane < SM_GCUM + GDN_HEADS, pltpu.roll(yc, shift, 1), 0.0)))
    sm_ref[...] = sm
    smt_ref[0] = sm.T

    pre = _dot(h16, wbig_ref[:, 3 * FOX_WIDTH:3 * FOX_WIDTH + 3 * GDN_WIDTH])
    prev = prev_ref[...]
    acc = pre * cw_ref[GDN_CONV - 1:GDN_CONV, :]
    for kk in range(1, GDN_CONV):
        acc = acc + _shift_rows(pre, prev, kk) * cw_ref[GDN_CONV - 1 - kk:GDN_CONV - kk, :]
    prev_ref[...] = pre[tm - SUBLANES:, :]
    cst_ref[...] = pre[tm - SUBLANES:, :]
    _gdn_qkv_norm(_silu(acc), gqkv_ref)

    gz_ref[...] = _dot(h16, wbig_ref[:, 3 * FOX_WIDTH + 3 * GDN_WIDTH:])


def _proj_prompt(x, nrm, wbig, wsm, par, cw, seq_len, make_rider):
    rows = x.shape[0]
    tm = ROW_TILE
    n_seq = rows // seq_len
    tiles_per_seq = seq_len // tm
    wb = wbig.shape[1]
    row_spec = lambda w: pl.BlockSpec((tm, w), lambda i, pt: (i, 0))
    col_spec = lambda w: pl.BlockSpec((1, w, tm), lambda i, pt: (i // tiles_per_seq, 0, i % tiles_per_seq))
    const = lambda s: pl.BlockSpec(s, lambda i, pt: (0, 0), pipeline_mode=pl.Buffered(1))
    rider = make_rider(rows // tm)
    aug_width = FOX_HEADS * LANES
    src = jnp.arange(3 * LANES)
    term, head = src // LANES, src % LANES
    dst = (head // 2) * LANES + (head % 2) * FOX_AUG + term
    scat = ((jnp.arange(FOX_WIDTH)[None, :] == dst[:, None]) & (head[:, None] < FOX_HEADS)).astype(BF16)
    out_shape = (
        jax.ShapeDtypeStruct((n_seq, FOX_WIDTH, seq_len), F32),
        jax.ShapeDtypeStruct((n_seq, FOX_WIDTH, seq_len), F32),
        jax.ShapeDtypeStruct((n_seq, aug_width, seq_len), BF16),
        jax.ShapeDtypeStruct((rows, aug_width), BF16),
        jax.ShapeDtypeStruct((n_seq, FOX_WIDTH, seq_len), BF16),
        jax.ShapeDtypeStruct((rows, 3 * GDN_WIDTH), F32),
        jax.ShapeDtypeStruct((rows, GDN_WIDTH), F32),
        jax.ShapeDtypeStruct((rows, LANES), F32),
        jax.ShapeDtypeStruct((n_seq, LANES, seq_len), F32),
        jax.ShapeDtypeStruct((n_seq * SUBLANES, 3 * GDN_WIDTH), F32),
    )
    out_specs = (
        col_spec(FOX_WIDTH), col_spec(FOX_WIDTH), col_spec(aug_width), row_spec(aug_width),
        col_spec(FOX_WIDTH), row_spec(3 * GDN_WIDTH), row_spec(GDN_WIDTH), row_spec(LANES),
        col_spec(LANES),
        pl.BlockSpec((SUBLANES, 3 * GDN_WIDTH), lambda i, pt: (i // tiles_per_seq, 0)),
    )
    grid_spec = pltpu.PrefetchScalarGridSpec(
        num_scalar_prefetch=1,
        grid=(rows // tm,),
        in_specs=[row_spec(D_MODEL), const((1, D_MODEL)), const((D_MODEL, wb)), const((D_MODEL, LANES)),
                  const((SUBLANES, LANES)), const((SUBLANES, 3 * GDN_WIDTH)), const(scat.shape)]
        + rider["in_specs"],
        out_specs=out_specs + (rider["out_spec"],),
        scratch_shapes=[pltpu.VMEM((SUBLANES, LANES), F32), pltpu.VMEM((SUBLANES, 3 * GDN_WIDTH), F32)]
        + rider["scratch"],
    )
    return pl.pallas_call(
        functools.partial(_proj_prompt_body, tiles_per_seq=tiles_per_seq, rider_cfg=rider["cfg"]),
        grid_spec=grid_spec,
        out_shape=out_shape + (rider["out_shape"],),
        compiler_params=pltpu.CompilerParams(dimension_semantics=("arbitrary",), vmem_limit_bytes=56 * MIB),
        name="proj_prompt",
    )(rider["prefetch"], x, nrm, wbig, wsm, par, cw, scat, *rider["operands"])


def _proj_sample_body(x_ref, nrm_ref, wbig_ref, wsm_ref, par_ref, cw_ref, c0_ref, c1_ref, c2_ref,
                      fk_ref, fv_ref, q_ref, gqkv_ref, gz_ref, sm_ref, pre_ref):
    h16 = _rms(x_ref[...], nrm_ref[...]).astype(BF16)
    fox = _dot(h16, wbig_ref[:, 0:3 * FOX_WIDTH])
    q_ref[...] = fox[:, :FOX_WIDTH] * FOX_SCALE
    fk_ref[...] = fox[:, FOX_WIDTH:2 * FOX_WIDTH]
    fv_ref[...] = fox[:, 2 * FOX_WIDTH:]
    sm_ref[...] = _gate_activations(_dot(h16, wsm_ref[...]), par_ref)
    pre = _dot(h16, wbig_ref[:, 3 * FOX_WIDTH:3 * FOX_WIDTH + 3 * GDN_WIDTH])
    pre_ref[...] = pre
    acc = (pre * cw_ref[3:4, :] + c2_ref[...] * cw_ref[2:3, :]
           + c1_ref[...] * cw_ref[1:2, :] + c0_ref[...] * cw_ref[0:1, :])
    _gdn_qkv_norm(_silu(acc), gqkv_ref)
    gz_ref[...] = _dot(h16, wbig_ref[:, 3 * FOX_WIDTH + 3 * GDN_WIDTH:])


def _proj_sample(x, nrm, wbig, wsm, par, cw, c0, c1, c2):
    rows = x.shape[0]
    full = lambda a: pl.BlockSpec(a.shape, lambda i: (0,) * a.ndim)
    args = (x, nrm, wbig, wsm, par, cw, c0, c1, c2)
    shapes = ((rows, FOX_WIDTH), (rows, FOX_WIDTH), (rows, FOX_WIDTH), (rows, 3 * GDN_WIDTH),
              (rows, GDN_WIDTH), (rows, LANES), (rows, 3 * GDN_WIDTH))
    return pl.pallas_call(
        _proj_sample_body,
        grid=(1,),
        in_specs=[full(a) for a in args],
        out_specs=tuple(pl.BlockSpec(s, lambda i: (0, 0)) for s in shapes),
        out_shape=tuple(jax.ShapeDtypeStruct(s, F32) for s in shapes),
        compiler_params=pltpu.CompilerParams(dimension_semantics=("arbitrary",), vmem_limit_bytes=48 * MIB),
        name="proj_sample",
    )(*args)


def _fox_prompt_body(qt_ref, kaug_ref, vt_ref, o_ref):
    i = pl.program_id(1)
    t = FOX_TILE
    hd = FOX_HEAD_DIM
    aug = 2 * LANES
    rr = lax.broadcasted_iota(jnp.int32, (t, t), 0)
    cc = lax.broadcasted_iota(jnp.int32, (t, t), 1)
    causal = rr <= cc
    row = lax.broadcasted_iota(jnp.int32, (aug, t), 0)
    qs = []
    for h in range(FOX_HEADS):
        e = h % 2
        qp = qt_ref[0, (h // 2) * aug:(h // 2 + 1) * aug, :]
        own = jnp.logical_or(jnp.logical_and(row >= e * hd, row < (e + 1) * hd),
                             jnp.logical_and(row >= LANES + e * FOX_AUG, row < LANES + (e + 1) * FOX_AUG))
        qs.append(jnp.where(own, qp, jnp.zeros_like(qp)))

    def tile(j, carry, masked):
        koff = pl.multiple_of(j * t, t)
        ss = [_dot(kaug_ref[pl.ds(koff, t), (h // 2) * aug:(h // 2 + 1) * aug], qs[h])
              for h in range(FOX_HEADS)]
        stats, pms = [], []
        for h in range(FOX_HEADS):
            m = carry[h][0]
            s = jnp.where(causal, ss[h], NEG_BIG) if masked else ss[h]
            m_new = jnp.maximum(m, jnp.max(s, axis=0, keepdims=True))
            stats.append((m_new, jnp.exp2(m - m_new)))
            pms.append(jnp.exp2(s - m_new).astype(BF16))
        out = []
        for h in range(FOX_HEADS):
            m_new, alpha = stats[h]
            v_ones = jnp.concatenate([vt_ref[0, h * hd:(h + 1) * hd, pl.ds(koff, t)], ones_rows], axis=0)
            out.append((m_new, alpha * carry[h][1] + _dot(v_ones, pms[h])))
        return tuple(out)

    ones_rows = jnp.ones((FOX_DEN_ROWS, t), BF16)
    init = tuple((jnp.full((1, t), NEG_BIG, F32), jnp.zeros((hd + FOX_DEN_ROWS, t), F32))
                 for _ in range(FOX_HEADS))
    carry = lax.fori_loop(0, i, lambda j, c: tile(j, c, False), init)
    final = tile(i, carry, True)
    for p in range(FOX_HEADS // 2):
        outs = [a[:hd] / a[hd:hd + 1] for _, a in final[2 * p:2 * p + 2]]
        o_ref[:, p * LANES:(p + 1) * LANES] = jnp.concatenate(outs, axis=0).T.astype(BF16)


def _fox_prompt(qt16, kaug, vt16, n_seq, seq_len):
    t = FOX_TILE
    nq = seq_len // t
    aug_width = kaug.shape[1]
    return pl.pallas_call(
        _fox_prompt_body,
        grid=(n_seq, nq),
        in_specs=[pl.BlockSpec((1, aug_width, t), lambda b, i: (b, 0, i)),
                  pl.BlockSpec((seq_len, aug_width), lambda b, i: (b, 0)),
                  pl.BlockSpec((1, FOX_WIDTH, seq_len), lambda b, i: (b, 0, 0))],
        out_specs=pl.BlockSpec((t, FOX_WIDTH), lambda b, i: (b * nq + i, 0)),
        out_shape=jax.ShapeDtypeStruct((n_seq * seq_len, FOX_WIDTH), BF16),
        compiler_params=pltpu.CompilerParams(dimension_semantics=("arbitrary", "arbitrary"),
                                             vmem_limit_bytes=40 * MIB),
        name="fox_prompt",
    )(qt16, kaug, vt16)


def _bdot(a, b):
    return lax.dot_general(a.astype(BF16), b.astype(BF16), (((2,), (1,)), ((0,), (0,))),
                           preferred_element_type=F32)


def _bdot_nt(a, b):
    return lax.dot_general(a.astype(BF16), b.astype(BF16), (((2,), (2,)), ((0,), (0,))),
                           preferred_element_type=F32)


def _unit_lower_inverse_minus_eye(a, r, c):
    blk16 = (r // 16) == (c // 16)
    blk32 = (r // 32) == (c // 32)
    p = jnp.where(blk16, -a, 0.0)
    dt = p
    for _ in range(3):
        p = _bdot(p, p)
        dt = dt + p + _bdot(dt, p)
    for off in (jnp.where(jnp.logical_and(blk32, jnp.logical_not(blk16)), a, 0.0),
                jnp.where(blk32, 0.0, a)):
        x = off + _bdot(dt, off)
        dt = dt - (x + _bdot(x, dt))
    return dt


def _gdn_prompt_body(qkv_ref, sm_ref, gcr_ref, z_ref, gn_ref, o_ref, sout_ref, s_ref):
    ci = pl.program_id(1)

    @pl.when(ci == 0)
    def _():
        s_ref[...] = jnp.zeros_like(s_ref)

    ch = GDN_CHUNK
    d = GDN_HEAD_DIM
    nh = GDN_HEADS
    n_seq = qkv_ref.shape[0]
    n_chunks = qkv_ref.shape[1] // ch
    units = [(cidx, g, hh) for cidx in range(n_chunks) for g in range(n_seq) for hh in range(nh)]
    per_chunk = n_seq * nh

    def gather(fn):
        return jnp.stack([fn(g, slice(cidx * ch, (cidx + 1) * ch), hh) for cidx, g, hh in units], axis=0)

    q = gather(lambda g, rows, hh: qkv_ref[g, rows, hh * d:(hh + 1) * d])
    k = gather(lambda g, rows, hh: qkv_ref[g, rows, GDN_WIDTH + hh * d:GDN_WIDTH + (hh + 1) * d])
    v = gather(lambda g, rows, hh: qkv_ref[g, rows, 2 * GDN_WIDTH + hh * d:2 * GDN_WIDTH + (hh + 1) * d])
    beta = gather(lambda g, rows, hh: sm_ref[g, rows, SM_BETA + hh:SM_BETA + hh + 1])
    gc = gather(lambda g, rows, hh: sm_ref[g, rows, SM_GCUM + hh:SM_GCUM + hh + 1])
    gr = gather(lambda g, rows, hh: gcr_ref[g, hh:hh + 1, rows])

    r = lax.broadcasted_iota(jnp.int32, (1, ch, ch), 1)
    c = lax.broadcasted_iota(jnp.int32, (1, ch, ch), 2)
    lower = r >= c
    beta = jnp.broadcast_to(beta, (len(units), ch, d))
    gc = jnp.broadcast_to(gc, (len(units), ch, d))
    decay = jnp.where(lower, jnp.exp(jnp.where(lower, gc[:, :, :ch] - gr, 0.0)), 0.0)
    qk_kk = _bdot_nt(jnp.concatenate([q, k], axis=1), k)
    qk = qk_kk[:, :ch] * decay
    a = jnp.where(r > c, beta[:, :, :ch] * qk_kk[:, ch:] * decay, 0.0)
    dt = _unit_lower_inverse_minus_eye(a, r, c)
    eg = jnp.exp(gc)
    rhs = jnp.concatenate([v * beta, k * (beta * eg)], axis=-1)
    uw = rhs + _bdot(dt, rhs)
    gl = gc[:, ch - 1:ch, :]
    qd = (q * eg).astype(BF16)
    kd = k * jnp.exp(gl - gc)
    g_last = jnp.exp(gl)
    qk16 = qk.astype(BF16)

    s = s_ref[...]
    for cidx in range(n_chunks):
        us = slice(cidx * per_chunk, (cidx + 1) * per_chunk)
        rows = slice(cidx * ch, (cidx + 1) * ch)
        s16 = s.astype(BF16)
        v_new = uw[us, :, :d] - _bdot(uw[us, :, d:], s16)
        o = _bdot(qd[us], s16) + _bdot(qk16[us], v_new)
        vn16 = v_new.astype(BF16)
        upd = jnp.stack([_dot(kd[cidx * per_chunk + j].T.astype(BF16), vn16[j]) for j in range(per_chunk)],
                        axis=0)
        s = s * g_last[us] + upd
        for g in range(n_seq):
            for hh in range(nh):
                hs = slice(hh * d, (hh + 1) * d)
                o_ref[g, rows, hs] = (_rms(o[g * nh + hh], gn_ref[...]) * _silu(z_ref[g, rows, hs])).astype(BF16)
    s_ref[...] = s

    @pl.when(ci == pl.num_programs(1) - 1)
    def _():
        sout_ref[...] = s.reshape(sout_ref.shape)


def _gdn_prompt(gqkv, sm, smt, gz, gn, n_seq, seq_len):
    tm = ROW_TILE
    nt = seq_len // tm
    gcum_block = SM_GCUM // SUBLANES
    group = 2 if n_seq % 2 == 0 else 1
    seq3 = lambda a: a.reshape(n_seq, seq_len, a.shape[-1])
    tile = lambda w: pl.BlockSpec((group, tm, w), lambda b, i: (b, i, 0))
    out, s_out = pl.pallas_call(
        _gdn_prompt_body,
        grid=(n_seq // group, nt),
        in_specs=[tile(3 * GDN_WIDTH), tile(LANES),
                  pl.BlockSpec((group, SUBLANES, tm), lambda b, i: (b, gcum_block, i)),
                  tile(GDN_WIDTH),
                  pl.BlockSpec((1, GDN_HEAD_DIM), lambda b, i: (0, 0))],
        out_specs=(tile(GDN_WIDTH),
                   pl.BlockSpec((group, GDN_HEADS, GDN_HEAD_DIM, GDN_HEAD_DIM), lambda b, i: (b, 0, 0, 0))),
        out_shape=(jax.ShapeDtypeStruct((n_seq, seq_len, GDN_WIDTH), BF16),
                   jax.ShapeDtypeStruct((n_seq, GDN_HEADS, GDN_HEAD_DIM, GDN_HEAD_DIM), F32)),
        scratch_shapes=[pltpu.VMEM((group * GDN_HEADS, GDN_HEAD_DIM, GDN_HEAD_DIM), F32)],
        compiler_params=pltpu.CompilerParams(dimension_semantics=("arbitrary", "arbitrary"),
                                             vmem_limit_bytes=40 * MIB),
        name="gdn_prompt",
    )(seq3(gqkv), seq3(sm), smt, seq3(gz), gn)
    return out.reshape(n_seq * seq_len, GDN_WIDTH), s_out


def _ffn_tail(x, fox_ref, gdn_ref, wo_ref, nf_ref, wup_ref, wd_ref, nfin_ref, conv):
    mix = _dot(fox_ref[...], wo_ref[0:FOX_WIDTH, :]) + _dot(gdn_ref[...], wo_ref[FOX_WIDTH:, :])
    x2 = x + mix
    h2 = _rms(x2, nf_ref[...]).astype(BF16)
    gu = _dot(h2, wup_ref[...])
    gate, up = gu[:, :FFN_DIM], gu[:, FFN_DIM:]
    act = (_silu(conv(gate)) * up).astype(BF16)
    x3 = x2 + _dot(act, wd_ref[...])
    return _rms(x3, nfin_ref[...])


def _ffn_prompt_body(pt_ref, *refs, tiles_per_seq, rider_cfg):
    (x_ref, fox_ref, gdn_ref, wo_ref, nf_ref, wup_ref, cw_ref, cb_ref, wd_ref, nfin_ref) = refs[:10]
    rider_in = refs[10:10 + RIDER_INPUTS]
    y_ref, cst_ref, rider_out, prev_ref = refs[10 + RIDER_INPUTS:14 + RIDER_INPUTS]
    rider_scratch = refs[14 + RIDER_INPUTS:]
    i = pl.program_id(0)

    @pl.when(i % tiles_per_seq == 0)
    def _():
        prev_ref[...] = jnp.zeros_like(prev_ref)

    _rider_step(pt_ref, rider_in, rider_out, rider_scratch, rider_cfg)

    tm = x_ref.shape[0]

    def conv(gate):
        prev = prev_ref[...]
        out = gate * cw_ref[FFN_CONV - 1:FFN_CONV, :] + cb_ref[...]
        for kk in range(1, FFN_CONV):
            out = out + _shift_rows(gate, prev, kk) * cw_ref[FFN_CONV - 1 - kk:FFN_CONV - kk, :]
        prev_ref[...] = gate[tm - SUBLANES:, :]
        cst_ref[...] = gate[tm - SUBLANES:, :]
        return out

    y_ref[...] = _ffn_tail(x_ref[...], fox_ref, gdn_ref, wo_ref, nf_ref, wup_ref, wd_ref, nfin_ref, conv)


def _ffn_prompt(x, fox, gdn, wo, nf, wup, cw, cb, wd, nfin, seq_len, make_rider):
    rows = x.shape[0]
    tm = ROW_TILE
    n_seq = rows // seq_len
    tiles_per_seq = seq_len // tm
    row_spec = lambda w: pl.BlockSpec((tm, w), lambda i, pt: (i, 0))
    const = lambda a: pl.BlockSpec(a.shape, lambda i, pt: (0, 0), pipeline_mode=pl.Buffered(1))
    rider = make_rider(rows // tm)
    grid_spec = pltpu.PrefetchScalarGridSpec(
        num_scalar_prefetch=1,
        grid=(rows // tm,),
        in_specs=[row_spec(D_MODEL), row_spec(FOX_WIDTH), row_spec(GDN_WIDTH), const(wo), const(nf),
                  const(wup), const(cw), const(cb), const(wd), const(nfin)] + rider["in_specs"],
        out_specs=(row_spec(D_MODEL),
                   pl.BlockSpec((SUBLANES, FFN_DIM), lambda i, pt: (i // tiles_per_seq, 0)),
                   rider["out_spec"]),
        scratch_shapes=[pltpu.VMEM((SUBLANES, FFN_DIM), F32)] + rider["scratch"],
    )
    return pl.pallas_call(
        functools.partial(_ffn_prompt_body, tiles_per_seq=tiles_per_seq, rider_cfg=rider["cfg"]),
        grid_spec=grid_spec,
        out_shape=(jax.ShapeDtypeStruct((rows, D_MODEL), F32),
                   jax.ShapeDtypeStruct((n_seq * SUBLANES, FFN_DIM), F32),
                   rider["out_shape"]),
        compiler_params=pltpu.CompilerParams(dimension_semantics=("arbitrary",), vmem_limit_bytes=58 * MIB),
        name="ffn_prompt",
    )(rider["prefetch"], x, fox, gdn, wo, nf, wup, cw, cb, wd, nfin, *rider["operands"])


def _ffn_sample_body(x_ref, fox_ref, gdn_ref, wo_ref, nf_ref, wup_ref, cw_ref, cb_ref, wd_ref, nfin_ref,
                     c0_ref, c1_ref, y_ref, gate_ref):
    def conv(gate):
        gate_ref[...] = gate
        return (gate * cw_ref[2:3, :] + c1_ref[...] * cw_ref[1:2, :] + c0_ref[...] * cw_ref[0:1, :]
                + cb_ref[...])

    y_ref[...] = _ffn_tail(x_ref[...], fox_ref, gdn_ref, wo_ref, nf_ref, wup_ref, wd_ref, nfin_ref, conv)


def _ffn_sample(x, fox, gdn, wo, nf, wup, cw, cb, wd, nfin, c0, c1):
    rows = x.shape[0]
    args = (x, fox, gdn, wo, nf, wup, cw, cb, wd, nfin, c0, c1)
    full = lambda a: pl.BlockSpec(a.shape, lambda i: (0, 0), pipeline_mode=pl.Buffered(1))
    return pl.pallas_call(
        _ffn_sample_body,
        grid=(1,),
        in_specs=[full(a) for a in args],
        out_specs=(pl.BlockSpec((rows, D_MODEL), lambda i: (0, 0)),
                   pl.BlockSpec((rows, FFN_DIM), lambda i: (0, 0))),
        out_shape=(jax.ShapeDtypeStruct((rows, D_MODEL), F32), jax.ShapeDtypeStruct((rows, FFN_DIM), F32)),
        compiler_params=pltpu.CompilerParams(dimension_semantics=("arbitrary",), vmem_limit_bytes=56 * MIB),
        name="ffn_sample",
    )(*args)


def _split3(x):
    hi = x.astype(BF16)
    r1 = x - hi.astype(F32)
    mid = r1.astype(BF16)
    lo = (r1 - mid.astype(F32)).astype(BF16)
    return hi, mid, lo


RIDER_INPUTS = 8
RIDER_SCRATCH = 4


def _rider_step(pt_ref, rider_in, o_ref, rider_scratch, cfg):
    q_ref, kn_ref, vn_ref, sm_ref, msuf_ref, kt_pool, vt_pool, lf_pool = rider_in
    kbuf, vbuf, lbuf, sems = rider_scratch
    base, count, per_step, n_pages = cfg
    i = pl.program_id(0)

    def copies(g, sl):
        out = []
        for j in range(n_pages):
            pg = pt_ref[g * n_pages + j]
            out.append(pltpu.make_async_copy(kt_pool.at[pg], kbuf.at[sl, j], sems.at[sl, 0]))
            out.append(pltpu.make_async_copy(vt_pool.at[pg], vbuf.at[sl, j], sems.at[sl, 1]))
            out.append(pltpu.make_async_copy(lf_pool.at[pg], lbuf.at[sl, j], sems.at[sl, 2]))
        return out

    @pl.when(i == 0)
    def _():
        for cp in copies(base, base % 2):
            cp.start()

    for r in range(per_step):
        g = base + i * per_step + r
        slot = g % 2

        @pl.when(g + 1 < base + count)
        def _():
            for cp in copies(g + 1, 1 - slot):
                cp.start()

        for cp in copies(g, slot):
            cp.wait()
        o_ref[r] = _decode_row(slot, q_ref[r], kn_ref[r], vn_ref[r], sm_ref[r], msuf_ref[...],
                               kbuf, vbuf, lbuf, n_pages)


def _decode_row(slot, q, k_new, v_new, sm_row, msuf, kbuf, vbuf, lbuf, n_pages):
    hd = FOX_HEAD_DIM
    page = kbuf.shape[-1]

    r8 = lax.broadcasted_iota(jnp.int32, (SUBLANES, LANES), 0)
    c8 = lax.broadcasted_iota(jnp.int32, (SUBLANES, LANES), 1)
    lf_new = jnp.sum(jnp.where(r8 == c8, sm_row, 0.0), axis=1, keepdims=True)
    n_rows = n_pages * FOX_HEADS
    lf = lbuf[slot].reshape(n_rows, page)
    within = functools.reduce(lambda a, c: a + c, [_dot(t, msuf) for t in _split3(lf)])
    tot = jnp.broadcast_to(jnp.sum(lf, axis=1, keepdims=True), (n_rows, page))
    row = lax.broadcasted_iota(jnp.int32, (n_rows, page), 0)
    later = tot
    k = FOX_HEADS
    while k < n_rows:
        later = later + jnp.where(row + k < n_rows, pltpu.roll(later, n_rows - k, 0), 0.0)
        k *= 2
    bias = (within + (later - tot)).reshape(n_pages, FOX_HEADS, page) + lf_new[None]

    q_t = q.T
    qb = jnp.stack([jnp.broadcast_to(q_t[:, h:h + 1], (hd, page)) for h in range(FOX_HEADS)], axis=0)
    logits = [jnp.sum(kbuf[slot, j] * qb, axis=1) + bias[j] for j in range(n_pages)]
    s_new = jnp.sum(k_new * q, axis=-1, keepdims=True)
    m = functools.reduce(jnp.maximum, [jnp.max(x, axis=-1, keepdims=True) for x in logits] + [s_new])
    ps = [jnp.exp(x - m) for x in logits]
    p_new = jnp.exp(s_new - m)
    l = functools.reduce(lambda a, c: a + c, [jnp.sum(x, axis=-1, keepdims=True) for x in ps] + [p_new])

    cols = []
    for h in range(FOX_HEADS):
        acc = vbuf[slot, 0, h] * ps[0][h:h + 1, :]
        for j in range(1, n_pages):
            acc = acc + vbuf[slot, j, h] * ps[j][h:h + 1, :]
        cols.append(jnp.sum(acc, axis=1, keepdims=True))
    mat = jnp.concatenate(cols + [jnp.zeros((hd, LANES - FOX_HEADS), F32)], axis=1)
    return (mat.T[0:FOX_HEADS, :] + p_new * v_new) / l


def _rider(page_table, q, kn, vn, sm, kt_pool, vt_pool, lf_pool, base, count, n_steps):
    n_pages = page_table.shape[1]
    page = kt_pool.shape[-1]
    per_step = count // n_steps
    assert per_step * n_steps == count, "sample rows must split evenly over the host's grid steps"
    tok = jnp.arange(page)
    msuf = (tok[:, None] > tok[None, :]).astype(BF16)
    rows = lambda a: a[base:base + count]
    tile = lambda a: pl.BlockSpec((per_step,) + a.shape[1:], lambda i, pt: (i,) + (0,) * (a.ndim - 1))
    hbm = pl.BlockSpec(memory_space=pl.ANY)
    operands = [rows(q), rows(kn), rows(vn), rows(sm).reshape(count, 1, LANES), msuf, kt_pool, vt_pool, lf_pool]
    kv_buf = pltpu.VMEM((2, n_pages, FOX_HEADS, FOX_HEAD_DIM, page), F32)
    return dict(
        prefetch=page_table.reshape(-1),
        operands=operands,
        in_specs=[tile(a) for a in operands[:4]] + [pl.BlockSpec(msuf.shape, lambda i, pt: (0, 0)), hbm, hbm, hbm],
        out_spec=pl.BlockSpec((per_step, FOX_HEADS, FOX_HEAD_DIM), lambda i, pt: (i, 0, 0)),
        out_shape=jax.ShapeDtypeStruct((count, FOX_HEADS, FOX_HEAD_DIM), F32),
        scratch=[kv_buf, kv_buf, pltpu.VMEM((2, n_pages, FOX_HEADS, page), F32), pltpu.SemaphoreType.DMA((2, 3))],
        cfg=(base, count, per_step, n_pages),
    )


def _gdn_sample_body(s_ref, qkv_ref, sm_ref, z_ref, gn_ref, sout_ref, o_ref):
    d = GDN_HEAD_DIM
    eye = lax.broadcasted_iota(jnp.int32, (d, d), 0) == lax.broadcasted_iota(jnp.int32, (d, d), 1)

    def col(rows):
        return jnp.sum(jnp.where(eye[None], rows[:, None, :], 0.0), axis=2, keepdims=True)

    for hh in range(GDN_HEADS):
        hs = slice(hh * d, (hh + 1) * d)
        q = qkv_ref[:, hs]
        k = qkv_ref[:, GDN_WIDTH + hh * d:GDN_WIDTH + (hh + 1) * d]
        v = qkv_ref[:, 2 * GDN_WIDTH + hh * d:2 * GDN_WIDTH + (hh + 1) * d]
        eg = jnp.exp(sm_ref[:, SM_G + hh:SM_G + hh + 1])
        beta = sm_ref[:, SM_BETA + hh:SM_BETA + hh + 1]
        s = s_ref[:, hh]
        k_col = col(k)
        v_new = beta * (v - eg * jnp.sum(k_col * s, axis=1))
        o = eg * jnp.sum(col(q) * s, axis=1) + jnp.sum(q * k, axis=1, keepdims=True) * v_new
        sout_ref[:, hh] = s * eg[:, :, None] + k_col * v_new[:, None, :]
        o_ref[:, hs] = _rms(o, gn_ref[...]) * _silu(z_ref[:, hs])


def _gdn_sample(state, gqkv, sm, gz, gn):
    nb = state.shape[0]
    bb = SUBLANES
    d = GDN_HEAD_DIM
    row_spec = lambda w: pl.BlockSpec((bb, w), lambda i: (i, 0))
    st_spec = pl.BlockSpec((bb, GDN_HEADS, d, d), lambda i: (i, 0, 0, 0))
    return pl.pallas_call(
        _gdn_sample_body,
        grid=(nb // bb,),
        in_specs=[st_spec, row_spec(3 * GDN_WIDTH), row_spec(LANES), row_spec(GDN_WIDTH),
                  pl.BlockSpec((1, d), lambda i: (0, 0))],
        out_specs=(st_spec, row_spec(GDN_WIDTH)),
        out_shape=(jax.ShapeDtypeStruct(state.shape, F32), jax.ShapeDtypeStruct((nb, GDN_WIDTH), F32)),
        compiler_params=pltpu.CompilerParams(dimension_semantics=("arbitrary",), vmem_limit_bytes=32 * MIB),
        name="gdn_sample",
    )(state, gqkv, sm, gz, gn)


def _pad_rows(a, rows):
    return jnp.concatenate([a, jnp.zeros((rows - a.shape[0],) + a.shape[1:], a.dtype)], axis=0)


def kernel(x_prompt, x_sample, cache_k, cache_v, cache_logf, state_gdn, state_gdn_conv, state_ffn_conv,
           page_table, norm_mix, w_in, b_forget, gdn_a_log, gdn_dt_bias, w_gdn_conv, gdn_out_norm, w_out,
           norm_ffn, w_up, w_ffn_conv, b_ffn_conv, w_down, norm_final):
    assert w_in.shape[0] == 1, "single-layer trunk"
    n_seq, seq_len, _ = x_prompt.shape
    nb = x_sample.shape[0]
    n_pool, page = cache_k.shape[1], cache_k.shape[2]

    w = w_in[0]
    o_ff = 3 * FOX_WIDTH
    o_g = o_ff + FOX_HEADS
    o_ga = o_g + 3 * GDN_WIDTH
    o_gz = o_ga + 2 * GDN_HEADS
    wbig = jnp.concatenate([w[:, :o_ff], w[:, o_g:o_ga], w[:, o_gz:]], axis=1).astype(BF16)
    wsm = jnp.concatenate([w[:, o_ff:o_g], w[:, o_ga:o_gz],
                           jnp.zeros((D_MODEL, LANES - FOX_HEADS - 2 * GDN_HEADS), F32)], axis=1).astype(BF16)
    par = jnp.zeros((SUBLANES, LANES), F32)
    par = par.at[0, SM_LF:SM_LF + FOX_HEADS].set(b_forget[0])
    par = par.at[0, SM_G:SM_G + GDN_HEADS].set(gdn_dt_bias[0])
    par = par.at[1, SM_G:SM_G + GDN_HEADS].set(gdn_a_log[0])
    nrm = norm_mix[0][None, :]
    cw = _pad_rows(w_gdn_conv[0], SUBLANES)
    gn = gdn_out_norm[0][None, :]
    wo = w_out[0].astype(BF16)
    nf = norm_ffn[0][None, :]
    wup = w_up[0].astype(BF16)
    cwf = _pad_rows(w_ffn_conv[0], SUBLANES)
    cbf = b_ffn_conv[0][None, :]
    wd = w_down[0].astype(BF16)
    nfin = norm_final[None, :]

    xs = x_sample.reshape(nb, D_MODEL)
    gctx = state_gdn_conv[0]
    fctx = state_ffn_conv[0]
    fk_s, fv_s, q_s, gqkv_s, gz_s, sm_s, pre_s = _proj_sample(
        xs, nrm, wbig, wsm, par, cw, gctx[:, 0], gctx[:, 1], gctx[:, 2])
    heads = lambda a: a.reshape(nb, FOX_HEADS, FOX_HEAD_DIM)
    half = nb // 2
    decode_args = (page_table, heads(q_s), heads(fk_s), heads(fv_s), sm_s,
                   jnp.transpose(cache_k[0], (0, 2, 3, 1)), jnp.transpose(cache_v[0], (0, 2, 3, 1)),
                   jnp.transpose(cache_logf[0], (0, 2, 1)))
    rider_a = functools.partial(_rider, *decode_args, 0, half)
    rider_b = functools.partial(_rider, *decode_args, half, nb - half)

    xp = x_prompt.reshape(n_seq * seq_len, D_MODEL)
    kt, vt, qt16, kaug, vt16, gqkv, gz, sm, smt, cst, fox_sa = _proj_prompt(
        xp, nrm, wbig, wsm, par, cw, seq_len, rider_a)
    fox = _fox_prompt(qt16, kaug, vt16, n_seq, seq_len)
    gdn, s_p = _gdn_prompt(gqkv, sm, smt, gz, gn, n_seq, seq_len)
    yp, cstf, fox_sb = _ffn_prompt(xp, fox, gdn, wo, nf, wup, cwf, cbf, wd, nfin, seq_len, rider_b)

    fox_s = jnp.concatenate([fox_sa, fox_sb], axis=0)
    s_s, gdn_s = _gdn_sample(state_gdn[0], gqkv_s, sm_s, gz_s, gn)
    ys, gate_s = _ffn_sample(xs, fox_s.reshape(nb, FOX_WIDTH).astype(BF16), gdn_s.astype(BF16), wo, nf, wup,
                             cwf, cbf, wd, nfin, fctx[:, 0], fctx[:, 1])

    kv_shape_s = (1, nb, 1, FOX_HEADS, FOX_HEAD_DIM)
    new_kv_p = lambda a: jnp.transpose(a.reshape(1, n_seq, FOX_HEADS, FOX_HEAD_DIM, seq_len), (0, 1, 4, 2, 3))
    return (
        yp.reshape(n_seq, seq_len, D_MODEL),
        ys.reshape(nb, 1, D_MODEL),
        new_kv_p(kt),
        new_kv_p(vt),
        sm[:, SM_LF:SM_LF + FOX_HEADS].reshape(1, n_seq, seq_len, FOX_HEADS),
        s_p[None],
        cst.reshape(n_seq, SUBLANES, 3 * GDN_WIDTH)[None, :, SUBLANES - (GDN_CONV - 1):],
        cstf.reshape(n_seq, SUBLANES, FFN_DIM)[None, :, SUBLANES - (FFN_CONV - 1):],
        fk_s.reshape(kv_shape_s),
        fv_s.reshape(kv_shape_s),
        sm_s[:, SM_LF:SM_LF + FOX_HEADS].reshape(1, nb, 1, FOX_HEADS),
        s_s[None],
        jnp.concatenate([gctx[:, 1:], pre_s[:, None, :]], axis=1)[None],
        jnp.concatenate([fctx[:, 1:], gate_s[:, None, :]], axis=1)[None],
    )
```

```python
import functools

import jax
import jax.numpy as jnp
from jax import lax
from jax.experimental import pallas as pl
from jax.experimental.pallas import tpu as pltpu

D_MODEL = 1024
FOX_HEADS = 8
FOX_HEAD_DIM = 64
FOX_WIDTH = FOX_HEADS * FOX_HEAD_DIM
GDN_HEADS = 4
GDN_HEAD_DIM = 128
GDN_WIDTH = GDN_HEADS * GDN_HEAD_DIM
GDN_CONV = 4
GDN_CHUNK = 64
FFN_DIM = 2816
FFN_CONV = 3
EPS = 1e-6
NEG_BIG = -1e30
FOX_SCALE = FOX_HEAD_DIM ** -0.5
GDN_SCALE = GDN_HEAD_DIM ** -0.5
LOG2E = 1.4426950408889634
FOX_QSCALE = FOX_SCALE * LOG2E
FOX_AUG = 3
FOX_DEN_ROWS = 16
FOX_GROUP = 2

LANES = 128
SUBLANES = 8
ROW_TILE = 256
FOX_TILE = 256
MIB = 1024 * 1024

SM_LF = 0
SM_G = 8
SM_BETA = 12
SM_CUM = 16
SM_GCUM = 24

F32 = jnp.float32
BF16 = jnp.bfloat16


def _sigmoid(x):
    return 1.0 / (1.0 + jnp.exp(-x))


def _silu(x):
    return x * _sigmoid(x)


def _rms(x, g):
    return x * lax.rsqrt(jnp.mean(x * x, axis=-1, keepdims=True) + EPS) * g


def _dot(a, b):
    return jnp.dot(a, b, preferred_element_type=F32)


def _shift_rows(x, prev8, k):
    r = pltpu.roll(x, k, 0)
    row8 = lax.broadcasted_iota(jnp.int32, prev8.shape, 0)
    top = jnp.where(row8 < k, pltpu.roll(prev8, k, 0), r[0:SUBLANES])
    return jnp.concatenate([top, r[SUBLANES:]], axis=0)


def _gate_activations(raw, par_ref):
    z = raw + par_ref[0:1, :]
    lane = lax.broadcasted_iota(jnp.int32, z.shape, 1)
    t = jnp.log1p(jnp.exp(-jnp.abs(z)))
    lf = jnp.minimum(z, 0.0) - t
    softplus = jnp.maximum(z, 0.0) + t
    g = -jnp.exp(par_ref[1:2, :]) * softplus
    beta = _sigmoid(z)
    return jnp.where(lane < SM_G, lf, jnp.where(lane < SM_BETA, g, jnp.where(lane < SM_CUM, beta, 0.0)))


def _gdn_qkv_norm(c, out_ref):
    for part, scale in ((0, GDN_SCALE), (1, 1.0)):
        for hh in range(GDN_HEADS):
            off = part * GDN_WIDTH + hh * GDN_HEAD_DIM
            seg = c[:, off:off + GDN_HEAD_DIM]
            n = lax.rsqrt(jnp.sum(seg * seg, axis=-1, keepdims=True) + EPS)
            out_ref[:, off:off + GDN_HEAD_DIM] = seg * n * scale
    out_ref[:, 2 * GDN_WIDTH:] = c[:, 2 * GDN_WIDTH:]


def _proj_prompt_body(pt_ref, *refs, tiles_per_seq, rider_cfg):
    (x_ref, nrm_ref, wbig_ref, wsm_ref, par_ref, cw_ref, scat_ref) = refs[:7]
    rider_in = refs[7:7 + RIDER_INPUTS]
    (kt_ref, vt_ref, qt16_ref, kaug_ref, vt16_ref, gqkv_ref, gz_ref, sm_ref, smt_ref, cst_ref,
     rider_out, carry_ref, prev_ref) = refs[7 + RIDER_INPUTS:20 + RIDER_INPUTS]
    rider_scratch = refs[20 + RIDER_INPUTS:]
    i = pl.program_id(0)

    @pl.when(i % tiles_per_seq == 0)
    def _():
        carry_ref[...] = jnp.zeros_like(carry_ref)
        prev_ref[...] = jnp.zeros_like(prev_ref)

    _rider_step(pt_ref, rider_in, rider_out, rider_scratch, rider_cfg)

    tm = x_ref.shape[0]
    h16 = _rms(x_ref[...], nrm_ref[...]).astype(BF16)

    fox = _dot(h16, wbig_ref[:, 0:3 * FOX_WIDTH])
    fq, fk, fv = fox[:, :FOX_WIDTH], fox[:, FOX_WIDTH:2 * FOX_WIDTH], fox[:, 2 * FOX_WIDTH:]
    fv_t = fv.T
    kt_ref[0] = fk.T
    vt_ref[0] = fv_t
    vt16_ref[0] = fv_t.astype(BF16)
    lane_p = lax.broadcasted_iota(jnp.int32, (tm, LANES), 1)
    ones_blk = jnp.where(lane_p < 2 * FOX_AUG, 1.0, 0.0)
    q_aug = []
    for p in range(FOX_HEADS // 2):
        q_aug += [fq[:, p * LANES:(p + 1) * LANES] * FOX_QSCALE, ones_blk]
        kaug_ref[:, 2 * p * LANES:(2 * p + 1) * LANES] = fk[:, p * LANES:(p + 1) * LANES].astype(BF16)
    qt16_ref[0] = jnp.concatenate(q_aug, axis=1).T.astype(BF16)

    act = _gate_activations(_dot(h16, wsm_ref[...]), par_ref)
    row = lax.broadcasted_iota(jnp.int32, act.shape, 0)
    lane = lax.broadcasted_iota(jnp.int32, act.shape, 1)
    y = act
    yc = act
    k = 1
    while k < tm:
        y = y + jnp.where(row >= k, pltpu.roll(y, k, 0), 0.0)
        if k < GDN_CHUNK:
            yc = yc + jnp.where((row & (GDN_CHUNK - 1)) >= k, pltpu.roll(yc, k, 0), 0.0)
        k *= 2
    y = y + carry_ref[0:1, :]
    carry_ref[0:1, :] = y[tm - 1:tm, :]
    terms = jnp.concatenate(_split3(y * (-LOG2E)), axis=1)
    extra = _dot(terms, scat_ref[...]).astype(BF16)
    for p in range(FOX_HEADS // 2):
        kaug_ref[:, (2 * p + 1) * LANES:(2 * p + 2) * LANES] = extra[:, p * LANES:(p + 1) * LANES]
    shift = SM_CUM - SM_LF
    sm = jnp.where(lane < SM_CUM, act,
                   jnp.where(lane < SM_GCUM, pltpu.roll(y, shift, 1),
                             jnp.where(lane < SM_GCUM + GDN_HEADS, pltpu.roll(yc, shift, 1), 0.0)))
    sm_ref[...] = sm
    smt_ref[0] = sm.T

    pre = _dot(h16, wbig_ref[:, 3 * FOX_WIDTH:3 * FOX_WIDTH + 3 * GDN_WIDTH])
    prev = prev_ref[...]
    acc = pre * cw_ref[GDN_CONV - 1:GDN_CONV, :]
    for kk in range(1, GDN_CONV):
        acc = acc + _shift_rows(pre, prev, kk) * cw_ref[GDN_CONV - 1 - kk:GDN_CONV - kk, :]
    prev_ref[...] = pre[tm - SUBLANES:, :]
    cst_ref[...] = pre[tm - SUBLANES:, :]
    _gdn_qkv_norm(_silu(acc), gqkv_ref)

    gz_ref[...] = _dot(h16, wbig_ref[:, 3 * FOX_WIDTH + 3 * GDN_WIDTH:])


def _proj_prompt(x, nrm, wbig, wsm, par, cw, seq_len, make_rider):
    rows = x.shape[0]
    tm = ROW_TILE
    n_seq = rows // seq_len
    tiles_per_seq = seq_len // tm
    wb = wbig.shape[1]
    row_spec = lambda w: pl.BlockSpec((tm, w), lambda i, pt: (i, 0))
    col_spec = lambda w: pl.BlockSpec((1, w, tm), lambda i, pt: (i // tiles_per_seq, 0, i % tiles_per_seq))
    const = lambda s: pl.BlockSpec(s, lambda i, pt: (0, 0), pipeline_mode=pl.Buffered(1))
    rider = make_rider(rows // tm)
    aug_width = FOX_HEADS * LANES
    src = jnp.arange(3 * LANES)
    term, head = src // LANES, src % LANES
    dst = (head // 2) * LANES + (head % 2) * FOX_AUG + term
    scat = ((jnp.arange(FOX_WIDTH)[None, :] == dst[:, None]) & (head[:, None] < FOX_HEADS)).astype(BF16)
    out_shape = (
        jax.ShapeDtypeStruct((n_seq, FOX_WIDTH, seq_len), F32),
        jax.ShapeDtypeStruct((n_seq, FOX_WIDTH, seq_len), F32),
        jax.ShapeDtypeStruct((n_seq, aug_width, seq_len), BF16),
        jax.ShapeDtypeStruct((rows, aug_width), BF16),
        jax.ShapeDtypeStruct((n_seq, FOX_WIDTH, seq_len), BF16),
        jax.ShapeDtypeStruct((rows, 3 * GDN_WIDTH), F32),
        jax.ShapeDtypeStruct((rows, GDN_WIDTH), F32),
        jax.ShapeDtypeStruct((rows, LANES), F32),
        jax.ShapeDtypeStruct((n_seq, LANES, seq_len), F32),
        jax.ShapeDtypeStruct((n_seq * SUBLANES, 3 * GDN_WIDTH), F32),
    )
    out_specs = (
        col_spec(FOX_WIDTH), col_spec(FOX_WIDTH), col_spec(aug_width), row_spec(aug_width),
        col_spec(FOX_WIDTH), row_spec(3 * GDN_WIDTH), row_spec(GDN_WIDTH), row_spec(LANES),
        col_spec(LANES),
        pl.BlockSpec((SUBLANES, 3 * GDN_WIDTH), lambda i, pt: (i // tiles_per_seq, 0)),
    )
    grid_spec = pltpu.PrefetchScalarGridSpec(
        num_scalar_prefetch=1,
        grid=(rows // tm,),
        in_specs=[row_spec(D_MODEL), const((1, D_MODEL)), const((D_MODEL, wb)), const((D_MODEL, LANES)),
                  const((SUBLANES, LANES)), const((SUBLANES, 3 * GDN_WIDTH)), const(scat.shape)]
        + rider["in_specs"],
        out_specs=out_specs + (rider["out_spec"],),
        scratch_shapes=[pltpu.VMEM((SUBLANES, LANES), F32), pltpu.VMEM((SUBLANES, 3 * GDN_WIDTH), F32)]
        + rider["scratch"],
    )
    return pl.pallas_call(
        functools.partial(_proj_prompt_body, tiles_per_seq=tiles_per_seq, rider_cfg=rider["cfg"]),
        grid_spec=grid_spec,
        out_shape=out_shape + (rider["out_shape"],),
        compiler_params=pltpu.CompilerParams(dimension_semantics=("arbitrary",), vmem_limit_bytes=56 * MIB),
        name="proj_prompt",
    )(rider["prefetch"], x, nrm, wbig, wsm, par, cw, scat, *rider["operands"])


def _proj_sample_body(x_ref, nrm_ref, wbig_ref, wsm_ref, par_ref, cw_ref, c0_ref, c1_ref, c2_ref,
                      fk_ref, fv_ref, q_ref, gqkv_ref, gz_ref, sm_ref, pre_ref):
    h16 = _rms(x_ref[...], nrm_ref[...]).astype(BF16)
    fox = _dot(h16, wbig_ref[:, 0:3 * FOX_WIDTH])
    q_ref[...] = fox[:, :FOX_WIDTH] * FOX_SCALE
    fk_ref[...] = fox[:, FOX_WIDTH:2 * FOX_WIDTH]
    fv_ref[...] = fox[:, 2 * FOX_WIDTH:]
    sm_ref[...] = _gate_activations(_dot(h16, wsm_ref[...]), par_ref)
    pre = _dot(h16, wbig_ref[:, 3 * FOX_WIDTH:3 * FOX_WIDTH + 3 * GDN_WIDTH])
    pre_ref[...] = pre
    acc = (pre * cw_ref[3:4, :] + c2_ref[...] * cw_ref[2:3, :]
           + c1_ref[...] * cw_ref[1:2, :] + c0_ref[...] * cw_ref[0:1, :])
    _gdn_qkv_norm(_silu(acc), gqkv_ref)
    gz_ref[...] = _dot(h16, wbig_ref[:, 3 * FOX_WIDTH + 3 * GDN_WIDTH:])


def _proj_sample(x, nrm, wbig, wsm, par, cw, c0, c1, c2):
    rows = x.shape[0]
    full = lambda a: pl.BlockSpec(a.shape, lambda i: (0,) * a.ndim)
    args = (x, nrm, wbig, wsm, par, cw, c0, c1, c2)
    shapes = ((rows, FOX_WIDTH), (rows, FOX_WIDTH), (rows, FOX_WIDTH), (rows, 3 * GDN_WIDTH),
              (rows, GDN_WIDTH), (rows, LANES), (rows, 3 * GDN_WIDTH))
    return pl.pallas_call(
        _proj_sample_body,
        grid=(1,),
        in_specs=[full(a) for a in args],
        out_specs=tuple(pl.BlockSpec(s, lambda i: (0, 0)) for s in shapes),
        out_shape=tuple(jax.ShapeDtypeStruct(s, F32) for s in shapes),
        compiler_params=pltpu.CompilerParams(dimension_semantics=("arbitrary",), vmem_limit_bytes=48 * MIB),
        name="proj_sample",
    )(*args)


def _fox_prompt_body(qt_ref, kaug_ref, vt_ref, o_ref):
    i = pl.program_id(1)
    t = FOX_TILE
    hd = FOX_HEAD_DIM
    aug = 2 * LANES
    rr = lax.broadcasted_iota(jnp.int32, (t, t), 0)
    cc = lax.broadcasted_iota(jnp.int32, (t, t), 1)
    causal = rr <= cc
    row = lax.broadcasted_iota(jnp.int32, (aug, t), 0)
    chains = [(g, h) for g in range(qt_ref.shape[0]) for h in range(FOX_HEADS)]
    qs = []
    for g, h in chains:
        e = h % 2
        qp = qt_ref[g, (h // 2) * aug:(h // 2 + 1) * aug, :]
        own = jnp.logical_or(jnp.logical_and(row >= e * hd, row < (e + 1) * hd),
                             jnp.logical_and(row >= LANES + e * FOX_AUG, row < LANES + (e + 1) * FOX_AUG))
        qs.append(jnp.where(own, qp, jnp.zeros_like(qp)))

    def tile(j, carry, masked):
        koff = pl.multiple_of(j * t, t)
        ss = [_dot(kaug_ref[g, pl.ds(koff, t), (h // 2) * aug:(h // 2 + 1) * aug], qs[n])
              for n, (g, h) in enumerate(chains)]
        stats, pms = [], []
        for n in range(len(chains)):
            m = carry[n][0]
            s = jnp.where(causal, ss[n], NEG_BIG) if masked else ss[n]
            m_new = jnp.maximum(m, jnp.max(s, axis=0, keepdims=True))
            stats.append((m_new, jnp.exp2(m - m_new)))
            pms.append(jnp.exp2(s - m_new).astype(BF16))
        out = []
        for n, (g, h) in enumerate(chains):
            m_new, alpha = stats[n]
            v_ones = jnp.concatenate([vt_ref[g, h * hd:(h + 1) * hd, pl.ds(koff, t)], ones_rows], axis=0)
            out.append((m_new, alpha * carry[n][1] + _dot(v_ones, pms[n])))
        return tuple(out)

    ones_rows = jnp.ones((FOX_DEN_ROWS, t), BF16)
    init = tuple((jnp.full((1, t), NEG_BIG, F32), jnp.zeros((hd + FOX_DEN_ROWS, t), F32)) for _ in chains)
    carry = lax.fori_loop(0, i, lambda j, c: tile(j, c, False), init)
    final = tile(i, carry, True)
    for n in range(0, len(chains), 2):
        g, h = chains[n]
        outs = [a[:hd] / a[hd:hd + 1] for _, a in final[n:n + 2]]
        o_ref[g, :, (h // 2) * LANES:(h // 2 + 1) * LANES] = jnp.concatenate(outs, axis=0).T.astype(BF16)


def _fox_prompt(qt16, kaug, vt16, n_seq, seq_len):
    t = FOX_TILE
    nq = seq_len // t
    aug_width = kaug.shape[1]
    group = FOX_GROUP if n_seq % FOX_GROUP == 0 else 1
    out = pl.pallas_call(
        _fox_prompt_body,
        grid=(n_seq // group, nq),
        in_specs=[pl.BlockSpec((group, aug_width, t), lambda b, i: (b, 0, i)),
                  pl.BlockSpec((group, seq_len, aug_width), lambda b, i: (b, 0, 0)),
                  pl.BlockSpec((group, FOX_WIDTH, seq_len), lambda b, i: (b, 0, 0))],
        out_specs=pl.BlockSpec((group, t, FOX_WIDTH), lambda b, i: (b, i, 0)),
        out_shape=jax.ShapeDtypeStruct((n_seq, seq_len, FOX_WIDTH), BF16),
        compiler_params=pltpu.CompilerParams(dimension_semantics=("arbitrary", "arbitrary"),
                                             vmem_limit_bytes=48 * MIB),
        name="fox_prompt",
    )(qt16, kaug.reshape(n_seq, seq_len, aug_width), vt16)
    return out.reshape(n_seq * seq_len, FOX_WIDTH)


def _bdot(a, b):
    return lax.dot_general(a.astype(BF16), b.astype(BF16), (((2,), (1,)), ((0,), (0,))),
                           preferred_element_type=F32)


def _bdot_nt(a, b):
    return lax.dot_general(a.astype(BF16), b.astype(BF16), (((2,), (2,)), ((0,), (0,))),
                           preferred_element_type=F32)


def _unit_lower_inverse_minus_eye(a, r, c):
    blk16 = (r // 16) == (c // 16)
    blk32 = (r // 32) == (c // 32)
    p = jnp.where(blk16, -a, 0.0)
    dt = p
    for _ in range(3):
        p = _bdot(p, p)
        dt = dt + p + _bdot(dt, p)
    for off in (jnp.where(jnp.logical_and(blk32, jnp.logical_not(blk16)), a, 0.0),
                jnp.where(blk32, 0.0, a)):
        x = off + _bdot(dt, off)
        dt = dt - (x + _bdot(x, dt))
    return dt


def _gdn_prompt_body(qkv_ref, sm_ref, gcr_ref, z_ref, gn_ref, o_ref, sout_ref, s_ref):
    ci = pl.program_id(1)

    @pl.when(ci == 0)
    def _():
        s_ref[...] = jnp.zeros_like(s_ref)

    ch = GDN_CHUNK
    d = GDN_HEAD_DIM
    nh = GDN_HEADS
    n_seq = qkv_ref.shape[0]
    n_chunks = qkv_ref.shape[1] // ch
    units = [(cidx, g, hh) for cidx in range(n_chunks) for g in range(n_seq) for hh in range(nh)]
    per_chunk = n_seq * nh

    def gather(fn):
        return jnp.stack([fn(g, slice(cidx * ch, (cidx + 1) * ch), hh) for cidx, g, hh in units], axis=0)

    q = gather(lambda g, rows, hh: qkv_ref[g, rows, hh * d:(hh + 1) * d])
    k = gather(lambda g, rows, hh: qkv_ref[g, rows, GDN_WIDTH + hh * d:GDN_WIDTH + (hh + 1) * d])
    v = gather(lambda g, rows, hh: qkv_ref[g, rows, 2 * GDN_WIDTH + hh * d:2 * GDN_WIDTH + (hh + 1) * d])
    beta = gather(lambda g, rows, hh: sm_ref[g, rows, SM_BETA + hh:SM_BETA + hh + 1])
    gc = gather(lambda g, rows, hh: sm_ref[g, rows, SM_GCUM + hh:SM_GCUM + hh + 1])
    gr = gather(lambda g, rows, hh: gcr_ref[g, hh:hh + 1, rows])

    r = lax.broadcasted_iota(jnp.int32, (1, ch, ch), 1)
    c = lax.broadcasted_iota(jnp.int32, (1, ch, ch), 2)
    lower = r >= c
    beta = jnp.broadcast_to(beta, (len(units), ch, d))
    gc = jnp.broadcast_to(gc, (len(units), ch, d))
    decay = jnp.where(lower, jnp.exp(jnp.where(lower, gc[:, :, :ch] - gr, 0.0)), 0.0)
    qk_kk = _bdot_nt(jnp.concatenate([q, k], axis=1), k)
    qk = qk_kk[:, :ch] * decay
    a = jnp.where(r > c, beta[:, :, :ch] * qk_kk[:, ch:] * decay, 0.0)
    dt = _unit_lower_inverse_minus_eye(a, r, c)
    eg = jnp.exp(gc)
    rhs = jnp.concatenate([v * beta, k * (beta * eg)], axis=-1)
    uw = rhs + _bdot(dt, rhs)
    gl = gc[:, ch - 1:ch, :]
    qd = (q * eg).astype(BF16)
    kd = k * jnp.exp(gl - gc)
    g_last = jnp.exp(gl)
    qk16 = qk.astype(BF16)

    s = s_ref[...]
    for cidx in range(n_chunks):
        us = slice(cidx * per_chunk, (cidx + 1) * per_chunk)
        rows = slice(cidx * ch, (cidx + 1) * ch)
        s16 = s.astype(BF16)
        v_new = uw[us, :, :d] - _bdot(uw[us, :, d:], s16)
        o = _bdot(qd[us], s16) + _bdot(qk16[us], v_new)
        vn16 = v_new.astype(BF16)
        upd = jnp.stack([_dot(kd[cidx * per_chunk + j].T.astype(BF16), vn16[j]) for j in range(per_chunk)],
                        axis=0)
        s = s * g_last[us] + upd
        for g in range(n_seq):
            for hh in range(nh):
                hs = slice(hh * d, (hh + 1) * d)
                o_ref[g, rows, hs] = (_rms(o[g * nh + hh], gn_ref[...]) * _silu(z_ref[g, rows, hs])).astype(BF16)
    s_ref[...] = s

    @pl.when(ci == pl.num_programs(1) - 1)
    def _():
        sout_ref[...] = s.reshape(sout_ref.shape)


def _gdn_prompt(gqkv, sm, smt, gz, gn, n_seq, seq_len):
    tm = ROW_TILE
    nt = seq_len // tm
    gcum_block = SM_GCUM // SUBLANES
    group = 2 if n_seq % 2 == 0 else 1
    seq3 = lambda a: a.reshape(n_seq, seq_len, a.shape[-1])
    tile = lambda w: pl.BlockSpec((group, tm, w), lambda b, i: (b, i, 0))
    out, s_out = pl.pallas_call(
        _gdn_prompt_body,
        grid=(n_seq // group, nt),
        in_specs=[tile(3 * GDN_WIDTH), tile(LANES),
                  pl.BlockSpec((group, SUBLANES, tm), lambda b, i: (b, gcum_block, i)),
                  tile(GDN_WIDTH),
                  pl.BlockSpec((1, GDN_HEAD_DIM), lambda b, i: (0, 0))],
        out_specs=(tile(GDN_WIDTH),
                   pl.BlockSpec((group, GDN_HEADS, GDN_HEAD_DIM, GDN_HEAD_DIM), lambda b, i: (b, 0, 0, 0))),
        out_shape=(jax.ShapeDtypeStruct((n_seq, seq_len, GDN_WIDTH), BF16),
                   jax.ShapeDtypeStruct((n_seq, GDN_HEADS, GDN_HEAD_DIM, GDN_HEAD_DIM), F32)),
        scratch_shapes=[pltpu.VMEM((group * GDN_HEADS, GDN_HEAD_DIM, GDN_HEAD_DIM), F32)],
        compiler_params=pltpu.CompilerParams(dimension_semantics=("arbitrary", "arbitrary"),
                                             vmem_limit_bytes=40 * MIB),
        name="gdn_prompt",
    )(seq3(gqkv), seq3(sm), smt, seq3(gz), gn)
    return out.reshape(n_seq * seq_len, GDN_WIDTH), s_out


def _ffn_tail(x, fox_ref, gdn_ref, wo_ref, nf_ref, wup_ref, wd_ref, nfin_ref, conv):
    mix = _dot(fox_ref[...], wo_ref[0:FOX_WIDTH, :]) + _dot(gdn_ref[...], wo_ref[FOX_WIDTH:, :])
    x2 = x + mix
    h2 = _rms(x2, nf_ref[...]).astype(BF16)
    gu = _dot(h2, wup_ref[...])
    gate, up = gu[:, :FFN_DIM], gu[:, FFN_DIM:]
    act = (_silu(conv(gate)) * up).astype(BF16)
    x3 = x2 + _dot(act, wd_ref[...])
    return _rms(x3, nfin_ref[...])


def _ffn_prompt_body(pt_ref, *refs, tiles_per_seq, rider_cfg):
    (x_ref, fox_ref, gdn_ref, wo_ref, nf_ref, wup_ref, cw_ref, cb_ref, wd_ref, nfin_ref) = refs[:10]
    rider_in = refs[10:10 + RIDER_INPUTS]
    y_ref, cst_ref, rider_out, prev_ref = refs[10 + RIDER_INPUTS:14 + RIDER_INPUTS]
    rider_scratch = refs[14 + RIDER_INPUTS:]
    i = pl.program_id(0)

    @pl.when(i % tiles_per_seq == 0)
    def _():
        prev_ref[...] = jnp.zeros_like(prev_ref)

    _rider_step(pt_ref, rider_in, rider_out, rider_scratch, rider_cfg)

    tm = x_ref.shape[0]

    def conv(gate):
        prev = prev_ref[...]
        out = gate * cw_ref[FFN_CONV - 1:FFN_CONV, :] + cb_ref[...]
        for kk in range(1, FFN_CONV):
            out = out + _shift_rows(gate, prev, kk) * cw_ref[FFN_CONV - 1 - kk:FFN_CONV - kk, :]
        prev_ref[...] = gate[tm - SUBLANES:, :]
        cst_ref[...] = gate[tm - SUBLANES:, :]
        return out

    y_ref[...] = _ffn_tail(x_ref[...], fox_ref, gdn_ref, wo_ref, nf_ref, wup_ref, wd_ref, nfin_ref, conv)


def _ffn_prompt(x, fox, gdn, wo, nf, wup, cw, cb, wd, nfin, seq_len, make_rider):
    rows = x.shape[0]
    tm = ROW_TILE
    n_seq = rows // seq_len
    tiles_per_seq = seq_len // tm
    row_spec = lambda w: pl.BlockSpec((tm, w), lambda i, pt: (i, 0))
    const = lambda a: pl.BlockSpec(a.shape, lambda i, pt: (0, 0), pipeline_mode=pl.Buffered(1))
    rider = make_rider(rows // tm)
    grid_spec = pltpu.PrefetchScalarGridSpec(
        num_scalar_prefetch=1,
        grid=(rows // tm,),
        in_specs=[row_spec(D_MODEL), row_spec(FOX_WIDTH), row_spec(GDN_WIDTH), const(wo), const(nf),
                  const(wup), const(cw), const(cb), const(wd), const(nfin)] + rider["in_specs"],
        out_specs=(row_spec(D_MODEL),
                   pl.BlockSpec((SUBLANES, FFN_DIM), lambda i, pt: (i // tiles_per_seq, 0)),
                   rider["out_spec"]),
        scratch_shapes=[pltpu.VMEM((SUBLANES, FFN_DIM), F32)] + rider["scratch"],
    )
    return pl.pallas_call(
        functools.partial(_ffn_prompt_body, tiles_per_seq=tiles_per_seq, rider_cfg=rider["cfg"]),
        grid_spec=grid_spec,
        out_shape=(jax.ShapeDtypeStruct((rows, D_MODEL), F32),
                   jax.ShapeDtypeStruct((n_seq * SUBLANES, FFN_DIM), F32),
                   rider["out_shape"]),
        compiler_params=pltpu.CompilerParams(dimension_semantics=("arbitrary",), vmem_limit_bytes=58 * MIB),
        name="ffn_prompt",
    )(rider["prefetch"], x, fox, gdn, wo, nf, wup, cw, cb, wd, nfin, *rider["operands"])


def _ffn_sample_body(x_ref, fox_ref, gdn_ref, wo_ref, nf_ref, wup_ref, cw_ref, cb_ref, wd_ref, nfin_ref,
                     c0_ref, c1_ref, y_ref, gate_ref):
    def conv(gate):
        gate_ref[...] = gate
        return (gate * cw_ref[2:3, :] + c1_ref[...] * cw_ref[1:2, :] + c0_ref[...] * cw_ref[0:1, :]
                + cb_ref[...])

    y_ref[...] = _ffn_tail(x_ref[...], fox_ref, gdn_ref, wo_ref, nf_ref, wup_ref, wd_ref, nfin_ref, conv)


def _ffn_sample(x, fox, gdn, wo, nf, wup, cw, cb, wd, nfin, c0, c1):
    rows = x.shape[0]
    args = (x, fox, gdn, wo, nf, wup, cw, cb, wd, nfin, c0, c1)
    full = lambda a: pl.BlockSpec(a.shape, lambda i: (0, 0), pipeline_mode=pl.Buffered(1))
    return pl.pallas_call(
        _ffn_sample_body,
        grid=(1,),
        in_specs=[full(a) for a in args],
        out_specs=(pl.BlockSpec((rows, D_MODEL), lambda i: (0, 0)),
                   pl.BlockSpec((rows, FFN_DIM), lambda i: (0, 0))),
        out_shape=(jax.ShapeDtypeStruct((rows, D_MODEL), F32), jax.ShapeDtypeStruct((rows, FFN_DIM), F32)),
        compiler_params=pltpu.CompilerParams(dimension_semantics=("arbitrary",), vmem_limit_bytes=56 * MIB),
        name="ffn_sample",
    )(*args)


def _split3(x):
    hi = x.astype(BF16)
    r1 = x - hi.astype(F32)
    mid = r1.astype(BF16)
    lo = (r1 - mid.astype(F32)).astype(BF16)
    return hi, mid, lo


RIDER_INPUTS = 8
RIDER_SCRATCH = 4


def _rider_step(pt_ref, rider_in, o_ref, rider_scratch, cfg):
    q_ref, kn_ref, vn_ref, sm_ref, msuf_ref, kt_pool, vt_pool, lf_pool = rider_in
    kbuf, vbuf, lbuf, sems = rider_scratch
    base, count, per_step, n_pages = cfg
    i = pl.program_id(0)

    def copies(g, sl):
        out = []
        for j in range(n_pages):
            pg = pt_ref[g * n_pages + j]
            out.append(pltpu.make_async_copy(kt_pool.at[pg], kbuf.at[sl, j], sems.at[sl, 0]))
            out.append(pltpu.make_async_copy(vt_pool.at[pg], vbuf.at[sl, j], sems.at[sl, 1]))
            out.append(pltpu.make_async_copy(lf_pool.at[pg], lbuf.at[sl, j], sems.at[sl, 2]))
        return out

    @pl.when(i == 0)
    def _():
        for cp in copies(base, base % 2):
            cp.start()

    for r in range(per_step):
        g = base + i * per_step + r
        slot = g % 2

        @pl.when(g + 1 < base + count)
        def _():
            for cp in copies(g + 1, 1 - slot):
                cp.start()

        for cp in copies(g, slot):
            cp.wait()
        o_ref[r] = _decode_row(slot, q_ref[r], kn_ref[r], vn_ref[r], sm_ref[r], msuf_ref[...],
                               kbuf, vbuf, lbuf, n_pages)


def _decode_row(slot, q, k_new, v_new, sm_row, msuf, kbuf, vbuf, lbuf, n_pages):
    hd = FOX_HEAD_DIM
    page = kbuf.shape[-1]

    r8 = lax.broadcasted_iota(jnp.int32, (SUBLANES, LANES), 0)
    c8 = lax.broadcasted_iota(jnp.int32, (SUBLANES, LANES), 1)
    lf_new = jnp.sum(jnp.where(r8 == c8, sm_row, 0.0), axis=1, keepdims=True)
    n_rows = n_pages * FOX_HEADS
    lf = lbuf[slot].reshape(n_rows, page)
    within = functools.reduce(lambda a, c: a + c, [_dot(t, msuf) for t in _split3(lf)])
    tot = jnp.broadcast_to(jnp.sum(lf, axis=1, keepdims=True), (n_rows, page))
    row = lax.broadcasted_iota(jnp.int32, (n_rows, page), 0)
    later = tot
    k = FOX_HEADS
    while k < n_rows:
        later = later + jnp.where(row + k < n_rows, pltpu.roll(later, n_rows - k, 0), 0.0)
        k *= 2
    bias = (within + (later - tot)).reshape(n_pages, FOX_HEADS, page) + lf_new[None]

    q_t = q.T
    qb = jnp.stack([jnp.broadcast_to(q_t[:, h:h + 1], (hd, page)) for h in range(FOX_HEADS)], axis=0)
    logits = [jnp.sum(kbuf[slot, j] * qb, axis=1) + bias[j] for j in range(n_pages)]
    s_new = jnp.sum(k_new * q, axis=-1, keepdims=True)
    m = functools.reduce(jnp.maximum, [jnp.max(x, axis=-1, keepdims=True) for x in logits] + [s_new])
    ps = [jnp.exp(x - m) for x in logits]
    p_new = jnp.exp(s_new - m)
    l = functools.reduce(lambda a, c: a + c, [jnp.sum(x, axis=-1, keepdims=True) for x in ps] + [p_new])

    cols = []
    for h in range(FOX_HEADS):
        acc = vbuf[slot, 0, h] * ps[0][h:h + 1, :]
        for j in range(1, n_pages):
            acc = acc + vbuf[slot, j, h] * ps[j][h:h + 1, :]
        cols.append(jnp.sum(acc, axis=1, keepdims=True))
    mat = jnp.concatenate(cols + [jnp.zeros((hd, LANES - FOX_HEADS), F32)], axis=1)
    return (mat.T[0:FOX_HEADS, :] + p_new * v_new) / l


def _rider(page_table, q, kn, vn, sm, kt_pool, vt_pool, lf_pool, base, count, n_steps):
    n_pages = page_table.shape[1]
    page = kt_pool.shape[-1]
    per_step = count // n_steps
    assert per_step * n_steps == count, "sample rows must split evenly over the host's grid steps"
    tok = jnp.arange(page)
    msuf = (tok[:, None] > tok[None, :]).astype(BF16)
    rows = lambda a: a[base:base + count]
    tile = lambda a: pl.BlockSpec((per_step,) + a.shape[1:], lambda i, pt: (i,) + (0,) * (a.ndim - 1))
    hbm = pl.BlockSpec(memory_space=pl.ANY)
    operands = [rows(q), rows(kn), rows(vn), rows(sm).reshape(count, 1, LANES), msuf, kt_pool, vt_pool, lf_pool]
    kv_buf = pltpu.VMEM((2, n_pages, FOX_HEADS, FOX_HEAD_DIM, page), F32)
    return dict(
        prefetch=page_table.reshape(-1),
        operands=operands,
        in_specs=[tile(a) for a in operands[:4]] + [pl.BlockSpec(msuf.shape, lambda i, pt: (0, 0)), hbm, hbm, hbm],
        out_spec=pl.BlockSpec((per_step, FOX_HEADS, FOX_HEAD_DIM), lambda i, pt: (i, 0, 0)),
        out_shape=jax.ShapeDtypeStruct((count, FOX_HEADS, FOX_HEAD_DIM), F32),
        scratch=[kv_buf, kv_buf, pltpu.VMEM((2, n_pages, FOX_HEADS, page), F32), pltpu.SemaphoreType.DMA((2, 3))],
        cfg=(base, count, per_step, n_pages),
    )


def _gdn_sample_body(s_ref, qkv_ref, sm_ref, z_ref, gn_ref, sout_ref, o_ref):
    d = GDN_HEAD_DIM
    eye = lax.broadcasted_iota(jnp.int32, (d, d), 0) == lax.broadcasted_iota(jnp.int32, (d, d), 1)

    def col(rows):
        return jnp.sum(jnp.where(eye[None], rows[:, None, :], 0.0), axis=2, keepdims=True)

    for hh in range(GDN_HEADS):
        hs = slice(hh * d, (hh + 1) * d)
        q = qkv_ref[:, hs]
        k = qkv_ref[:, GDN_WIDTH + hh * d:GDN_WIDTH + (hh + 1) * d]
        v = qkv_ref[:, 2 * GDN_WIDTH + hh * d:2 * GDN_WIDTH + (hh + 1) * d]
        eg = jnp.exp(sm_ref[:, SM_G + hh:SM_G + hh + 1])
        beta = sm_ref[:, SM_BETA + hh:SM_BETA + hh + 1]
        s = s_ref[:, hh]
        k_col = col(k)
        v_new = beta * (v - eg * jnp.sum(k_col * s, axis=1))
        o = eg * jnp.sum(col(q) * s, axis=1) + jnp.sum(q * k, axis=1, keepdims=True) * v_new
        sout_ref[:, hh] = s * eg[:, :, None] + k_col * v_new[:, None, :]
        o_ref[:, hs] = _rms(o, gn_ref[...]) * _silu(z_ref[:, hs])


def _gdn_sample(state, gqkv, sm, gz, gn):
    nb = state.shape[0]
    bb = SUBLANES
    d = GDN_HEAD_DIM
    row_spec = lambda w: pl.BlockSpec((bb, w), lambda i: (i, 0))
    st_spec = pl.BlockSpec((bb, GDN_HEADS, d, d), lambda i: (i, 0, 0, 0))
    return pl.pallas_call(
        _gdn_sample_body,
        grid=(nb // bb,),
        in_specs=[st_spec, row_spec(3 * GDN_WIDTH), row_spec(LANES), row_spec(GDN_WIDTH),
                  pl.BlockSpec((1, d), lambda i: (0, 0))],
        out_specs=(st_spec, row_spec(GDN_WIDTH)),
        out_shape=(jax.ShapeDtypeStruct(state.shape, F32), jax.ShapeDtypeStruct((nb, GDN_WIDTH), F32)),
        compiler_params=pltpu.CompilerParams(dimension_semantics=("arbitrary",), vmem_limit_bytes=32 * MIB),
        name="gdn_sample",
    )(state, gqkv, sm, gz, gn)


def _pad_rows(a, rows):
    return jnp.concatenate([a, jnp.zeros((rows - a.shape[0],) + a.shape[1:], a.dtype)], axis=0)


def kernel(x_prompt, x_sample, cache_k, cache_v, cache_logf, state_gdn, state_gdn_conv, state_ffn_conv,
           page_table, norm_mix, w_in, b_forget, gdn_a_log, gdn_dt_bias, w_gdn_conv, gdn_out_norm, w_out,
           norm_ffn, w_up, w_ffn_conv, b_ffn_conv, w_down, norm_final):
    assert w_in.shape[0] == 1, "single-layer trunk"
    n_seq, seq_len, _ = x_prompt.shape
    nb = x_sample.shape[0]
    n_pool, page = cache_k.shape[1], cache_k.shape[2]

    w = w_in[0]
    o_ff = 3 * FOX_WIDTH
    o_g = o_ff + FOX_HEADS
    o_ga = o_g + 3 * GDN_WIDTH
    o_gz = o_ga + 2 * GDN_HEADS
    wbig = jnp.concatenate([w[:, :o_ff], w[:, o_g:o_ga], w[:, o_gz:]], axis=1).astype(BF16)
    wsm = jnp.concatenate([w[:, o_ff:o_g], w[:, o_ga:o_gz],
                           jnp.zeros((D_MODEL, LANES - FOX_HEADS - 2 * GDN_HEADS), F32)], axis=1).astype(BF16)
    par = jnp.zeros((SUBLANES, LANES), F32)
    par = par.at[0, SM_LF:SM_LF + FOX_HEADS].set(b_forget[0])
    par = par.at[0, SM_G:SM_G + GDN_HEADS].set(gdn_dt_bias[0])
    par = par.at[1, SM_G:SM_G + GDN_HEADS].set(gdn_a_log[0])
    nrm = norm_mix[0][None, :]
    cw = _pad_rows(w_gdn_conv[0], SUBLANES)
    gn = gdn_out_norm[0][None, :]
    wo = w_out[0].astype(BF16)
    nf = norm_ffn[0][None, :]
    wup = w_up[0].astype(BF16)
    cwf = _pad_rows(w_ffn_conv[0], SUBLANES)
    cbf = b_ffn_conv[0][None, :]
    wd = w_down[0].astype(BF16)
    nfin = norm_final[None, :]

    xs = x_sample.reshape(nb, D_MODEL)
    gctx = state_gdn_conv[0]
    fctx = state_ffn_conv[0]
    fk_s, fv_s, q_s, gqkv_s, gz_s, sm_s, pre_s = _proj_sample(
        xs, nrm, wbig, wsm, par, cw, gctx[:, 0], gctx[:, 1], gctx[:, 2])
    heads = lambda a: a.reshape(nb, FOX_HEADS, FOX_HEAD_DIM)
    half = nb // 2
    decode_args = (page_table, heads(q_s), heads(fk_s), heads(fv_s), sm_s,
                   jnp.transpose(cache_k[0], (0, 2, 3, 1)), jnp.transpose(cache_v[0], (0, 2, 3, 1)),
                   jnp.transpose(cache_logf[0], (0, 2, 1)))
    rider_a = functools.partial(_rider, *decode_args, 0, half)
    rider_b = functools.partial(_rider, *decode_args, half, nb - half)

    xp = x_prompt.reshape(n_seq * seq_len, D_MODEL)
    kt, vt, qt16, kaug, vt16, gqkv, gz, sm, smt, cst, fox_sa = _proj_prompt(
        xp, nrm, wbig, wsm, par, cw, seq_len, rider_a)
    fox = _fox_prompt(qt16, kaug, vt16, n_seq, seq_len)
    gdn, s_p = _gdn_prompt(gqkv, sm, smt, gz, gn, n_seq, seq_len)
    yp, cstf, fox_sb = _ffn_prompt(xp, fox, gdn, wo, nf, wup, cwf, cbf, wd, nfin, seq_len, rider_b)

    fox_s = jnp.concatenate([fox_sa, fox_sb], axis=0)
    s_s, gdn_s = _gdn_sample(state_gdn[0], gqkv_s, sm_s, gz_s, gn)
    ys, gate_s = _ffn_sample(xs, fox_s.reshape(nb, FOX_WIDTH).astype(BF16), gdn_s.astype(BF16), wo, nf, wup,
                             cwf, cbf, wd, nfin, fctx[:, 0], fctx[:, 1])

    kv_shape_s = (1, nb, 1, FOX_HEADS, FOX_HEAD_DIM)
    new_kv_p = lambda a: jnp.transpose(a.reshape(1, n_seq, FOX_HEADS, FOX_HEAD_DIM, seq_len), (0, 1, 4, 2, 3))
    return (
        yp.reshape(n_seq, seq_len, D_MODEL),
        ys.reshape(nb, 1, D_MODEL),
        new_kv_p(kt),
        new_kv_p(vt),
        sm[:, SM_LF:SM_LF + FOX_HEADS].reshape(1, n_seq, seq_len, FOX_HEADS),
        s_p[None],
        cst.reshape(n_seq, SUBLANES, 3 * GDN_WIDTH)[None, :, SUBLANES - (GDN_CONV - 1):],
        cstf.reshape(n_seq, SUBLANES, FFN_DIM)[None, :, SUBLANES - (FFN_CONV - 1):],
        fk_s.reshape(kv_shape_s),
        fv_s.reshape(kv_shape_s),
        sm_s[:, SM_LF:SM_LF + FOX_HEADS].reshape(1, nb, 1, FOX_HEADS),
        s_s[None],
        jnp.concatenate([gctx[:, 1:], pre_s[:, None, :]], axis=1)[None],
        jnp.concatenate([fctx[:, 1:], gate_s[:, None, :]], axis=1)[None],
    )
```

```python
import functools

import jax
import jax.numpy as jnp
from jax import lax
from jax.experimental import pallas as pl
from jax.experimental.pallas import tpu as pltpu

D_MODEL = 1024
FOX_HEADS = 8
FOX_HEAD_DIM = 64
FOX_WIDTH = FOX_HEADS * FOX_HEAD_DIM
GDN_HEADS = 4
GDN_HEAD_DIM = 128
GDN_WIDTH = GDN_HEADS * GDN_HEAD_DIM
GDN_CONV = 4
GDN_CHUNK = 64
FFN_DIM = 2816
FFN_CONV = 3
EPS = 1e-6
NEG_BIG = -1e30
FOX_SCALE = FOX_HEAD_DIM ** -0.5
GDN_SCALE = GDN_HEAD_DIM ** -0.5
LOG2E = 1.4426950408889634
FOX_QSCALE = FOX_SCALE * LOG2E
FOX_AUG = 3
FOX_DEN_ROWS = 16
FOX_GROUP = 2

LANES = 128
SUBLANES = 8
ROW_TILE = 256
FOX_TILE = 256
MIB = 1024 * 1024

SM_LF = 0
SM_G = 8
SM_BETA = 12
SM_CUM = 16
SM_GCUM = 24

F32 = jnp.float32
BF16 = jnp.bfloat16


def _sigmoid(x):
    return 1.0 / (1.0 + jnp.exp(-x))


def _silu(x):
    return x * _sigmoid(x)


def _rms(x, g):
    return x * lax.rsqrt(jnp.mean(x * x, axis=-1, keepdims=True) + EPS) * g


def _dot(a, b):
    return jnp.dot(a, b, preferred_element_type=F32)


def _shift_rows(x, prev8, k):
    r = pltpu.roll(x, k, 0)
    row8 = lax.broadcasted_iota(jnp.int32, prev8.shape, 0)
    top = jnp.where(row8 < k, pltpu.roll(prev8, k, 0), r[0:SUBLANES])
    return jnp.concatenate([top, r[SUBLANES:]], axis=0)


def _gate_activations(raw, par_ref):
    z = raw + par_ref[0:1, :]
    lane = lax.broadcasted_iota(jnp.int32, z.shape, 1)
    t = jnp.log1p(jnp.exp(-jnp.abs(z)))
    lf = jnp.minimum(z, 0.0) - t
    softplus = jnp.maximum(z, 0.0) + t
    g = -jnp.exp(par_ref[1:2, :]) * softplus
    beta = _sigmoid(z)
    return jnp.where(lane < SM_G, lf, jnp.where(lane < SM_BETA, g, jnp.where(lane < SM_CUM, beta, 0.0)))


def _gdn_qkv_norm(c, out_ref):
    for part, scale in ((0, GDN_SCALE), (1, 1.0)):
        for hh in range(GDN_HEADS):
            off = part * GDN_WIDTH + hh * GDN_HEAD_DIM
            seg = c[:, off:off + GDN_HEAD_DIM]
            n = lax.rsqrt(jnp.sum(seg * seg, axis=-1, keepdims=True) + EPS)
            out_ref[:, off:off + GDN_HEAD_DIM] = seg * n * scale
    out_ref[:, 2 * GDN_WIDTH:] = c[:, 2 * GDN_WIDTH:]


def _proj_prompt_body(pt_ref, *refs, tiles_per_seq, rider_cfg):
    (x_ref, nrm_ref, wbig_ref, wsm_ref, par_ref, cw_ref, scat_ref) = refs[:7]
    rider_in = refs[7:7 + RIDER_INPUTS]
    (kt_ref, vt_ref, qt16_ref, kaug_ref, vt16_ref, gqkv_ref, gz_ref, sm_ref, smt_ref, cst_ref,
     rider_out, carry_ref, prev_ref) = refs[7 + RIDER_INPUTS:20 + RIDER_INPUTS]
    rider_scratch = refs[20 + RIDER_INPUTS:]
    i = pl.program_id(0)

    @pl.when(i % tiles_per_seq == 0)
    def _():
        carry_ref[...] = jnp.zeros_like(carry_ref)
        prev_ref[...] = jnp.zeros_like(prev_ref)

    _rider_step(pt_ref, rider_in, rider_out, rider_scratch, rider_cfg)

    tm = x_ref.shape[0]
    h16 = _rms(x_ref[...], nrm_ref[...]).astype(BF16)

    fox = _dot(h16, wbig_ref[:, 0:3 * FOX_WIDTH])
    fq, fk, fv = fox[:, :FOX_WIDTH], fox[:, FOX_WIDTH:2 * FOX_WIDTH], fox[:, 2 * FOX_WIDTH:]
    fv_t = fv.T
    kt_ref[0] = fk.T
    vt_ref[0] = fv_t
    vt16_ref[0] = fv_t.astype(BF16)
    lane_p = lax.broadcasted_iota(jnp.int32, (tm, LANES), 1)
    ones_blk = jnp.where(lane_p < 2 * FOX_AUG, 1.0, 0.0)
    q_aug = []
    for p in range(FOX_HEADS // 2):
        q_aug += [fq[:, p * LANES:(p + 1) * LANES] * FOX_QSCALE, ones_blk]
        kaug_ref[:, 2 * p * LANES:(2 * p + 1) * LANES] = fk[:, p * LANES:(p + 1) * LANES].astype(BF16)
    qt16_ref[0] = jnp.concatenate(q_aug, axis=1).T.astype(BF16)

    act = _gate_activations(_dot(h16, wsm_ref[...]), par_ref)
    row = lax.broadcasted_iota(jnp.int32, act.shape, 0)
    lane = lax.broadcasted_iota(jnp.int32, act.shape, 1)
    y = act
    yc = act
    k = 1
    while k < tm:
        y = y + jnp.where(row >= k, pltpu.roll(y, k, 0), 0.0)
        if k < GDN_CHUNK:
            yc = yc + jnp.where((row & (GDN_CHUNK - 1)) >= k, pltpu.roll(yc, k, 0), 0.0)
        k *= 2
    y = y + carry_ref[0:1, :]
    carry_ref[0:1, :] = y[tm - 1:tm, :]
    terms = jnp.concatenate(_split3(y * (-LOG2E)), axis=1)
    extra = _dot(terms, scat_ref[...]).astype(BF16)
    for p in range(FOX_HEADS // 2):
        kaug_ref[:, (2 * p + 1) * LANES:(2 * p + 2) * LANES] = extra[:, p * LANES:(p + 1) * LANES]
    shift = SM_CUM - SM_LF
    sm = jnp.where(lane < SM_CUM, act,
                   jnp.where(lane < SM_GCUM, pltpu.roll(y, shift, 1),
                             jnp.where(lane < SM_GCUM + GDN_HEADS, pltpu.roll(yc, shift, 1), 0.0)))
    sm_ref[...] = sm
    smt_ref[0] = sm.T

    pre = _dot(h16, wbig_ref[:, 3 * FOX_WIDTH:3 * FOX_WIDTH + 3 * GDN_WIDTH])
    assert GDN_CONV == 4
    s1 = _shift_rows(pre, prev_ref[0:SUBLANES, :], 1)
    far = pre * cw_ref[1:2, :] + s1 * cw_ref[0:1, :]
    acc = pre * cw_ref[3:4, :] + s1 * cw_ref[2:3, :] + _shift_rows(far, prev_ref[SUBLANES:, :], 2)
    prev_ref[0:SUBLANES, :] = pre[tm - SUBLANES:, :]
    prev_ref[SUBLANES:, :] = far[tm - SUBLANES:, :]
    cst_ref[...] = pre[tm - SUBLANES:, :]
    _gdn_qkv_norm(_silu(acc), gqkv_ref)

    gz_ref[...] = _dot(h16, wbig_ref[:, 3 * FOX_WIDTH + 3 * GDN_WIDTH:])


def _proj_prompt(x, nrm, wbig, wsm, par, cw, seq_len, make_rider):
    rows = x.shape[0]
    tm = ROW_TILE
    n_seq = rows // seq_len
    tiles_per_seq = seq_len // tm
    wb = wbig.shape[1]
    row_spec = lambda w: pl.BlockSpec((tm, w), lambda i, pt: (i, 0))
    col_spec = lambda w: pl.BlockSpec((1, w, tm), lambda i, pt: (i // tiles_per_seq, 0, i % tiles_per_seq))
    const = lambda s: pl.BlockSpec(s, lambda i, pt: (0, 0), pipeline_mode=pl.Buffered(1))
    rider = make_rider(rows // tm)
    aug_width = FOX_HEADS * LANES
    src = jnp.arange(3 * LANES)
    term, head = src // LANES, src % LANES
    dst = (head // 2) * LANES + (head % 2) * FOX_AUG + term
    scat = ((jnp.arange(FOX_WIDTH)[None, :] == dst[:, None]) & (head[:, None] < FOX_HEADS)).astype(BF16)
    out_shape = (
        jax.ShapeDtypeStruct((n_seq, FOX_WIDTH, seq_len), F32),
        jax.ShapeDtypeStruct((n_seq, FOX_WIDTH, seq_len), F32),
        jax.ShapeDtypeStruct((n_seq, aug_width, seq_len), BF16),
        jax.ShapeDtypeStruct((rows, aug_width), BF16),
        jax.ShapeDtypeStruct((n_seq, FOX_WIDTH, seq_len), BF16),
        jax.ShapeDtypeStruct((rows, 3 * GDN_WIDTH), F32),
        jax.ShapeDtypeStruct((rows, GDN_WIDTH), F32),
        jax.ShapeDtypeStruct((rows, LANES), F32),
        jax.ShapeDtypeStruct((n_seq, LANES, seq_len), F32),
        jax.ShapeDtypeStruct((n_seq * SUBLANES, 3 * GDN_WIDTH), F32),
    )
    out_specs = (
        col_spec(FOX_WIDTH), col_spec(FOX_WIDTH), col_spec(aug_width), row_spec(aug_width),
        col_spec(FOX_WIDTH), row_spec(3 * GDN_WIDTH), row_spec(GDN_WIDTH), row_spec(LANES),
        col_spec(LANES),
        pl.BlockSpec((SUBLANES, 3 * GDN_WIDTH), lambda i, pt: (i // tiles_per_seq, 0)),
    )
    grid_spec = pltpu.PrefetchScalarGridSpec(
        num_scalar_prefetch=1,
        grid=(rows // tm,),
        in_specs=[row_spec(D_MODEL), const((1, D_MODEL)), const((D_MODEL, wb)), const((D_MODEL, LANES)),
                  const((SUBLANES, LANES)), const((SUBLANES, 3 * GDN_WIDTH)), const(scat.shape)]
        + rider["in_specs"],
        out_specs=out_specs + (rider["out_spec"],),
        scratch_shapes=[pltpu.VMEM((SUBLANES, LANES), F32), pltpu.VMEM((2 * SUBLANES, 3 * GDN_WIDTH), F32)]
        + rider["scratch"],
    )
    return pl.pallas_call(
        functools.partial(_proj_prompt_body, tiles_per_seq=tiles_per_seq, rider_cfg=rider["cfg"]),
        grid_spec=grid_spec,
        out_shape=out_shape + (rider["out_shape"],),
        compiler_params=pltpu.CompilerParams(dimension_semantics=("arbitrary",), vmem_limit_bytes=56 * MIB),
        name="proj_prompt",
    )(rider["prefetch"], x, nrm, wbig, wsm, par, cw, scat, *rider["operands"])


def _proj_sample_body(x_ref, nrm_ref, wbig_ref, wsm_ref, par_ref, cw_ref, c0_ref, c1_ref, c2_ref,
                      fk_ref, fv_ref, q_ref, gqkv_ref, gz_ref, sm_ref, pre_ref):
    h16 = _rms(x_ref[...], nrm_ref[...]).astype(BF16)
    fox = _dot(h16, wbig_ref[:, 0:3 * FOX_WIDTH])
    q_ref[...] = fox[:, :FOX_WIDTH] * FOX_SCALE
    fk_ref[...] = fox[:, FOX_WIDTH:2 * FOX_WIDTH]
    fv_ref[...] = fox[:, 2 * FOX_WIDTH:]
    sm_ref[...] = _gate_activations(_dot(h16, wsm_ref[...]), par_ref)
    pre = _dot(h16, wbig_ref[:, 3 * FOX_WIDTH:3 * FOX_WIDTH + 3 * GDN_WIDTH])
    pre_ref[...] = pre
    acc = (pre * cw_ref[3:4, :] + c2_ref[...] * cw_ref[2:3, :]
           + c1_ref[...] * cw_ref[1:2, :] + c0_ref[...] * cw_ref[0:1, :])
    _gdn_qkv_norm(_silu(acc), gqkv_ref)
    gz_ref[...] = _dot(h16, wbig_ref[:, 3 * FOX_WIDTH + 3 * GDN_WIDTH:])


def _proj_sample(x, nrm, wbig, wsm, par, cw, c0, c1, c2):
    rows = x.shape[0]
    full = lambda a: pl.BlockSpec(a.shape, lambda i: (0,) * a.ndim)
    args = (x, nrm, wbig, wsm, par, cw, c0, c1, c2)
    shapes = ((rows, FOX_WIDTH), (rows, FOX_WIDTH), (rows, FOX_WIDTH), (rows, 3 * GDN_WIDTH),
              (rows, GDN_WIDTH), (rows, LANES), (rows, 3 * GDN_WIDTH))
    return pl.pallas_call(
        _proj_sample_body,
        grid=(1,),
        in_specs=[full(a) for a in args],
        out_specs=tuple(pl.BlockSpec(s, lambda i: (0, 0)) for s in shapes),
        out_shape=tuple(jax.ShapeDtypeStruct(s, F32) for s in shapes),
        compiler_params=pltpu.CompilerParams(dimension_semantics=("arbitrary",), vmem_limit_bytes=48 * MIB),
        name="proj_sample",
    )(*args)


def _fox_prompt_body(qt_ref, kaug_ref, vt_ref, o_ref):
    i = pl.program_id(1)
    t = FOX_TILE
    hd = FOX_HEAD_DIM
    aug = 2 * LANES
    rr = lax.broadcasted_iota(jnp.int32, (t, t), 0)
    cc = lax.broadcasted_iota(jnp.int32, (t, t), 1)
    causal = rr <= cc
    row = lax.broadcasted_iota(jnp.int32, (aug, t), 0)
    chains = [(g, h) for g in range(qt_ref.shape[0]) for h in range(FOX_HEADS)]
    qs = []
    for g, h in chains:
        e = h % 2
        qp = qt_ref[g, (h // 2) * aug:(h // 2 + 1) * aug, :]
        own = jnp.logical_or(jnp.logical_and(row >= e * hd, row < (e + 1) * hd),
                             jnp.logical_and(row >= LANES + e * FOX_AUG, row < LANES + (e + 1) * FOX_AUG))
        qs.append(jnp.where(own, qp, jnp.zeros_like(qp)))

    def tile(j, carry, masked):
        koff = pl.multiple_of(j * t, t)
        ss = [_dot(kaug_ref[g, pl.ds(koff, t), (h // 2) * aug:(h // 2 + 1) * aug], qs[n])
              for n, (g, h) in enumerate(chains)]
        stats, pms = [], []
        for n in range(len(chains)):
            m = carry[n][0]
            s = jnp.where(causal, ss[n], NEG_BIG) if masked else ss[n]
            m_new = jnp.maximum(m, jnp.max(s, axis=0, keepdims=True))
            stats.append((m_new, jnp.exp2(m - m_new)))
            pms.append(jnp.exp2(s - m_new).astype(BF16))
        out = []
        for n, (g, h) in enumerate(chains):
            m_new, alpha = stats[n]
            v_ones = jnp.concatenate([vt_ref[g, h * hd:(h + 1) * hd, pl.ds(koff, t)], ones_rows], axis=0)
            out.append((m_new, alpha * carry[n][1] + _dot(v_ones, pms[n])))
        return tuple(out)

    ones_rows = jnp.ones((FOX_DEN_ROWS, t), BF16)
    init = tuple((jnp.full((1, t), NEG_BIG, F32), jnp.zeros((hd + FOX_DEN_ROWS, t), F32)) for _ in chains)
    carry = lax.fori_loop(0, i, lambda j, c: tile(j, c, False), init)
    final = tile(i, carry, True)
    for n in range(0, len(chains), 2):
        g, h = chains[n]
        outs = [a[:hd] / a[hd:hd + 1] for _, a in final[n:n + 2]]
        o_ref[g, :, (h // 2) * LANES:(h // 2 + 1) * LANES] = jnp.concatenate(outs, axis=0).T.astype(BF16)


def _fox_prompt(qt16, kaug, vt16, n_seq, seq_len):
    t = FOX_TILE
    nq = seq_len // t
    aug_width = kaug.shape[1]
    group = FOX_GROUP if n_seq % FOX_GROUP == 0 else 1
    out = pl.pallas_call(
        _fox_prompt_body,
        grid=(n_seq // group, nq),
        in_specs=[pl.BlockSpec((group, aug_width, t), lambda b, i: (b, 0, i)),
                  pl.BlockSpec((group, seq_len, aug_width), lambda b, i: (b, 0, 0)),
                  pl.BlockSpec((group, FOX_WIDTH, seq_len), lambda b, i: (b, 0, 0))],
        out_specs=pl.BlockSpec((group, t, FOX_WIDTH), lambda b, i: (b, i, 0)),
        out_shape=jax.ShapeDtypeStruct((n_seq, seq_len, FOX_WIDTH), BF16),
        compiler_params=pltpu.CompilerParams(dimension_semantics=("arbitrary", "arbitrary"),
                                             vmem_limit_bytes=48 * MIB),
        name="fox_prompt",
    )(qt16, kaug.reshape(n_seq, seq_len, aug_width), vt16)
    return out.reshape(n_seq * seq_len, FOX_WIDTH)


def _bdot(a, b):
    return lax.dot_general(a.astype(BF16), b.astype(BF16), (((2,), (1,)), ((0,), (0,))),
                           preferred_element_type=F32)


def _bdot_nt(a, b):
    return lax.dot_general(a.astype(BF16), b.astype(BF16), (((2,), (2,)), ((0,), (0,))),
                           preferred_element_type=F32)


def _unit_lower_inverse_minus_eye(a, r, c):
    blk16 = (r // 16) == (c // 16)
    blk32 = (r // 32) == (c // 32)
    p = jnp.where(blk16, -a, 0.0)
    dt = p
    for _ in range(3):
        p = _bdot(p, p)
        dt = dt + p + _bdot(dt, p)
    for off in (jnp.where(jnp.logical_and(blk32, jnp.logical_not(blk16)), a, 0.0),
                jnp.where(blk32, 0.0, a)):
        x = off + _bdot(dt, off)
        dt = dt - (x + _bdot(x, dt))
    return dt


def _gdn_prompt_body(qkv_ref, sm_ref, gcr_ref, z_ref, gn_ref, o_ref, sout_ref, s_ref):
    ci = pl.program_id(1)

    @pl.when(ci == 0)
    def _():
        s_ref[...] = jnp.zeros_like(s_ref)

    ch = GDN_CHUNK
    d = GDN_HEAD_DIM
    nh = GDN_HEADS
    n_seq = qkv_ref.shape[0]
    n_chunks = qkv_ref.shape[1] // ch
    units = [(cidx, g, hh) for cidx in range(n_chunks) for g in range(n_seq) for hh in range(nh)]
    per_chunk = n_seq * nh

    def gather(fn):
        return jnp.stack([fn(g, slice(cidx * ch, (cidx + 1) * ch), hh) for cidx, g, hh in units], axis=0)

    q = gather(lambda g, rows, hh: qkv_ref[g, rows, hh * d:(hh + 1) * d])
    k = gather(lambda g, rows, hh: qkv_ref[g, rows, GDN_WIDTH + hh * d:GDN_WIDTH + (hh + 1) * d])
    v = gather(lambda g, rows, hh: qkv_ref[g, rows, 2 * GDN_WIDTH + hh * d:2 * GDN_WIDTH + (hh + 1) * d])
    beta = gather(lambda g, rows, hh: sm_ref[g, rows, SM_BETA + hh:SM_BETA + hh + 1])
    gc = gather(lambda g, rows, hh: sm_ref[g, rows, SM_GCUM + hh:SM_GCUM + hh + 1])
    gr = gather(lambda g, rows, hh: gcr_ref[g, hh:hh + 1, rows])

    r = lax.broadcasted_iota(jnp.int32, (1, ch, ch), 1)
    c = lax.broadcasted_iota(jnp.int32, (1, ch, ch), 2)
    lower = r >= c
    beta = jnp.broadcast_to(beta, (len(units), ch, d))
    gc = jnp.broadcast_to(gc, (len(units), ch, d))
    decay = jnp.where(lower, jnp.exp(jnp.where(lower, gc[:, :, :ch] - gr, 0.0)), 0.0)
    qk_kk = _bdot_nt(jnp.concatenate([q, k], axis=1), k)
    qk = qk_kk[:, :ch] * decay
    a = jnp.where(r > c, beta[:, :, :ch] * qk_kk[:, ch:] * decay, 0.0)
    dt = _unit_lower_inverse_minus_eye(a, r, c)
    eg = jnp.exp(gc)
    rhs = jnp.concatenate([v * beta, k * (beta * eg)], axis=-1)
    uw = rhs + _bdot(dt, rhs)
    gl = gc[:, ch - 1:ch, :]
    qd = (q * eg).astype(BF16)
    kd = k * jnp.exp(gl - gc)
    g_last = jnp.exp(gl)
    qk16 = qk.astype(BF16)

    s = s_ref[...]
    for cidx in range(n_chunks):
        us = slice(cidx * per_chunk, (cidx + 1) * per_chunk)
        rows = slice(cidx * ch, (cidx + 1) * ch)
        s16 = s.astype(BF16)
        v_new = uw[us, :, :d] - _bdot(uw[us, :, d:], s16)
        o = _bdot(qd[us], s16) + _bdot(qk16[us], v_new)
        vn16 = v_new.astype(BF16)
        upd = jnp.stack([_dot(kd[cidx * per_chunk + j].T.astype(BF16), vn16[j]) for j in range(per_chunk)],
                        axis=0)
        s = s * g_last[us] + upd
        for g in range(n_seq):
            for hh in range(nh):
                hs = slice(hh * d, (hh + 1) * d)
                o_ref[g, rows, hs] = (_rms(o[g * nh + hh], gn_ref[...]) * _silu(z_ref[g, rows, hs])).astype(BF16)
    s_ref[...] = s

    @pl.when(ci == pl.num_programs(1) - 1)
    def _():
        sout_ref[...] = s.reshape(sout_ref.shape)


def _gdn_prompt(gqkv, sm, smt, gz, gn, n_seq, seq_len):
    tm = ROW_TILE
    nt = seq_len // tm
    gcum_block = SM_GCUM // SUBLANES
    group = 2 if n_seq % 2 == 0 else 1
    seq3 = lambda a: a.reshape(n_seq, seq_len, a.shape[-1])
    tile = lambda w: pl.BlockSpec((group, tm, w), lambda b, i: (b, i, 0))
    out, s_out = pl.pallas_call(
        _gdn_prompt_body,
        grid=(n_seq // group, nt),
        in_specs=[tile(3 * GDN_WIDTH), tile(LANES),
                  pl.BlockSpec((group, SUBLANES, tm), lambda b, i: (b, gcum_block, i)),
                  tile(GDN_WIDTH),
                  pl.BlockSpec((1, GDN_HEAD_DIM), lambda b, i: (0, 0))],
        out_specs=(tile(GDN_WIDTH),
                   pl.BlockSpec((group, GDN_HEADS, GDN_HEAD_DIM, GDN_HEAD_DIM), lambda b, i: (b, 0, 0, 0))),
        out_shape=(jax.ShapeDtypeStruct((n_seq, seq_len, GDN_WIDTH), BF16),
                   jax.ShapeDtypeStruct((n_seq, GDN_HEADS, GDN_HEAD_DIM, GDN_HEAD_DIM), F32)),
        scratch_shapes=[pltpu.VMEM((group * GDN_HEADS, GDN_HEAD_DIM, GDN_HEAD_DIM), F32)],
        compiler_params=pltpu.CompilerParams(dimension_semantics=("arbitrary", "arbitrary"),
                                             vmem_limit_bytes=40 * MIB),
        name="gdn_prompt",
    )(seq3(gqkv), seq3(sm), smt, seq3(gz), gn)
    return out.reshape(n_seq * seq_len, GDN_WIDTH), s_out


def _ffn_tail(x, fox_ref, gdn_ref, wo_ref, nf_ref, wup_ref, wd_ref, nfin_ref, conv):
    mix = _dot(fox_ref[...], wo_ref[0:FOX_WIDTH, :]) + _dot(gdn_ref[...], wo_ref[FOX_WIDTH:, :])
    x2 = x + mix
    h2 = _rms(x2, nf_ref[...]).astype(BF16)
    gu = _dot(h2, wup_ref[...])
    gate, up = gu[:, :FFN_DIM], gu[:, FFN_DIM:]
    act = (_silu(conv(gate)) * up).astype(BF16)
    x3 = x2 + _dot(act, wd_ref[...])
    return _rms(x3, nfin_ref[...])


def _ffn_prompt_body(pt_ref, *refs, tiles_per_seq, rider_cfg):
    (x_ref, fox_ref, gdn_ref, wo_ref, nf_ref, wup_ref, cw_ref, cb_ref, wd_ref, nfin_ref) = refs[:10]
    rider_in = refs[10:10 + RIDER_INPUTS]
    y_ref, cst_ref, rider_out, prev_ref = refs[10 + RIDER_INPUTS:14 + RIDER_INPUTS]
    rider_scratch = refs[14 + RIDER_INPUTS:]
    i = pl.program_id(0)

    @pl.when(i % tiles_per_seq == 0)
    def _():
        prev_ref[...] = jnp.zeros_like(prev_ref)

    _rider_step(pt_ref, rider_in, rider_out, rider_scratch, rider_cfg)

    tm = x_ref.shape[0]

    def conv(gate):
        prev = prev_ref[...]
        out = gate * cw_ref[FFN_CONV - 1:FFN_CONV, :] + cb_ref[...]
        for kk in range(1, FFN_CONV):
            out = out + _shift_rows(gate, prev, kk) * cw_ref[FFN_CONV - 1 - kk:FFN_CONV - kk, :]
        prev_ref[...] = gate[tm - SUBLANES:, :]
        cst_ref[...] = gate[tm - SUBLANES:, :]
        return out

    y_ref[...] = _ffn_tail(x_ref[...], fox_ref, gdn_ref, wo_ref, nf_ref, wup_ref, wd_ref, nfin_ref, conv)


def _ffn_prompt(x, fox, gdn, wo, nf, wup, cw, cb, wd, nfin, seq_len, make_rider):
    rows = x.shape[0]
    tm = ROW_TILE
    n_seq = rows // seq_len
    tiles_per_seq = seq_len // tm
    row_spec = lambda w: pl.BlockSpec((tm, w), lambda i, pt: (i, 0))
    const = lambda a: pl.BlockSpec(a.shape, lambda i, pt: (0, 0), pipeline_mode=pl.Buffered(1))
    rider = make_rider(rows // tm)
    grid_spec = pltpu.PrefetchScalarGridSpec(
        num_scalar_prefetch=1,
        grid=(rows // tm,),
        in_specs=[row_spec(D_MODEL), row_spec(FOX_WIDTH), row_spec(GDN_WIDTH), const(wo), const(nf),
                  const(wup), const(cw), const(cb), const(wd), const(nfin)] + rider["in_specs"],
        out_specs=(row_spec(D_MODEL),
                   pl.BlockSpec((SUBLANES, FFN_DIM), lambda i, pt: (i // tiles_per_seq, 0)),
                   rider["out_spec"]),
        scratch_shapes=[pltpu.VMEM((SUBLANES, FFN_DIM), F32)] + rider["scratch"],
    )
    return pl.pallas_call(
        functools.partial(_ffn_prompt_body, tiles_per_seq=tiles_per_seq, rider_cfg=rider["cfg"]),
        grid_spec=grid_spec,
        out_shape=(jax.ShapeDtypeStruct((rows, D_MODEL), F32),
                   jax.ShapeDtypeStruct((n_seq * SUBLANES, FFN_DIM), F32),
                   rider["out_shape"]),
        compiler_params=pltpu.CompilerParams(dimension_semantics=("arbitrary",), vmem_limit_bytes=58 * MIB),
        name="ffn_prompt",
    )(rider["prefetch"], x, fox, gdn, wo, nf, wup, cw, cb, wd, nfin, *rider["operands"])


def _ffn_sample_body(x_ref, fox_ref, gdn_ref, wo_ref, nf_ref, wup_ref, cw_ref, cb_ref, wd_ref, nfin_ref,
                     c0_ref, c1_ref, y_ref, gate_ref):
    def conv(gate):
        gate_ref[...] = gate
        return (gate * cw_ref[2:3, :] + c1_ref[...] * cw_ref[1:2, :] + c0_ref[...] * cw_ref[0:1, :]
                + cb_ref[...])

    y_ref[...] = _ffn_tail(x_ref[...], fox_ref, gdn_ref, wo_ref, nf_ref, wup_ref, wd_ref, nfin_ref, conv)


def _ffn_sample(x, fox, gdn, wo, nf, wup, cw, cb, wd, nfin, c0, c1):
    rows = x.shape[0]
    args = (x, fox, gdn, wo, nf, wup, cw, cb, wd, nfin, c0, c1)
    full = lambda a: pl.BlockSpec(a.shape, lambda i: (0, 0), pipeline_mode=pl.Buffered(1))
    return pl.pallas_call(
        _ffn_sample_body,
        grid=(1,),
        in_specs=[full(a) for a in args],
        out_specs=(pl.BlockSpec((rows, D_MODEL), lambda i: (0, 0)),
                   pl.BlockSpec((rows, FFN_DIM), lambda i: (0, 0))),
        out_shape=(jax.ShapeDtypeStruct((rows, D_MODEL), F32), jax.ShapeDtypeStruct((rows, FFN_DIM), F32)),
        compiler_params=pltpu.CompilerParams(dimension_semantics=("arbitrary",), vmem_limit_bytes=56 * MIB),
        name="ffn_sample",
    )(*args)


def _split3(x):
    hi = x.astype(BF16)
    r1 = x - hi.astype(F32)
    mid = r1.astype(BF16)
    lo = (r1 - mid.astype(F32)).astype(BF16)
    return hi, mid, lo


RIDER_INPUTS = 8
RIDER_SCRATCH = 4


def _rider_step(pt_ref, rider_in, o_ref, rider_scratch, cfg):
    q_ref, kn_ref, vn_ref, sm_ref, msuf_ref, kt_pool, vt_pool, lf_pool = rider_in
    kbuf, vbuf, lbuf, sems = rider_scratch
    base, count, per_step, n_pages = cfg
    i = pl.program_id(0)

    def copies(g, sl):
        out = []
        for j in range(n_pages):
            pg = pt_ref[g * n_pages + j]
            out.append(pltpu.make_async_copy(kt_pool.at[pg], kbuf.at[sl, j], sems.at[sl, 0]))
            out.append(pltpu.make_async_copy(vt_pool.at[pg], vbuf.at[sl, j], sems.at[sl, 1]))
            out.append(pltpu.make_async_copy(lf_pool.at[pg], lbuf.at[sl, j], sems.at[sl, 2]))
        return out

    @pl.when(i == 0)
    def _():
        for cp in copies(base, base % 2):
            cp.start()

    for r in range(per_step):
        g = base + i * per_step + r
        slot = g % 2

        @pl.when(g + 1 < base + count)
        def _():
            for cp in copies(g + 1, 1 - slot):
                cp.start()

        for cp in copies(g, slot):
            cp.wait()
        o_ref[r] = _decode_row(slot, q_ref[r], kn_ref[r], vn_ref[r], sm_ref[r], msuf_ref[...],
                               kbuf, vbuf, lbuf, n_pages)


def _decode_row(slot, q, k_new, v_new, sm_row, msuf, kbuf, vbuf, lbuf, n_pages):
    hd = FOX_HEAD_DIM
    page = kbuf.shape[-1]

    r8 = lax.broadcasted_iota(jnp.int32, (SUBLANES, LANES), 0)
    c8 = lax.broadcasted_iota(jnp.int32, (SUBLANES, LANES), 1)
    lf_new = jnp.sum(jnp.where(r8 == c8, sm_row, 0.0), axis=1, keepdims=True)
    n_rows = n_pages * FOX_HEADS
    lf = lbuf[slot].reshape(n_rows, page)
    within = functools.reduce(lambda a, c: a + c, [_dot(t, msuf) for t in _split3(lf)])
    tot = jnp.broadcast_to(jnp.sum(lf, axis=1, keepdims=True), (n_rows, page))
    row = lax.broadcasted_iota(jnp.int32, (n_rows, page), 0)
    later = tot
    k = FOX_HEADS
    while k < n_rows:
        later = later + jnp.where(row + k < n_rows, pltpu.roll(later, n_rows - k, 0), 0.0)
        k *= 2
    bias = (within + (later - tot)).reshape(n_pages, FOX_HEADS, page) + lf_new[None]

    q_t = q.T
    qb = jnp.stack([jnp.broadcast_to(q_t[:, h:h + 1], (hd, page)) for h in range(FOX_HEADS)], axis=0)
    logits = [jnp.sum(kbuf[slot, j] * qb, axis=1) + bias[j] for j in range(n_pages)]
    s_new = jnp.sum(k_new * q, axis=-1, keepdims=True)
    m = functools.reduce(jnp.maximum, [jnp.max(x, axis=-1, keepdims=True) for x in logits] + [s_new])
    ps = [jnp.exp(x - m) for x in logits]
    p_new = jnp.exp(s_new - m)
    l = functools.reduce(lambda a, c: a + c, [jnp.sum(x, axis=-1, keepdims=True) for x in ps] + [p_new])

    cols = []
    for h in range(FOX_HEADS):
        acc = vbuf[slot, 0, h] * ps[0][h:h + 1, :]
        for j in range(1, n_pages):
            acc = acc + vbuf[slot, j, h] * ps[j][h:h + 1, :]
        cols.append(jnp.sum(acc, axis=1, keepdims=True))
    mat = jnp.concatenate(cols + [jnp.zeros((hd, LANES - FOX_HEADS), F32)], axis=1)
    return (mat.T[0:FOX_HEADS, :] + p_new * v_new) / l


def _rider(page_table, q, kn, vn, sm, kt_pool, vt_pool, lf_pool, base, count, n_steps):
    n_pages = page_table.shape[1]
    page = kt_pool.shape[-1]
    per_step = count // n_steps
    assert per_step * n_steps == count, "sample rows must split evenly over the host's grid steps"
    tok = jnp.arange(page)
    msuf = (tok[:, None] > tok[None, :]).astype(BF16)
    rows = lambda a: a[base:base + count]
    tile = lambda a: pl.BlockSpec((per_step,) + a.shape[1:], lambda i, pt: (i,) + (0,) * (a.ndim - 1))
    hbm = pl.BlockSpec(memory_space=pl.ANY)
    operands = [rows(q), rows(kn), rows(vn), rows(sm).reshape(count, 1, LANES), msuf, kt_pool, vt_pool, lf_pool]
    kv_buf = pltpu.VMEM((2, n_pages, FOX_HEADS, FOX_HEAD_DIM, page), F32)
    return dict(
        prefetch=page_table.reshape(-1),
        operands=operands,
        in_specs=[tile(a) for a in operands[:4]] + [pl.BlockSpec(msuf.shape, lambda i, pt: (0, 0)), hbm, hbm, hbm],
        out_spec=pl.BlockSpec((per_step, FOX_HEADS, FOX_HEAD_DIM), lambda i, pt: (i, 0, 0)),
        out_shape=jax.ShapeDtypeStruct((count, FOX_HEADS, FOX_HEAD_DIM), F32),
        scratch=[kv_buf, kv_buf, pltpu.VMEM((2, n_pages, FOX_HEADS, page), F32), pltpu.SemaphoreType.DMA((2, 3))],
        cfg=(base, count, per_step, n_pages),
    )


def _gdn_sample_body(s_ref, qkv_ref, sm_ref, z_ref, gn_ref, sout_ref, o_ref):
    d = GDN_HEAD_DIM
    eye = lax.broadcasted_iota(jnp.int32, (d, d), 0) == lax.broadcasted_iota(jnp.int32, (d, d), 1)

    def col(rows):
        return jnp.sum(jnp.where(eye[None], rows[:, None, :], 0.0), axis=2, keepdims=True)

    for hh in range(GDN_HEADS):
        hs = slice(hh * d, (hh + 1) * d)
        q = qkv_ref[:, hs]
        k = qkv_ref[:, GDN_WIDTH + hh * d:GDN_WIDTH + (hh + 1) * d]
        v = qkv_ref[:, 2 * GDN_WIDTH + hh * d:2 * GDN_WIDTH + (hh + 1) * d]
        eg = jnp.exp(sm_ref[:, SM_G + hh:SM_G + hh + 1])
        beta = sm_ref[:, SM_BETA + hh:SM_BETA + hh + 1]
        s = s_ref[:, hh]
        k_col = col(k)
        v_new = beta * (v - eg * jnp.sum(k_col * s, axis=1))
        o = eg * jnp.sum(col(q) * s, axis=1) + jnp.sum(q * k, axis=1, keepdims=True) * v_new
        sout_ref[:, hh] = s * eg[:, :, None] + k_col * v_new[:, None, :]
        o_ref[:, hs] = _rms(o, gn_ref[...]) * _silu(z_ref[:, hs])


def _gdn_sample(state, gqkv, sm, gz, gn):
    nb = state.shape[0]
    bb = SUBLANES
    d = GDN_HEAD_DIM
    row_spec = lambda w: pl.BlockSpec((bb, w), lambda i: (i, 0))
    st_spec = pl.BlockSpec((bb, GDN_HEADS, d, d), lambda i: (i, 0, 0, 0))
    return pl.pallas_call(
        _gdn_sample_body,
        grid=(nb // bb,),
        in_specs=[st_spec, row_spec(3 * GDN_WIDTH), row_spec(LANES), row_spec(GDN_WIDTH),
                  pl.BlockSpec((1, d), lambda i: (0, 0))],
        out_specs=(st_spec, row_spec(GDN_WIDTH)),
        out_shape=(jax.ShapeDtypeStruct(state.shape, F32), jax.ShapeDtypeStruct((nb, GDN_WIDTH), F32)),
        compiler_params=pltpu.CompilerParams(dimension_semantics=("arbitrary",), vmem_limit_bytes=32 * MIB),
        name="gdn_sample",
    )(state, gqkv, sm, gz, gn)


def _pad_rows(a, rows):
    return jnp.concatenate([a, jnp.zeros((rows - a.shape[0],) + a.shape[1:], a.dtype)], axis=0)


def kernel(x_prompt, x_sample, cache_k, cache_v, cache_logf, state_gdn, state_gdn_conv, state_ffn_conv,
           page_table, norm_mix, w_in, b_forget, gdn_a_log, gdn_dt_bias, w_gdn_conv, gdn_out_norm, w_out,
           norm_ffn, w_up, w_ffn_conv, b_ffn_conv, w_down, norm_final):
    assert w_in.shape[0] == 1, "single-layer trunk"
    n_seq, seq_len, _ = x_prompt.shape
    nb = x_sample.shape[0]
    n_pool, page = cache_k.shape[1], cache_k.shape[2]

    w = w_in[0]
    o_ff = 3 * FOX_WIDTH
    o_g = o_ff + FOX_HEADS
    o_ga = o_g + 3 * GDN_WIDTH
    o_gz = o_ga + 2 * GDN_HEADS
    wbig = jnp.concatenate([w[:, :o_ff], w[:, o_g:o_ga], w[:, o_gz:]], axis=1).astype(BF16)
    wsm = jnp.concatenate([w[:, o_ff:o_g], w[:, o_ga:o_gz],
                           jnp.zeros((D_MODEL, LANES - FOX_HEADS - 2 * GDN_HEADS), F32)], axis=1).astype(BF16)
    par = jnp.zeros((SUBLANES, LANES), F32)
    par = par.at[0, SM_LF:SM_LF + FOX_HEADS].set(b_forget[0])
    par = par.at[0, SM_G:SM_G + GDN_HEADS].set(gdn_dt_bias[0])
    par = par.at[1, SM_G:SM_G + GDN_HEADS].set(gdn_a_log[0])
    nrm = norm_mix[0][None, :]
    cw = _pad_rows(w_gdn_conv[0], SUBLANES)
    gn = gdn_out_norm[0][None, :]
    wo = w_out[0].astype(BF16)
    nf = norm_ffn[0][None, :]
    wup = w_up[0].astype(BF16)
    cwf = _pad_rows(w_ffn_conv[0], SUBLANES)
    cbf = b_ffn_conv[0][None, :]
    wd = w_down[0].astype(BF16)
    nfin = norm_final[None, :]

    xs = x_sample.reshape(nb, D_MODEL)
    gctx = state_gdn_conv[0]
    fctx = state_ffn_conv[0]
    fk_s, fv_s, q_s, gqkv_s, gz_s, sm_s, pre_s = _proj_sample(
        xs, nrm, wbig, wsm, par, cw, gctx[:, 0], gctx[:, 1], gctx[:, 2])
    heads = lambda a: a.reshape(nb, FOX_HEADS, FOX_HEAD_DIM)
    half = nb // 2
    decode_args = (page_table, heads(q_s), heads(fk_s), heads(fv_s), sm_s,
                   jnp.transpose(cache_k[0], (0, 2, 3, 1)), jnp.transpose(cache_v[0], (0, 2, 3, 1)),
                   jnp.transpose(cache_logf[0], (0, 2, 1)))
    rider_a = functools.partial(_rider, *decode_args, 0, half)
    rider_b = functools.partial(_rider, *decode_args, half, nb - half)

    xp = x_prompt.reshape(n_seq * seq_len, D_MODEL)
    kt, vt, qt16, kaug, vt16, gqkv, gz, sm, smt, cst, fox_sa = _proj_prompt(
        xp, nrm, wbig, wsm, par, cw, seq_len, rider_a)
    fox = _fox_prompt(qt16, kaug, vt16, n_seq, seq_len)
    gdn, s_p = _gdn_prompt(gqkv, sm, smt, gz, gn, n_seq, seq_len)
    yp, cstf, fox_sb = _ffn_prompt(xp, fox, gdn, wo, nf, wup, cwf, cbf, wd, nfin, seq_len, rider_b)

    fox_s = jnp.concatenate([fox_sa, fox_sb], axis=0)
    s_s, gdn_s = _gdn_sample(state_gdn[0], gqkv_s, sm_s, gz_s, gn)
    ys, gate_s = _ffn_sample(xs, fox_s.reshape(nb, FOX_WIDTH).astype(BF16), gdn_s.astype(BF16), wo, nf, wup,
                             cwf, cbf, wd, nfin, fctx[:, 0], fctx[:, 1])

    kv_shape_s = (1, nb, 1, FOX_HEADS, FOX_HEAD_DIM)
    new_kv_p = lambda a: jnp.transpose(a.reshape(1, n_seq, FOX_HEADS, FOX_HEAD_DIM, seq_len), (0, 1, 4, 2, 3))
    return (
        yp.reshape(n_seq, seq_len, D_MODEL),
        ys.reshape(nb, 1, D_MODEL),
        new_kv_p(kt),
        new_kv_p(vt),
        sm[:, SM_LF:SM_LF + FOX_HEADS].reshape(1, n_seq, seq_len, FOX_HEADS),
        s_p[None],
        cst.reshape(n_seq, SUBLANES, 3 * GDN_WIDTH)[None, :, SUBLANES - (GDN_CONV - 1):],
        cstf.reshape(n_seq, SUBLANES, FFN_DIM)[None, :, SUBLANES - (FFN_CONV - 1):],
        fk_s.reshape(kv_shape_s),
        fv_s.reshape(kv_shape_s),
        sm_s[:, SM_LF:SM_LF + FOX_HEADS].reshape(1, nb, 1, FOX_HEADS),
        s_s[None],
        jnp.concatenate([gctx[:, 1:], pre_s[:, None, :]], axis=1)[None],
        jnp.concatenate([fctx[:, 1:], gate_s[:, None, :]], axis=1)[None],
    )
```

```python
import functools

import jax
import jax.numpy as jnp
from jax import lax
from jax.experimental import pallas as pl
from jax.experimental.pallas import tpu as pltpu

D_MODEL = 1024
FOX_HEADS = 8
FOX_HEAD_DIM = 64
FOX_WIDTH = FOX_HEADS * FOX_HEAD_DIM
GDN_HEADS = 4
GDN_HEAD_DIM = 128
GDN_WIDTH = GDN_HEADS * GDN_HEAD_DIM
GDN_CONV = 4
GDN_CHUNK = 64
FFN_DIM = 2816
FFN_CONV = 3
EPS = 1e-6
NEG_BIG = -1e30
FOX_SCALE = FOX_HEAD_DIM ** -0.5
GDN_SCALE = GDN_HEAD_DIM ** -0.5
LOG2E = 1.4426950408889634
FOX_QSCALE = FOX_SCALE * LOG2E
FOX_AUG = 3
FOX_DEN_ROWS = 16
FOX_GROUP = 2

LANES = 128
SUBLANES = 8
ROW_TILE = 256
FOX_TILE = 256
MIB = 1024 * 1024

SM_LF = 0
SM_G = 8
SM_BETA = 12
SM_CUM = 16
SM_GCUM = 24

F32 = jnp.float32
BF16 = jnp.bfloat16


def _sigmoid(x):
    return 1.0 / (1.0 + jnp.exp(-x))


def _silu(x):
    return x * _sigmoid(x)


def _rms(x, g):
    return x * lax.rsqrt(jnp.mean(x * x, axis=-1, keepdims=True) + EPS) * g


def _dot(a, b):
    return jnp.dot(a, b, preferred_element_type=F32)


def _shift_rows(x, prev8, k):
    r = pltpu.roll(x, k, 0)
    row8 = lax.broadcasted_iota(jnp.int32, prev8.shape, 0)
    top = jnp.where(row8 < k, pltpu.roll(prev8, k, 0), r[0:SUBLANES])
    return jnp.concatenate([top, r[SUBLANES:]], axis=0)


def _gate_activations(raw, par_ref):
    z = raw + par_ref[0:1, :]
    lane = lax.broadcasted_iota(jnp.int32, z.shape, 1)
    t = jnp.log1p(jnp.exp(-jnp.abs(z)))
    lf = jnp.minimum(z, 0.0) - t
    softplus = jnp.maximum(z, 0.0) + t
    g = -jnp.exp(par_ref[1:2, :]) * softplus
    beta = _sigmoid(z)
    return jnp.where(lane < SM_G, lf, jnp.where(lane < SM_BETA, g, jnp.where(lane < SM_CUM, beta, 0.0)))


def _gdn_qkv_norm(c, out_ref):
    for part, scale in ((0, GDN_SCALE), (1, 1.0)):
        for hh in range(GDN_HEADS):
            off = part * GDN_WIDTH + hh * GDN_HEAD_DIM
            seg = c[:, off:off + GDN_HEAD_DIM]
            n = lax.rsqrt(jnp.sum(seg * seg, axis=-1, keepdims=True) + EPS)
            out_ref[:, off:off + GDN_HEAD_DIM] = seg * n * scale
    out_ref[:, 2 * GDN_WIDTH:] = c[:, 2 * GDN_WIDTH:]


def _proj_prompt_body(pt_ref, *refs, tiles_per_seq, rider_cfg):
    (x_ref, nrm_ref, wfox_ref, wgdn_ref, wgz_ref, wsm_ref, par_ref, cw_ref, scat_ref) = refs[:9]
    rider_in = refs[9:9 + RIDER_INPUTS]
    (kt_ref, vt_ref, qt16_ref, kaug_ref, vt16_ref, gqkv_ref, gz_ref, sm_ref, smt_ref, cst_ref,
     rider_out, carry_ref, prev_ref) = refs[9 + RIDER_INPUTS:22 + RIDER_INPUTS]
    rider_scratch = refs[22 + RIDER_INPUTS:]
    i = pl.program_id(0)

    @pl.when(i % tiles_per_seq == 0)
    def _():
        carry_ref[...] = jnp.zeros_like(carry_ref)
        prev_ref[...] = jnp.zeros_like(prev_ref)

    _rider_step(pt_ref, rider_in, rider_out, rider_scratch, rider_cfg)

    tm = x_ref.shape[0]
    h16 = _rms(x_ref[...], nrm_ref[...]).astype(BF16)

    fox = _dot(h16, wfox_ref[...])
    fq, fk, fv = fox[:, :FOX_WIDTH], fox[:, FOX_WIDTH:2 * FOX_WIDTH], fox[:, 2 * FOX_WIDTH:]
    fv_t = fv.T
    kt_ref[0] = fk.T
    vt_ref[0] = fv_t
    vt16_ref[0] = fv_t.astype(BF16)
    lane_p = lax.broadcasted_iota(jnp.int32, (tm, LANES), 1)
    ones_blk = jnp.where(lane_p < 2 * FOX_AUG, 1.0, 0.0)
    q_aug = []
    for p in range(FOX_HEADS // 2):
        q_aug += [fq[:, p * LANES:(p + 1) * LANES] * FOX_QSCALE, ones_blk]
        kaug_ref[:, 2 * p * LANES:(2 * p + 1) * LANES] = fk[:, p * LANES:(p + 1) * LANES].astype(BF16)
    qt16_ref[0] = jnp.concatenate(q_aug, axis=1).T.astype(BF16)

    act = _gate_activations(_dot(h16, wsm_ref[...]), par_ref)
    row = lax.broadcasted_iota(jnp.int32, act.shape, 0)
    lane = lax.broadcasted_iota(jnp.int32, act.shape, 1)
    y = act
    yc = act
    k = 1
    while k < tm:
        y = y + jnp.where(row >= k, pltpu.roll(y, k, 0), 0.0)
        if k < GDN_CHUNK:
            yc = yc + jnp.where((row & (GDN_CHUNK - 1)) >= k, pltpu.roll(yc, k, 0), 0.0)
        k *= 2
    y = y + carry_ref[0:1, :]
    carry_ref[0:1, :] = y[tm - 1:tm, :]
    terms = jnp.concatenate(_split3(y * (-LOG2E)), axis=1)
    extra = _dot(terms, scat_ref[...]).astype(BF16)
    for p in range(FOX_HEADS // 2):
        kaug_ref[:, (2 * p + 1) * LANES:(2 * p + 2) * LANES] = extra[:, p * LANES:(p + 1) * LANES]
    shift = SM_CUM - SM_LF
    sm = jnp.where(lane < SM_CUM, act,
                   jnp.where(lane < SM_GCUM, pltpu.roll(y, shift, 1),
                             jnp.where(lane < SM_GCUM + GDN_HEADS, pltpu.roll(yc, shift, 1), 0.0)))
    sm_ref[...] = sm
    smt_ref[0] = sm.T

    pre = _dot(h16, wgdn_ref[...])
    assert GDN_CONV == 4
    s1 = _shift_rows(pre, prev_ref[0:SUBLANES, :], 1)
    far = pre * cw_ref[1:2, :] + s1 * cw_ref[0:1, :]
    acc = pre * cw_ref[3:4, :] + s1 * cw_ref[2:3, :] + _shift_rows(far, prev_ref[SUBLANES:, :], 2)
    prev_ref[0:SUBLANES, :] = pre[tm - SUBLANES:, :]
    prev_ref[SUBLANES:, :] = far[tm - SUBLANES:, :]
    cst_ref[...] = pre[tm - SUBLANES:, :]
    _gdn_qkv_norm(_silu(acc), gqkv_ref)

    gz_ref[...] = _dot(h16, wgz_ref[...])


def _proj_prompt(x, nrm, wfox, wgdn, wgz, wsm, par, cw, seq_len, make_rider):
    rows = x.shape[0]
    tm = ROW_TILE
    n_seq = rows // seq_len
    tiles_per_seq = seq_len // tm
    row_spec = lambda w: pl.BlockSpec((tm, w), lambda i, pt: (i, 0))
    col_spec = lambda w: pl.BlockSpec((1, w, tm), lambda i, pt: (i // tiles_per_seq, 0, i % tiles_per_seq))
    const = lambda s: pl.BlockSpec(s, lambda i, pt: (0, 0), pipeline_mode=pl.Buffered(1))
    rider = make_rider(rows // tm)
    aug_width = FOX_HEADS * LANES
    src = jnp.arange(3 * LANES)
    term, head = src // LANES, src % LANES
    dst = (head // 2) * LANES + (head % 2) * FOX_AUG + term
    scat = ((jnp.arange(FOX_WIDTH)[None, :] == dst[:, None]) & (head[:, None] < FOX_HEADS)).astype(BF16)
    out_shape = (
        jax.ShapeDtypeStruct((n_seq, FOX_WIDTH, seq_len), F32),
        jax.ShapeDtypeStruct((n_seq, FOX_WIDTH, seq_len), F32),
        jax.ShapeDtypeStruct((n_seq, aug_width, seq_len), BF16),
        jax.ShapeDtypeStruct((rows, aug_width), BF16),
        jax.ShapeDtypeStruct((n_seq, FOX_WIDTH, seq_len), BF16),
        jax.ShapeDtypeStruct((rows, 3 * GDN_WIDTH), F32),
        jax.ShapeDtypeStruct((rows, GDN_WIDTH), F32),
        jax.ShapeDtypeStruct((rows, LANES), F32),
        jax.ShapeDtypeStruct((n_seq, LANES, seq_len), F32),
        jax.ShapeDtypeStruct((n_seq * SUBLANES, 3 * GDN_WIDTH), F32),
    )
    out_specs = (
        col_spec(FOX_WIDTH), col_spec(FOX_WIDTH), col_spec(aug_width), row_spec(aug_width),
        col_spec(FOX_WIDTH), row_spec(3 * GDN_WIDTH), row_spec(GDN_WIDTH), row_spec(LANES),
        col_spec(LANES),
        pl.BlockSpec((SUBLANES, 3 * GDN_WIDTH), lambda i, pt: (i // tiles_per_seq, 0)),
    )
    grid_spec = pltpu.PrefetchScalarGridSpec(
        num_scalar_prefetch=1,
        grid=(rows // tm,),
        in_specs=[row_spec(D_MODEL), const((1, D_MODEL)), const(wfox.shape), const(wgdn.shape), const(wgz.shape),
                  const((D_MODEL, LANES)), const((SUBLANES, LANES)), const((SUBLANES, 3 * GDN_WIDTH)),
                  const(scat.shape)] + rider["in_specs"],
        out_specs=out_specs + (rider["out_spec"],),
        scratch_shapes=[pltpu.VMEM((SUBLANES, LANES), F32), pltpu.VMEM((2 * SUBLANES, 3 * GDN_WIDTH), F32)]
        + rider["scratch"],
    )
    return pl.pallas_call(
        functools.partial(_proj_prompt_body, tiles_per_seq=tiles_per_seq, rider_cfg=rider["cfg"]),
        grid_spec=grid_spec,
        out_shape=out_shape + (rider["out_shape"],),
        compiler_params=pltpu.CompilerParams(dimension_semantics=("arbitrary",), vmem_limit_bytes=56 * MIB),
        name="proj_prompt",
    )(rider["prefetch"], x, nrm, wfox, wgdn, wgz, wsm, par, cw, scat, *rider["operands"])


def _proj_sample_body(x_ref, nrm_ref, wfox_ref, wgdn_ref, wgz_ref, wsm_ref, par_ref, cw_ref, c0_ref, c1_ref, c2_ref,
                      fk_ref, fv_ref, q_ref, gqkv_ref, gz_ref, sm_ref, pre_ref):
    h16 = _rms(x_ref[...], nrm_ref[...]).astype(BF16)
    fox = _dot(h16, wfox_ref[...])
    q_ref[...] = fox[:, :FOX_WIDTH] * FOX_SCALE
    fk_ref[...] = fox[:, FOX_WIDTH:2 * FOX_WIDTH]
    fv_ref[...] = fox[:, 2 * FOX_WIDTH:]
    sm_ref[...] = _gate_activations(_dot(h16, wsm_ref[...]), par_ref)
    pre = _dot(h16, wgdn_ref[...])
    pre_ref[...] = pre
    acc = (pre * cw_ref[3:4, :] + c2_ref[...] * cw_ref[2:3, :]
           + c1_ref[...] * cw_ref[1:2, :] + c0_ref[...] * cw_ref[0:1, :])
    _gdn_qkv_norm(_silu(acc), gqkv_ref)
    gz_ref[...] = _dot(h16, wgz_ref[...])


def _proj_sample(x, nrm, wfox, wgdn, wgz, wsm, par, cw, c0, c1, c2):
    rows = x.shape[0]
    full = lambda a: pl.BlockSpec(a.shape, lambda i: (0,) * a.ndim)
    args = (x, nrm, wfox, wgdn, wgz, wsm, par, cw, c0, c1, c2)
    shapes = ((rows, FOX_WIDTH), (rows, FOX_WIDTH), (rows, FOX_WIDTH), (rows, 3 * GDN_WIDTH),
              (rows, GDN_WIDTH), (rows, LANES), (rows, 3 * GDN_WIDTH))
    return pl.pallas_call(
        _proj_sample_body,
        grid=(1,),
        in_specs=[full(a) for a in args],
        out_specs=tuple(pl.BlockSpec(s, lambda i: (0, 0)) for s in shapes),
        out_shape=tuple(jax.ShapeDtypeStruct(s, F32) for s in shapes),
        compiler_params=pltpu.CompilerParams(dimension_semantics=("arbitrary",), vmem_limit_bytes=48 * MIB),
        name="proj_sample",
    )(*args)


def _fox_prompt_body(qt_ref, kaug_ref, vt_ref, o_ref):
    i = pl.program_id(1)
    t = FOX_TILE
    hd = FOX_HEAD_DIM
    aug = 2 * LANES
    rr = lax.broadcasted_iota(jnp.int32, (t, t), 0)
    cc = lax.broadcasted_iota(jnp.int32, (t, t), 1)
    causal = rr <= cc
    row = lax.broadcasted_iota(jnp.int32, (aug, t), 0)
    chains = [(g, h) for g in range(qt_ref.shape[0]) for h in range(FOX_HEADS)]
    qs = []
    for g, h in chains:
        e = h % 2
        qp = qt_ref[g, (h // 2) * aug:(h // 2 + 1) * aug, :]
        own = jnp.logical_or(jnp.logical_and(row >= e * hd, row < (e + 1) * hd),
                             jnp.logical_and(row >= LANES + e * FOX_AUG, row < LANES + (e + 1) * FOX_AUG))
        qs.append(jnp.where(own, qp, jnp.zeros_like(qp)))

    def tile(j, carry, masked):
        koff = pl.multiple_of(j * t, t)
        ss = [_dot(kaug_ref[g, pl.ds(koff, t), (h // 2) * aug:(h // 2 + 1) * aug], qs[n])
              for n, (g, h) in enumerate(chains)]
        stats, pms = [], []
        for n in range(len(chains)):
            m = carry[n][0]
            s = jnp.where(causal, ss[n], NEG_BIG) if masked else ss[n]
            m_new = jnp.maximum(m, jnp.max(s, axis=0, keepdims=True))
            stats.append((m_new, jnp.exp2(m - m_new)))
            pms.append(jnp.exp2(s - m_new).astype(BF16))
        out = []
        for n, (g, h) in enumerate(chains):
            m_new, alpha = stats[n]
            v_ones = jnp.concatenate([vt_ref[g, h * hd:(h + 1) * hd, pl.ds(koff, t)], ones_rows], axis=0)
            out.append((m_new, alpha * carry[n][1] + _dot(v_ones, pms[n])))
        return tuple(out)

    ones_rows = jnp.ones((FOX_DEN_ROWS, t), BF16)
    init = tuple((jnp.full((1, t), NEG_BIG, F32), jnp.zeros((hd + FOX_DEN_ROWS, t), F32)) for _ in chains)
    carry = lax.fori_loop(0, i, lambda j, c: tile(j, c, False), init)
    final = tile(i, carry, True)
    for n in range(0, len(chains), 2):
        g, h = chains[n]
        outs = [a[:hd] / a[hd:hd + 1] for _, a in final[n:n + 2]]
        o_ref[g, :, (h // 2) * LANES:(h // 2 + 1) * LANES] = jnp.concatenate(outs, axis=0).T.astype(BF16)


def _fox_prompt(qt16, kaug, vt16, n_seq, seq_len):
    t = FOX_TILE
    nq = seq_len // t
    aug_width = kaug.shape[1]
    group = FOX_GROUP if n_seq % FOX_GROUP == 0 else 1
    out = pl.pallas_call(
        _fox_prompt_body,
        grid=(n_seq // group, nq),
        in_specs=[pl.BlockSpec((group, aug_width, t), lambda b, i: (b, 0, i)),
                  pl.BlockSpec((group, seq_len, aug_width), lambda b, i: (b, 0, 0)),
                  pl.BlockSpec((group, FOX_WIDTH, seq_len), lambda b, i: (b, 0, 0))],
        out_specs=pl.BlockSpec((group, t, FOX_WIDTH), lambda b, i: (b, i, 0)),
        out_shape=jax.ShapeDtypeStruct((n_seq, seq_len, FOX_WIDTH), BF16),
        compiler_params=pltpu.CompilerParams(dimension_semantics=("arbitrary", "arbitrary"),
                                             vmem_limit_bytes=48 * MIB),
        name="fox_prompt",
    )(qt16, kaug.reshape(n_seq, seq_len, aug_width), vt16)
    return out.reshape(n_seq * seq_len, FOX_WIDTH)


def _bdot(a, b):
    return lax.dot_general(a.astype(BF16), b.astype(BF16), (((2,), (1,)), ((0,), (0,))),
                           preferred_element_type=F32)


def _bdot_nt(a, b):
    return lax.dot_general(a.astype(BF16), b.astype(BF16), (((2,), (2,)), ((0,), (0,))),
                           preferred_element_type=F32)


def _unit_lower_inverse_minus_eye(a, r, c):
    blk16 = (r // 16) == (c // 16)
    blk32 = (r // 32) == (c // 32)
    p = jnp.where(blk16, -a, 0.0)
    dt = p
    for _ in range(3):
        p = _bdot(p, p)
        dt = dt + p + _bdot(dt, p)
    for off in (jnp.where(jnp.logical_and(blk32, jnp.logical_not(blk16)), a, 0.0),
                jnp.where(blk32, 0.0, a)):
        x = off + _bdot(dt, off)
        dt = dt - (x + _bdot(x, dt))
    return dt


def _gdn_prompt_body(qkv_ref, sm_ref, gcr_ref, z_ref, gn_ref, o_ref, sout_ref, s_ref):
    ci = pl.program_id(1)

    @pl.when(ci == 0)
    def _():
        s_ref[...] = jnp.zeros_like(s_ref)

    ch = GDN_CHUNK
    d = GDN_HEAD_DIM
    nh = GDN_HEADS
    n_seq = qkv_ref.shape[0]
    n_chunks = qkv_ref.shape[1] // ch
    units = [(cidx, g, hh) for cidx in range(n_chunks) for g in range(n_seq) for hh in range(nh)]
    per_chunk = n_seq * nh

    def gather(fn):
        return jnp.stack([fn(g, slice(cidx * ch, (cidx + 1) * ch), hh) for cidx, g, hh in units], axis=0)

    q = gather(lambda g, rows, hh: qkv_ref[g, rows, hh * d:(hh + 1) * d])
    k = gather(lambda g, rows, hh: qkv_ref[g, rows, GDN_WIDTH + hh * d:GDN_WIDTH + (hh + 1) * d])
    v = gather(lambda g, rows, hh: qkv_ref[g, rows, 2 * GDN_WIDTH + hh * d:2 * GDN_WIDTH + (hh + 1) * d])
    beta = gather(lambda g, rows, hh: sm_ref[g, rows, SM_BETA + hh:SM_BETA + hh + 1])
    gc = gather(lambda g, rows, hh: sm_ref[g, rows, SM_GCUM + hh:SM_GCUM + hh + 1])
    gr = gather(lambda g, rows, hh: gcr_ref[g, hh:hh + 1, rows])

    r = lax.broadcasted_iota(jnp.int32, (1, ch, ch), 1)
    c = lax.broadcasted_iota(jnp.int32, (1, ch, ch), 2)
    lower = r >= c
    beta = jnp.broadcast_to(beta, (len(units), ch, d))
    gc = jnp.broadcast_to(gc, (len(units), ch, d))
    decay = jnp.where(lower, jnp.exp(jnp.where(lower, gc[:, :, :ch] - gr, 0.0)), 0.0)
    qk_kk = _bdot_nt(jnp.concatenate([q, k], axis=1), k)
    qk = qk_kk[:, :ch] * decay
    a = jnp.where(r > c, beta[:, :, :ch] * qk_kk[:, ch:] * decay, 0.0)
    dt = _unit_lower_inverse_minus_eye(a, r, c)
    eg = jnp.exp(gc)
    rhs = jnp.concatenate([v * beta, k * (beta * eg)], axis=-1)
    uw = rhs + _bdot(dt, rhs)
    gl = gc[:, ch - 1:ch, :]
    qd = (q * eg).astype(BF16)
    kd = k * jnp.exp(gl - gc)
    g_last = jnp.exp(gl)
    qk16 = qk.astype(BF16)

    s = s_ref[...]
    for cidx in range(n_chunks):
        us = slice(cidx * per_chunk, (cidx + 1) * per_chunk)
        rows = slice(cidx * ch, (cidx + 1) * ch)
        s16 = s.astype(BF16)
        v_new = uw[us, :, :d] - _bdot(uw[us, :, d:], s16)
        o = _bdot(qd[us], s16) + _bdot(qk16[us], v_new)
        vn16 = v_new.astype(BF16)
        upd = jnp.stack([_dot(kd[cidx * per_chunk + j].T.astype(BF16), vn16[j]) for j in range(per_chunk)],
                        axis=0)
        s = s * g_last[us] + upd
        for g in range(n_seq):
            for hh in range(nh):
                hs = slice(hh * d, (hh + 1) * d)
                o_ref[g, rows, hs] = (_rms(o[g * nh + hh], gn_ref[...]) * _silu(z_ref[g, rows, hs])).astype(BF16)
    s_ref[...] = s

    @pl.when(ci == pl.num_programs(1) - 1)
    def _():
        sout_ref[...] = s.reshape(sout_ref.shape)


def _gdn_prompt(gqkv, sm, smt, gz, gn, n_seq, seq_len):
    tm = ROW_TILE
    nt = seq_len // tm
    gcum_block = SM_GCUM // SUBLANES
    group = 2 if n_seq % 2 == 0 else 1
    seq3 = lambda a: a.reshape(n_seq, seq_len, a.shape[-1])
    tile = lambda w: pl.BlockSpec((group, tm, w), lambda b, i: (b, i, 0))
    out, s_out = pl.pallas_call(
        _gdn_prompt_body,
        grid=(n_seq // group, nt),
        in_specs=[tile(3 * GDN_WIDTH), tile(LANES),
                  pl.BlockSpec((group, SUBLANES, tm), lambda b, i: (b, gcum_block, i)),
                  tile(GDN_WIDTH),
                  pl.BlockSpec((1, GDN_HEAD_DIM), lambda b, i: (0, 0))],
        out_specs=(tile(GDN_WIDTH),
                   pl.BlockSpec((group, GDN_HEADS, GDN_HEAD_DIM, GDN_HEAD_DIM), lambda b, i: (b, 0, 0, 0))),
        out_shape=(jax.ShapeDtypeStruct((n_seq, seq_len, GDN_WIDTH), BF16),
                   jax.ShapeDtypeStruct((n_seq, GDN_HEADS, GDN_HEAD_DIM, GDN_HEAD_DIM), F32)),
        scratch_shapes=[pltpu.VMEM((group * GDN_HEADS, GDN_HEAD_DIM, GDN_HEAD_DIM), F32)],
        compiler_params=pltpu.CompilerParams(dimension_semantics=("arbitrary", "arbitrary"),
                                             vmem_limit_bytes=40 * MIB),
        name="gdn_prompt",
    )(seq3(gqkv), seq3(sm), smt, seq3(gz), gn)
    return out.reshape(n_seq * seq_len, GDN_WIDTH), s_out


def _ffn_tail(x, fox_ref, gdn_ref, wo_ref, nf_ref, wup_ref, wd_ref, nfin_ref, conv):
    mix = _dot(fox_ref[...], wo_ref[0:FOX_WIDTH, :]) + _dot(gdn_ref[...], wo_ref[FOX_WIDTH:, :])
    x2 = x + mix
    h2 = _rms(x2, nf_ref[...]).astype(BF16)
    gu = _dot(h2, wup_ref[...])
    gate, up = gu[:, :FFN_DIM], gu[:, FFN_DIM:]
    act = (_silu(conv(gate)) * up).astype(BF16)
    x3 = x2 + _dot(act, wd_ref[...])
    return _rms(x3, nfin_ref[...])


def _ffn_prompt_body(pt_ref, *refs, tiles_per_seq, rider_cfg):
    (x_ref, fox_ref, gdn_ref, wo_ref, nf_ref, wup_ref, cw_ref, cb_ref, wd_ref, nfin_ref) = refs[:10]
    rider_in = refs[10:10 + RIDER_INPUTS]
    y_ref, cst_ref, rider_out, prev_ref = refs[10 + RIDER_INPUTS:14 + RIDER_INPUTS]
    rider_scratch = refs[14 + RIDER_INPUTS:]
    i = pl.program_id(0)

    @pl.when(i % tiles_per_seq == 0)
    def _():
        prev_ref[...] = jnp.zeros_like(prev_ref)

    _rider_step(pt_ref, rider_in, rider_out, rider_scratch, rider_cfg)

    tm = x_ref.shape[0]

    def conv(gate):
        prev = prev_ref[...]
        out = gate * cw_ref[FFN_CONV - 1:FFN_CONV, :] + cb_ref[...]
        for kk in range(1, FFN_CONV):
            out = out + _shift_rows(gate, prev, kk) * cw_ref[FFN_CONV - 1 - kk:FFN_CONV - kk, :]
        prev_ref[...] = gate[tm - SUBLANES:, :]
        cst_ref[...] = gate[tm - SUBLANES:, :]
        return out

    y_ref[...] = _ffn_tail(x_ref[...], fox_ref, gdn_ref, wo_ref, nf_ref, wup_ref, wd_ref, nfin_ref, conv)


def _ffn_prompt(x, fox, gdn, wo, nf, wup, cw, cb, wd, nfin, seq_len, make_rider):
    rows = x.shape[0]
    tm = ROW_TILE
    n_seq = rows // seq_len
    tiles_per_seq = seq_len // tm
    row_spec = lambda w: pl.BlockSpec((tm, w), lambda i, pt: (i, 0))
    const = lambda a: pl.BlockSpec(a.shape, lambda i, pt: (0, 0), pipeline_mode=pl.Buffered(1))
    rider = make_rider(rows // tm)
    grid_spec = pltpu.PrefetchScalarGridSpec(
        num_scalar_prefetch=1,
        grid=(rows // tm,),
        in_specs=[row_spec(D_MODEL), row_spec(FOX_WIDTH), row_spec(GDN_WIDTH), const(wo), const(nf),
                  const(wup), const(cw), const(cb), const(wd), const(nfin)] + rider["in_specs"],
        out_specs=(row_spec(D_MODEL),
                   pl.BlockSpec((SUBLANES, FFN_DIM), lambda i, pt: (i // tiles_per_seq, 0)),
                   rider["out_spec"]),
        scratch_shapes=[pltpu.VMEM((SUBLANES, FFN_DIM), F32)] + rider["scratch"],
    )
    return pl.pallas_call(
        functools.partial(_ffn_prompt_body, tiles_per_seq=tiles_per_seq, rider_cfg=rider["cfg"]),
        grid_spec=grid_spec,
        out_shape=(jax.ShapeDtypeStruct((rows, D_MODEL), F32),
                   jax.ShapeDtypeStruct((n_seq * SUBLANES, FFN_DIM), F32),
                   rider["out_shape"]),
        compiler_params=pltpu.CompilerParams(dimension_semantics=("arbitrary",), vmem_limit_bytes=58 * MIB),
        name="ffn_prompt",
    )(rider["prefetch"], x, fox, gdn, wo, nf, wup, cw, cb, wd, nfin, *rider["operands"])


def _ffn_sample_body(x_ref, fox_ref, gdn_ref, wo_ref, nf_ref, wup_ref, cw_ref, cb_ref, wd_ref, nfin_ref,
                     c0_ref, c1_ref, y_ref, gate_ref):
    def conv(gate):
        gate_ref[...] = gate
        return (gate * cw_ref[2:3, :] + c1_ref[...] * cw_ref[1:2, :] + c0_ref[...] * cw_ref[0:1, :]
                + cb_ref[...])

    y_ref[...] = _ffn_tail(x_ref[...], fox_ref, gdn_ref, wo_ref, nf_ref, wup_ref, wd_ref, nfin_ref, conv)


def _ffn_sample(x, fox, gdn, wo, nf, wup, cw, cb, wd, nfin, c0, c1):
    rows = x.shape[0]
    args = (x, fox, gdn, wo, nf, wup, cw, cb, wd, nfin, c0, c1)
    full = lambda a: pl.BlockSpec(a.shape, lambda i: (0, 0), pipeline_mode=pl.Buffered(1))
    return pl.pallas_call(
        _ffn_sample_body,
        grid=(1,),
        in_specs=[full(a) for a in args],
        out_specs=(pl.BlockSpec((rows, D_MODEL), lambda i: (0, 0)),
                   pl.BlockSpec((rows, FFN_DIM), lambda i: (0, 0))),
        out_shape=(jax.ShapeDtypeStruct((rows, D_MODEL), F32), jax.ShapeDtypeStruct((rows, FFN_DIM), F32)),
        compiler_params=pltpu.CompilerParams(dimension_semantics=("arbitrary",), vmem_limit_bytes=56 * MIB),
        name="ffn_sample",
    )(*args)


def _split3(x):
    hi = x.astype(BF16)
    r1 = x - hi.astype(F32)
    mid = r1.astype(BF16)
    lo = (r1 - mid.astype(F32)).astype(BF16)
    return hi, mid, lo


RIDER_INPUTS = 8
RIDER_SCRATCH = 4


def _rider_step(pt_ref, rider_in, o_ref, rider_scratch, cfg):
    q_ref, kn_ref, vn_ref, sm_ref, msuf_ref, kt_pool, vt_pool, lf_pool = rider_in
    kbuf, vbuf, lbuf, sems = rider_scratch
    base, count, per_step, n_pages = cfg
    i = pl.program_id(0)

    def copies(g, sl):
        out = []
        for j in range(n_pages):
            pg = pt_ref[g * n_pages + j]
            out.append(pltpu.make_async_copy(kt_pool.at[pg], kbuf.at[sl, j], sems.at[sl, 0]))
            out.append(pltpu.make_async_copy(vt_pool.at[pg], vbuf.at[sl, j], sems.at[sl, 1]))
            out.append(pltpu.make_async_copy(lf_pool.at[pg], lbuf.at[sl, j], sems.at[sl, 2]))
        return out

    @pl.when(i == 0)
    def _():
        for cp in copies(base, base % 2):
            cp.start()

    for r in range(per_step):
        g = base + i * per_step + r
        slot = g % 2

        @pl.when(g + 1 < base + count)
        def _():
            for cp in copies(g + 1, 1 - slot):
                cp.start()

        for cp in copies(g, slot):
            cp.wait()
        o_ref[r] = _decode_row(slot, q_ref[r], kn_ref[r], vn_ref[r], sm_ref[r], msuf_ref[...],
                               kbuf, vbuf, lbuf, n_pages)


def _decode_row(slot, q, k_new, v_new, sm_row, msuf, kbuf, vbuf, lbuf, n_pages):
    hd = FOX_HEAD_DIM
    page = kbuf.shape[-1]

    r8 = lax.broadcasted_iota(jnp.int32, (SUBLANES, LANES), 0)
    c8 = lax.broadcasted_iota(jnp.int32, (SUBLANES, LANES), 1)
    lf_new = jnp.sum(jnp.where(r8 == c8, sm_row, 0.0), axis=1, keepdims=True)
    n_rows = n_pages * FOX_HEADS
    lf = lbuf[slot].reshape(n_rows, page)
    within = functools.reduce(lambda a, c: a + c, [_dot(t, msuf) for t in _split3(lf)])
    tot = jnp.broadcast_to(jnp.sum(lf, axis=1, keepdims=True), (n_rows, page))
    row = lax.broadcasted_iota(jnp.int32, (n_rows, page), 0)
    later = tot
    k = FOX_HEADS
    while k < n_rows:
        later = later + jnp.where(row + k < n_rows, pltpu.roll(later, n_rows - k, 0), 0.0)
        k *= 2
    bias = (within + (later - tot)).reshape(n_pages, FOX_HEADS, page) + lf_new[None]

    q_t = q.T
    qb = jnp.stack([jnp.broadcast_to(q_t[:, h:h + 1], (hd, page)) for h in range(FOX_HEADS)], axis=0)
    logits = [jnp.sum(kbuf[slot, j] * qb, axis=1) + bias[j] for j in range(n_pages)]
    s_new = jnp.sum(k_new * q, axis=-1, keepdims=True)
    m = functools.reduce(jnp.maximum, [jnp.max(x, axis=-1, keepdims=True) for x in logits] + [s_new])
    ps = [jnp.exp(x - m) for x in logits]
    p_new = jnp.exp(s_new - m)
    l = functools.reduce(lambda a, c: a + c, [jnp.sum(x, axis=-1, keepdims=True) for x in ps] + [p_new])

    cols = []
    for h in range(FOX_HEADS):
        acc = vbuf[slot, 0, h] * ps[0][h:h + 1, :]
        for j in range(1, n_pages):
            acc = acc + vbuf[slot, j, h] * ps[j][h:h + 1, :]
        cols.append(jnp.sum(acc, axis=1, keepdims=True))
    mat = jnp.concatenate(cols + [jnp.zeros((hd, LANES - FOX_HEADS), F32)], axis=1)
    return (mat.T[0:FOX_HEADS, :] + p_new * v_new) / l


def _rider(page_table, q, kn, vn, sm, kt_pool, vt_pool, lf_pool, base, count, n_steps):
    n_pages = page_table.shape[1]
    page = kt_pool.shape[-1]
    per_step = count // n_steps
    assert per_step * n_steps == count, "sample rows must split evenly over the host's grid steps"
    tok = jnp.arange(page)
    msuf = (tok[:, None] > tok[None, :]).astype(BF16)
    rows = lambda a: a[base:base + count]
    tile = lambda a: pl.BlockSpec((per_step,) + a.shape[1:], lambda i, pt: (i,) + (0,) * (a.ndim - 1))
    hbm = pl.BlockSpec(memory_space=pl.ANY)
    operands = [rows(q), rows(kn), rows(vn), rows(sm).reshape(count, 1, LANES), msuf, kt_pool, vt_pool, lf_pool]
    kv_buf = pltpu.VMEM((2, n_pages, FOX_HEADS, FOX_HEAD_DIM, page), F32)
    return dict(
        prefetch=page_table.reshape(-1),
        operands=operands,
        in_specs=[tile(a) for a in operands[:4]] + [pl.BlockSpec(msuf.shape, lambda i, pt: (0, 0)), hbm, hbm, hbm],
        out_spec=pl.BlockSpec((per_step, FOX_HEADS, FOX_HEAD_DIM), lambda i, pt: (i, 0, 0)),
        out_shape=jax.ShapeDtypeStruct((count, FOX_HEADS, FOX_HEAD_DIM), F32),
        scratch=[kv_buf, kv_buf, pltpu.VMEM((2, n_pages, FOX_HEADS, page), F32), pltpu.SemaphoreType.DMA((2, 3))],
        cfg=(base, count, per_step, n_pages),
    )


def _gdn_sample_body(s_ref, qkv_ref, sm_ref, z_ref, gn_ref, sout_ref, o_ref):
    d = GDN_HEAD_DIM
    eye = lax.broadcasted_iota(jnp.int32, (d, d), 0) == lax.broadcasted_iota(jnp.int32, (d, d), 1)

    def col(rows):
        return jnp.sum(jnp.where(eye[None], rows[:, None, :], 0.0), axis=2, keepdims=True)

    for hh in range(GDN_HEADS):
        hs = slice(hh * d, (hh + 1) * d)
        q = qkv_ref[:, hs]
        k = qkv_ref[:, GDN_WIDTH + hh * d:GDN_WIDTH + (hh + 1) * d]
        v = qkv_ref[:, 2 * GDN_WIDTH + hh * d:2 * GDN_WIDTH + (hh + 1) * d]
        eg = jnp.exp(sm_ref[:, SM_G + hh:SM_G + hh + 1])
        beta = sm_ref[:, SM_BETA + hh:SM_BETA + hh + 1]
        s = s_ref[:, hh]
        k_col = col(k)
        v_new = beta * (v - eg * jnp.sum(k_col * s, axis=1))
        o = eg * jnp.sum(col(q) * s, axis=1) + jnp.sum(q * k, axis=1, keepdims=True) * v_new
        sout_ref[:, hh] = s * eg[:, :, None] + k_col * v_new[:, None, :]
        o_ref[:, hs] = _rms(o, gn_ref[...]) * _silu(z_ref[:, hs])


def _gdn_sample(state, gqkv, sm, gz, gn):
    nb = state.shape[0]
    bb = 2 * SUBLANES if nb % (2 * SUBLANES) == 0 else SUBLANES
    d = GDN_HEAD_DIM
    row_spec = lambda w: pl.BlockSpec((bb, w), lambda i: (i, 0))
    st_spec = pl.BlockSpec((bb, GDN_HEADS, d, d), lambda i: (i, 0, 0, 0))
    return pl.pallas_call(
        _gdn_sample_body,
        grid=(nb // bb,),
        in_specs=[st_spec, row_spec(3 * GDN_WIDTH), row_spec(LANES), row_spec(GDN_WIDTH),
                  pl.BlockSpec((1, d), lambda i: (0, 0))],
        out_specs=(st_spec, row_spec(GDN_WIDTH)),
        out_shape=(jax.ShapeDtypeStruct(state.shape, F32), jax.ShapeDtypeStruct((nb, GDN_WIDTH), F32)),
        compiler_params=pltpu.CompilerParams(dimension_semantics=("arbitrary",), vmem_limit_bytes=32 * MIB),
        name="gdn_sample",
    )(state, gqkv, sm, gz, gn)


def _pad_rows(a, rows):
    return jnp.concatenate([a, jnp.zeros((rows - a.shape[0],) + a.shape[1:], a.dtype)], axis=0)


def kernel(x_prompt, x_sample, cache_k, cache_v, cache_logf, state_gdn, state_gdn_conv, state_ffn_conv,
           page_table, norm_mix, w_in, b_forget, gdn_a_log, gdn_dt_bias, w_gdn_conv, gdn_out_norm, w_out,
           norm_ffn, w_up, w_ffn_conv, b_ffn_conv, w_down, norm_final):
    assert w_in.shape[0] == 1, "single-layer trunk"
    n_seq, seq_len, _ = x_prompt.shape
    nb = x_sample.shape[0]
    n_pool, page = cache_k.shape[1], cache_k.shape[2]

    w = w_in[0]
    o_ff = 3 * FOX_WIDTH
    o_g = o_ff + FOX_HEADS
    o_ga = o_g + 3 * GDN_WIDTH
    o_gz = o_ga + 2 * GDN_HEADS
    wfox, wgdn, wgz = w[:, :o_ff].astype(BF16), w[:, o_g:o_ga].astype(BF16), w[:, o_gz:].astype(BF16)
    wsm = jnp.concatenate([w[:, o_ff:o_g], w[:, o_ga:o_gz],
                           jnp.zeros((D_MODEL, LANES - FOX_HEADS - 2 * GDN_HEADS), F32)], axis=1).astype(BF16)
    par = jnp.zeros((SUBLANES, LANES), F32)
    par = par.at[0, SM_LF:SM_LF + FOX_HEADS].set(b_forget[0])
    par = par.at[0, SM_G:SM_G + GDN_HEADS].set(gdn_dt_bias[0])
    par = par.at[1, SM_G:SM_G + GDN_HEADS].set(gdn_a_log[0])
    nrm = norm_mix[0][None, :]
    cw = _pad_rows(w_gdn_conv[0], SUBLANES)
    gn = gdn_out_norm[0][None, :]
    wo = w_out[0].astype(BF16)
    nf = norm_ffn[0][None, :]
    wup = w_up[0].astype(BF16)
    cwf = _pad_rows(w_ffn_conv[0], SUBLANES)
    cbf = b_ffn_conv[0][None, :]
    wd = w_down[0].astype(BF16)
    nfin = norm_final[None, :]

    xs = x_sample.reshape(nb, D_MODEL)
    gctx = state_gdn_conv[0]
    fctx = state_ffn_conv[0]
    fk_s, fv_s, q_s, gqkv_s, gz_s, sm_s, pre_s = _proj_sample(
        xs, nrm, wfox, wgdn, wgz, wsm, par, cw, gctx[:, 0], gctx[:, 1], gctx[:, 2])
    heads = lambda a: a.reshape(nb, FOX_HEADS, FOX_HEAD_DIM)
    half = nb // 2
    decode_args = (page_table, heads(q_s), heads(fk_s), heads(fv_s), sm_s,
                   jnp.transpose(cache_k[0], (0, 2, 3, 1)), jnp.transpose(cache_v[0], (0, 2, 3, 1)),
                   jnp.transpose(cache_logf[0], (0, 2, 1)))
    rider_a = functools.partial(_rider, *decode_args, 0, half)
    rider_b = functools.partial(_rider, *decode_args, half, nb - half)

    xp = x_prompt.reshape(n_seq * seq_len, D_MODEL)
    kt, vt, qt16, kaug, vt16, gqkv, gz, sm, smt, cst, fox_sa = _proj_prompt(
        xp, nrm, wfox, wgdn, wgz, wsm, par, cw, seq_len, rider_a)
    fox = _fox_prompt(qt16, kaug, vt16, n_seq, seq_len)
    gdn, s_p = _gdn_prompt(gqkv, sm, smt, gz, gn, n_seq, seq_len)
    yp, cstf, fox_sb = _ffn_prompt(xp, fox, gdn, wo, nf, wup, cwf, cbf, wd, nfin, seq_len, rider_b)

    fox_s = jnp.concatenate([fox_sa, fox_sb], axis=0)
    s_s, gdn_s = _gdn_sample(state_gdn[0], gqkv_s, sm_s, gz_s, gn)
    ys, gate_s = _ffn_sample(xs, fox_s.reshape(nb, FOX_WIDTH).astype(BF16), gdn_s.astype(BF16), wo, nf, wup,
                             cwf, cbf, wd, nfin, fctx[:, 0], fctx[:, 1])

    kv_shape_s = (1, nb, 1, FOX_HEADS, FOX_HEAD_DIM)
    new_kv_p = lambda a: jnp.transpose(a.reshape(1, n_seq, FOX_HEADS, FOX_HEAD_DIM, seq_len), (0, 1, 4, 2, 3))
    return (
        yp.reshape(n_seq, seq_len, D_MODEL),
        ys.reshape(nb, 1, D_MODEL),
        new_kv_p(kt),
        new_kv_p(vt),
        sm[:, SM_LF:SM_LF + FOX_HEADS].reshape(1, n_seq, seq_len, FOX_HEADS),
        s_p[None],
        cst.reshape(n_seq, SUBLANES, 3 * GDN_WIDTH)[None, :, SUBLANES - (GDN_CONV - 1):],
        cstf.reshape(n_seq, SUBLANES, FFN_DIM)[None, :, SUBLANES - (FFN_CONV - 1):],
        fk_s.reshape(kv_shape_s),
        fv_s.reshape(kv_shape_s),
        sm_s[:, SM_LF:SM_LF + FOX_HEADS].reshape(1, nb, 1, FOX_HEADS),
        s_s[None],
        jnp.concatenate([gctx[:, 1:], pre_s[:, None, :]], axis=1)[None],
        jnp.concatenate([fctx[:, 1:], gate_s[:, None, :]], axis=1)[None],
    )
```

```python
import functools

import jax
import jax.numpy as jnp
from jax import lax
from jax.experimental import pallas as pl
from jax.experimental.pallas import tpu as pltpu

D_MODEL = 1024
FOX_HEADS = 8
FOX_HEAD_DIM = 64
FOX_WIDTH = FOX_HEADS * FOX_HEAD_DIM
GDN_HEADS = 4
GDN_HEAD_DIM = 128
GDN_WIDTH = GDN_HEADS * GDN_HEAD_DIM
GDN_CONV = 4
GDN_CHUNK = 64
FFN_DIM = 2816
FFN_CONV = 3
EPS = 1e-6
NEG_BIG = -1e30
FOX_SCALE = FOX_HEAD_DIM ** -0.5
GDN_SCALE = GDN_HEAD_DIM ** -0.5
LOG2E = 1.4426950408889634
FOX_QSCALE = FOX_SCALE * LOG2E
FOX_AUG = 3
FOX_DEN_ROWS = 16
FOX_GROUP = 2

LANES = 128
SUBLANES = 8
ROW_TILE = 256
FOX_TILE = 256
MIB = 1024 * 1024

SM_LF = 0
SM_G = 8
SM_BETA = 12
SM_CUM = 16
SM_GCUM = 24

F32 = jnp.float32
BF16 = jnp.bfloat16


def _sigmoid(x):
    return 1.0 / (1.0 + jnp.exp(-x))


def _silu(x):
    return x * _sigmoid(x)


def _rms(x, g):
    return x * lax.rsqrt(jnp.mean(x * x, axis=-1, keepdims=True) + EPS) * g


def _dot(a, b):
    return jnp.dot(a, b, preferred_element_type=F32)


def _dot_nt(a, b):
    return lax.dot_general(a, b, (((1,), (1,)), ((), ())), preferred_element_type=F32)


def _shift_rows(x, prev8, k):
    r = pltpu.roll(x, k, 0)
    row8 = lax.broadcasted_iota(jnp.int32, prev8.shape, 0)
    top = jnp.where(row8 < k, pltpu.roll(prev8, k, 0), r[0:SUBLANES])
    return jnp.concatenate([top, r[SUBLANES:]], axis=0)


def _gate_activations(raw, par_ref):
    z = raw + par_ref[0:1, :]
    lane = lax.broadcasted_iota(jnp.int32, z.shape, 1)
    t = jnp.log1p(jnp.exp(-jnp.abs(z)))
    lf = jnp.minimum(z, 0.0) - t
    softplus = jnp.maximum(z, 0.0) + t
    g = -jnp.exp(par_ref[1:2, :]) * softplus
    beta = _sigmoid(z)
    return jnp.where(lane < SM_G, lf, jnp.where(lane < SM_BETA, g, jnp.where(lane < SM_CUM, beta, 0.0)))


def _gdn_qkv_norm(c, out_ref):
    for part, scale in ((0, GDN_SCALE), (1, 1.0)):
        for hh in range(GDN_HEADS):
            off = part * GDN_WIDTH + hh * GDN_HEAD_DIM
            seg = c[:, off:off + GDN_HEAD_DIM]
            n = lax.rsqrt(jnp.sum(seg * seg, axis=-1, keepdims=True) + EPS)
            out_ref[:, off:off + GDN_HEAD_DIM] = seg * n * scale
    out_ref[:, 2 * GDN_WIDTH:] = c[:, 2 * GDN_WIDTH:]


def _proj_prompt_body(pt_ref, *refs, tiles_per_seq, rider_cfg):
    (x_ref, nrm_ref, wfox_ref, wgdn_ref, wgz_ref, wsm_ref, par_ref, cw_ref, scat_ref) = refs[:9]
    rider_in = refs[9:9 + RIDER_INPUTS]
    (kt_ref, vt_ref, qt16_ref, kaug_ref, vt16_ref, gqkv_ref, gz_ref, sm_ref, smt_ref, cst_ref,
     rider_out, carry_ref, prev_ref) = refs[9 + RIDER_INPUTS:22 + RIDER_INPUTS]
    rider_scratch = refs[22 + RIDER_INPUTS:]
    i = pl.program_id(0)

    @pl.when(i % tiles_per_seq == 0)
    def _():
        carry_ref[...] = jnp.zeros_like(carry_ref)
        prev_ref[...] = jnp.zeros_like(prev_ref)

    _rider_step(pt_ref, rider_in, rider_out, rider_scratch, rider_cfg)

    tm = x_ref.shape[0]
    h16 = _rms(x_ref[...], nrm_ref[...]).astype(BF16)

    fox = _dot_nt(h16, wfox_ref[...])
    fq, fk, fv = fox[:, :FOX_WIDTH], fox[:, FOX_WIDTH:2 * FOX_WIDTH], fox[:, 2 * FOX_WIDTH:]
    fv_t = fv.T
    kt_ref[0] = fk.T
    vt_ref[0] = fv_t
    vt16_ref[0] = fv_t.astype(BF16)
    lane_p = lax.broadcasted_iota(jnp.int32, (tm, LANES), 1)
    ones_blk = jnp.where(lane_p < 2 * FOX_AUG, 1.0, 0.0)
    q_aug = []
    for p in range(FOX_HEADS // 2):
        q_aug += [fq[:, p * LANES:(p + 1) * LANES] * FOX_QSCALE, ones_blk]
        kaug_ref[:, 2 * p * LANES:(2 * p + 1) * LANES] = fk[:, p * LANES:(p + 1) * LANES].astype(BF16)
    qt16_ref[0] = jnp.concatenate(q_aug, axis=1).T.astype(BF16)

    act = _gate_activations(_dot_nt(h16, wsm_ref[...]), par_ref)
    row = lax.broadcasted_iota(jnp.int32, act.shape, 0)
    lane = lax.broadcasted_iota(jnp.int32, act.shape, 1)
    y = act
    yc = act
    k = 1
    while k < tm:
        y = y + jnp.where(row >= k, pltpu.roll(y, k, 0), 0.0)
        if k < GDN_CHUNK:
            yc = yc + jnp.where((row & (GDN_CHUNK - 1)) >= k, pltpu.roll(yc, k, 0), 0.0)
        k *= 2
    y = y + carry_ref[0:1, :]
    carry_ref[0:1, :] = y[tm - 1:tm, :]
    terms = jnp.concatenate(_split3(y * (-LOG2E)), axis=1)
    extra = _dot(terms, scat_ref[...]).astype(BF16)
    for p in range(FOX_HEADS // 2):
        kaug_ref[:, (2 * p + 1) * LANES:(2 * p + 2) * LANES] = extra[:, p * LANES:(p + 1) * LANES]
    shift = SM_CUM - SM_LF
    sm = jnp.where(lane < SM_CUM, act,
                   jnp.where(lane < SM_GCUM, pltpu.roll(y, shift, 1),
                             jnp.where(lane < SM_GCUM + GDN_HEADS, pltpu.roll(yc, shift, 1), 0.0)))
    sm_ref[...] = sm
    smt_ref[0] = sm.T

    pre = _dot_nt(h16, wgdn_ref[...])
    assert GDN_CONV == 4
    s1 = _shift_rows(pre, prev_ref[0:SUBLANES, :], 1)
    far = pre * cw_ref[1:2, :] + s1 * cw_ref[0:1, :]
    acc = pre * cw_ref[3:4, :] + s1 * cw_ref[2:3, :] + _shift_rows(far, prev_ref[SUBLANES:, :], 2)
    prev_ref[0:SUBLANES, :] = pre[tm - SUBLANES:, :]
    prev_ref[SUBLANES:, :] = far[tm - SUBLANES:, :]
    cst_ref[...] = pre[tm - SUBLANES:, :]
    _gdn_qkv_norm(_silu(acc), gqkv_ref)

    gz_ref[...] = _dot_nt(h16, wgz_ref[...])


def _proj_prompt(x, nrm, wfox, wgdn, wgz, wsm, par, cw, seq_len, make_rider):
    rows = x.shape[0]
    tm = ROW_TILE
    n_seq = rows // seq_len
    tiles_per_seq = seq_len // tm
    row_spec = lambda w: pl.BlockSpec((tm, w), lambda i, pt: (i, 0))
    col_spec = lambda w: pl.BlockSpec((1, w, tm), lambda i, pt: (i // tiles_per_seq, 0, i % tiles_per_seq))
    const = lambda s: pl.BlockSpec(s, lambda i, pt: (0, 0), pipeline_mode=pl.Buffered(1))
    rider = make_rider(rows // tm)
    aug_width = FOX_HEADS * LANES
    src = jnp.arange(3 * LANES)
    term, head = src // LANES, src % LANES
    dst = (head // 2) * LANES + (head % 2) * FOX_AUG + term
    scat = ((jnp.arange(FOX_WIDTH)[None, :] == dst[:, None]) & (head[:, None] < FOX_HEADS)).astype(BF16)
    out_shape = (
        jax.ShapeDtypeStruct((n_seq, FOX_WIDTH, seq_len), F32),
        jax.ShapeDtypeStruct((n_seq, FOX_WIDTH, seq_len), F32),
        jax.ShapeDtypeStruct((n_seq, aug_width, seq_len), BF16),
        jax.ShapeDtypeStruct((rows, aug_width), BF16),
        jax.ShapeDtypeStruct((n_seq, FOX_WIDTH, seq_len), BF16),
        jax.ShapeDtypeStruct((rows, 3 * GDN_WIDTH), F32),
        jax.ShapeDtypeStruct((rows, GDN_WIDTH), F32),
        jax.ShapeDtypeStruct((rows, LANES), F32),
        jax.ShapeDtypeStruct((n_seq, LANES, seq_len), F32),
        jax.ShapeDtypeStruct((n_seq * SUBLANES, 3 * GDN_WIDTH), F32),
    )
    out_specs = (
        col_spec(FOX_WIDTH), col_spec(FOX_WIDTH), col_spec(aug_width), row_spec(aug_width),
        col_spec(FOX_WIDTH), row_spec(3 * GDN_WIDTH), row_spec(GDN_WIDTH), row_spec(LANES),
        col_spec(LANES),
        pl.BlockSpec((SUBLANES, 3 * GDN_WIDTH), lambda i, pt: (i // tiles_per_seq, 0)),
    )
    grid_spec = pltpu.PrefetchScalarGridSpec(
        num_scalar_prefetch=1,
        grid=(rows // tm,),
        in_specs=[row_spec(D_MODEL), const((1, D_MODEL)), const(wfox.shape), const(wgdn.shape), const(wgz.shape),
                  const(wsm.shape), const((SUBLANES, LANES)), const((SUBLANES, 3 * GDN_WIDTH)),
                  const(scat.shape)] + rider["in_specs"],
        out_specs=out_specs + (rider["out_spec"],),
        scratch_shapes=[pltpu.VMEM((SUBLANES, LANES), F32), pltpu.VMEM((2 * SUBLANES, 3 * GDN_WIDTH), F32)]
        + rider["scratch"],
    )
    return pl.pallas_call(
        functools.partial(_proj_prompt_body, tiles_per_seq=tiles_per_seq, rider_cfg=rider["cfg"]),
        grid_spec=grid_spec,
        out_shape=out_shape + (rider["out_shape"],),
        compiler_params=pltpu.CompilerParams(dimension_semantics=("arbitrary",), vmem_limit_bytes=56 * MIB),
        name="proj_prompt",
    )(rider["prefetch"], x, nrm, wfox, wgdn, wgz, wsm, par, cw, scat, *rider["operands"])


def _proj_sample_body(x_ref, nrm_ref, wfox_ref, wgdn_ref, wgz_ref, wsm_ref, par_ref, cw_ref, c0_ref, c1_ref, c2_ref,
                      fk_ref, fv_ref, q_ref, gqkv_ref, gz_ref, sm_ref, pre_ref):
    h16 = _rms(x_ref[...], nrm_ref[...]).astype(BF16)
    fox = _dot_nt(h16, wfox_ref[...])
    q_ref[...] = fox[:, :FOX_WIDTH] * FOX_SCALE
    fk_ref[...] = fox[:, FOX_WIDTH:2 * FOX_WIDTH]
    fv_ref[...] = fox[:, 2 * FOX_WIDTH:]
    sm_ref[...] = _gate_activations(_dot_nt(h16, wsm_ref[...]), par_ref)
    pre = _dot_nt(h16, wgdn_ref[...])
    pre_ref[...] = pre
    acc = (pre * cw_ref[3:4, :] + c2_ref[...] * cw_ref[2:3, :]
           + c1_ref[...] * cw_ref[1:2, :] + c0_ref[...] * cw_ref[0:1, :])
    _gdn_qkv_norm(_silu(acc), gqkv_ref)
    gz_ref[...] = _dot_nt(h16, wgz_ref[...])


def _proj_sample(x, nrm, wfox, wgdn, wgz, wsm, par, cw, c0, c1, c2):
    rows = x.shape[0]
    full = lambda a: pl.BlockSpec(a.shape, lambda i: (0,) * a.ndim)
    args = (x, nrm, wfox, wgdn, wgz, wsm, par, cw, c0, c1, c2)
    shapes = ((rows, FOX_WIDTH), (rows, FOX_WIDTH), (rows, FOX_WIDTH), (rows, 3 * GDN_WIDTH),
              (rows, GDN_WIDTH), (rows, LANES), (rows, 3 * GDN_WIDTH))
    return pl.pallas_call(
        _proj_sample_body,
        grid=(1,),
        in_specs=[full(a) for a in args],
        out_specs=tuple(pl.BlockSpec(s, lambda i: (0, 0)) for s in shapes),
        out_shape=tuple(jax.ShapeDtypeStruct(s, F32) for s in shapes),
        compiler_params=pltpu.CompilerParams(dimension_semantics=("arbitrary",), vmem_limit_bytes=48 * MIB),
        name="proj_sample",
    )(*args)


def _fox_prompt_body(qt_ref, kaug_ref, vt_ref, o_ref):
    i = pl.program_id(1)
    t = FOX_TILE
    hd = FOX_HEAD_DIM
    aug = 2 * LANES
    rr = lax.broadcasted_iota(jnp.int32, (t, t), 0)
    cc = lax.broadcasted_iota(jnp.int32, (t, t), 1)
    causal = rr <= cc
    row = lax.broadcasted_iota(jnp.int32, (aug, t), 0)
    chains = [(g, h) for g in range(qt_ref.shape[0]) for h in range(FOX_HEADS)]
    qs = []
    for g, h in chains:
        e = h % 2
        qp = qt_ref[g, (h // 2) * aug:(h // 2 + 1) * aug, :]
        own = jnp.logical_or(jnp.logical_and(row >= e * hd, row < (e + 1) * hd),
                             jnp.logical_and(row >= LANES + e * FOX_AUG, row < LANES + (e + 1) * FOX_AUG))
        qs.append(jnp.where(own, qp, jnp.zeros_like(qp)))

    def tile(j, carry, masked):
        koff = pl.multiple_of(j * t, t)
        ss = [_dot(kaug_ref[g, pl.ds(koff, t), (h // 2) * aug:(h // 2 + 1) * aug], qs[n])
              for n, (g, h) in enumerate(chains)]
        stats, pms = [], []
        for n in range(len(chains)):
            m = carry[n][0]
            s = jnp.where(causal, ss[n], NEG_BIG) if masked else ss[n]
            m_new = jnp.maximum(m, jnp.max(s, axis=0, keepdims=True))
            stats.append((m_new, jnp.exp2(m - m_new)))
            pms.append(jnp.exp2(s - m_new).astype(BF16))
        out = []
        for n, (g, h) in enumerate(chains):
            m_new, alpha = stats[n]
            v_ones = jnp.concatenate([vt_ref[g, h * hd:(h + 1) * hd, pl.ds(koff, t)], ones_rows], axis=0)
            out.append((m_new, alpha * carry[n][1] + _dot(v_ones, pms[n])))
        return tuple(out)

    ones_rows = jnp.ones((FOX_DEN_ROWS, t), BF16)
    init = tuple((jnp.full((1, t), NEG_BIG, F32), jnp.zeros((hd + FOX_DEN_ROWS, t), F32)) for _ in chains)
    carry = lax.fori_loop(0, i, lambda j, c: tile(j, c, False), init)
    final = tile(i, carry, True)
    for n in range(0, len(chains), 2):
        g, h = chains[n]
        outs = [a[:hd] / a[hd:hd + 1] for _, a in final[n:n + 2]]
        o_ref[g, :, (h // 2) * LANES:(h // 2 + 1) * LANES] = jnp.concatenate(outs, axis=0).T.astype(BF16)


def _fox_prompt(qt16, kaug, vt16, n_seq, seq_len):
    t = FOX_TILE
    nq = seq_len // t
    aug_width = kaug.shape[1]
    group = FOX_GROUP if n_seq % FOX_GROUP == 0 else 1
    out = pl.pallas_call(
        _fox_prompt_body,
        grid=(n_seq // group, nq),
        in_specs=[pl.BlockSpec((group, aug_width, t), lambda b, i: (b, 0, i)),
                  pl.BlockSpec((group, seq_len, aug_width), lambda b, i: (b, 0, 0)),
                  pl.BlockSpec((group, FOX_WIDTH, seq_len), lambda b, i: (b, 0, 0))],
        out_specs=pl.BlockSpec((group, t, FOX_WIDTH), lambda b, i: (b, i, 0)),
        out_shape=jax.ShapeDtypeStruct((n_seq, seq_len, FOX_WIDTH), BF16),
        compiler_params=pltpu.CompilerParams(dimension_semantics=("arbitrary", "arbitrary"),
                                             vmem_limit_bytes=48 * MIB),
        name="fox_prompt",
    )(qt16, kaug.reshape(n_seq, seq_len, aug_width), vt16)
    return out.reshape(n_seq * seq_len, FOX_WIDTH)


def _bdot(a, b):
    return lax.dot_general(a.astype(BF16), b.astype(BF16), (((2,), (1,)), ((0,), (0,))),
                           preferred_element_type=F32)


def _bdot_nt(a, b):
    return lax.dot_general(a.astype(BF16), b.astype(BF16), (((2,), (2,)), ((0,), (0,))),
                           preferred_element_type=F32)


def _unit_lower_inverse_minus_eye(a, r, c):
    blk16 = (r // 16) == (c // 16)
    blk32 = (r // 32) == (c // 32)
    p = jnp.where(blk16, -a, 0.0)
    dt = p
    for _ in range(3):
        p = _bdot(p, p)
        dt = dt + p + _bdot(dt, p)
    for off in (jnp.where(jnp.logical_and(blk32, jnp.logical_not(blk16)), a, 0.0),
                jnp.where(blk32, 0.0, a)):
        x = off + _bdot(dt, off)
        dt = dt - (x + _bdot(x, dt))
    return dt


def _gdn_prompt_body(qkv_ref, sm_ref, gcr_ref, z_ref, gn_ref, o_ref, sout_ref, s_ref):
    ci = pl.program_id(1)

    @pl.when(ci == 0)
    def _():
        s_ref[...] = jnp.zeros_like(s_ref)

    ch = GDN_CHUNK
    d = GDN_HEAD_DIM
    nh = GDN_HEADS
    n_seq = qkv_ref.shape[0]
    n_chunks = qkv_ref.shape[1] // ch
    units = [(cidx, g, hh) for cidx in range(n_chunks) for g in range(n_seq) for hh in range(nh)]
    per_chunk = n_seq * nh

    def gather(fn):
        return jnp.stack([fn(g, slice(cidx * ch, (cidx + 1) * ch), hh) for cidx, g, hh in units], axis=0)

    q = gather(lambda g, rows, hh: qkv_ref[g, rows, hh * d:(hh + 1) * d])
    k = gather(lambda g, rows, hh: qkv_ref[g, rows, GDN_WIDTH + hh * d:GDN_WIDTH + (hh + 1) * d])
    v = gather(lambda g, rows, hh: qkv_ref[g, rows, 2 * GDN_WIDTH + hh * d:2 * GDN_WIDTH + (hh + 1) * d])
    beta = gather(lambda g, rows, hh: sm_ref[g, rows, SM_BETA + hh:SM_BETA + hh + 1])
    gc = gather(lambda g, rows, hh: sm_ref[g, rows, SM_GCUM + hh:SM_GCUM + hh + 1])
    gr = gather(lambda g, rows, hh: gcr_ref[g, hh:hh + 1, rows])

    r = lax.broadcasted_iota(jnp.int32, (1, ch, ch), 1)
    c = lax.broadcasted_iota(jnp.int32, (1, ch, ch), 2)
    lower = r >= c
    beta = jnp.broadcast_to(beta, (len(units), ch, d))
    gc = jnp.broadcast_to(gc, (len(units), ch, d))
    decay = jnp.where(lower, jnp.exp(jnp.where(lower, gc[:, :, :ch] - gr, 0.0)), 0.0)
    qk_kk = _bdot_nt(jnp.concatenate([q, k], axis=1), k)
    qk = qk_kk[:, :ch] * decay
    a = jnp.where(r > c, beta[:, :, :ch] * qk_kk[:, ch:] * decay, 0.0)
    dt = _unit_lower_inverse_minus_eye(a, r, c)
    eg = jnp.exp(gc)
    rhs = jnp.concatenate([v * beta, k * (beta * eg)], axis=-1)
    uw = rhs + _bdot(dt, rhs)
    gl = gc[:, ch - 1:ch, :]
    qd = (q * eg).astype(BF16)
    kd = k * jnp.exp(gl - gc)
    g_last = jnp.exp(gl)
    qk16 = qk.astype(BF16)

    s = s_ref[...]
    for cidx in range(n_chunks):
        us = slice(cidx * per_chunk, (cidx + 1) * per_chunk)
        rows = slice(cidx * ch, (cidx + 1) * ch)
        s16 = s.astype(BF16)
        v_new = uw[us, :, :d] - _bdot(uw[us, :, d:], s16)
        o = _bdot(qd[us], s16) + _bdot(qk16[us], v_new)
        vn16 = v_new.astype(BF16)
        upd = jnp.stack([_dot(kd[cidx * per_chunk + j].T.astype(BF16), vn16[j]) for j in range(per_chunk)],
                        axis=0)
        s = s * g_last[us] + upd
        for g in range(n_seq):
            for hh in range(nh):
                hs = slice(hh * d, (hh + 1) * d)
                o_ref[g, rows, hs] = (_rms(o[g * nh + hh], gn_ref[...]) * _silu(z_ref[g, rows, hs])).astype(BF16)
    s_ref[...] = s

    @pl.when(ci == pl.num_programs(1) - 1)
    def _():
        sout_ref[...] = s.reshape(sout_ref.shape)


def _gdn_prompt(gqkv, sm, smt, gz, gn, n_seq, seq_len):
    tm = ROW_TILE
    nt = seq_len // tm
    gcum_block = SM_GCUM // SUBLANES
    group = 2 if n_seq % 2 == 0 else 1
    seq3 = lambda a: a.reshape(n_seq, seq_len, a.shape[-1])
    tile = lambda w: pl.BlockSpec((group, tm, w), lambda b, i: (b, i, 0))
    out, s_out = pl.pallas_call(
        _gdn_prompt_body,
        grid=(n_seq // group, nt),
        in_specs=[tile(3 * GDN_WIDTH), tile(LANES),
                  pl.BlockSpec((group, SUBLANES, tm), lambda b, i: (b, gcum_block, i)),
                  tile(GDN_WIDTH),
                  pl.BlockSpec((1, GDN_HEAD_DIM), lambda b, i: (0, 0))],
        out_specs=(tile(GDN_WIDTH),
                   pl.BlockSpec((group, GDN_HEADS, GDN_HEAD_DIM, GDN_HEAD_DIM), lambda b, i: (b, 0, 0, 0))),
        out_shape=(jax.ShapeDtypeStruct((n_seq, seq_len, GDN_WIDTH), BF16),
                   jax.ShapeDtypeStruct((n_seq, GDN_HEADS, GDN_HEAD_DIM, GDN_HEAD_DIM), F32)),
        scratch_shapes=[pltpu.VMEM((group * GDN_HEADS, GDN_HEAD_DIM, GDN_HEAD_DIM), F32)],
        compiler_params=pltpu.CompilerParams(dimension_semantics=("arbitrary", "arbitrary"),
                                             vmem_limit_bytes=40 * MIB),
        name="gdn_prompt",
    )(seq3(gqkv), seq3(sm), smt, seq3(gz), gn)
    return out.reshape(n_seq * seq_len, GDN_WIDTH), s_out


def _ffn_tail(x, fox_ref, gdn_ref, wo_ref, nf_ref, wup_ref, wd_ref, nfin_ref, conv):
    mix = _dot(fox_ref[...], wo_ref[0:FOX_WIDTH, :]) + _dot(gdn_ref[...], wo_ref[FOX_WIDTH:, :])
    x2 = x + mix
    h2 = _rms(x2, nf_ref[...]).astype(BF16)
    gu = _dot(h2, wup_ref[...])
    gate, up = gu[:, :FFN_DIM], gu[:, FFN_DIM:]
    act = (_silu(conv(gate)) * up).astype(BF16)
    x3 = x2 + _dot(act, wd_ref[...])
    return _rms(x3, nfin_ref[...])


def _ffn_prompt_body(pt_ref, *refs, tiles_per_seq, rider_cfg):
    (x_ref, fox_ref, gdn_ref, wo_ref, nf_ref, wup_ref, cw_ref, cb_ref, wd_ref, nfin_ref) = refs[:10]
    rider_in = refs[10:10 + RIDER_INPUTS]
    y_ref, cst_ref, rider_out, prev_ref = refs[10 + RIDER_INPUTS:14 + RIDER_INPUTS]
    rider_scratch = refs[14 + RIDER_INPUTS:]
    i = pl.program_id(0)

    @pl.when(i % tiles_per_seq == 0)
    def _():
        prev_ref[...] = jnp.zeros_like(prev_ref)

    _rider_step(pt_ref, rider_in, rider_out, rider_scratch, rider_cfg)

    tm = x_ref.shape[0]

    def conv(gate):
        prev = prev_ref[...]
        out = gate * cw_ref[FFN_CONV - 1:FFN_CONV, :] + cb_ref[...]
        for kk in range(1, FFN_CONV):
            out = out + _shift_rows(gate, prev, kk) * cw_ref[FFN_CONV - 1 - kk:FFN_CONV - kk, :]
        prev_ref[...] = gate[tm - SUBLANES:, :]
        cst_ref[...] = gate[tm - SUBLANES:, :]
        return out

    y_ref[...] = _ffn_tail(x_ref[...], fox_ref, gdn_ref, wo_ref, nf_ref, wup_ref, wd_ref, nfin_ref, conv)


def _ffn_prompt(x, fox, gdn, wo, nf, wup, cw, cb, wd, nfin, seq_len, make_rider):
    rows = x.shape[0]
    tm = ROW_TILE
    n_seq = rows // seq_len
    tiles_per_seq = seq_len // tm
    row_spec = lambda w: pl.BlockSpec((tm, w), lambda i, pt: (i, 0))
    const = lambda a: pl.BlockSpec(a.shape, lambda i, pt: (0, 0), pipeline_mode=pl.Buffered(1))
    rider = make_rider(rows // tm)
    grid_spec = pltpu.PrefetchScalarGridSpec(
        num_scalar_prefetch=1,
        grid=(rows // tm,),
        in_specs=[row_spec(D_MODEL), row_spec(FOX_WIDTH), row_spec(GDN_WIDTH), const(wo), const(nf),
                  const(wup), const(cw), const(cb), const(wd), const(nfin)] + rider["in_specs"],
        out_specs=(row_spec(D_MODEL),
                   pl.BlockSpec((SUBLANES, FFN_DIM), lambda i, pt: (i // tiles_per_seq, 0)),
                   rider["out_spec"]),
        scratch_shapes=[pltpu.VMEM((SUBLANES, FFN_DIM), F32)] + rider["scratch"],
    )
    return pl.pallas_call(
        functools.partial(_ffn_prompt_body, tiles_per_seq=tiles_per_seq, rider_cfg=rider["cfg"]),
        grid_spec=grid_spec,
        out_shape=(jax.ShapeDtypeStruct((rows, D_MODEL), F32),
                   jax.ShapeDtypeStruct((n_seq * SUBLANES, FFN_DIM), F32),
                   rider["out_shape"]),
        compiler_params=pltpu.CompilerParams(dimension_semantics=("arbitrary",), vmem_limit_bytes=58 * MIB),
        name="ffn_prompt",
    )(rider["prefetch"], x, fox, gdn, wo, nf, wup, cw, cb, wd, nfin, *rider["operands"])


def _ffn_sample_body(x_ref, fox_ref, gdn_ref, wo_ref, nf_ref, wup_ref, cw_ref, cb_ref, wd_ref, nfin_ref,
                     c0_ref, c1_ref, y_ref, gate_ref):
    def conv(gate):
        gate_ref[...] = gate
        return (gate * cw_ref[2:3, :] + c1_ref[...] * cw_ref[1:2, :] + c0_ref[...] * cw_ref[0:1, :]
                + cb_ref[...])

    y_ref[...] = _ffn_tail(x_ref[...], fox_ref, gdn_ref, wo_ref, nf_ref, wup_ref, wd_ref, nfin_ref, conv)


def _ffn_sample(x, fox, gdn, wo, nf, wup, cw, cb, wd, nfin, c0, c1):
    rows = x.shape[0]
    args = (x, fox, gdn, wo, nf, wup, cw, cb, wd, nfin, c0, c1)
    full = lambda a: pl.BlockSpec(a.shape, lambda i: (0, 0), pipeline_mode=pl.Buffered(1))
    return pl.pallas_call(
        _ffn_sample_body,
        grid=(1,),
        in_specs=[full(a) for a in args],
        out_specs=(pl.BlockSpec((rows, D_MODEL), lambda i: (0, 0)),
                   pl.BlockSpec((rows, FFN_DIM), lambda i: (0, 0))),
        out_shape=(jax.ShapeDtypeStruct((rows, D_MODEL), F32), jax.ShapeDtypeStruct((rows, FFN_DIM), F32)),
        compiler_params=pltpu.CompilerParams(dimension_semantics=("arbitrary",), vmem_limit_bytes=56 * MIB),
        name="ffn_sample",
    )(*args)


def _split3(x):
    hi = x.astype(BF16)
    r1 = x - hi.astype(F32)
    mid = r1.astype(BF16)
    lo = (r1 - mid.astype(F32)).astype(BF16)
    return hi, mid, lo


RIDER_INPUTS = 8
RIDER_SCRATCH = 4


def _rider_step(pt_ref, rider_in, o_ref, rider_scratch, cfg):
    q_ref, kn_ref, vn_ref, sm_ref, msuf_ref, kt_pool, vt_pool, lf_pool = rider_in
    kbuf, vbuf, lbuf, sems = rider_scratch
    base, count, per_step, n_pages = cfg
    i = pl.program_id(0)

    def copies(g, sl):
        out = []
        for j in range(n_pages):
            pg = pt_ref[g * n_pages + j]
            out.append(pltpu.make_async_copy(kt_pool.at[pg], kbuf.at[sl, j], sems.at[sl, 0]))
            out.append(pltpu.make_async_copy(vt_pool.at[pg], vbuf.at[sl, j], sems.at[sl, 1]))
            out.append(pltpu.make_async_copy(lf_pool.at[pg], lbuf.at[sl, j], sems.at[sl, 2]))
        return out

    @pl.when(i == 0)
    def _():
        for cp in copies(base, base % 2):
            cp.start()

    for r in range(per_step):
        g = base + i * per_step + r
        slot = g % 2

        @pl.when(g + 1 < base + count)
        def _():
            for cp in copies(g + 1, 1 - slot):
                cp.start()

        for cp in copies(g, slot):
            cp.wait()
        o_ref[r] = _decode_row(slot, q_ref[r], kn_ref[r], vn_ref[r], sm_ref[r], msuf_ref[...],
                               kbuf, vbuf, lbuf, n_pages)


def _decode_row(slot, q, k_new, v_new, sm_row, msuf, kbuf, vbuf, lbuf, n_pages):
    hd = FOX_HEAD_DIM
    page = kbuf.shape[-1]

    r8 = lax.broadcasted_iota(jnp.int32, (SUBLANES, LANES), 0)
    c8 = lax.broadcasted_iota(jnp.int32, (SUBLANES, LANES), 1)
    lf_new = jnp.sum(jnp.where(r8 == c8, sm_row, 0.0), axis=1, keepdims=True)
    n_rows = n_pages * FOX_HEADS
    lf = lbuf[slot].reshape(n_rows, page)
    within = functools.reduce(lambda a, c: a + c, [_dot(t, msuf) for t in _split3(lf)])
    tot = jnp.broadcast_to(jnp.sum(lf, axis=1, keepdims=True), (n_rows, page))
    row = lax.broadcasted_iota(jnp.int32, (n_rows, page), 0)
    later = tot
    k = FOX_HEADS
    while k < n_rows:
        later = later + jnp.where(row + k < n_rows, pltpu.roll(later, n_rows - k, 0), 0.0)
        k *= 2
    bias = (within + (later - tot)).reshape(n_pages, FOX_HEADS, page) + lf_new[None]

    q_t = q.T
    qb = jnp.stack([jnp.broadcast_to(q_t[:, h:h + 1], (hd, page)) for h in range(FOX_HEADS)], axis=0)
    logits = [jnp.sum(kbuf[slot, j] * qb, axis=1) + bias[j] for j in range(n_pages)]
    s_new = jnp.sum(k_new * q, axis=-1, keepdims=True)
    m = functools.reduce(jnp.maximum, [jnp.max(x, axis=-1, keepdims=True) for x in logits] + [s_new])
    ps = [jnp.exp(x - m) for x in logits]
    p_new = jnp.exp(s_new - m)
    l = functools.reduce(lambda a, c: a + c, [jnp.sum(x, axis=-1, keepdims=True) for x in ps] + [p_new])

    cols = []
    for h in range(FOX_HEADS):
        acc = vbuf[slot, 0, h] * ps[0][h:h + 1, :]
        for j in range(1, n_pages):
            acc = acc + vbuf[slot, j, h] * ps[j][h:h + 1, :]
        cols.append(jnp.sum(acc, axis=1, keepdims=True))
    mat = jnp.concatenate(cols + [jnp.zeros((hd, LANES - FOX_HEADS), F32)], axis=1)
    return (mat.T[0:FOX_HEADS, :] + p_new * v_new) / l


def _rider(page_table, q, kn, vn, sm, kt_pool, vt_pool, lf_pool, base, count, n_steps):
    n_pages = page_table.shape[1]
    page = kt_pool.shape[-1]
    per_step = count // n_steps
    assert per_step * n_steps == count, "sample rows must split evenly over the host's grid steps"
    tok = jnp.arange(page)
    msuf = (tok[:, None] > tok[None, :]).astype(BF16)
    rows = lambda a: a[base:base + count]
    tile = lambda a: pl.BlockSpec((per_step,) + a.shape[1:], lambda i, pt: (i,) + (0,) * (a.ndim - 1))
    hbm = pl.BlockSpec(memory_space=pl.ANY)
    operands = [rows(q), rows(kn), rows(vn), rows(sm).reshape(count, 1, LANES), msuf, kt_pool, vt_pool, lf_pool]
    kv_buf = pltpu.VMEM((2, n_pages, FOX_HEADS, FOX_HEAD_DIM, page), F32)
    return dict(
        prefetch=page_table.reshape(-1),
        operands=operands,
        in_specs=[tile(a) for a in operands[:4]] + [pl.BlockSpec(msuf.shape, lambda i, pt: (0, 0)), hbm, hbm, hbm],
        out_spec=pl.BlockSpec((per_step, FOX_HEADS, FOX_HEAD_DIM), lambda i, pt: (i, 0, 0)),
        out_shape=jax.ShapeDtypeStruct((count, FOX_HEADS, FOX_HEAD_DIM), F32),
        scratch=[kv_buf, kv_buf, pltpu.VMEM((2, n_pages, FOX_HEADS, page), F32), pltpu.SemaphoreType.DMA((2, 3))],
        cfg=(base, count, per_step, n_pages),
    )


def _gdn_sample_body(s_ref, qkv_ref, sm_ref, z_ref, gn_ref, sout_ref, o_ref):
    d = GDN_HEAD_DIM
    eye = lax.broadcasted_iota(jnp.int32, (d, d), 0) == lax.broadcasted_iota(jnp.int32, (d, d), 1)

    def col(rows):
        return jnp.sum(jnp.where(eye[None], rows[:, None, :], 0.0), axis=2, keepdims=True)

    for hh in range(GDN_HEADS):
        hs = slice(hh * d, (hh + 1) * d)
        q = qkv_ref[:, hs]
        k = qkv_ref[:, GDN_WIDTH + hh * d:GDN_WIDTH + (hh + 1) * d]
        v = qkv_ref[:, 2 * GDN_WIDTH + hh * d:2 * GDN_WIDTH + (hh + 1) * d]
        eg = jnp.exp(sm_ref[:, SM_G + hh:SM_G + hh + 1])
        beta = sm_ref[:, SM_BETA + hh:SM_BETA + hh + 1]
        s = s_ref[:, hh]
        k_col = col(k)
        v_new = beta * (v - eg * jnp.sum(k_col * s, axis=1))
        o = eg * jnp.sum(col(q) * s, axis=1) + jnp.sum(q * k, axis=1, keepdims=True) * v_new
        sout_ref[:, hh] = s * eg[:, :, None] + k_col * v_new[:, None, :]
        o_ref[:, hs] = _rms(o, gn_ref[...]) * _silu(z_ref[:, hs])


def _gdn_sample(state, gqkv, sm, gz, gn):
    nb = state.shape[0]
    bb = 2 * SUBLANES if nb % (2 * SUBLANES) == 0 else SUBLANES
    d = GDN_HEAD_DIM
    row_spec = lambda w: pl.BlockSpec((bb, w), lambda i: (i, 0))
    st_spec = pl.BlockSpec((bb, GDN_HEADS, d, d), lambda i: (i, 0, 0, 0))
    return pl.pallas_call(
        _gdn_sample_body,
        grid=(nb // bb,),
        in_specs=[st_spec, row_spec(3 * GDN_WIDTH), row_spec(LANES), row_spec(GDN_WIDTH),
                  pl.BlockSpec((1, d), lambda i: (0, 0))],
        out_specs=(st_spec, row_spec(GDN_WIDTH)),
        out_shape=(jax.ShapeDtypeStruct(state.shape, F32), jax.ShapeDtypeStruct((nb, GDN_WIDTH), F32)),
        compiler_params=pltpu.CompilerParams(dimension_semantics=("arbitrary",), vmem_limit_bytes=32 * MIB),
        name="gdn_sample",
    )(state, gqkv, sm, gz, gn)


def _pad_rows(a, rows):
    return jnp.concatenate([a, jnp.zeros((rows - a.shape[0],) + a.shape[1:], a.dtype)], axis=0)


def kernel(x_prompt, x_sample, cache_k, cache_v, cache_logf, state_gdn, state_gdn_conv, state_ffn_conv,
           page_table, norm_mix, w_in, b_forget, gdn_a_log, gdn_dt_bias, w_gdn_conv, gdn_out_norm, w_out,
           norm_ffn, w_up, w_ffn_conv, b_ffn_conv, w_down, norm_final):
    assert w_in.shape[0] == 1, "single-layer trunk"
    n_seq, seq_len, _ = x_prompt.shape
    nb = x_sample.shape[0]
    n_pool, page = cache_k.shape[1], cache_k.shape[2]

    w = jnp.transpose(w_in[0])
    o_ff = 3 * FOX_WIDTH
    o_g = o_ff + FOX_HEADS
    o_ga = o_g + 3 * GDN_WIDTH
    o_gz = o_ga + 2 * GDN_HEADS
    wfox, wgdn, wgz = w[:o_ff].astype(BF16), w[o_g:o_ga].astype(BF16), w[o_gz:].astype(BF16)
    wsm = jnp.concatenate([w[o_ff:o_g], w[o_ga:o_gz],
                           jnp.zeros((LANES - FOX_HEADS - 2 * GDN_HEADS, D_MODEL), F32)], axis=0).astype(BF16)
    par = jnp.zeros((SUBLANES, LANES), F32)
    par = par.at[0, SM_LF:SM_LF + FOX_HEADS].set(b_forget[0])
    par = par.at[0, SM_G:SM_G + GDN_HEADS].set(gdn_dt_bias[0])
    par = par.at[1, SM_G:SM_G + GDN_HEADS].set(gdn_a_log[0])
    nrm = norm_mix[0][None, :]
    cw = _pad_rows(w_gdn_conv[0], SUBLANES)
    gn = gdn_out_norm[0][None, :]
    wo = w_out[0].astype(BF16)
    nf = norm_ffn[0][None, :]
    wup = w_up[0].astype(BF16)
    cwf = _pad_rows(w_ffn_conv[0], SUBLANES)
    cbf = b_ffn_conv[0][None, :]
    wd = w_down[0].astype(BF16)
    nfin = norm_final[None, :]

    xs = x_sample.reshape(nb, D_MODEL)
    gctx = state_gdn_conv[0]
    fctx = state_ffn_conv[0]
    fk_s, fv_s, q_s, gqkv_s, gz_s, sm_s, pre_s = _proj_sample(
        xs, nrm, wfox, wgdn, wgz, wsm, par, cw, gctx[:, 0], gctx[:, 1], gctx[:, 2])
    heads = lambda a: a.reshape(nb, FOX_HEADS, FOX_HEAD_DIM)
    half = nb // 2
    decode_args = (page_table, heads(q_s), heads(fk_s), heads(fv_s), sm_s,
                   jnp.transpose(cache_k[0], (0, 2, 3, 1)), jnp.transpose(cache_v[0], (0, 2, 3, 1)),
                   jnp.transpose(cache_logf[0], (0, 2, 1)))
    rider_a = functools.partial(_rider, *decode_args, 0, half)
    rider_b = functools.partial(_rider, *decode_args, half, nb - half)

    xp = x_prompt.reshape(n_seq * seq_len, D_MODEL)
    kt, vt, qt16, kaug, vt16, gqkv, gz, sm, smt, cst, fox_sa = _proj_prompt(
        xp, nrm, wfox, wgdn, wgz, wsm, par, cw, seq_len, rider_a)
    fox = _fox_prompt(qt16, kaug, vt16, n_seq, seq_len)
    gdn, s_p = _gdn_prompt(gqkv, sm, smt, gz, gn, n_seq, seq_len)
    yp, cstf, fox_sb = _ffn_prompt(xp, fox, gdn, wo, nf, wup, cwf, cbf, wd, nfin, seq_len, rider_b)

    fox_s = jnp.concatenate([fox_sa, fox_sb], axis=0)
    s_s, gdn_s = _gdn_sample(state_gdn[0], gqkv_s, sm_s, gz_s, gn)
    ys, gate_s = _ffn_sample(xs, fox_s.reshape(nb, FOX_WIDTH).astype(BF16), gdn_s.astype(BF16), wo, nf, wup,
                             cwf, cbf, wd, nfin, fctx[:, 0], fctx[:, 1])

    kv_shape_s = (1, nb, 1, FOX_HEADS, FOX_HEAD_DIM)
    new_kv_p = lambda a: jnp.transpose(a.reshape(1, n_seq, FOX_HEADS, FOX_HEAD_DIM, seq_len), (0, 1, 4, 2, 3))
    return (
        yp.reshape(n_seq, seq_len, D_MODEL),
        ys.reshape(nb, 1, D_MODEL),
        new_kv_p(kt),
        new_kv_p(vt),
        sm[:, SM_LF:SM_LF + FOX_HEADS].reshape(1, n_seq, seq_len, FOX_HEADS),
        s_p[None],
        cst.reshape(n_seq, SUBLANES, 3 * GDN_WIDTH)[None, :, SUBLANES - (GDN_CONV - 1):],
        cstf.reshape(n_seq, SUBLANES, FFN_DIM)[None, :, SUBLANES - (FFN_CONV - 1):],
        fk_s.reshape(kv_shape_s),
        fv_s.reshape(kv_shape_s),
        sm_s[:, SM_LF:SM_LF + FOX_HEADS].reshape(1, nb, 1, FOX_HEADS),
        s_s[None],
        jnp.concatenate([gctx[:, 1:], pre_s[:, None, :]], axis=1)[None],
        jnp.concatenate([fctx[:, 1:], gate_s[:, None, :]], axis=1)[None],
    )
```

```python
import functools

import jax
import jax.numpy as jnp
from jax import lax
from jax.experimental import pallas as pl
from jax.experimental.pallas import tpu as pltpu

D_MODEL = 1024
FOX_HEADS = 8
FOX_HEAD_DIM = 64
FOX_WIDTH = FOX_HEADS * FOX_HEAD_DIM
GDN_HEADS = 4
GDN_HEAD_DIM = 128
GDN_WIDTH = GDN_HEADS * GDN_HEAD_DIM
GDN_CONV = 4
GDN_CHUNK = 64
FFN_DIM = 2816
FFN_CONV = 3
EPS = 1e-6
NEG_BIG = -1e30
FOX_SCALE = FOX_HEAD_DIM ** -0.5
GDN_SCALE = GDN_HEAD_DIM ** -0.5
LOG2E = 1.4426950408889634
FOX_QSCALE = FOX_SCALE * LOG2E
FOX_AUG = 3
FOX_DEN_ROWS = 16
FOX_GROUP = 2
GDN_GROUP = 2

LANES = 128
SUBLANES = 8
ROW_TILE = 256
FOX_TILE = 256
MIB = 1024 * 1024

SM_LF = 0
SM_G = 8
SM_BETA = 12
SM_CUM = 16
SM_GCUM = 24

F32 = jnp.float32
BF16 = jnp.bfloat16


def _sigmoid(x):
    return 1.0 / (1.0 + jnp.exp(-x))


def _silu(x):
    return x * _sigmoid(x)


def _rms(x, g):
    return x * lax.rsqrt(jnp.mean(x * x, axis=-1, keepdims=True) + EPS) * g


def _dot(a, b):
    return jnp.dot(a, b, preferred_element_type=F32)


def _dot_nt(a, b):
    return lax.dot_general(a, b, (((1,), (1,)), ((), ())), preferred_element_type=F32)


def _shift_rows(x, prev8, k):
    r = pltpu.roll(x, k, 0)
    row8 = lax.broadcasted_iota(jnp.int32, prev8.shape, 0)
    top = jnp.where(row8 < k, pltpu.roll(prev8, k, 0), r[0:SUBLANES])
    return jnp.concatenate([top, r[SUBLANES:]], axis=0)


def _gate_activations(raw, par_ref):
    z = raw + par_ref[0:1, :]
    lane = lax.broadcasted_iota(jnp.int32, z.shape, 1)
    t = jnp.log1p(jnp.exp(-jnp.abs(z)))
    lf = jnp.minimum(z, 0.0) - t
    softplus = jnp.maximum(z, 0.0) + t
    g = -jnp.exp(par_ref[1:2, :]) * softplus
    beta = _sigmoid(z)
    return jnp.where(lane < SM_G, lf, jnp.where(lane < SM_BETA, g, jnp.where(lane < SM_CUM, beta, 0.0)))


def _gdn_qkv_norm(c, out_ref):
    for part, scale in ((0, GDN_SCALE), (1, 1.0)):
        for hh in range(GDN_HEADS):
            off = part * GDN_WIDTH + hh * GDN_HEAD_DIM
            seg = c[:, off:off + GDN_HEAD_DIM]
            n = lax.rsqrt(jnp.sum(seg * seg, axis=-1, keepdims=True) + EPS)
            out_ref[:, off:off + GDN_HEAD_DIM] = seg * n * scale
    out_ref[:, 2 * GDN_WIDTH:] = c[:, 2 * GDN_WIDTH:]


def _proj_prompt_body(pt_ref, *refs, tiles_per_seq, rider_cfg):
    (x_ref, nrm_ref, wfox_ref, wgdn_ref, wgz_ref, wsm_ref, par_ref, cw_ref, scat_ref) = refs[:9]
    rider_in = refs[9:9 + RIDER_INPUTS]
    (kt_ref, vt_ref, qt16_ref, kaug_ref, vt16_ref, gqkv_ref, gz_ref, sm_ref, smt_ref, cst_ref,
     rider_out, carry_ref, prev_ref) = refs[9 + RIDER_INPUTS:22 + RIDER_INPUTS]
    rider_scratch = refs[22 + RIDER_INPUTS:]
    i = pl.program_id(0)

    @pl.when(i % tiles_per_seq == 0)
    def _():
        carry_ref[...] = jnp.zeros_like(carry_ref)
        prev_ref[...] = jnp.zeros_like(prev_ref)

    _rider_step(pt_ref, rider_in, rider_out, rider_scratch, rider_cfg)

    tm = x_ref.shape[0]
    h16 = _rms(x_ref[...], nrm_ref[...]).astype(BF16)

    fox = _dot_nt(h16, wfox_ref[...])
    fq, fk, fv = fox[:, :FOX_WIDTH], fox[:, FOX_WIDTH:2 * FOX_WIDTH], fox[:, 2 * FOX_WIDTH:]
    fv_t = fv.T
    kt_ref[0] = fk.T
    vt_ref[0] = fv_t
    vt16_ref[0] = fv_t.astype(BF16)
    lane_p = lax.broadcasted_iota(jnp.int32, (tm, LANES), 1)
    ones_blk = jnp.where(lane_p < 2 * FOX_AUG, 1.0, 0.0)
    q_aug = []
    for p in range(FOX_HEADS // 2):
        q_aug += [fq[:, p * LANES:(p + 1) * LANES] * FOX_QSCALE, ones_blk]
        kaug_ref[:, 2 * p * LANES:(2 * p + 1) * LANES] = fk[:, p * LANES:(p + 1) * LANES].astype(BF16)
    qt16_ref[0] = jnp.concatenate(q_aug, axis=1).T.astype(BF16)

    act = _gate_activations(_dot_nt(h16, wsm_ref[...]), par_ref)
    row = lax.broadcasted_iota(jnp.int32, act.shape, 0)
    lane = lax.broadcasted_iota(jnp.int32, act.shape, 1)
    y = act
    yc = act
    k = 1
    while k < tm:
        y = y + jnp.where(row >= k, pltpu.roll(y, k, 0), 0.0)
        if k < GDN_CHUNK:
            yc = yc + jnp.where((row & (GDN_CHUNK - 1)) >= k, pltpu.roll(yc, k, 0), 0.0)
        k *= 2
    y = y + carry_ref[0:1, :]
    carry_ref[0:1, :] = y[tm - 1:tm, :]
    terms = jnp.concatenate(_split3(y * (-LOG2E)), axis=1)
    extra = _dot(terms, scat_ref[...]).astype(BF16)
    for p in range(FOX_HEADS // 2):
        kaug_ref[:, (2 * p + 1) * LANES:(2 * p + 2) * LANES] = extra[:, p * LANES:(p + 1) * LANES]
    shift = SM_CUM - SM_LF
    sm = jnp.where(lane < SM_CUM, act,
                   jnp.where(lane < SM_GCUM, pltpu.roll(y, shift, 1),
                             jnp.where(lane < SM_GCUM + GDN_HEADS, pltpu.roll(yc, shift, 1), 0.0)))
    sm_ref[...] = sm
    smt_ref[0] = sm.T

    pre = _dot_nt(h16, wgdn_ref[...])
    assert GDN_CONV == 4
    s1 = _shift_rows(pre, prev_ref[0:SUBLANES, :], 1)
    far = pre * cw_ref[1:2, :] + s1 * cw_ref[0:1, :]
    acc = pre * cw_ref[3:4, :] + s1 * cw_ref[2:3, :] + _shift_rows(far, prev_ref[SUBLANES:, :], 2)
    prev_ref[0:SUBLANES, :] = pre[tm - SUBLANES:, :]
    prev_ref[SUBLANES:, :] = far[tm - SUBLANES:, :]
    cst_ref[...] = pre[tm - SUBLANES:, :]
    _gdn_qkv_norm(_silu(acc), gqkv_ref)

    gz_ref[...] = _dot_nt(h16, wgz_ref[...])


def _proj_prompt(x, nrm, wfox, wgdn, wgz, wsm, par, cw, seq_len, make_rider):
    rows = x.shape[0]
    tm = ROW_TILE
    n_seq = rows // seq_len
    tiles_per_seq = seq_len // tm
    row_spec = lambda w: pl.BlockSpec((tm, w), lambda i, pt: (i, 0))
    col_spec = lambda w: pl.BlockSpec((1, w, tm), lambda i, pt: (i // tiles_per_seq, 0, i % tiles_per_seq))
    const = lambda s: pl.BlockSpec(s, lambda i, pt: (0, 0), pipeline_mode=pl.Buffered(1))
    rider = make_rider(rows // tm)
    aug_width = FOX_HEADS * LANES
    src = jnp.arange(3 * LANES)
    term, head = src // LANES, src % LANES
    dst = (head // 2) * LANES + (head % 2) * FOX_AUG + term
    scat = ((jnp.arange(FOX_WIDTH)[None, :] == dst[:, None]) & (head[:, None] < FOX_HEADS)).astype(BF16)
    out_shape = (
        jax.ShapeDtypeStruct((n_seq, FOX_WIDTH, seq_len), F32),
        jax.ShapeDtypeStruct((n_seq, FOX_WIDTH, seq_len), F32),
        jax.ShapeDtypeStruct((n_seq, aug_width, seq_len), BF16),
        jax.ShapeDtypeStruct((rows, aug_width), BF16),
        jax.ShapeDtypeStruct((n_seq, FOX_WIDTH, seq_len), BF16),
        jax.ShapeDtypeStruct((rows, 3 * GDN_WIDTH), F32),
        jax.ShapeDtypeStruct((rows, GDN_WIDTH), F32),
        jax.ShapeDtypeStruct((rows, LANES), F32),
        jax.ShapeDtypeStruct((n_seq, LANES, seq_len), F32),
        jax.ShapeDtypeStruct((n_seq * SUBLANES, 3 * GDN_WIDTH), F32),
    )
    out_specs = (
        col_spec(FOX_WIDTH), col_spec(FOX_WIDTH), col_spec(aug_width), row_spec(aug_width),
        col_spec(FOX_WIDTH), row_spec(3 * GDN_WIDTH), row_spec(GDN_WIDTH), row_spec(LANES),
        col_spec(LANES),
        pl.BlockSpec((SUBLANES, 3 * GDN_WIDTH), lambda i, pt: (i // tiles_per_seq, 0)),
    )
    grid_spec = pltpu.PrefetchScalarGridSpec(
        num_scalar_prefetch=1,
        grid=(rows // tm,),
        in_specs=[row_spec(D_MODEL), const((1, D_MODEL)), const(wfox.shape), const(wgdn.shape), const(wgz.shape),
                  const(wsm.shape), const((SUBLANES, LANES)), const((SUBLANES, 3 * GDN_WIDTH)),
                  const(scat.shape)] + rider["in_specs"],
        out_specs=out_specs + (rider["out_spec"],),
        scratch_shapes=[pltpu.VMEM((SUBLANES, LANES), F32), pltpu.VMEM((2 * SUBLANES, 3 * GDN_WIDTH), F32)]
        + rider["scratch"],
    )
    return pl.pallas_call(
        functools.partial(_proj_prompt_body, tiles_per_seq=tiles_per_seq, rider_cfg=rider["cfg"]),
        grid_spec=grid_spec,
        out_shape=out_shape + (rider["out_shape"],),
        compiler_params=pltpu.CompilerParams(dimension_semantics=("arbitrary",), vmem_limit_bytes=56 * MIB),
        name="proj_prompt",
    )(rider["prefetch"], x, nrm, wfox, wgdn, wgz, wsm, par, cw, scat, *rider["operands"])


def _proj_sample_body(x_ref, nrm_ref, wfox_ref, wgdn_ref, wgz_ref, wsm_ref, par_ref, cw_ref, c0_ref, c1_ref, c2_ref,
                      fk_ref, fv_ref, q_ref, gqkv_ref, gz_ref, sm_ref, pre_ref):
    h16 = _rms(x_ref[...], nrm_ref[...]).astype(BF16)
    fox = _dot_nt(h16, wfox_ref[...])
    q_ref[...] = fox[:, :FOX_WIDTH] * FOX_SCALE
    fk_ref[...] = fox[:, FOX_WIDTH:2 * FOX_WIDTH]
    fv_ref[...] = fox[:, 2 * FOX_WIDTH:]
    sm_ref[...] = _gate_activations(_dot_nt(h16, wsm_ref[...]), par_ref)
    pre = _dot_nt(h16, wgdn_ref[...])
    pre_ref[...] = pre
    acc = (pre * cw_ref[3:4, :] + c2_ref[...] * cw_ref[2:3, :]
           + c1_ref[...] * cw_ref[1:2, :] + c0_ref[...] * cw_ref[0:1, :])
    _gdn_qkv_norm(_silu(acc), gqkv_ref)
    gz_ref[...] = _dot_nt(h16, wgz_ref[...])


def _proj_sample(x, nrm, wfox, wgdn, wgz, wsm, par, cw, c0, c1, c2):
    rows = x.shape[0]
    full = lambda a: pl.BlockSpec(a.shape, lambda i: (0,) * a.ndim)
    args = (x, nrm, wfox, wgdn, wgz, wsm, par, cw, c0, c1, c2)
    shapes = ((rows, FOX_WIDTH), (rows, FOX_WIDTH), (rows, FOX_WIDTH), (rows, 3 * GDN_WIDTH),
              (rows, GDN_WIDTH), (rows, LANES), (rows, 3 * GDN_WIDTH))
    return pl.pallas_call(
        _proj_sample_body,
        grid=(1,),
        in_specs=[full(a) for a in args],
        out_specs=tuple(pl.BlockSpec(s, lambda i: (0, 0)) for s in shapes),
        out_shape=tuple(jax.ShapeDtypeStruct(s, F32) for s in shapes),
        compiler_params=pltpu.CompilerParams(dimension_semantics=("arbitrary",), vmem_limit_bytes=48 * MIB),
        name="proj_sample",
    )(*args)


def _fox_prompt_body(qt_ref, kaug_ref, vt_ref, o_ref):
    i = pl.program_id(1)
    t = FOX_TILE
    hd = FOX_HEAD_DIM
    aug = 2 * LANES
    rr = lax.broadcasted_iota(jnp.int32, (t, t), 0)
    cc = lax.broadcasted_iota(jnp.int32, (t, t), 1)
    causal = rr <= cc
    row = lax.broadcasted_iota(jnp.int32, (aug, t), 0)
    chains = [(g, h) for g in range(qt_ref.shape[0]) for h in range(FOX_HEADS)]
    qs = []
    for g, h in chains:
        e = h % 2
        qp = qt_ref[g, (h // 2) * aug:(h // 2 + 1) * aug, :]
        own = jnp.logical_or(jnp.logical_and(row >= e * hd, row < (e + 1) * hd),
                             jnp.logical_and(row >= LANES + e * FOX_AUG, row < LANES + (e + 1) * FOX_AUG))
        qs.append(jnp.where(own, qp, jnp.zeros_like(qp)))

    def tile(j, carry, masked):
        koff = pl.multiple_of(j * t, t)
        ss = [_dot(kaug_ref[g, pl.ds(koff, t), (h // 2) * aug:(h // 2 + 1) * aug], qs[n])
              for n, (g, h) in enumerate(chains)]
        stats, pms = [], []
        for n in range(len(chains)):
            m = carry[n][0]
            s = jnp.where(causal, ss[n], NEG_BIG) if masked else ss[n]
            m_new = jnp.maximum(m, jnp.max(s, axis=0, keepdims=True))
            stats.append((m_new, jnp.exp2(m - m_new)))
            pms.append(jnp.exp2(s - m_new).astype(BF16))
        out = []
        for n, (g, h) in enumerate(chains):
            m_new, alpha = stats[n]
            v_ones = jnp.concatenate([vt_ref[g, h * hd:(h + 1) * hd, pl.ds(koff, t)], ones_rows], axis=0)
            out.append((m_new, alpha * carry[n][1] + _dot(v_ones, pms[n])))
        return tuple(out)

    ones_rows = jnp.ones((FOX_DEN_ROWS, t), BF16)
    init = tuple((jnp.full((1, t), NEG_BIG, F32), jnp.zeros((hd + FOX_DEN_ROWS, t), F32)) for _ in chains)
    carry = lax.fori_loop(0, i, lambda j, c: tile(j, c, False), init)
    final = tile(i, carry, True)
    for n in range(0, len(chains), 2):
        g, h = chains[n]
        outs = [a[:hd] / a[hd:hd + 1] for _, a in final[n:n + 2]]
        o_ref[g, :, (h // 2) * LANES:(h // 2 + 1) * LANES] = jnp.concatenate(outs, axis=0).T.astype(BF16)


def _fox_prompt(qt16, kaug, vt16, n_seq, seq_len):
    t = FOX_TILE
    nq = seq_len // t
    aug_width = kaug.shape[1]
    group = FOX_GROUP if n_seq % FOX_GROUP == 0 else 1
    out = pl.pallas_call(
        _fox_prompt_body,
        grid=(n_seq // group, nq),
        in_specs=[pl.BlockSpec((group, aug_width, t), lambda b, i: (b, 0, i)),
                  pl.BlockSpec((group, seq_len, aug_width), lambda b, i: (b, 0, 0)),
                  pl.BlockSpec((group, FOX_WIDTH, seq_len), lambda b, i: (b, 0, 0))],
        out_specs=pl.BlockSpec((group, t, FOX_WIDTH), lambda b, i: (b, i, 0)),
        out_shape=jax.ShapeDtypeStruct((n_seq, seq_len, FOX_WIDTH), BF16),
        compiler_params=pltpu.CompilerParams(dimension_semantics=("arbitrary", "arbitrary"),
                                             vmem_limit_bytes=48 * MIB),
        name="fox_prompt",
    )(qt16, kaug.reshape(n_seq, seq_len, aug_width), vt16)
    return out.reshape(n_seq * seq_len, FOX_WIDTH)


def _bdot(a, b):
    return lax.dot_general(a.astype(BF16), b.astype(BF16), (((2,), (1,)), ((0,), (0,))),
                           preferred_element_type=F32)


def _bdot_nt(a, b):
    return lax.dot_general(a.astype(BF16), b.astype(BF16), (((2,), (2,)), ((0,), (0,))),
                           preferred_element_type=F32)


def _unit_lower_inverse_minus_eye(a, r, c):
    blk16 = (r // 16) == (c // 16)
    blk32 = (r // 32) == (c // 32)
    p = jnp.where(blk16, -a, 0.0)
    dt = p
    for _ in range(3):
        p = _bdot(p, p)
        dt = dt + p + _bdot(dt, p)
    for off in (jnp.where(jnp.logical_and(blk32, jnp.logical_not(blk16)), a, 0.0),
                jnp.where(blk32, 0.0, a)):
        x = off + _bdot(dt, off)
        dt = dt - (x + _bdot(x, dt))
    return dt


def _gdn_prompt_body(qkv_ref, sm_ref, gcr_ref, z_ref, gn_ref, o_ref, sout_ref, s_ref):
    ci = pl.program_id(1)

    @pl.when(ci == 0)
    def _():
        s_ref[...] = jnp.zeros_like(s_ref)

    ch = GDN_CHUNK
    d = GDN_HEAD_DIM
    nh = GDN_HEADS
    n_seq = qkv_ref.shape[0]
    n_chunks = qkv_ref.shape[1] // ch
    units = [(cidx, g, hh) for cidx in range(n_chunks) for g in range(n_seq) for hh in range(nh)]
    per_chunk = n_seq * nh

    def gather(fn):
        return jnp.stack([fn(g, slice(cidx * ch, (cidx + 1) * ch), hh) for cidx, g, hh in units], axis=0)

    q = gather(lambda g, rows, hh: qkv_ref[g, rows, hh * d:(hh + 1) * d])
    k = gather(lambda g, rows, hh: qkv_ref[g, rows, GDN_WIDTH + hh * d:GDN_WIDTH + (hh + 1) * d])
    v = gather(lambda g, rows, hh: qkv_ref[g, rows, 2 * GDN_WIDTH + hh * d:2 * GDN_WIDTH + (hh + 1) * d])
    beta = gather(lambda g, rows, hh: sm_ref[g, rows, SM_BETA + hh:SM_BETA + hh + 1])
    gc = gather(lambda g, rows, hh: sm_ref[g, rows, SM_GCUM + hh:SM_GCUM + hh + 1])
    gr = gather(lambda g, rows, hh: gcr_ref[g, hh:hh + 1, rows])

    r = lax.broadcasted_iota(jnp.int32, (1, ch, ch), 1)
    c = lax.broadcasted_iota(jnp.int32, (1, ch, ch), 2)
    lower = r >= c
    beta = jnp.broadcast_to(beta, (len(units), ch, d))
    gc = jnp.broadcast_to(gc, (len(units), ch, d))
    decay = jnp.where(lower, jnp.exp(jnp.where(lower, gc[:, :, :ch] - gr, 0.0)), 0.0)
    qk_kk = _bdot_nt(jnp.concatenate([q, k], axis=1), k)
    qk = qk_kk[:, :ch] * decay
    a = jnp.where(r > c, beta[:, :, :ch] * qk_kk[:, ch:] * decay, 0.0)
    dt = _unit_lower_inverse_minus_eye(a, r, c)
    eg = jnp.exp(gc)
    rhs = jnp.concatenate([v * beta, k * (beta * eg)], axis=-1)
    uw = rhs + _bdot(dt, rhs)
    gl = gc[:, ch - 1:ch, :]
    qd = (q * eg).astype(BF16)
    kd = k * jnp.exp(gl - gc)
    g_last = jnp.exp(gl)
    qk16 = qk.astype(BF16)

    s = s_ref[...]
    for cidx in range(n_chunks):
        us = slice(cidx * per_chunk, (cidx + 1) * per_chunk)
        rows = slice(cidx * ch, (cidx + 1) * ch)
        s16 = s.astype(BF16)
        v_new = uw[us, :, :d] - _bdot(uw[us, :, d:], s16)
        o = _bdot(qd[us], s16) + _bdot(qk16[us], v_new)
        vn16 = v_new.astype(BF16)
        upd = jnp.stack([_dot(kd[cidx * per_chunk + j].T.astype(BF16), vn16[j]) for j in range(per_chunk)],
                        axis=0)
        s = s * g_last[us] + upd
        for g in range(n_seq):
            for hh in range(nh):
                hs = slice(hh * d, (hh + 1) * d)
                o_ref[g, rows, hs] = (_rms(o[g * nh + hh], gn_ref[...]) * _silu(z_ref[g, rows, hs])).astype(BF16)
    s_ref[...] = s

    @pl.when(ci == pl.num_programs(1) - 1)
    def _():
        sout_ref[...] = s.reshape(sout_ref.shape)


def _gdn_prompt(gqkv, sm, smt, gz, gn, n_seq, seq_len):
    tm = ROW_TILE
    nt = seq_len // tm
    gcum_block = SM_GCUM // SUBLANES
    group = GDN_GROUP if n_seq % GDN_GROUP == 0 else 1
    seq3 = lambda a: a.reshape(n_seq, seq_len, a.shape[-1])
    tile = lambda w: pl.BlockSpec((group, tm, w), lambda b, i: (b, i, 0))
    out, s_out = pl.pallas_call(
        _gdn_prompt_body,
        grid=(n_seq // group, nt),
        in_specs=[tile(3 * GDN_WIDTH), tile(LANES),
                  pl.BlockSpec((group, SUBLANES, tm), lambda b, i: (b, gcum_block, i)),
                  tile(GDN_WIDTH),
                  pl.BlockSpec((1, GDN_HEAD_DIM), lambda b, i: (0, 0))],
        out_specs=(tile(GDN_WIDTH),
                   pl.BlockSpec((group, GDN_HEADS, GDN_HEAD_DIM, GDN_HEAD_DIM), lambda b, i: (b, 0, 0, 0))),
        out_shape=(jax.ShapeDtypeStruct((n_seq, seq_len, GDN_WIDTH), BF16),
                   jax.ShapeDtypeStruct((n_seq, GDN_HEADS, GDN_HEAD_DIM, GDN_HEAD_DIM), F32)),
        scratch_shapes=[pltpu.VMEM((group * GDN_HEADS, GDN_HEAD_DIM, GDN_HEAD_DIM), F32)],
        compiler_params=pltpu.CompilerParams(dimension_semantics=("arbitrary", "arbitrary"),
                                             vmem_limit_bytes=40 * MIB),
        name="gdn_prompt",
    )(seq3(gqkv), seq3(sm), smt, seq3(gz), gn)
    return out.reshape(n_seq * seq_len, GDN_WIDTH), s_out


def _ffn_tail(x, fox_ref, gdn_ref, wo_ref, nf_ref, wup_ref, wd_ref, nfin_ref, conv):
    mix = _dot(fox_ref[...], wo_ref[0:FOX_WIDTH, :]) + _dot(gdn_ref[...], wo_ref[FOX_WIDTH:, :])
    x2 = x + mix
    h2 = _rms(x2, nf_ref[...]).astype(BF16)
    gu = _dot(h2, wup_ref[...])
    gate, up = gu[:, :FFN_DIM], gu[:, FFN_DIM:]
    act = (_silu(conv(gate)) * up).astype(BF16)
    x3 = x2 + _dot(act, wd_ref[...])
    return _rms(x3, nfin_ref[...])


def _ffn_prompt_body(pt_ref, *refs, tiles_per_seq, rider_cfg):
    (x_ref, fox_ref, gdn_ref, wo_ref, nf_ref, wup_ref, cw_ref, cb_ref, wd_ref, nfin_ref) = refs[:10]
    rider_in = refs[10:10 + RIDER_INPUTS]
    y_ref, cst_ref, rider_out, prev_ref = refs[10 + RIDER_INPUTS:14 + RIDER_INPUTS]
    rider_scratch = refs[14 + RIDER_INPUTS:]
    i = pl.program_id(0)

    @pl.when(i % tiles_per_seq == 0)
    def _():
        prev_ref[...] = jnp.zeros_like(prev_ref)

    _rider_step(pt_ref, rider_in, rider_out, rider_scratch, rider_cfg)

    tm = x_ref.shape[0]

    def conv(gate):
        prev = prev_ref[...]
        out = gate * cw_ref[FFN_CONV - 1:FFN_CONV, :] + cb_ref[...]
        for kk in range(1, FFN_CONV):
            out = out + _shift_rows(gate, prev, kk) * cw_ref[FFN_CONV - 1 - kk:FFN_CONV - kk, :]
        prev_ref[...] = gate[tm - SUBLANES:, :]
        cst_ref[...] = gate[tm - SUBLANES:, :]
        return out

    y_ref[...] = _ffn_tail(x_ref[...], fox_ref, gdn_ref, wo_ref, nf_ref, wup_ref, wd_ref, nfin_ref, conv)


def _ffn_prompt(x, fox, gdn, wo, nf, wup, cw, cb, wd, nfin, seq_len, make_rider):
    rows = x.shape[0]
    tm = ROW_TILE
    n_seq = rows // seq_len
    tiles_per_seq = seq_len // tm
    row_spec = lambda w: pl.BlockSpec((tm, w), lambda i, pt: (i, 0))
    const = lambda a: pl.BlockSpec(a.shape, lambda i, pt: (0, 0), pipeline_mode=pl.Buffered(1))
    rider = make_rider(rows // tm)
    grid_spec = pltpu.PrefetchScalarGridSpec(
        num_scalar_prefetch=1,
        grid=(rows // tm,),
        in_specs=[row_spec(D_MODEL), row_spec(FOX_WIDTH), row_spec(GDN_WIDTH), const(wo), const(nf),
                  const(wup), const(cw), const(cb), const(wd), const(nfin)] + rider["in_specs"],
        out_specs=(row_spec(D_MODEL),
                   pl.BlockSpec((SUBLANES, FFN_DIM), lambda i, pt: (i // tiles_per_seq, 0)),
                   rider["out_spec"]),
        scratch_shapes=[pltpu.VMEM((SUBLANES, FFN_DIM), F32)] + rider["scratch"],
    )
    return pl.pallas_call(
        functools.partial(_ffn_prompt_body, tiles_per_seq=tiles_per_seq, rider_cfg=rider["cfg"]),
        grid_spec=grid_spec,
        out_shape=(jax.ShapeDtypeStruct((rows, D_MODEL), F32),
                   jax.ShapeDtypeStruct((n_seq * SUBLANES, FFN_DIM), F32),
                   rider["out_shape"]),
        compiler_params=pltpu.CompilerParams(dimension_semantics=("arbitrary",), vmem_limit_bytes=58 * MIB),
        name="ffn_prompt",
    )(rider["prefetch"], x, fox, gdn, wo, nf, wup, cw, cb, wd, nfin, *rider["operands"])


def _ffn_sample_body(x_ref, fox_ref, gdn_ref, wo_ref, nf_ref, wup_ref, cw_ref, cb_ref, wd_ref, nfin_ref,
                     c0_ref, c1_ref, y_ref, gate_ref):
    def conv(gate):
        gate_ref[...] = gate
        return (gate * cw_ref[2:3, :] + c1_ref[...] * cw_ref[1:2, :] + c0_ref[...] * cw_ref[0:1, :]
                + cb_ref[...])

    y_ref[...] = _ffn_tail(x_ref[...], fox_ref, gdn_ref, wo_ref, nf_ref, wup_ref, wd_ref, nfin_ref, conv)


def _ffn_sample(x, fox, gdn, wo, nf, wup, cw, cb, wd, nfin, c0, c1):
    rows = x.shape[0]
    args = (x, fox, gdn, wo, nf, wup, cw, cb, wd, nfin, c0, c1)
    full = lambda a: pl.BlockSpec(a.shape, lambda i: (0, 0), pipeline_mode=pl.Buffered(1))
    return pl.pallas_call(
        _ffn_sample_body,
        grid=(1,),
        in_specs=[full(a) for a in args],
        out_specs=(pl.BlockSpec((rows, D_MODEL), lambda i: (0, 0)),
                   pl.BlockSpec((rows, FFN_DIM), lambda i: (0, 0))),
        out_shape=(jax.ShapeDtypeStruct((rows, D_MODEL), F32), jax.ShapeDtypeStruct((rows, FFN_DIM), F32)),
        compiler_params=pltpu.CompilerParams(dimension_semantics=("arbitrary",), vmem_limit_bytes=56 * MIB),
        name="ffn_sample",
    )(*args)


def _split3(x):
    hi = x.astype(BF16)
    r1 = x - hi.astype(F32)
    mid = r1.astype(BF16)
    lo = (r1 - mid.astype(F32)).astype(BF16)
    return hi, mid, lo


RIDER_INPUTS = 8
RIDER_SCRATCH = 4


def _rider_step(pt_ref, rider_in, o_ref, rider_scratch, cfg):
    q_ref, kn_ref, vn_ref, sm_ref, msuf_ref, kt_pool, vt_pool, lf_pool = rider_in
    kbuf, vbuf, lbuf, sems = rider_scratch
    base, count, per_step, n_pages = cfg
    i = pl.program_id(0)

    def copies(g, sl):
        out = []
        for j in range(n_pages):
            pg = pt_ref[g * n_pages + j]
            out.append(pltpu.make_async_copy(kt_pool.at[pg], kbuf.at[sl, j], sems.at[sl, 0]))
            out.append(pltpu.make_async_copy(vt_pool.at[pg], vbuf.at[sl, j], sems.at[sl, 1]))
            out.append(pltpu.make_async_copy(lf_pool.at[pg], lbuf.at[sl, j], sems.at[sl, 2]))
        return out

    @pl.when(i == 0)
    def _():
        for cp in copies(base, base % 2):
            cp.start()

    for r in range(per_step):
        g = base + i * per_step + r
        slot = g % 2

        @pl.when(g + 1 < base + count)
        def _():
            for cp in copies(g + 1, 1 - slot):
                cp.start()

        for cp in copies(g, slot):
            cp.wait()
        o_ref[r] = _decode_row(slot, q_ref[r], kn_ref[r], vn_ref[r], sm_ref[r], msuf_ref[...],
                               kbuf, vbuf, lbuf, n_pages)


def _decode_row(slot, q, k_new, v_new, sm_row, msuf, kbuf, vbuf, lbuf, n_pages):
    hd = FOX_HEAD_DIM
    page = kbuf.shape[-1]

    r8 = lax.broadcasted_iota(jnp.int32, (SUBLANES, LANES), 0)
    c8 = lax.broadcasted_iota(jnp.int32, (SUBLANES, LANES), 1)
    lf_new = jnp.sum(jnp.where(r8 == c8, sm_row, 0.0), axis=1, keepdims=True)
    n_rows = n_pages * FOX_HEADS
    lf = lbuf[slot].reshape(n_rows, page)
    within = functools.reduce(lambda a, c: a + c, [_dot(t, msuf) for t in _split3(lf)])
    tot = jnp.broadcast_to(jnp.sum(lf, axis=1, keepdims=True), (n_rows, page))
    row = lax.broadcasted_iota(jnp.int32, (n_rows, page), 0)
    later = tot
    k = FOX_HEADS
    while k < n_rows:
        later = later + jnp.where(row + k < n_rows, pltpu.roll(later, n_rows - k, 0), 0.0)
        k *= 2
    bias = (within + (later - tot)).reshape(n_pages, FOX_HEADS, page) + lf_new[None]

    q_t = q.T
    qb = jnp.stack([jnp.broadcast_to(q_t[:, h:h + 1], (hd, page)) for h in range(FOX_HEADS)], axis=0)
    logits = [jnp.sum(kbuf[slot, j] * qb, axis=1) + bias[j] for j in range(n_pages)]
    s_new = jnp.sum(k_new * q, axis=-1, keepdims=True)
    m = functools.reduce(jnp.maximum, [jnp.max(x, axis=-1, keepdims=True) for x in logits] + [s_new])
    ps = [jnp.exp(x - m) for x in logits]
    p_new = jnp.exp(s_new - m)
    l = functools.reduce(lambda a, c: a + c, [jnp.sum(x, axis=-1, keepdims=True) for x in ps] + [p_new])

    cols = []
    for h in range(FOX_HEADS):
        acc = vbuf[slot, 0, h] * ps[0][h:h + 1, :]
        for j in range(1, n_pages):
            acc = acc + vbuf[slot, j, h] * ps[j][h:h + 1, :]
        cols.append(jnp.sum(acc, axis=1, keepdims=True))
    mat = jnp.concatenate(cols + [jnp.zeros((hd, LANES - FOX_HEADS), F32)], axis=1)
    return (mat.T[0:FOX_HEADS, :] + p_new * v_new) / l


def _rider(page_table, q, kn, vn, sm, kt_pool, vt_pool, lf_pool, base, count, n_steps):
    n_pages = page_table.shape[1]
    page = kt_pool.shape[-1]
    per_step = count // n_steps
    assert per_step * n_steps == count, "sample rows must split evenly over the host's grid steps"
    tok = jnp.arange(page)
    msuf = (tok[:, None] > tok[None, :]).astype(BF16)
    rows = lambda a: a[base:base + count]
    tile = lambda a: pl.BlockSpec((per_step,) + a.shape[1:], lambda i, pt: (i,) + (0,) * (a.ndim - 1))
    hbm = pl.BlockSpec(memory_space=pl.ANY)
    operands = [rows(q), rows(kn), rows(vn), rows(sm).reshape(count, 1, LANES), msuf, kt_pool, vt_pool, lf_pool]
    kv_buf = pltpu.VMEM((2, n_pages, FOX_HEADS, FOX_HEAD_DIM, page), F32)
    return dict(
        prefetch=page_table.reshape(-1),
        operands=operands,
        in_specs=[tile(a) for a in operands[:4]] + [pl.BlockSpec(msuf.shape, lambda i, pt: (0, 0)), hbm, hbm, hbm],
        out_spec=pl.BlockSpec((per_step, FOX_HEADS, FOX_HEAD_DIM), lambda i, pt: (i, 0, 0)),
        out_shape=jax.ShapeDtypeStruct((count, FOX_HEADS, FOX_HEAD_DIM), F32),
        scratch=[kv_buf, kv_buf, pltpu.VMEM((2, n_pages, FOX_HEADS, page), F32), pltpu.SemaphoreType.DMA((2, 3))],
        cfg=(base, count, per_step, n_pages),
    )


def _gdn_sample_body(s_ref, qkv_ref, sm_ref, z_ref, gn_ref, sout_ref, o_ref):
    d = GDN_HEAD_DIM
    eye = lax.broadcasted_iota(jnp.int32, (d, d), 0) == lax.broadcasted_iota(jnp.int32, (d, d), 1)

    def col(rows):
        return jnp.sum(jnp.where(eye[None], rows[:, None, :], 0.0), axis=2, keepdims=True)

    for hh in range(GDN_HEADS):
        hs = slice(hh * d, (hh + 1) * d)
        q = qkv_ref[:, hs]
        k = qkv_ref[:, GDN_WIDTH + hh * d:GDN_WIDTH + (hh + 1) * d]
        v = qkv_ref[:, 2 * GDN_WIDTH + hh * d:2 * GDN_WIDTH + (hh + 1) * d]
        eg = jnp.exp(sm_ref[:, SM_G + hh:SM_G + hh + 1])
        beta = sm_ref[:, SM_BETA + hh:SM_BETA + hh + 1]
        s = s_ref[:, hh]
        k_col = col(k)
        v_new = beta * (v - eg * jnp.sum(k_col * s, axis=1))
        o = eg * jnp.sum(col(q) * s, axis=1) + jnp.sum(q * k, axis=1, keepdims=True) * v_new
        sout_ref[:, hh] = s * eg[:, :, None] + k_col * v_new[:, None, :]
        o_ref[:, hs] = _rms(o, gn_ref[...]) * _silu(z_ref[:, hs])


def _gdn_sample(state, gqkv, sm, gz, gn):
    nb = state.shape[0]
    bb = 2 * SUBLANES if nb % (2 * SUBLANES) == 0 else SUBLANES
    d = GDN_HEAD_DIM
    row_spec = lambda w: pl.BlockSpec((bb, w), lambda i: (i, 0))
    st_spec = pl.BlockSpec((bb, GDN_HEADS, d, d), lambda i: (i, 0, 0, 0))
    return pl.pallas_call(
        _gdn_sample_body,
        grid=(nb // bb,),
        in_specs=[st_spec, row_spec(3 * GDN_WIDTH), row_spec(LANES), row_spec(GDN_WIDTH),
                  pl.BlockSpec((1, d), lambda i: (0, 0))],
        out_specs=(st_spec, row_spec(GDN_WIDTH)),
        out_shape=(jax.ShapeDtypeStruct(state.shape, F32), jax.ShapeDtypeStruct((nb, GDN_WIDTH), F32)),
        compiler_params=pltpu.CompilerParams(dimension_semantics=("arbitrary",), vmem_limit_bytes=32 * MIB),
        name="gdn_sample",
    )(state, gqkv, sm, gz, gn)


def _pad_rows(a, rows):
    return jnp.concatenate([a, jnp.zeros((rows - a.shape[0],) + a.shape[1:], a.dtype)], axis=0)


def kernel(x_prompt, x_sample, cache_k, cache_v, cache_logf, state_gdn, state_gdn_conv, state_ffn_conv,
           page_table, norm_mix, w_in, b_forget, gdn_a_log, gdn_dt_bias, w_gdn_conv, gdn_out_norm, w_out,
           norm_ffn, w_up, w_ffn_conv, b_ffn_conv, w_down, norm_final):
    assert w_in.shape[0] == 1, "single-layer trunk"
    n_seq, seq_len, _ = x_prompt.shape
    nb = x_sample.shape[0]
    n_pool, page = cache_k.shape[1], cache_k.shape[2]

    w = jnp.transpose(w_in[0])
    o_ff = 3 * FOX_WIDTH
    o_g = o_ff + FOX_HEADS
    o_ga = o_g + 3 * GDN_WIDTH
    o_gz = o_ga + 2 * GDN_HEADS
    wfox, wgdn, wgz = w[:o_ff].astype(BF16), w[o_g:o_ga].astype(BF16), w[o_gz:].astype(BF16)
    wsm = jnp.concatenate([w[o_ff:o_g], w[o_ga:o_gz],
                           jnp.zeros((LANES - FOX_HEADS - 2 * GDN_HEADS, D_MODEL), F32)], axis=0).astype(BF16)
    par = jnp.zeros((SUBLANES, LANES), F32)
    par = par.at[0, SM_LF:SM_LF + FOX_HEADS].set(b_forget[0])
    par = par.at[0, SM_G:SM_G + GDN_HEADS].set(gdn_dt_bias[0])
    par = par.at[1, SM_G:SM_G + GDN_HEADS].set(gdn_a_log[0])
    nrm = norm_mix[0][None, :]
    cw = _pad_rows(w_gdn_conv[0], SUBLANES)
    gn = gdn_out_norm[0][None, :]
    wo = w_out[0].astype(BF16)
    nf = norm_ffn[0][None, :]
    wup = w_up[0].astype(BF16)
    cwf = _pad_rows(w_ffn_conv[0], SUBLANES)
    cbf = b_ffn_conv[0][None, :]
    wd = w_down[0].astype(BF16)
    nfin = norm_final[None, :]

    xs = x_sample.reshape(nb, D_MODEL)
    gctx = state_gdn_conv[0]
    fctx = state_ffn_conv[0]
    fk_s, fv_s, q_s, gqkv_s, gz_s, sm_s, pre_s = _proj_sample(
        xs, nrm, wfox, wgdn, wgz, wsm, par, cw, gctx[:, 0], gctx[:, 1], gctx[:, 2])
    heads = lambda a: a.reshape(nb, FOX_HEADS, FOX_HEAD_DIM)
    half = nb // 2
    decode_args = (page_table, heads(q_s), heads(fk_s), heads(fv_s), sm_s,
                   jnp.transpose(cache_k[0], (0, 2, 3, 1)), jnp.transpose(cache_v[0], (0, 2, 3, 1)),
                   jnp.transpose(cache_logf[0], (0, 2, 1)))
    rider_a = functools.partial(_rider, *decode_args, 0, half)
    rider_b = functools.partial(_rider, *decode_args, half, nb - half)

    xp = x_prompt.reshape(n_seq * seq_len, D_MODEL)
    kt, vt, qt16, kaug, vt16, gqkv, gz, sm, smt, cst, fox_sa = _proj_prompt(
        xp, nrm, wfox, wgdn, wgz, wsm, par, cw, seq_len, rider_a)
    fox = _fox_prompt(qt16, kaug, vt16, n_seq, seq_len)
    gdn, s_p = _gdn_prompt(gqkv, sm, smt, gz, gn, n_seq, seq_len)
    yp, cstf, fox_sb = _ffn_prompt(xp, fox, gdn, wo, nf, wup, cwf, cbf, wd, nfin, seq_len, rider_b)

    fox_s = jnp.concatenate([fox_sa, fox_sb], axis=0)
    s_s, gdn_s = _gdn_sample(state_gdn[0], gqkv_s, sm_s, gz_s, gn)
    ys, gate_s = _ffn_sample(xs, fox_s.reshape(nb, FOX_WIDTH).astype(BF16), gdn_s.astype(BF16), wo, nf, wup,
                             cwf, cbf, wd, nfin, fctx[:, 0], fctx[:, 1])

    kv_shape_s = (1, nb, 1, FOX_HEADS, FOX_HEAD_DIM)
    new_kv_p = lambda a: jnp.transpose(a.reshape(1, n_seq, FOX_HEADS, FOX_HEAD_DIM, seq_len), (0, 1, 4, 2, 3))
    return (
        yp.reshape(n_seq, seq_len, D_MODEL),
        ys.reshape(nb, 1, D_MODEL),
        new_kv_p(kt),
        new_kv_p(vt),
        sm[:, SM_LF:SM_LF + FOX_HEADS].reshape(1, n_seq, seq_len, FOX_HEADS),
        s_p[None],
        cst.reshape(n_seq, SUBLANES, 3 * GDN_WIDTH)[None, :, SUBLANES - (GDN_CONV - 1):],
        cstf.reshape(n_seq, SUBLANES, FFN_DIM)[None, :, SUBLANES - (FFN_CONV - 1):],
        fk_s.reshape(kv_shape_s),
        fv_s.reshape(kv_shape_s),
        sm_s[:, SM_LF:SM_LF + FOX_HEADS].reshape(1, nb, 1, FOX_HEADS),
        s_s[None],
        jnp.concatenate([gctx[:, 1:], pre_s[:, None, :]], axis=1)[None],
        jnp.concatenate([fctx[:, 1:], gate_s[:, None, :]], axis=1)[None],
    )
```

```python
import functools

import jax
import jax.numpy as jnp
from jax import lax
from jax.experimental import pallas as pl
from jax.experimental.pallas import tpu as pltpu

D_MODEL = 1024
FOX_HEADS = 8
FOX_HEAD_DIM = 64
FOX_WIDTH = FOX_HEADS * FOX_HEAD_DIM
GDN_HEADS = 4
GDN_HEAD_DIM = 128
GDN_WIDTH = GDN_HEADS * GDN_HEAD_DIM
GDN_CONV = 4
GDN_CHUNK = 64
FFN_DIM = 2816
FFN_CONV = 3
EPS = 1e-6
NEG_BIG = -1e30
FOX_SCALE = FOX_HEAD_DIM ** -0.5
GDN_SCALE = GDN_HEAD_DIM ** -0.5
LOG2E = 1.4426950408889634
FOX_QSCALE = FOX_SCALE * LOG2E
FOX_AUG = 3
FOX_DEN_ROWS = 16
FOX_GROUP = 2
GDN_TILE = 128
GDN_GROUP = 4

LANES = 128
SUBLANES = 8
ROW_TILE = 256
FOX_TILE = 256
MIB = 1024 * 1024

SM_LF = 0
SM_G = 8
SM_BETA = 12
SM_CUM = 16
SM_GCUM = 24

F32 = jnp.float32
BF16 = jnp.bfloat16


def _sigmoid(x):
    return 1.0 / (1.0 + jnp.exp(-x))


def _silu(x):
    return x * _sigmoid(x)


def _rms(x, g):
    return x * lax.rsqrt(jnp.mean(x * x, axis=-1, keepdims=True) + EPS) * g


def _dot(a, b):
    return jnp.dot(a, b, preferred_element_type=F32)


def _dot_nt(a, b):
    return lax.dot_general(a, b, (((1,), (1,)), ((), ())), preferred_element_type=F32)


def _shift_rows(x, prev8, k):
    r = pltpu.roll(x, k, 0)
    row8 = lax.broadcasted_iota(jnp.int32, prev8.shape, 0)
    top = jnp.where(row8 < k, pltpu.roll(prev8, k, 0), r[0:SUBLANES])
    return jnp.concatenate([top, r[SUBLANES:]], axis=0)


def _gate_activations(raw, par_ref):
    z = raw + par_ref[0:1, :]
    lane = lax.broadcasted_iota(jnp.int32, z.shape, 1)
    t = jnp.log1p(jnp.exp(-jnp.abs(z)))
    lf = jnp.minimum(z, 0.0) - t
    softplus = jnp.maximum(z, 0.0) + t
    g = -jnp.exp(par_ref[1:2, :]) * softplus
    beta = _sigmoid(z)
    return jnp.where(lane < SM_G, lf, jnp.where(lane < SM_BETA, g, jnp.where(lane < SM_CUM, beta, 0.0)))


def _gdn_qkv_norm(c, out_ref):
    for part, scale in ((0, GDN_SCALE), (1, 1.0)):
        for hh in range(GDN_HEADS):
            off = part * GDN_WIDTH + hh * GDN_HEAD_DIM
            seg = c[:, off:off + GDN_HEAD_DIM]
            n = lax.rsqrt(jnp.sum(seg * seg, axis=-1, keepdims=True) + EPS)
            out_ref[:, off:off + GDN_HEAD_DIM] = seg * n * scale
    out_ref[:, 2 * GDN_WIDTH:] = c[:, 2 * GDN_WIDTH:]


def _proj_prompt_body(pt_ref, *refs, tiles_per_seq, rider_cfg):
    (x_ref, nrm_ref, wfox_ref, wgdn_ref, wgz_ref, wsm_ref, par_ref, cw_ref, scat_ref) = refs[:9]
    rider_in = refs[9:9 + RIDER_INPUTS]
    (kt_ref, vt_ref, qt16_ref, kaug_ref, vt16_ref, gqkv_ref, gz_ref, sm_ref, smt_ref, cst_ref,
     rider_out, carry_ref, prev_ref) = refs[9 + RIDER_INPUTS:22 + RIDER_INPUTS]
    rider_scratch = refs[22 + RIDER_INPUTS:]
    i = pl.program_id(0)

    @pl.when(i % tiles_per_seq == 0)
    def _():
        carry_ref[...] = jnp.zeros_like(carry_ref)
        prev_ref[...] = jnp.zeros_like(prev_ref)

    _rider_step(pt_ref, rider_in, rider_out, rider_scratch, rider_cfg)

    tm = x_ref.shape[0]
    h16 = _rms(x_ref[...], nrm_ref[...]).astype(BF16)

    fox = _dot_nt(h16, wfox_ref[...])
    fq, fk, fv = fox[:, :FOX_WIDTH], fox[:, FOX_WIDTH:2 * FOX_WIDTH], fox[:, 2 * FOX_WIDTH:]
    fv_t = fv.T
    kt_ref[0] = fk.T
    vt_ref[0] = fv_t
    vt16_ref[0] = fv_t.astype(BF16)
    lane_p = lax.broadcasted_iota(jnp.int32, (tm, LANES), 1)
    ones_blk = jnp.where(lane_p < 2 * FOX_AUG, 1.0, 0.0)
    q_aug = []
    for p in range(FOX_HEADS // 2):
        q_aug += [fq[:, p * LANES:(p + 1) * LANES] * FOX_QSCALE, ones_blk]
        kaug_ref[:, 2 * p * LANES:(2 * p + 1) * LANES] = fk[:, p * LANES:(p + 1) * LANES].astype(BF16)
    qt16_ref[0] = jnp.concatenate(q_aug, axis=1).T.astype(BF16)

    act = _gate_activations(_dot_nt(h16, wsm_ref[...]), par_ref)
    row = lax.broadcasted_iota(jnp.int32, act.shape, 0)
    lane = lax.broadcasted_iota(jnp.int32, act.shape, 1)
    y = act
    yc = act
    k = 1
    while k < tm:
        y = y + jnp.where(row >= k, pltpu.roll(y, k, 0), 0.0)
        if k < GDN_CHUNK:
            yc = yc + jnp.where((row & (GDN_CHUNK - 1)) >= k, pltpu.roll(yc, k, 0), 0.0)
        k *= 2
    y = y + carry_ref[0:1, :]
    carry_ref[0:1, :] = y[tm - 1:tm, :]
    terms = jnp.concatenate(_split3(y * (-LOG2E)), axis=1)
    extra = _dot(terms, scat_ref[...]).astype(BF16)
    for p in range(FOX_HEADS // 2):
        kaug_ref[:, (2 * p + 1) * LANES:(2 * p + 2) * LANES] = extra[:, p * LANES:(p + 1) * LANES]
    shift = SM_CUM - SM_LF
    sm = jnp.where(lane < SM_CUM, act,
                   jnp.where(lane < SM_GCUM, pltpu.roll(y, shift, 1),
                             jnp.where(lane < SM_GCUM + GDN_HEADS, pltpu.roll(yc, shift, 1), 0.0)))
    sm_ref[...] = sm
    smt_ref[0] = sm.T

    pre = _dot_nt(h16, wgdn_ref[...])
    assert GDN_CONV == 4
    s1 = _shift_rows(pre, prev_ref[0:SUBLANES, :], 1)
    far = pre * cw_ref[1:2, :] + s1 * cw_ref[0:1, :]
    acc = pre * cw_ref[3:4, :] + s1 * cw_ref[2:3, :] + _shift_rows(far, prev_ref[SUBLANES:, :], 2)
    prev_ref[0:SUBLANES, :] = pre[tm - SUBLANES:, :]
    prev_ref[SUBLANES:, :] = far[tm - SUBLANES:, :]
    cst_ref[...] = pre[tm - SUBLANES:, :]
    _gdn_qkv_norm(_silu(acc), gqkv_ref)

    gz_ref[...] = _dot_nt(h16, wgz_ref[...])


def _proj_prompt(x, nrm, wfox, wgdn, wgz, wsm, par, cw, seq_len, make_rider):
    rows = x.shape[0]
    tm = ROW_TILE
    n_seq = rows // seq_len
    tiles_per_seq = seq_len // tm
    row_spec = lambda w: pl.BlockSpec((tm, w), lambda i, pt: (i, 0))
    col_spec = lambda w: pl.BlockSpec((1, w, tm), lambda i, pt: (i // tiles_per_seq, 0, i % tiles_per_seq))
    const = lambda s: pl.BlockSpec(s, lambda i, pt: (0, 0), pipeline_mode=pl.Buffered(1))
    rider = make_rider(rows // tm)
    aug_width = FOX_HEADS * LANES
    src = jnp.arange(3 * LANES)
    term, head = src // LANES, src % LANES
    dst = (head // 2) * LANES + (head % 2) * FOX_AUG + term
    scat = ((jnp.arange(FOX_WIDTH)[None, :] == dst[:, None]) & (head[:, None] < FOX_HEADS)).astype(BF16)
    out_shape = (
        jax.ShapeDtypeStruct((n_seq, FOX_WIDTH, seq_len), F32),
        jax.ShapeDtypeStruct((n_seq, FOX_WIDTH, seq_len), F32),
        jax.ShapeDtypeStruct((n_seq, aug_width, seq_len), BF16),
        jax.ShapeDtypeStruct((rows, aug_width), BF16),
        jax.ShapeDtypeStruct((n_seq, FOX_WIDTH, seq_len), BF16),
        jax.ShapeDtypeStruct((rows, 3 * GDN_WIDTH), F32),
        jax.ShapeDtypeStruct((rows, GDN_WIDTH), F32),
        jax.ShapeDtypeStruct((rows, LANES), F32),
        jax.ShapeDtypeStruct((n_seq, LANES, seq_len), F32),
        jax.ShapeDtypeStruct((n_seq * SUBLANES, 3 * GDN_WIDTH), F32),
    )
    out_specs = (
        col_spec(FOX_WIDTH), col_spec(FOX_WIDTH), col_spec(aug_width), row_spec(aug_width),
        col_spec(FOX_WIDTH), row_spec(3 * GDN_WIDTH), row_spec(GDN_WIDTH), row_spec(LANES),
        col_spec(LANES),
        pl.BlockSpec((SUBLANES, 3 * GDN_WIDTH), lambda i, pt: (i // tiles_per_seq, 0)),
    )
    grid_spec = pltpu.PrefetchScalarGridSpec(
        num_scalar_prefetch=1,
        grid=(rows // tm,),
        in_specs=[row_spec(D_MODEL), const((1, D_MODEL)), const(wfox.shape), const(wgdn.shape), const(wgz.shape),
                  const(wsm.shape), const((SUBLANES, LANES)), const((SUBLANES, 3 * GDN_WIDTH)),
                  const(scat.shape)] + rider["in_specs"],
        out_specs=out_specs + (rider["out_spec"],),
        scratch_shapes=[pltpu.VMEM((SUBLANES, LANES), F32), pltpu.VMEM((2 * SUBLANES, 3 * GDN_WIDTH), F32)]
        + rider["scratch"],
    )
    return pl.pallas_call(
        functools.partial(_proj_prompt_body, tiles_per_seq=tiles_per_seq, rider_cfg=rider["cfg"]),
        grid_spec=grid_spec,
        out_shape=out_shape + (rider["out_shape"],),
        compiler_params=pltpu.CompilerParams(dimension_semantics=("arbitrary",), vmem_limit_bytes=56 * MIB),
        name="proj_prompt",
    )(rider["prefetch"], x, nrm, wfox, wgdn, wgz, wsm, par, cw, scat, *rider["operands"])


def _proj_sample_body(x_ref, nrm_ref, wfox_ref, wgdn_ref, wgz_ref, wsm_ref, par_ref, cw_ref, c0_ref, c1_ref, c2_ref,
                      fk_ref, fv_ref, q_ref, gqkv_ref, gz_ref, sm_ref, pre_ref):
    h16 = _rms(x_ref[...], nrm_ref[...]).astype(BF16)
    fox = _dot_nt(h16, wfox_ref[...])
    q_ref[...] = fox[:, :FOX_WIDTH] * FOX_SCALE
    fk_ref[...] = fox[:, FOX_WIDTH:2 * FOX_WIDTH]
    fv_ref[...] = fox[:, 2 * FOX_WIDTH:]
    sm_ref[...] = _gate_activations(_dot_nt(h16, wsm_ref[...]), par_ref)
    pre = _dot_nt(h16, wgdn_ref[...])
    pre_ref[...] = pre
    acc = (pre * cw_ref[3:4, :] + c2_ref[...] * cw_ref[2:3, :]
           + c1_ref[...] * cw_ref[1:2, :] + c0_ref[...] * cw_ref[0:1, :])
    _gdn_qkv_norm(_silu(acc), gqkv_ref)
    gz_ref[...] = _dot_nt(h16, wgz_ref[...])


def _proj_sample(x, nrm, wfox, wgdn, wgz, wsm, par, cw, c0, c1, c2):
    rows = x.shape[0]
    full = lambda a: pl.BlockSpec(a.shape, lambda i: (0,) * a.ndim)
    args = (x, nrm, wfox, wgdn, wgz, wsm, par, cw, c0, c1, c2)
    shapes = ((rows, FOX_WIDTH), (rows, FOX_WIDTH), (rows, FOX_WIDTH), (rows, 3 * GDN_WIDTH),
              (rows, GDN_WIDTH), (rows, LANES), (rows, 3 * GDN_WIDTH))
    return pl.pallas_call(
        _proj_sample_body,
        grid=(1,),
        in_specs=[full(a) for a in args],
        out_specs=tuple(pl.BlockSpec(s, lambda i: (0, 0)) for s in shapes),
        out_shape=tuple(jax.ShapeDtypeStruct(s, F32) for s in shapes),
        compiler_params=pltpu.CompilerParams(dimension_semantics=("arbitrary",), vmem_limit_bytes=48 * MIB),
        name="proj_sample",
    )(*args)


def _fox_prompt_body(qt_ref, kaug_ref, vt_ref, o_ref):
    i = pl.program_id(1)
    t = FOX_TILE
    hd = FOX_HEAD_DIM
    aug = 2 * LANES
    rr = lax.broadcasted_iota(jnp.int32, (t, t), 0)
    cc = lax.broadcasted_iota(jnp.int32, (t, t), 1)
    causal = rr <= cc
    row = lax.broadcasted_iota(jnp.int32, (aug, t), 0)
    chains = [(g, h) for g in range(qt_ref.shape[0]) for h in range(FOX_HEADS)]
    qs = []
    for g, h in chains:
        e = h % 2
        qp = qt_ref[g, (h // 2) * aug:(h // 2 + 1) * aug, :]
        own = jnp.logical_or(jnp.logical_and(row >= e * hd, row < (e + 1) * hd),
                             jnp.logical_and(row >= LANES + e * FOX_AUG, row < LANES + (e + 1) * FOX_AUG))
        qs.append(jnp.where(own, qp, jnp.zeros_like(qp)))

    def tile(j, carry, masked):
        koff = pl.multiple_of(j * t, t)
        ss = [_dot(kaug_ref[g, pl.ds(koff, t), (h // 2) * aug:(h // 2 + 1) * aug], qs[n])
              for n, (g, h) in enumerate(chains)]
        stats, pms = [], []
        for n in range(len(chains)):
            m = carry[n][0]
            s = jnp.where(causal, ss[n], NEG_BIG) if masked else ss[n]
            m_new = jnp.maximum(m, jnp.max(s, axis=0, keepdims=True))
            stats.append((m_new, jnp.exp2(m - m_new)))
            pms.append(jnp.exp2(s - m_new).astype(BF16))
        out = []
        for n, (g, h) in enumerate(chains):
            m_new, alpha = stats[n]
            v_ones = jnp.concatenate([vt_ref[g, h * hd:(h + 1) * hd, pl.ds(koff, t)], ones_rows], axis=0)
            out.append((m_new, alpha * carry[n][1] + _dot(v_ones, pms[n])))
        return tuple(out)

    ones_rows = jnp.ones((FOX_DEN_ROWS, t), BF16)
    init = tuple((jnp.full((1, t), NEG_BIG, F32), jnp.zeros((hd + FOX_DEN_ROWS, t), F32)) for _ in chains)
    carry = lax.fori_loop(0, i, lambda j, c: tile(j, c, False), init)
    final = tile(i, carry, True)
    for n in range(0, len(chains), 2):
        g, h = chains[n]
        outs = [a[:hd] / a[hd:hd + 1] for _, a in final[n:n + 2]]
        o_ref[g, :, (h // 2) * LANES:(h // 2 + 1) * LANES] = jnp.concatenate(outs, axis=0).T.astype(BF16)


def _fox_prompt(qt16, kaug, vt16, n_seq, seq_len):
    t = FOX_TILE
    nq = seq_len // t
    aug_width = kaug.shape[1]
    group = FOX_GROUP if n_seq % FOX_GROUP == 0 else 1
    out = pl.pallas_call(
        _fox_prompt_body,
        grid=(n_seq // group, nq),
        in_specs=[pl.BlockSpec((group, aug_width, t), lambda b, i: (b, 0, i)),
                  pl.BlockSpec((group, seq_len, aug_width), lambda b, i: (b, 0, 0)),
                  pl.BlockSpec((group, FOX_WIDTH, seq_len), lambda b, i: (b, 0, 0))],
        out_specs=pl.BlockSpec((group, t, FOX_WIDTH), lambda b, i: (b, i, 0)),
        out_shape=jax.ShapeDtypeStruct((n_seq, seq_len, FOX_WIDTH), BF16),
        compiler_params=pltpu.CompilerParams(dimension_semantics=("arbitrary", "arbitrary"),
                                             vmem_limit_bytes=48 * MIB),
        name="fox_prompt",
    )(qt16, kaug.reshape(n_seq, seq_len, aug_width), vt16)
    return out.reshape(n_seq * seq_len, FOX_WIDTH)


def _bdot(a, b):
    return lax.dot_general(a.astype(BF16), b.astype(BF16), (((2,), (1,)), ((0,), (0,))),
                           preferred_element_type=F32)


def _bdot_nt(a, b):
    return lax.dot_general(a.astype(BF16), b.astype(BF16), (((2,), (2,)), ((0,), (0,))),
                           preferred_element_type=F32)


def _unit_lower_inverse_minus_eye(a, r, c):
    blk16 = (r // 16) == (c // 16)
    blk32 = (r // 32) == (c // 32)
    p = jnp.where(blk16, -a, 0.0)
    dt = p
    for _ in range(3):
        p = _bdot(p, p)
        dt = dt + p + _bdot(dt, p)
    for off in (jnp.where(jnp.logical_and(blk32, jnp.logical_not(blk16)), a, 0.0),
                jnp.where(blk32, 0.0, a)):
        x = off + _bdot(dt, off)
        dt = dt - (x + _bdot(x, dt))
    return dt


def _gdn_prompt_body(qkv_ref, sm_ref, gcr_ref, z_ref, gn_ref, o_ref, sout_ref, s_ref):
    ci = pl.program_id(1)

    @pl.when(ci == 0)
    def _():
        s_ref[...] = jnp.zeros_like(s_ref)

    ch = GDN_CHUNK
    d = GDN_HEAD_DIM
    nh = GDN_HEADS
    n_seq = qkv_ref.shape[0]
    n_chunks = qkv_ref.shape[1] // ch
    units = [(cidx, g, hh) for cidx in range(n_chunks) for g in range(n_seq) for hh in range(nh)]
    per_chunk = n_seq * nh

    def gather(fn):
        return jnp.stack([fn(g, slice(cidx * ch, (cidx + 1) * ch), hh) for cidx, g, hh in units], axis=0)

    q = gather(lambda g, rows, hh: qkv_ref[g, rows, hh * d:(hh + 1) * d])
    k = gather(lambda g, rows, hh: qkv_ref[g, rows, GDN_WIDTH + hh * d:GDN_WIDTH + (hh + 1) * d])
    v = gather(lambda g, rows, hh: qkv_ref[g, rows, 2 * GDN_WIDTH + hh * d:2 * GDN_WIDTH + (hh + 1) * d])
    beta = gather(lambda g, rows, hh: sm_ref[g, rows, SM_BETA + hh:SM_BETA + hh + 1])
    gc = gather(lambda g, rows, hh: sm_ref[g, rows, SM_GCUM + hh:SM_GCUM + hh + 1])
    gr = gather(lambda g, rows, hh: gcr_ref[g, hh:hh + 1, rows])

    r = lax.broadcasted_iota(jnp.int32, (1, ch, ch), 1)
    c = lax.broadcasted_iota(jnp.int32, (1, ch, ch), 2)
    lower = r >= c
    beta = jnp.broadcast_to(beta, (len(units), ch, d))
    gc = jnp.broadcast_to(gc, (len(units), ch, d))
    decay = jnp.where(lower, jnp.exp(jnp.where(lower, gc[:, :, :ch] - gr, 0.0)), 0.0)
    qk_kk = _bdot_nt(jnp.concatenate([q, k], axis=1), k)
    qk = qk_kk[:, :ch] * decay
    a = jnp.where(r > c, beta[:, :, :ch] * qk_kk[:, ch:] * decay, 0.0)
    dt = _unit_lower_inverse_minus_eye(a, r, c)
    eg = jnp.exp(gc)
    rhs = jnp.concatenate([v * beta, k * (beta * eg)], axis=-1)
    uw = rhs + _bdot(dt, rhs)
    gl = gc[:, ch - 1:ch, :]
    qd = (q * eg).astype(BF16)
    kd = k * jnp.exp(gl - gc)
    g_last = jnp.exp(gl)
    qk16 = qk.astype(BF16)

    s = s_ref[...]
    for cidx in range(n_chunks):
        us = slice(cidx * per_chunk, (cidx + 1) * per_chunk)
        rows = slice(cidx * ch, (cidx + 1) * ch)
        s16 = s.astype(BF16)
        v_new = uw[us, :, :d] - _bdot(uw[us, :, d:], s16)
        o = _bdot(qd[us], s16) + _bdot(qk16[us], v_new)
        vn16 = v_new.astype(BF16)
        upd = jnp.stack([_dot(kd[cidx * per_chunk + j].T.astype(BF16), vn16[j]) for j in range(per_chunk)],
                        axis=0)
        s = s * g_last[us] + upd
        for g in range(n_seq):
            for hh in range(nh):
                hs = slice(hh * d, (hh + 1) * d)
                o_ref[g, rows, hs] = (_rms(o[g * nh + hh], gn_ref[...]) * _silu(z_ref[g, rows, hs])).astype(BF16)
    s_ref[...] = s

    @pl.when(ci == pl.num_programs(1) - 1)
    def _():
        sout_ref[...] = s.reshape(sout_ref.shape)


def _gdn_prompt(gqkv, sm, smt, gz, gn, n_seq, seq_len):
    tm = GDN_TILE
    nt = seq_len // tm
    gcum_block = SM_GCUM // SUBLANES
    group = next(g for g in (GDN_GROUP, 2, 1) if n_seq % g == 0)
    seq3 = lambda a: a.reshape(n_seq, seq_len, a.shape[-1])
    tile = lambda w: pl.BlockSpec((group, tm, w), lambda b, i: (b, i, 0))
    out, s_out = pl.pallas_call(
        _gdn_prompt_body,
        grid=(n_seq // group, nt),
        in_specs=[tile(3 * GDN_WIDTH), tile(LANES),
                  pl.BlockSpec((group, SUBLANES, tm), lambda b, i: (b, gcum_block, i)),
                  tile(GDN_WIDTH),
                  pl.BlockSpec((1, GDN_HEAD_DIM), lambda b, i: (0, 0))],
        out_specs=(tile(GDN_WIDTH),
                   pl.BlockSpec((group, GDN_HEADS, GDN_HEAD_DIM, GDN_HEAD_DIM), lambda b, i: (b, 0, 0, 0))),
        out_shape=(jax.ShapeDtypeStruct((n_seq, seq_len, GDN_WIDTH), BF16),
                   jax.ShapeDtypeStruct((n_seq, GDN_HEADS, GDN_HEAD_DIM, GDN_HEAD_DIM), F32)),
        scratch_shapes=[pltpu.VMEM((group * GDN_HEADS, GDN_HEAD_DIM, GDN_HEAD_DIM), F32)],
        compiler_params=pltpu.CompilerParams(dimension_semantics=("arbitrary", "arbitrary"),
                                             vmem_limit_bytes=40 * MIB),
        name="gdn_prompt",
    )(seq3(gqkv), seq3(sm), smt, seq3(gz), gn)
    return out.reshape(n_seq * seq_len, GDN_WIDTH), s_out


def _ffn_tail(x, fox_ref, gdn_ref, wo_ref, nf_ref, wup_ref, wd_ref, nfin_ref, conv):
    mix = _dot(fox_ref[...], wo_ref[0:FOX_WIDTH, :]) + _dot(gdn_ref[...], wo_ref[FOX_WIDTH:, :])
    x2 = x + mix
    h2 = _rms(x2, nf_ref[...]).astype(BF16)
    gu = _dot(h2, wup_ref[...])
    gate, up = gu[:, :FFN_DIM], gu[:, FFN_DIM:]
    act = (_silu(conv(gate)) * up).astype(BF16)
    x3 = x2 + _dot(act, wd_ref[...])
    return _rms(x3, nfin_ref[...])


def _ffn_prompt_body(pt_ref, *refs, tiles_per_seq, rider_cfg):
    (x_ref, fox_ref, gdn_ref, wo_ref, nf_ref, wup_ref, cw_ref, cb_ref, wd_ref, nfin_ref) = refs[:10]
    rider_in = refs[10:10 + RIDER_INPUTS]
    y_ref, cst_ref, rider_out, prev_ref = refs[10 + RIDER_INPUTS:14 + RIDER_INPUTS]
    rider_scratch = refs[14 + RIDER_INPUTS:]
    i = pl.program_id(0)

    @pl.when(i % tiles_per_seq == 0)
    def _():
        prev_ref[...] = jnp.zeros_like(prev_ref)

    _rider_step(pt_ref, rider_in, rider_out, rider_scratch, rider_cfg)

    tm = x_ref.shape[0]

    def conv(gate):
        prev = prev_ref[...]
        out = gate * cw_ref[FFN_CONV - 1:FFN_CONV, :] + cb_ref[...]
        for kk in range(1, FFN_CONV):
            out = out + _shift_rows(gate, prev, kk) * cw_ref[FFN_CONV - 1 - kk:FFN_CONV - kk, :]
        prev_ref[...] = gate[tm - SUBLANES:, :]
        cst_ref[...] = gate[tm - SUBLANES:, :]
        return out

    y_ref[...] = _ffn_tail(x_ref[...], fox_ref, gdn_ref, wo_ref, nf_ref, wup_ref, wd_ref, nfin_ref, conv)


def _ffn_prompt(x, fox, gdn, wo, nf, wup, cw, cb, wd, nfin, seq_len, make_rider):
    rows = x.shape[0]
    tm = ROW_TILE
    n_seq = rows // seq_len
    tiles_per_seq = seq_len // tm
    row_spec = lambda w: pl.BlockSpec((tm, w), lambda i, pt: (i, 0))
    const = lambda a: pl.BlockSpec(a.shape, lambda i, pt: (0, 0), pipeline_mode=pl.Buffered(1))
    rider = make_rider(rows // tm)
    grid_spec = pltpu.PrefetchScalarGridSpec(
        num_scalar_prefetch=1,
        grid=(rows // tm,),
        in_specs=[row_spec(D_MODEL), row_spec(FOX_WIDTH), row_spec(GDN_WIDTH), const(wo), const(nf),
                  const(wup), const(cw), const(cb), const(wd), const(nfin)] + rider["in_specs"],
        out_specs=(row_spec(D_MODEL),
                   pl.BlockSpec((SUBLANES, FFN_DIM), lambda i, pt: (i // tiles_per_seq, 0)),
                   rider["out_spec"]),
        scratch_shapes=[pltpu.VMEM((SUBLANES, FFN_DIM), F32)] + rider["scratch"],
    )
    return pl.pallas_call(
        functools.partial(_ffn_prompt_body, tiles_per_seq=tiles_per_seq, rider_cfg=rider["cfg"]),
        grid_spec=grid_spec,
        out_shape=(jax.ShapeDtypeStruct((rows, D_MODEL), F32),
                   jax.ShapeDtypeStruct((n_seq * SUBLANES, FFN_DIM), F32),
                   rider["out_shape"]),
        compiler_params=pltpu.CompilerParams(dimension_semantics=("arbitrary",), vmem_limit_bytes=58 * MIB),
        name="ffn_prompt",
    )(rider["prefetch"], x, fox, gdn, wo, nf, wup, cw, cb, wd, nfin, *rider["operands"])


def _ffn_sample_body(x_ref, fox_ref, gdn_ref, wo_ref, nf_ref, wup_ref, cw_ref, cb_ref, wd_ref, nfin_ref,
                     c0_ref, c1_ref, y_ref, gate_ref):
    def conv(gate):
        gate_ref[...] = gate
        return (gate * cw_ref[2:3, :] + c1_ref[...] * cw_ref[1:2, :] + c0_ref[...] * cw_ref[0:1, :]
                + cb_ref[...])

    y_ref[...] = _ffn_tail(x_ref[...], fox_ref, gdn_ref, wo_ref, nf_ref, wup_ref, wd_ref, nfin_ref, conv)


def _ffn_sample(x, fox, gdn, wo, nf, wup, cw, cb, wd, nfin, c0, c1):
    rows = x.shape[0]
    args = (x, fox, gdn, wo, nf, wup, cw, cb, wd, nfin, c0, c1)
    full = lambda a: pl.BlockSpec(a.shape, lambda i: (0, 0), pipeline_mode=pl.Buffered(1))
    return pl.pallas_call(
        _ffn_sample_body,
        grid=(1,),
        in_specs=[full(a) for a in args],
        out_specs=(pl.BlockSpec((rows, D_MODEL), lambda i: (0, 0)),
                   pl.BlockSpec((rows, FFN_DIM), lambda i: (0, 0))),
        out_shape=(jax.ShapeDtypeStruct((rows, D_MODEL), F32), jax.ShapeDtypeStruct((rows, FFN_DIM), F32)),
        compiler_params=pltpu.CompilerParams(dimension_semantics=("arbitrary",), vmem_limit_bytes=56 * MIB),
        name="ffn_sample",
    )(*args)


def _split3(x):
    hi = x.astype(BF16)
    r1 = x - hi.astype(F32)
    mid = r1.astype(BF16)
    lo = (r1 - mid.astype(F32)).astype(BF16)
    return hi, mid, lo


RIDER_INPUTS = 8
RIDER_SCRATCH = 4


def _rider_step(pt_ref, rider_in, o_ref, rider_scratch, cfg):
    q_ref, kn_ref, vn_ref, sm_ref, msuf_ref, kt_pool, vt_pool, lf_pool = rider_in
    kbuf, vbuf, lbuf, sems = rider_scratch
    base, count, per_step, n_pages = cfg
    i = pl.program_id(0)

    def copies(g, sl):
        out = []
        for j in range(n_pages):
            pg = pt_ref[g * n_pages + j]
            out.append(pltpu.make_async_copy(kt_pool.at[pg], kbuf.at[sl, j], sems.at[sl, 0]))
            out.append(pltpu.make_async_copy(vt_pool.at[pg], vbuf.at[sl, j], sems.at[sl, 1]))
            out.append(pltpu.make_async_copy(lf_pool.at[pg], lbuf.at[sl, j], sems.at[sl, 2]))
        return out

    @pl.when(i == 0)
    def _():
        for cp in copies(base, base % 2):
            cp.start()

    for r in range(per_step):
        g = base + i * per_step + r
        slot = g % 2

        @pl.when(g + 1 < base + count)
        def _():
            for cp in copies(g + 1, 1 - slot):
                cp.start()

        for cp in copies(g, slot):
            cp.wait()
        o_ref[r] = _decode_row(slot, q_ref[r], kn_ref[r], vn_ref[r], sm_ref[r], msuf_ref[...],
                               kbuf, vbuf, lbuf, n_pages)


def _decode_row(slot, q, k_new, v_new, sm_row, msuf, kbuf, vbuf, lbuf, n_pages):
    hd = FOX_HEAD_DIM
    page = kbuf.shape[-1]

    r8 = lax.broadcasted_iota(jnp.int32, (SUBLANES, LANES), 0)
    c8 = lax.broadcasted_iota(jnp.int32, (SUBLANES, LANES), 1)
    lf_new = jnp.sum(jnp.where(r8 == c8, sm_row, 0.0), axis=1, keepdims=True)
    n_rows = n_pages * FOX_HEADS
    lf = lbuf[slot].reshape(n_rows, page)
    within = functools.reduce(lambda a, c: a + c, [_dot(t, msuf) for t in _split3(lf)])
    tot = jnp.broadcast_to(jnp.sum(lf, axis=1, keepdims=True), (n_rows, page))
    row = lax.broadcasted_iota(jnp.int32, (n_rows, page), 0)
    later = tot
    k = FOX_HEADS
    while k < n_rows:
        later = later + jnp.where(row + k < n_rows, pltpu.roll(later, n_rows - k, 0), 0.0)
        k *= 2
    bias = (within + (later - tot)).reshape(n_pages, FOX_HEADS, page) + lf_new[None]

    q_t = q.T
    qb = jnp.stack([jnp.broadcast_to(q_t[:, h:h + 1], (hd, page)) for h in range(FOX_HEADS)], axis=0)
    logits = [jnp.sum(kbuf[slot, j] * qb, axis=1) + bias[j] for j in range(n_pages)]
    s_new = jnp.sum(k_new * q, axis=-1, keepdims=True)
    m = functools.reduce(jnp.maximum, [jnp.max(x, axis=-1, keepdims=True) for x in logits] + [s_new])
    ps = [jnp.exp(x - m) for x in logits]
    p_new = jnp.exp(s_new - m)
    l = functools.reduce(lambda a, c: a + c, [jnp.sum(x, axis=-1, keepdims=True) for x in ps] + [p_new])

    cols = []
    for h in range(FOX_HEADS):
        acc = vbuf[slot, 0, h] * ps[0][h:h + 1, :]
        for j in range(1, n_pages):
            acc = acc + vbuf[slot, j, h] * ps[j][h:h + 1, :]
        cols.append(jnp.sum(acc, axis=1, keepdims=True))
    mat = jnp.concatenate(cols + [jnp.zeros((hd, LANES - FOX_HEADS), F32)], axis=1)
    return (mat.T[0:FOX_HEADS, :] + p_new * v_new) / l


def _rider(page_table, q, kn, vn, sm, kt_pool, vt_pool, lf_pool, base, count, n_steps):
    n_pages = page_table.shape[1]
    page = kt_pool.shape[-1]
    per_step = count // n_steps
    assert per_step * n_steps == count, "sample rows must split evenly over the host's grid steps"
    tok = jnp.arange(page)
    msuf = (tok[:, None] > tok[None, :]).astype(BF16)
    rows = lambda a: a[base:base + count]
    tile = lambda a: pl.BlockSpec((per_step,) + a.shape[1:], lambda i, pt: (i,) + (0,) * (a.ndim - 1))
    hbm = pl.BlockSpec(memory_space=pl.ANY)
    operands = [rows(q), rows(kn), rows(vn), rows(sm).reshape(count, 1, LANES), msuf, kt_pool, vt_pool, lf_pool]
    kv_buf = pltpu.VMEM((2, n_pages, FOX_HEADS, FOX_HEAD_DIM, page), F32)
    return dict(
        prefetch=page_table.reshape(-1),
        operands=operands,
        in_specs=[tile(a) for a in operands[:4]] + [pl.BlockSpec(msuf.shape, lambda i, pt: (0, 0)), hbm, hbm, hbm],
        out_spec=pl.BlockSpec((per_step, FOX_HEADS, FOX_HEAD_DIM), lambda i, pt: (i, 0, 0)),
        out_shape=jax.ShapeDtypeStruct((count, FOX_HEADS, FOX_HEAD_DIM), F32),
        scratch=[kv_buf, kv_buf, pltpu.VMEM((2, n_pages, FOX_HEADS, page), F32), pltpu.SemaphoreType.DMA((2, 3))],
        cfg=(base, count, per_step, n_pages),
    )


def _gdn_sample_body(s_ref, qkv_ref, sm_ref, z_ref, gn_ref, sout_ref, o_ref):
    d = GDN_HEAD_DIM
    eye = lax.broadcasted_iota(jnp.int32, (d, d), 0) == lax.broadcasted_iota(jnp.int32, (d, d), 1)

    def col(rows):
        return jnp.sum(jnp.where(eye[None], rows[:, None, :], 0.0), axis=2, keepdims=True)

    for hh in range(GDN_HEADS):
        hs = slice(hh * d, (hh + 1) * d)
        q = qkv_ref[:, hs]
        k = qkv_ref[:, GDN_WIDTH + hh * d:GDN_WIDTH + (hh + 1) * d]
        v = qkv_ref[:, 2 * GDN_WIDTH + hh * d:2 * GDN_WIDTH + (hh + 1) * d]
        eg = jnp.exp(sm_ref[:, SM_G + hh:SM_G + hh + 1])
        beta = sm_ref[:, SM_BETA + hh:SM_BETA + hh + 1]
        s = s_ref[:, hh]
        k_col = col(k)
        v_new = beta * (v - eg * jnp.sum(k_col * s, axis=1))
        o = eg * jnp.sum(col(q) * s, axis=1) + jnp.sum(q * k, axis=1, keepdims=True) * v_new
        sout_ref[:, hh] = s * eg[:, :, None] + k_col * v_new[:, None, :]
        o_ref[:, hs] = _rms(o, gn_ref[...]) * _silu(z_ref[:, hs])


def _gdn_sample(state, gqkv, sm, gz, gn):
    nb = state.shape[0]
    bb = 2 * SUBLANES if nb % (2 * SUBLANES) == 0 else SUBLANES
    d = GDN_HEAD_DIM
    row_spec = lambda w: pl.BlockSpec((bb, w), lambda i: (i, 0))
    st_spec = pl.BlockSpec((bb, GDN_HEADS, d, d), lambda i: (i, 0, 0, 0))
    return pl.pallas_call(
        _gdn_sample_body,
        grid=(nb // bb,),
        in_specs=[st_spec, row_spec(3 * GDN_WIDTH), row_spec(LANES), row_spec(GDN_WIDTH),
                  pl.BlockSpec((1, d), lambda i: (0, 0))],
        out_specs=(st_spec, row_spec(GDN_WIDTH)),
        out_shape=(jax.ShapeDtypeStruct(state.shape, F32), jax.ShapeDtypeStruct((nb, GDN_WIDTH), F32)),
        compiler_params=pltpu.CompilerParams(dimension_semantics=("arbitrary",), vmem_limit_bytes=32 * MIB),
        name="gdn_sample",
    )(state, gqkv, sm, gz, gn)


def _pad_rows(a, rows):
    return jnp.concatenate([a, jnp.zeros((rows - a.shape[0],) + a.shape[1:], a.dtype)], axis=0)


def kernel(x_prompt, x_sample, cache_k, cache_v, cache_logf, state_gdn, state_gdn_conv, state_ffn_conv,
           page_table, norm_mix, w_in, b_forget, gdn_a_log, gdn_dt_bias, w_gdn_conv, gdn_out_norm, w_out,
           norm_ffn, w_up, w_ffn_conv, b_ffn_conv, w_down, norm_final):
    assert w_in.shape[0] == 1, "single-layer trunk"
    n_seq, seq_len, _ = x_prompt.shape
    nb = x_sample.shape[0]
    n_pool, page = cache_k.shape[1], cache_k.shape[2]

    w = jnp.transpose(w_in[0])
    o_ff = 3 * FOX_WIDTH
    o_g = o_ff + FOX_HEADS
    o_ga = o_g + 3 * GDN_WIDTH
    o_gz = o_ga + 2 * GDN_HEADS
    wfox, wgdn, wgz = w[:o_ff].astype(BF16), w[o_g:o_ga].astype(BF16), w[o_gz:].astype(BF16)
    wsm = jnp.concatenate([w[o_ff:o_g], w[o_ga:o_gz],
                           jnp.zeros((LANES - FOX_HEADS - 2 * GDN_HEADS, D_MODEL), F32)], axis=0).astype(BF16)
    par = jnp.zeros((SUBLANES, LANES), F32)
    par = par.at[0, SM_LF:SM_LF + FOX_HEADS].set(b_forget[0])
    par = par.at[0, SM_G:SM_G + GDN_HEADS].set(gdn_dt_bias[0])
    par = par.at[1, SM_G:SM_G + GDN_HEADS].set(gdn_a_log[0])
    nrm = norm_mix[0][None, :]
    cw = _pad_rows(w_gdn_conv[0], SUBLANES)
    gn = gdn_out_norm[0][None, :]
    wo = w_out[0].astype(BF16)
    nf = norm_ffn[0][None, :]
    wup = w_up[0].astype(BF16)
    cwf = _pad_rows(w_ffn_conv[0], SUBLANES)
    cbf = b_ffn_conv[0][None, :]
    wd = w_down[0].astype(BF16)
    nfin = norm_final[None, :]

    xs = x_sample.reshape(nb, D_MODEL)
    gctx = state_gdn_conv[0]
    fctx = state_ffn_conv[0]
    fk_s, fv_s, q_s, gqkv_s, gz_s, sm_s, pre_s = _proj_sample(
        xs, nrm, wfox, wgdn, wgz, wsm, par, cw, gctx[:, 0], gctx[:, 1], gctx[:, 2])
    heads = lambda a: a.reshape(nb, FOX_HEADS, FOX_HEAD_DIM)
    half = nb // 2
    decode_args = (page_table, heads(q_s), heads(fk_s), heads(fv_s), sm_s,
                   jnp.transpose(cache_k[0], (0, 2, 3, 1)), jnp.transpose(cache_v[0], (0, 2, 3, 1)),
                   jnp.transpose(cache_logf[0], (0, 2, 1)))
    rider_a = functools.partial(_rider, *decode_args, 0, half)
    rider_b = functools.partial(_rider, *decode_args, half, nb - half)

    xp = x_prompt.reshape(n_seq * seq_len, D_MODEL)
    kt, vt, qt16, kaug, vt16, gqkv, gz, sm, smt, cst, fox_sa = _proj_prompt(
        xp, nrm, wfox, wgdn, wgz, wsm, par, cw, seq_len, rider_a)
    fox = _fox_prompt(qt16, kaug, vt16, n_seq, seq_len)
    gdn, s_p = _gdn_prompt(gqkv, sm, smt, gz, gn, n_seq, seq_len)
    yp, cstf, fox_sb = _ffn_prompt(xp, fox, gdn, wo, nf, wup, cwf, cbf, wd, nfin, seq_len, rider_b)

    fox_s = jnp.concatenate([fox_sa, fox_sb], axis=0)
    s_s, gdn_s = _gdn_sample(state_gdn[0], gqkv_s, sm_s, gz_s, gn)
    ys, gate_s = _ffn_sample(xs, fox_s.reshape(nb, FOX_WIDTH).astype(BF16), gdn_s.astype(BF16), wo, nf, wup,
                             cwf, cbf, wd, nfin, fctx[:, 0], fctx[:, 1])

    kv_shape_s = (1, nb, 1, FOX_HEADS, FOX_HEAD_DIM)
    new_kv_p = lambda a: jnp.transpose(a.reshape(1, n_seq, FOX_HEADS, FOX_HEAD_DIM, seq_len), (0, 1, 4, 2, 3))
    return (
        yp.reshape(n_seq, seq_len, D_MODEL),
        ys.reshape(nb, 1, D_MODEL),
        new_kv_p(kt),
        new_kv_p(vt),
        sm[:, SM_LF:SM_LF + FOX_HEADS].reshape(1, n_seq, seq_len, FOX_HEADS),
        s_p[None],
        cst.reshape(n_seq, SUBLANES, 3 * GDN_WIDTH)[None, :, SUBLANES - (GDN_CONV - 1):],
        cstf.reshape(n_seq, SUBLANES, FFN_DIM)[None, :, SUBLANES - (FFN_CONV - 1):],
        fk_s.reshape(kv_shape_s),
        fv_s.reshape(kv_shape_s),
        sm_s[:, SM_LF:SM_LF + FOX_HEADS].reshape(1, nb, 1, FOX_HEADS),
        s_s[None],
        jnp.concatenate([gctx[:, 1:], pre_s[:, None, :]], axis=1)[None],
        jnp.concatenate([fctx[:, 1:], gate_s[:, None, :]], axis=1)[None],
    )
```

```python
import functools

import jax
import jax.numpy as jnp
from jax import lax
from jax.experimental import pallas as pl
from jax.experimental.pallas import tpu as pltpu

D_MODEL = 1024
FOX_HEADS = 8
FOX_HEAD_DIM = 64
FOX_WIDTH = FOX_HEADS * FOX_HEAD_DIM
GDN_HEADS = 4
GDN_HEAD_DIM = 128
GDN_WIDTH = GDN_HEADS * GDN_HEAD_DIM
GDN_CONV = 4
GDN_CHUNK = 64
FFN_DIM = 2816
FFN_CONV = 3
EPS = 1e-6
NEG_BIG = -1e30
FOX_SCALE = FOX_HEAD_DIM ** -0.5
GDN_SCALE = GDN_HEAD_DIM ** -0.5
LOG2E = 1.4426950408889634
FOX_QSCALE = FOX_SCALE * LOG2E
FOX_AUG = 3
FOX_DEN_ROWS = 16
FOX_GROUP = 2
GDN_TILE = 128
GDN_GROUP = 4

LANES = 128
SUBLANES = 8
ROW_TILE = 256
FOX_TILE = 256
MIB = 1024 * 1024

SM_LF = 0
SM_G = 8
SM_BETA = 12
SM_CUM = 16
SM_GCUM = 24

F32 = jnp.float32
BF16 = jnp.bfloat16


def _sigmoid(x):
    return 1.0 / (1.0 + jnp.exp(-x))


def _silu(x):
    return x * _sigmoid(x)


def _rms(x, g):
    return x * lax.rsqrt(jnp.mean(x * x, axis=-1, keepdims=True) + EPS) * g


def _dot(a, b):
    return jnp.dot(a, b, preferred_element_type=F32)


def _dot_nt(a, b):
    return lax.dot_general(a, b, (((1,), (1,)), ((), ())), preferred_element_type=F32)


def _shift_rows(x, prev8, k):
    r = pltpu.roll(x, k, 0)
    row8 = lax.broadcasted_iota(jnp.int32, prev8.shape, 0)
    top = jnp.where(row8 < k, pltpu.roll(prev8, k, 0), r[0:SUBLANES])
    return jnp.concatenate([top, r[SUBLANES:]], axis=0)


def _gate_activations(raw, par_ref):
    z = raw + par_ref[0:1, :]
    lane = lax.broadcasted_iota(jnp.int32, z.shape, 1)
    t = jnp.log1p(jnp.exp(-jnp.abs(z)))
    lf = jnp.minimum(z, 0.0) - t
    softplus = jnp.maximum(z, 0.0) + t
    g = -jnp.exp(par_ref[1:2, :]) * softplus
    beta = _sigmoid(z)
    return jnp.where(lane < SM_G, lf, jnp.where(lane < SM_BETA, g, jnp.where(lane < SM_CUM, beta, 0.0)))


def _gdn_qkv_norm(c, out_ref):
    for part, scale in ((0, GDN_SCALE), (1, 1.0)):
        for hh in range(GDN_HEADS):
            off = part * GDN_WIDTH + hh * GDN_HEAD_DIM
            seg = c[:, off:off + GDN_HEAD_DIM]
            n = lax.rsqrt(jnp.sum(seg * seg, axis=-1, keepdims=True) + EPS)
            out_ref[:, off:off + GDN_HEAD_DIM] = seg * n * scale
    out_ref[:, 2 * GDN_WIDTH:] = c[:, 2 * GDN_WIDTH:]


def _proj_prompt_body(pt_ref, *refs, tiles_per_seq, rider_cfg):
    (x_ref, nrm_ref, wfox_ref, wgdn_ref, wgz_ref, wsm_ref, par_ref, cw_ref, scat_ref) = refs[:9]
    rider_in = refs[9:9 + RIDER_INPUTS]
    (kt_ref, vt_ref, qt16_ref, kaug_ref, vt16_ref, gqkv_ref, gz_ref, sm_ref, smt_ref, cst_ref,
     rider_out, carry_ref, prev_ref) = refs[9 + RIDER_INPUTS:22 + RIDER_INPUTS]
    rider_scratch = refs[22 + RIDER_INPUTS:]
    i = pl.program_id(0)

    @pl.when(i % tiles_per_seq == 0)
    def _():
        carry_ref[...] = jnp.zeros_like(carry_ref)
        prev_ref[...] = jnp.zeros_like(prev_ref)

    _rider_step(pt_ref, rider_in, rider_out, rider_scratch, rider_cfg)

    tm = x_ref.shape[0]
    h16 = _rms(x_ref[...], nrm_ref[...]).astype(BF16)

    fox = _dot_nt(h16, wfox_ref[...])
    fq, fk, fv = fox[:, :FOX_WIDTH], fox[:, FOX_WIDTH:2 * FOX_WIDTH], fox[:, 2 * FOX_WIDTH:]
    fv_t = fv.T
    kt_ref[0] = fk.T
    vt_ref[0] = fv_t
    vt16_ref[0] = fv_t.astype(BF16)
    lane_p = lax.broadcasted_iota(jnp.int32, (tm, LANES), 1)
    ones_blk = jnp.where(lane_p < 2 * FOX_AUG, 1.0, 0.0)
    q_aug = []
    for p in range(FOX_HEADS // 2):
        q_aug += [fq[:, p * LANES:(p + 1) * LANES] * FOX_QSCALE, ones_blk]
        kaug_ref[:, 2 * p * LANES:(2 * p + 1) * LANES] = fk[:, p * LANES:(p + 1) * LANES].astype(BF16)
    qt16_ref[0] = jnp.concatenate(q_aug, axis=1).T.astype(BF16)

    act = _gate_activations(_dot_nt(h16, wsm_ref[...]), par_ref)
    row = lax.broadcasted_iota(jnp.int32, act.shape, 0)
    lane = lax.broadcasted_iota(jnp.int32, act.shape, 1)
    y = act
    yc = act
    k = 1
    while k < tm:
        y = y + jnp.where(row >= k, pltpu.roll(y, k, 0), 0.0)
        if k < GDN_CHUNK:
            yc = yc + jnp.where((row & (GDN_CHUNK - 1)) >= k, pltpu.roll(yc, k, 0), 0.0)
        k *= 2
    y = y + carry_ref[0:1, :]
    carry_ref[0:1, :] = y[tm - 1:tm, :]
    terms = jnp.concatenate(_split3(y * (-LOG2E)), axis=1)
    extra = _dot(terms, scat_ref[...]).astype(BF16)
    for p in range(FOX_HEADS // 2):
        kaug_ref[:, (2 * p + 1) * LANES:(2 * p + 2) * LANES] = extra[:, p * LANES:(p + 1) * LANES]
    shift = SM_CUM - SM_LF
    sm = jnp.where(lane < SM_CUM, act,
                   jnp.where(lane < SM_GCUM, pltpu.roll(y, shift, 1),
                             jnp.where(lane < SM_GCUM + GDN_HEADS, pltpu.roll(yc, shift, 1), 0.0)))
    sm_ref[...] = sm
    smt_ref[0] = sm.T

    pre = _dot_nt(h16, wgdn_ref[...])
    assert GDN_CONV == 4
    s1 = _shift_rows(pre, prev_ref[0:SUBLANES, :], 1)
    far = pre * cw_ref[1:2, :] + s1 * cw_ref[0:1, :]
    acc = pre * cw_ref[3:4, :] + s1 * cw_ref[2:3, :] + _shift_rows(far, prev_ref[SUBLANES:, :], 2)
    prev_ref[0:SUBLANES, :] = pre[tm - SUBLANES:, :]
    prev_ref[SUBLANES:, :] = far[tm - SUBLANES:, :]
    cst_ref[...] = pre[tm - SUBLANES:, :]
    _gdn_qkv_norm(_silu(acc), gqkv_ref)

    gz_ref[...] = _dot_nt(h16, wgz_ref[...])


def _proj_prompt(x, nrm, wfox, wgdn, wgz, wsm, par, cw, seq_len, make_rider):
    rows = x.shape[0]
    tm = ROW_TILE
    n_seq = rows // seq_len
    tiles_per_seq = seq_len // tm
    row_spec = lambda w: pl.BlockSpec((tm, w), lambda i, pt: (i, 0))
    col_spec = lambda w: pl.BlockSpec((1, w, tm), lambda i, pt: (i // tiles_per_seq, 0, i % tiles_per_seq))
    const = lambda s: pl.BlockSpec(s, lambda i, pt: (0, 0), pipeline_mode=pl.Buffered(1))
    rider = make_rider(rows // tm)
    aug_width = FOX_HEADS * LANES
    src = jnp.arange(3 * LANES)
    term, head = src // LANES, src % LANES
    dst = (head // 2) * LANES + (head % 2) * FOX_AUG + term
    scat = ((jnp.arange(FOX_WIDTH)[None, :] == dst[:, None]) & (head[:, None] < FOX_HEADS)).astype(BF16)
    out_shape = (
        jax.ShapeDtypeStruct((n_seq, FOX_WIDTH, seq_len), F32),
        jax.ShapeDtypeStruct((n_seq, FOX_WIDTH, seq_len), F32),
        jax.ShapeDtypeStruct((n_seq, aug_width, seq_len), BF16),
        jax.ShapeDtypeStruct((rows, aug_width), BF16),
        jax.ShapeDtypeStruct((n_seq, FOX_WIDTH, seq_len), BF16),
        jax.ShapeDtypeStruct((rows, 3 * GDN_WIDTH), F32),
        jax.ShapeDtypeStruct((rows, GDN_WIDTH), F32),
        jax.ShapeDtypeStruct((rows, LANES), F32),
        jax.ShapeDtypeStruct((n_seq, LANES, seq_len), F32),
        jax.ShapeDtypeStruct((n_seq * SUBLANES, 3 * GDN_WIDTH), F32),
    )
    out_specs = (
        col_spec(FOX_WIDTH), col_spec(FOX_WIDTH), col_spec(aug_width), row_spec(aug_width),
        col_spec(FOX_WIDTH), row_spec(3 * GDN_WIDTH), row_spec(GDN_WIDTH), row_spec(LANES),
        col_spec(LANES),
        pl.BlockSpec((SUBLANES, 3 * GDN_WIDTH), lambda i, pt: (i // tiles_per_seq, 0)),
    )
    grid_spec = pltpu.PrefetchScalarGridSpec(
        num_scalar_prefetch=1,
        grid=(rows // tm,),
        in_specs=[row_spec(D_MODEL), const((1, D_MODEL)), const(wfox.shape), const(wgdn.shape), const(wgz.shape),
                  const(wsm.shape), const((SUBLANES, LANES)), const((SUBLANES, 3 * GDN_WIDTH)),
                  const(scat.shape)] + rider["in_specs"],
        out_specs=out_specs + (rider["out_spec"],),
        scratch_shapes=[pltpu.VMEM((SUBLANES, LANES), F32), pltpu.VMEM((2 * SUBLANES, 3 * GDN_WIDTH), F32)]
        + rider["scratch"],
    )
    return pl.pallas_call(
        functools.partial(_proj_prompt_body, tiles_per_seq=tiles_per_seq, rider_cfg=rider["cfg"]),
        grid_spec=grid_spec,
        out_shape=out_shape + (rider["out_shape"],),
        compiler_params=pltpu.CompilerParams(dimension_semantics=("arbitrary",), vmem_limit_bytes=56 * MIB),
        name="proj_prompt",
    )(rider["prefetch"], x, nrm, wfox, wgdn, wgz, wsm, par, cw, scat, *rider["operands"])


def _proj_sample_body(x_ref, nrm_ref, wfox_ref, wgdn_ref, wgz_ref, wsm_ref, par_ref, cw_ref, c0_ref, c1_ref, c2_ref,
                      fk_ref, fv_ref, q_ref, gqkv_ref, gz_ref, sm_ref, pre_ref):
    h16 = _rms(x_ref[...], nrm_ref[...]).astype(BF16)
    fox = _dot_nt(h16, wfox_ref[...])
    q_ref[...] = fox[:, :FOX_WIDTH] * FOX_SCALE
    fk_ref[...] = fox[:, FOX_WIDTH:2 * FOX_WIDTH]
    fv_ref[...] = fox[:, 2 * FOX_WIDTH:]
    sm_ref[...] = _gate_activations(_dot_nt(h16, wsm_ref[...]), par_ref)
    pre = _dot_nt(h16, wgdn_ref[...])
    pre_ref[...] = pre
    acc = (pre * cw_ref[3:4, :] + c2_ref[...] * cw_ref[2:3, :]
           + c1_ref[...] * cw_ref[1:2, :] + c0_ref[...] * cw_ref[0:1, :])
    _gdn_qkv_norm(_silu(acc), gqkv_ref)
    gz_ref[...] = _dot_nt(h16, wgz_ref[...])


def _proj_sample(x, nrm, wfox, wgdn, wgz, wsm, par, cw, c0, c1, c2):
    rows = x.shape[0]
    full = lambda a: pl.BlockSpec(a.shape, lambda i: (0,) * a.ndim)
    args = (x, nrm, wfox, wgdn, wgz, wsm, par, cw, c0, c1, c2)
    shapes = ((rows, FOX_WIDTH), (rows, FOX_WIDTH), (rows, FOX_WIDTH), (rows, 3 * GDN_WIDTH),
              (rows, GDN_WIDTH), (rows, LANES), (rows, 3 * GDN_WIDTH))
    return pl.pallas_call(
        _proj_sample_body,
        grid=(1,),
        in_specs=[full(a) for a in args],
        out_specs=tuple(pl.BlockSpec(s, lambda i: (0, 0)) for s in shapes),
        out_shape=tuple(jax.ShapeDtypeStruct(s, F32) for s in shapes),
        compiler_params=pltpu.CompilerParams(dimension_semantics=("arbitrary",), vmem_limit_bytes=48 * MIB),
        name="proj_sample",
    )(*args)


def _fox_prompt_body(qt_ref, kaug_ref, vt_ref, o_ref):
    i = pl.program_id(1)
    t = FOX_TILE
    hd = FOX_HEAD_DIM
    aug = 2 * LANES
    rr = lax.broadcasted_iota(jnp.int32, (t, t), 0)
    cc = lax.broadcasted_iota(jnp.int32, (t, t), 1)
    causal = rr <= cc
    row = lax.broadcasted_iota(jnp.int32, (aug, t), 0)
    chains = [(g, h) for g in range(qt_ref.shape[0]) for h in range(FOX_HEADS)]
    qs = []
    for g, h in chains:
        e = h % 2
        qp = qt_ref[g, (h // 2) * aug:(h // 2 + 1) * aug, :]
        own = jnp.logical_or(jnp.logical_and(row >= e * hd, row < (e + 1) * hd),
                             jnp.logical_and(row >= LANES + e * FOX_AUG, row < LANES + (e + 1) * FOX_AUG))
        qs.append(jnp.where(own, qp, jnp.zeros_like(qp)))

    def tile(j, carry, masked):
        koff = pl.multiple_of(j * t, t)
        ss = [_dot(kaug_ref[g, pl.ds(koff, t), (h // 2) * aug:(h // 2 + 1) * aug], qs[n])
              for n, (g, h) in enumerate(chains)]
        stats, pms = [], []
        for n in range(len(chains)):
            m = carry[n][0]
            s = jnp.where(causal, ss[n], NEG_BIG) if masked else ss[n]
            m_new = jnp.maximum(m, jnp.max(s, axis=0, keepdims=True))
            stats.append((m_new, jnp.exp2(m - m_new)))
            pms.append(jnp.exp2(s - m_new).astype(BF16))
        out = []
        for n, (g, h) in enumerate(chains):
            m_new, alpha = stats[n]
            v_ones = jnp.concatenate([vt_ref[g, h * hd:(h + 1) * hd, pl.ds(koff, t)], ones_rows], axis=0)
            out.append((m_new, alpha * carry[n][1] + _dot(v_ones, pms[n])))
        return tuple(out)

    ones_rows = jnp.ones((FOX_DEN_ROWS, t), BF16)
    init = tuple((jnp.full((1, t), NEG_BIG, F32), jnp.zeros((hd + FOX_DEN_ROWS, t), F32)) for _ in chains)
    carry = lax.fori_loop(0, i, lambda j, c: tile(j, c, False), init)
    final = tile(i, carry, True)
    for n in range(0, len(chains), 2):
        g, h = chains[n]
        outs = [a[:hd] / a[hd:hd + 1] for _, a in final[n:n + 2]]
        o_ref[g, :, (h // 2) * LANES:(h // 2 + 1) * LANES] = jnp.concatenate(outs, axis=0).T.astype(BF16)


def _fox_prompt(qt16, kaug, vt16, n_seq, seq_len):
    t = FOX_TILE
    nq = seq_len // t
    aug_width = kaug.shape[1]
    group = FOX_GROUP if n_seq % FOX_GROUP == 0 else 1
    out = pl.pallas_call(
        _fox_prompt_body,
        grid=(n_seq // group, nq),
        in_specs=[pl.BlockSpec((group, aug_width, t), lambda b, i: (b, 0, i)),
                  pl.BlockSpec((group, seq_len, aug_width), lambda b, i: (b, 0, 0)),
                  pl.BlockSpec((group, FOX_WIDTH, seq_len), lambda b, i: (b, 0, 0))],
        out_specs=pl.BlockSpec((group, t, FOX_WIDTH), lambda b, i: (b, i, 0)),
        out_shape=jax.ShapeDtypeStruct((n_seq, seq_len, FOX_WIDTH), BF16),
        compiler_params=pltpu.CompilerParams(dimension_semantics=("arbitrary", "arbitrary"),
                                             vmem_limit_bytes=48 * MIB),
        name="fox_prompt",
    )(qt16, kaug.reshape(n_seq, seq_len, aug_width), vt16)
    return out.reshape(n_seq * seq_len, FOX_WIDTH)


def _bdot(a, b):
    return lax.dot_general(a.astype(BF16), b.astype(BF16), (((2,), (1,)), ((0,), (0,))),
                           preferred_element_type=F32)


def _bdot_nt(a, b):
    return lax.dot_general(a.astype(BF16), b.astype(BF16), (((2,), (2,)), ((0,), (0,))),
                           preferred_element_type=F32)


def _unit_lower_inverse_minus_eye(a, r, c):
    blk16 = (r // 16) == (c // 16)
    blk32 = (r // 32) == (c // 32)
    p = jnp.where(blk16, -a, 0.0)
    dt = p
    for _ in range(3):
        p = _bdot(p, p)
        dt = dt + p + _bdot(dt, p)
    for off in (jnp.where(jnp.logical_and(blk32, jnp.logical_not(blk16)), a, 0.0),
                jnp.where(blk32, 0.0, a)):
        x = off + _bdot(dt, off)
        dt = dt - (x + _bdot(x, dt))
    return dt


def _gdn_prompt_body(qkv_ref, sm_ref, gcr_ref, z_ref, gn_ref, o_ref, sout_ref, s_ref):
    ci = pl.program_id(1)

    @pl.when(ci == 0)
    def _():
        s_ref[...] = jnp.zeros_like(s_ref)

    ch = GDN_CHUNK
    d = GDN_HEAD_DIM
    nh = GDN_HEADS
    n_seq = qkv_ref.shape[0]
    n_chunks = qkv_ref.shape[1] // ch
    units = [(cidx, g, hh) for cidx in range(n_chunks) for g in range(n_seq) for hh in range(nh)]
    per_chunk = n_seq * nh

    def gather(fn):
        return jnp.stack([fn(g, slice(cidx * ch, (cidx + 1) * ch), hh) for cidx, g, hh in units], axis=0)

    q = gather(lambda g, rows, hh: qkv_ref[g, rows, hh * d:(hh + 1) * d])
    k = gather(lambda g, rows, hh: qkv_ref[g, rows, GDN_WIDTH + hh * d:GDN_WIDTH + (hh + 1) * d])
    v = gather(lambda g, rows, hh: qkv_ref[g, rows, 2 * GDN_WIDTH + hh * d:2 * GDN_WIDTH + (hh + 1) * d])
    beta = gather(lambda g, rows, hh: sm_ref[g, rows, SM_BETA + hh:SM_BETA + hh + 1])
    gc = gather(lambda g, rows, hh: sm_ref[g, rows, SM_GCUM + hh:SM_GCUM + hh + 1])
    gr = gather(lambda g, rows, hh: gcr_ref[g, hh:hh + 1, rows])

    r = lax.broadcasted_iota(jnp.int32, (1, ch, ch), 1)
    c = lax.broadcasted_iota(jnp.int32, (1, ch, ch), 2)
    lower = r >= c
    beta = jnp.broadcast_to(beta, (len(units), ch, d))
    gc = jnp.broadcast_to(gc, (len(units), ch, d))
    decay = jnp.where(lower, jnp.exp(jnp.where(lower, gc[:, :, :ch] - gr, 0.0)), 0.0)
    qk_kk = _bdot_nt(jnp.concatenate([q, k], axis=1), k)
    qk = qk_kk[:, :ch] * decay
    a = jnp.where(r > c, beta[:, :, :ch] * qk_kk[:, ch:] * decay, 0.0)
    dt = _unit_lower_inverse_minus_eye(a, r, c)
    eg = jnp.exp(gc)
    rhs = jnp.concatenate([v * beta, k * (beta * eg)], axis=-1)
    uw = rhs + _bdot(dt, rhs)
    gl = gc[:, ch - 1:ch, :]
    qd = (q * eg).astype(BF16)
    kd = k * jnp.exp(gl - gc)
    g_last = jnp.exp(gl)
    qk16 = qk.astype(BF16)

    s = s_ref[...]
    for cidx in range(n_chunks):
        us = slice(cidx * per_chunk, (cidx + 1) * per_chunk)
        rows = slice(cidx * ch, (cidx + 1) * ch)
        s16 = s.astype(BF16)
        v_new = uw[us, :, :d] - _bdot(uw[us, :, d:], s16)
        o = _bdot(qd[us], s16) + _bdot(qk16[us], v_new)
        vn16 = v_new.astype(BF16)
        upd = jnp.stack([_dot(kd[cidx * per_chunk + j].T.astype(BF16), vn16[j]) for j in range(per_chunk)],
                        axis=0)
        s = s * g_last[us] + upd
        for g in range(n_seq):
            for hh in range(nh):
                hs = slice(hh * d, (hh + 1) * d)
                o_ref[g, rows, hs] = (_rms(o[g * nh + hh], gn_ref[...]) * _silu(z_ref[g, rows, hs])).astype(BF16)
    s_ref[...] = s

    @pl.when(ci == pl.num_programs(1) - 1)
    def _():
        sout_ref[...] = s.reshape(sout_ref.shape)


def _gdn_prompt(gqkv, sm, smt, gz, gn, n_seq, seq_len):
    tm = GDN_TILE
    nt = seq_len // tm
    gcum_block = SM_GCUM // SUBLANES
    group = next(g for g in (GDN_GROUP, 2, 1) if n_seq % g == 0)
    seq3 = lambda a: a.reshape(n_seq, seq_len, a.shape[-1])
    tile = lambda w: pl.BlockSpec((group, tm, w), lambda b, i: (b, i, 0))
    out, s_out = pl.pallas_call(
        _gdn_prompt_body,
        grid=(n_seq // group, nt),
        in_specs=[tile(3 * GDN_WIDTH), tile(LANES),
                  pl.BlockSpec((group, SUBLANES, tm), lambda b, i: (b, gcum_block, i)),
                  tile(GDN_WIDTH),
                  pl.BlockSpec((1, GDN_HEAD_DIM), lambda b, i: (0, 0))],
        out_specs=(tile(GDN_WIDTH),
                   pl.BlockSpec((group, GDN_HEADS, GDN_HEAD_DIM, GDN_HEAD_DIM), lambda b, i: (b, 0, 0, 0))),
        out_shape=(jax.ShapeDtypeStruct((n_seq, seq_len, GDN_WIDTH), BF16),
                   jax.ShapeDtypeStruct((n_seq, GDN_HEADS, GDN_HEAD_DIM, GDN_HEAD_DIM), F32)),
        scratch_shapes=[pltpu.VMEM((group * GDN_HEADS, GDN_HEAD_DIM, GDN_HEAD_DIM), F32)],
        compiler_params=pltpu.CompilerParams(dimension_semantics=("arbitrary", "arbitrary"),
                                             vmem_limit_bytes=40 * MIB),
        name="gdn_prompt",
    )(seq3(gqkv), seq3(sm), smt, seq3(gz), gn)
    return out.reshape(n_seq * seq_len, GDN_WIDTH), s_out


def _ffn_tail(x, fox_ref, gdn_ref, wo_ref, nf_ref, wup_ref, wd_ref, nfin_ref, conv):
    mix = _dot(fox_ref[...], wo_ref[0:FOX_WIDTH, :]) + _dot(gdn_ref[...], wo_ref[FOX_WIDTH:, :])
    x2 = x + mix
    h2 = _rms(x2, nf_ref[...]).astype(BF16)
    gu = _dot(h2, wup_ref[...])
    gate, up = gu[:, :FFN_DIM], gu[:, FFN_DIM:]
    act = (_silu(conv(gate)) * up).astype(BF16)
    x3 = x2 + _dot(act, wd_ref[...])
    return _rms(x3, nfin_ref[...])


def _ffn_prompt_body(pt_ref, *refs, tiles_per_seq, rider_cfg):
    (x_ref, fox_ref, gdn_ref, wo_ref, nf_ref, wup_ref, cw_ref, cb_ref, wd_ref, nfin_ref) = refs[:10]
    rider_in = refs[10:10 + RIDER_INPUTS]
    y_ref, cst_ref, rider_out, prev_ref = refs[10 + RIDER_INPUTS:14 + RIDER_INPUTS]
    rider_scratch = refs[14 + RIDER_INPUTS:]
    i = pl.program_id(0)

    @pl.when(i % tiles_per_seq == 0)
    def _():
        prev_ref[...] = jnp.zeros_like(prev_ref)

    _rider_step(pt_ref, rider_in, rider_out, rider_scratch, rider_cfg)

    tm = x_ref.shape[0]

    def conv(gate):
        prev = prev_ref[...]
        out = gate * cw_ref[FFN_CONV - 1:FFN_CONV, :] + cb_ref[...]
        for kk in range(1, FFN_CONV):
            out = out + _shift_rows(gate, prev, kk) * cw_ref[FFN_CONV - 1 - kk:FFN_CONV - kk, :]
        prev_ref[...] = gate[tm - SUBLANES:, :]
        cst_ref[...] = gate[tm - SUBLANES:, :]
        return out

    y_ref[...] = _ffn_tail(x_ref[...], fox_ref, gdn_ref, wo_ref, nf_ref, wup_ref, wd_ref, nfin_ref, conv)


def _ffn_prompt(x, fox, gdn, wo, nf, wup, cw, cb, wd, nfin, seq_len, make_rider):
    rows = x.shape[0]
    tm = ROW_TILE
    n_seq = rows // seq_len
    tiles_per_seq = seq_len // tm
    row_spec = lambda w: pl.BlockSpec((tm, w), lambda i, pt: (i, 0))
    const = lambda a: pl.BlockSpec(a.shape, lambda i, pt: (0, 0), pipeline_mode=pl.Buffered(1))
    rider = make_rider(rows // tm)
    grid_spec = pltpu.PrefetchScalarGridSpec(
        num_scalar_prefetch=1,
        grid=(rows // tm,),
        in_specs=[row_spec(D_MODEL), row_spec(FOX_WIDTH), row_spec(GDN_WIDTH), const(wo), const(nf),
                  const(wup), const(cw), const(cb), const(wd), const(nfin)] + rider["in_specs"],
        out_specs=(row_spec(D_MODEL),
                   pl.BlockSpec((SUBLANES, FFN_DIM), lambda i, pt: (i // tiles_per_seq, 0)),
                   rider["out_spec"]),
        scratch_shapes=[pltpu.VMEM((SUBLANES, FFN_DIM), F32)] + rider["scratch"],
    )
    return pl.pallas_call(
        functools.partial(_ffn_prompt_body, tiles_per_seq=tiles_per_seq, rider_cfg=rider["cfg"]),
        grid_spec=grid_spec,
        out_shape=(jax.ShapeDtypeStruct((rows, D_MODEL), F32),
                   jax.ShapeDtypeStruct((n_seq * SUBLANES, FFN_DIM), F32),
                   rider["out_shape"]),
        compiler_params=pltpu.CompilerParams(dimension_semantics=("arbitrary",), vmem_limit_bytes=58 * MIB),
        name="ffn_prompt",
    )(rider["prefetch"], x, fox, gdn, wo, nf, wup, cw, cb, wd, nfin, *rider["operands"])


def _ffn_sample_body(x_ref, fox_ref, gdn_ref, wo_ref, nf_ref, wup_ref, cw_ref, cb_ref, wd_ref, nfin_ref,
                     c0_ref, c1_ref, y_ref, gate_ref):
    def conv(gate):
        gate_ref[...] = gate
        return (gate * cw_ref[2:3, :] + c1_ref[...] * cw_ref[1:2, :] + c0_ref[...] * cw_ref[0:1, :]
                + cb_ref[...])

    y_ref[...] = _ffn_tail(x_ref[...], fox_ref, gdn_ref, wo_ref, nf_ref, wup_ref, wd_ref, nfin_ref, conv)


def _ffn_sample(x, fox, gdn, wo, nf, wup, cw, cb, wd, nfin, c0, c1):
    rows = x.shape[0]
    args = (x, fox, gdn, wo, nf, wup, cw, cb, wd, nfin, c0, c1)
    full = lambda a: pl.BlockSpec(a.shape, lambda i: (0, 0), pipeline_mode=pl.Buffered(1))
    return pl.pallas_call(
        _ffn_sample_body,
        grid=(1,),
        in_specs=[full(a) for a in args],
        out_specs=(pl.BlockSpec((rows, D_MODEL), lambda i: (0, 0)),
                   pl.BlockSpec((rows, FFN_DIM), lambda i: (0, 0))),
        out_shape=(jax.ShapeDtypeStruct((rows, D_MODEL), F32), jax.ShapeDtypeStruct((rows, FFN_DIM), F32)),
        compiler_params=pltpu.CompilerParams(dimension_semantics=("arbitrary",), vmem_limit_bytes=56 * MIB),
        name="ffn_sample",
    )(*args)


def _split3(x):
    hi = x.astype(BF16)
    r1 = x - hi.astype(F32)
    mid = r1.astype(BF16)
    lo = (r1 - mid.astype(F32)).astype(BF16)
    return hi, mid, lo


RIDER_INPUTS = 8
RIDER_SCRATCH = 4


def _rider_step(pt_ref, rider_in, o_ref, rider_scratch, cfg):
    q_ref, kn_ref, vn_ref, sm_ref, msuf_ref, kt_pool, vt_pool, lf_pool = rider_in
    kbuf, vbuf, lbuf, sems = rider_scratch
    base, count, per_step, n_pages = cfg
    i = pl.program_id(0)

    def copies(g, sl):
        out = []
        for j in range(n_pages):
            pg = pt_ref[g * n_pages + j]
            out.append(pltpu.make_async_copy(kt_pool.at[pg], kbuf.at[sl, j], sems.at[sl, 0]))
            out.append(pltpu.make_async_copy(vt_pool.at[pg], vbuf.at[sl, j], sems.at[sl, 1]))
            out.append(pltpu.make_async_copy(lf_pool.at[pg], lbuf.at[sl, j], sems.at[sl, 2]))
        return out

    def start_all(g, sl):
        for n, cp in enumerate(copies(g, sl)):
            kind, j = n % 3, n // 3
            cp.start(priority=(0, 1, j % 2)[kind])

    @pl.when(i == 0)
    def _():
        start_all(base, base % 2)

    for r in range(per_step):
        g = base + i * per_step + r
        slot = g % 2

        @pl.when(g + 1 < base + count)
        def _():
            start_all(g + 1, 1 - slot)

        for cp in copies(g, slot):
            cp.wait()
        o_ref[r] = _decode_row(slot, q_ref[r], kn_ref[r], vn_ref[r], sm_ref[r], msuf_ref[...],
                               kbuf, vbuf, lbuf, n_pages)


def _decode_row(slot, q, k_new, v_new, sm_row, msuf, kbuf, vbuf, lbuf, n_pages):
    hd = FOX_HEAD_DIM
    page = kbuf.shape[-1]

    r8 = lax.broadcasted_iota(jnp.int32, (SUBLANES, LANES), 0)
    c8 = lax.broadcasted_iota(jnp.int32, (SUBLANES, LANES), 1)
    lf_new = jnp.sum(jnp.where(r8 == c8, sm_row, 0.0), axis=1, keepdims=True)
    n_rows = n_pages * FOX_HEADS
    lf = lbuf[slot].reshape(n_rows, page)
    within = functools.reduce(lambda a, c: a + c, [_dot(t, msuf) for t in _split3(lf)])
    tot = jnp.broadcast_to(jnp.sum(lf, axis=1, keepdims=True), (n_rows, page))
    row = lax.broadcasted_iota(jnp.int32, (n_rows, page), 0)
    later = tot
    k = FOX_HEADS
    while k < n_rows:
        later = later + jnp.where(row + k < n_rows, pltpu.roll(later, n_rows - k, 0), 0.0)
        k *= 2
    bias = (within + (later - tot)).reshape(n_pages, FOX_HEADS, page) + lf_new[None]

    q_t = q.T
    qb = jnp.stack([jnp.broadcast_to(q_t[:, h:h + 1], (hd, page)) for h in range(FOX_HEADS)], axis=0)
    logits = [jnp.sum(kbuf[slot, j] * qb, axis=1) + bias[j] for j in range(n_pages)]
    s_new = jnp.sum(k_new * q, axis=-1, keepdims=True)
    m = functools.reduce(jnp.maximum, [jnp.max(x, axis=-1, keepdims=True) for x in logits] + [s_new])
    ps = [jnp.exp(x - m) for x in logits]
    p_new = jnp.exp(s_new - m)
    l = functools.reduce(lambda a, c: a + c, [jnp.sum(x, axis=-1, keepdims=True) for x in ps] + [p_new])

    cols = []
    for h in range(FOX_HEADS):
        acc = vbuf[slot, 0, h] * ps[0][h:h + 1, :]
        for j in range(1, n_pages):
            acc = acc + vbuf[slot, j, h] * ps[j][h:h + 1, :]
        cols.append(jnp.sum(acc, axis=1, keepdims=True))
    mat = jnp.concatenate(cols + [jnp.zeros((hd, LANES - FOX_HEADS), F32)], axis=1)
    return (mat.T[0:FOX_HEADS, :] + p_new * v_new) / l


def _rider(page_table, q, kn, vn, sm, kt_pool, vt_pool, lf_pool, base, count, n_steps):
    n_pages = page_table.shape[1]
    page = kt_pool.shape[-1]
    per_step = count // n_steps
    assert per_step * n_steps == count, "sample rows must split evenly over the host's grid steps"
    tok = jnp.arange(page)
    msuf = (tok[:, None] > tok[None, :]).astype(BF16)
    rows = lambda a: a[base:base + count]
    tile = lambda a: pl.BlockSpec((per_step,) + a.shape[1:], lambda i, pt: (i,) + (0,) * (a.ndim - 1))
    hbm = pl.BlockSpec(memory_space=pl.ANY)
    operands = [rows(q), rows(kn), rows(vn), rows(sm).reshape(count, 1, LANES), msuf, kt_pool, vt_pool, lf_pool]
    kv_buf = pltpu.VMEM((2, n_pages, FOX_HEADS, FOX_HEAD_DIM, page), F32)
    return dict(
        prefetch=page_table.reshape(-1),
        operands=operands,
        in_specs=[tile(a) for a in operands[:4]] + [pl.BlockSpec(msuf.shape, lambda i, pt: (0, 0)), hbm, hbm, hbm],
        out_spec=pl.BlockSpec((per_step, FOX_HEADS, FOX_HEAD_DIM), lambda i, pt: (i, 0, 0)),
        out_shape=jax.ShapeDtypeStruct((count, FOX_HEADS, FOX_HEAD_DIM), F32),
        scratch=[kv_buf, kv_buf, pltpu.VMEM((2, n_pages, FOX_HEADS, page), F32), pltpu.SemaphoreType.DMA((2, 3))],
        cfg=(base, count, per_step, n_pages),
    )


def _gdn_sample_body(s_ref, qkv_ref, sm_ref, z_ref, gn_ref, sout_ref, o_ref):
    d = GDN_HEAD_DIM
    eye = lax.broadcasted_iota(jnp.int32, (d, d), 0) == lax.broadcasted_iota(jnp.int32, (d, d), 1)

    def col(rows):
        return jnp.sum(jnp.where(eye[None], rows[:, None, :], 0.0), axis=2, keepdims=True)

    for hh in range(GDN_HEADS):
        hs = slice(hh * d, (hh + 1) * d)
        q = qkv_ref[:, hs]
        k = qkv_ref[:, GDN_WIDTH + hh * d:GDN_WIDTH + (hh + 1) * d]
        v = qkv_ref[:, 2 * GDN_WIDTH + hh * d:2 * GDN_WIDTH + (hh + 1) * d]
        eg = jnp.exp(sm_ref[:, SM_G + hh:SM_G + hh + 1])
        beta = sm_ref[:, SM_BETA + hh:SM_BETA + hh + 1]
        s = s_ref[:, hh]
        k_col = col(k)
        v_new = beta * (v - eg * jnp.sum(k_col * s, axis=1))
        o = eg * jnp.sum(col(q) * s, axis=1) + jnp.sum(q * k, axis=1, keepdims=True) * v_new
        sout_ref[:, hh] = s * eg[:, :, None] + k_col * v_new[:, None, :]
        o_ref[:, hs] = _rms(o, gn_ref[...]) * _silu(z_ref[:, hs])


def _gdn_sample(state, gqkv, sm, gz, gn):
    nb = state.shape[0]
    bb = 2 * SUBLANES if nb % (2 * SUBLANES) == 0 else SUBLANES
    d = GDN_HEAD_DIM
    row_spec = lambda w: pl.BlockSpec((bb, w), lambda i: (i, 0))
    st_spec = pl.BlockSpec((bb, GDN_HEADS, d, d), lambda i: (i, 0, 0, 0))
    return pl.pallas_call(
        _gdn_sample_body,
        grid=(nb // bb,),
        in_specs=[st_spec, row_spec(3 * GDN_WIDTH), row_spec(LANES), row_spec(GDN_WIDTH),
                  pl.BlockSpec((1, d), lambda i: (0, 0))],
        out_specs=(st_spec, row_spec(GDN_WIDTH)),
        out_shape=(jax.ShapeDtypeStruct(state.shape, F32), jax.ShapeDtypeStruct((nb, GDN_WIDTH), F32)),
        compiler_params=pltpu.CompilerParams(dimension_semantics=("arbitrary",), vmem_limit_bytes=32 * MIB),
        name="gdn_sample",
    )(state, gqkv, sm, gz, gn)


def _pad_rows(a, rows):
    return jnp.concatenate([a, jnp.zeros((rows - a.shape[0],) + a.shape[1:], a.dtype)], axis=0)


def kernel(x_prompt, x_sample, cache_k, cache_v, cache_logf, state_gdn, state_gdn_conv, state_ffn_conv,
           page_table, norm_mix, w_in, b_forget, gdn_a_log, gdn_dt_bias, w_gdn_conv, gdn_out_norm, w_out,
           norm_ffn, w_up, w_ffn_conv, b_ffn_conv, w_down, norm_final):
    assert w_in.shape[0] == 1, "single-layer trunk"
    n_seq, seq_len, _ = x_prompt.shape
    nb = x_sample.shape[0]
    n_pool, page = cache_k.shape[1], cache_k.shape[2]

    w = jnp.transpose(w_in[0])
    o_ff = 3 * FOX_WIDTH
    o_g = o_ff + FOX_HEADS
    o_ga = o_g + 3 * GDN_WIDTH
    o_gz = o_ga + 2 * GDN_HEADS
    wfox, wgdn, wgz = w[:o_ff].astype(BF16), w[o_g:o_ga].astype(BF16), w[o_gz:].astype(BF16)
    wsm = jnp.concatenate([w[o_ff:o_g], w[o_ga:o_gz],
                           jnp.zeros((LANES - FOX_HEADS - 2 * GDN_HEADS, D_MODEL), F32)], axis=0).astype(BF16)
    par = jnp.zeros((SUBLANES, LANES), F32)
    par = par.at[0, SM_LF:SM_LF + FOX_HEADS].set(b_forget[0])
    par = par.at[0, SM_G:SM_G + GDN_HEADS].set(gdn_dt_bias[0])
    par = par.at[1, SM_G:SM_G + GDN_HEADS].set(gdn_a_log[0])
    nrm = norm_mix[0][None, :]
    cw = _pad_rows(w_gdn_conv[0], SUBLANES)
    gn = gdn_out_norm[0][None, :]
    wo = w_out[0].astype(BF16)
    nf = norm_ffn[0][None, :]
    wup = w_up[0].astype(BF16)
    cwf = _pad_rows(w_ffn_conv[0], SUBLANES)
    cbf = b_ffn_conv[0][None, :]
    wd = w_down[0].astype(BF16)
    nfin = norm_final[None, :]

    xs = x_sample.reshape(nb, D_MODEL)
    gctx = state_gdn_conv[0]
    fctx = state_ffn_conv[0]
    fk_s, fv_s, q_s, gqkv_s, gz_s, sm_s, pre_s = _proj_sample(
        xs, nrm, wfox, wgdn, wgz, wsm, par, cw, gctx[:, 0], gctx[:, 1], gctx[:, 2])
    heads = lambda a: a.reshape(nb, FOX_HEADS, FOX_HEAD_DIM)
    half = nb // 2
    decode_args = (page_table, heads(q_s), heads(fk_s), heads(fv_s), sm_s,
                   jnp.transpose(cache_k[0], (0, 2, 3, 1)), jnp.transpose(cache_v[0], (0, 2, 3, 1)),
                   jnp.transpose(cache_logf[0], (0, 2, 1)))
    rider_a = functools.partial(_rider, *decode_args, 0, half)
    rider_b = functools.partial(_rider, *decode_args, half, nb - half)

    xp = x_prompt.reshape(n_seq * seq_len, D_MODEL)
    kt, vt, qt16, kaug, vt16, gqkv, gz, sm, smt, cst, fox_sa = _proj_prompt(
        xp, nrm, wfox, wgdn, wgz, wsm, par, cw, seq_len, rider_a)
    fox = _fox_prompt(qt16, kaug, vt16, n_seq, seq_len)
    gdn, s_p = _gdn_prompt(gqkv, sm, smt, gz, gn, n_seq, seq_len)
    yp, cstf, fox_sb = _ffn_prompt(xp, fox, gdn, wo, nf, wup, cwf, cbf, wd, nfin, seq_len, rider_b)

    fox_s = jnp.concatenate([fox_sa, fox_sb], axis=0)
    s_s, gdn_s = _gdn_sample(state_gdn[0], gqkv_s, sm_s, gz_s, gn)
    ys, gate_s = _ffn_sample(xs, fox_s.reshape(nb, FOX_WIDTH).astype(BF16), gdn_s.astype(BF16), wo, nf, wup,
                             cwf, cbf, wd, nfin, fctx[:, 0], fctx[:, 1])

    kv_shape_s = (1, nb, 1, FOX_HEADS, FOX_HEAD_DIM)
    new_kv_p = lambda a: jnp.transpose(a.reshape(1, n_seq, FOX_HEADS, FOX_HEAD_DIM, seq_len), (0, 1, 4, 2, 3))
    return (
        yp.reshape(n_seq, seq_len, D_MODEL),
        ys.reshape(nb, 1, D_MODEL),
        new_kv_p(kt),
        new_kv_p(vt),
        sm[:, SM_LF:SM_LF + FOX_HEADS].reshape(1, n_seq, seq_len, FOX_HEADS),
        s_p[None],
        cst.reshape(n_seq, SUBLANES, 3 * GDN_WIDTH)[None, :, SUBLANES - (GDN_CONV - 1):],
        cstf.reshape(n_seq, SUBLANES, FFN_DIM)[None, :, SUBLANES - (FFN_CONV - 1):],
        fk_s.reshape(kv_shape_s),
        fv_s.reshape(kv_shape_s),
        sm_s[:, SM_LF:SM_LF + FOX_HEADS].reshape(1, nb, 1, FOX_HEADS),
        s_s[None],
        jnp.concatenate([gctx[:, 1:], pre_s[:, None, :]], axis=1)[None],
        jnp.concatenate([fctx[:, 1:], gate_s[:, None, :]], axis=1)[None],
    )
```
